```python
import math
import jax, jax.numpy as jnp
from jax import lax
import numpy as np

D_MODEL = 4096
BATCH = 4
SEQ = 2048
DEPTH = 1

D_MIX = D_MODEL
D_ATTN = D_MIX // 2
D_GMLP = D_MIX - D_ATTN
ATTN_HEADS = 8
ATTN_HEAD_DIM = D_ATTN // (2 * ATTN_HEADS)
ATTN_V_DIM = 2 * ATTN_HEAD_DIM
GMLP_HEADS = 8
GMLP_HEAD_DIM = D_GMLP // GMLP_HEADS
CHUNK = 128
Q_BLOCK = 128
D_IN = 3 * D_ATTN + 2 * D_GMLP
REL_BUCKETS = 32
REL_MAX_DIST = 128
N_GROUPS = 8
EXPERTS_PER_GROUP = 8
N_EXPERTS = N_GROUPS * EXPERTS_PER_GROUP
TOP_K = 2
D_FF_EXPERT = 512
MOE_BLOCK = 128
LN_EPS = 1e-5
DN_ALPHA = (2 * DEPTH) ** 0.25
DN_BETA = (8 * DEPTH) ** -0.25
NEG_INF = -1e30

kernel_name = "hybrid_diffattn_gmlp_hiermoe_deepnorm"


def layer_norm(x, g, b):
    xf = x.astype(jnp.float32)
    mu = jnp.mean(xf, axis=-1, keepdims=True)
    var = jnp.mean(jnp.square(xf - mu), axis=-1, keepdims=True)
    return ((xf - mu) * lax.rsqrt(var + LN_EPS) * g.astype(jnp.float32) + b.astype(jnp.float32)).astype(x.dtype)


def rms_norm(x, g):
    xf = x.astype(jnp.float32)
    return (xf * lax.rsqrt(jnp.mean(jnp.square(xf), axis=-1, keepdims=True) + LN_EPS) * g.astype(jnp.float32)).astype(x.dtype)


def rel_bucket(q_pos, k_pos):
    n = jnp.maximum(q_pos[:, None] - k_pos[None, :], 0)
    max_exact = REL_BUCKETS // 2
    nf = jnp.maximum(n, max_exact).astype(jnp.float32)
    large = max_exact + (jnp.log(nf / max_exact) / math.log(REL_MAX_DIST / max_exact)
                         * (REL_BUCKETS - max_exact)).astype(jnp.int32)
    large = jnp.minimum(large, REL_BUCKETS - 1)
    return jnp.where(n < max_exact, n, large)


def diff_attention(q, k, v, lam, lambda_init, subln_g, rel_table):
    B, H, _, S, Dh = q.shape
    nb = S // Q_BLOCK
    scale = ATTN_HEAD_DIM ** -0.5
    k_pos = jnp.arange(S)
    qb = q.reshape(B, H, 2, nb, Q_BLOCK, Dh).transpose(3, 0, 1, 2, 4, 5)

    def block(args):
        i, q_i = args
        q_pos = i * Q_BLOCK + jnp.arange(Q_BLOCK)
        bias = rel_table[rel_bucket(q_pos, k_pos)].transpose(2, 0, 1).astype(jnp.float32)
        s = jnp.einsum('bhmqd,bhmkd->bhmqk', q_i, k).astype(jnp.float32) * scale
        s = s + bias[None, :, None]
        causal = k_pos[None, :] <= q_pos[:, None]
        s = jnp.where(causal, s, NEG_INF)
        p = jax.nn.softmax(s, axis=-1)
        a = p[:, :, 0] - lam * p[:, :, 1]
        return jnp.einsum('bhqk,bhkd->bhqd', a.astype(v.dtype), v)

    out = lax.map(block, (jnp.arange(nb), qb))
    out = out.transpose(1, 0, 3, 2, 4).reshape(B, S, H, ATTN_V_DIM)
    out = rms_norm(out, subln_g) * (1.0 - lambda_init)
    return out.reshape(B, S, H * ATTN_V_DIM)


def chunked_spatial_gating(u, g, ln_g, ln_b, w_s, b_s):
    B, S, _ = u.shape
    u = jax.nn.gelu(u, approximate=False)
    v = layer_norm(jax.nn.gelu(g, approximate=False), ln_g, ln_b)
    vc = v.reshape(B, S // CHUNK, CHUNK, GMLP_HEADS, GMLP_HEAD_DIM)
    tril = jnp.tril(jnp.ones((CHUNK, CHUNK), dtype=bool))
    ws = jnp.where(tril[None], w_s, 0)
    mixed = jnp.einsum('hts,bcshd->bcthd', ws, vc) + b_s.T[None, None, :, :, None]
    return u * mixed.reshape(B, S, D_GMLP)


def hierarchical_moe(x, w_group, b_group, w_expert, b_expert, w_gate, w_up, w_down):
    B, S, D = x.shape
    N = B * S
    xf = x.reshape(N, D)
    g_logits = (xf @ w_group + b_group).astype(jnp.float32)
    g_prob = jax.nn.softmax(g_logits, axis=-1)
    g_idx = jnp.argmax(g_logits, axis=-1)
    g_w = jnp.take_along_axis(g_prob, g_idx[:, None], axis=-1)
    e_logits = (jnp.einsum('nd,dge->nge', xf, w_expert) + b_expert).astype(jnp.float32)
    e_logits = jnp.take_along_axis(e_logits, g_idx[:, None, None], axis=1)[:, 0]
    e_top, e_idx = lax.top_k(e_logits, TOP_K)
    e_w = jax.nn.softmax(e_top, axis=-1)
    weights = (g_w * e_w).reshape(-1)
    expert_id = (g_idx[:, None] * EXPERTS_PER_GROUP + e_idx).reshape(-1).astype(jnp.int32)
    token_id = jnp.repeat(jnp.arange(N, dtype=jnp.int32), TOP_K)
    A = N * TOP_K
    order = jnp.argsort(expert_id)
    s_eid = expert_id[order]
    s_tok = token_id[order]
    s_w = weights[order]
    counts = jnp.bincount(expert_id, length=N_EXPERTS)
    starts = jnp.cumsum(counts) - counts
    padded = (counts + MOE_BLOCK - 1) // MOE_BLOCK * MOE_BLOCK
    pad_ends = jnp.cumsum(padded)
    pad_starts = pad_ends - padded
    n_blocks = -(-A // MOE_BLOCK) + N_EXPERTS
    dest = pad_starts[s_eid] + (jnp.arange(A) - starts[s_eid])
    tok_pad = jnp.full((n_blocks * MOE_BLOCK,), N, jnp.int32).at[dest].set(s_tok)
    w_pad = jnp.zeros((n_blocks * MOE_BLOCK,), jnp.float32).at[dest].set(s_w)
    block_expert = jnp.minimum(
        jnp.searchsorted(pad_ends, jnp.arange(n_blocks) * MOE_BLOCK, side='right'), N_EXPERTS - 1)
    x_pad = jnp.concatenate([xf, jnp.zeros((1, D), xf.dtype)], axis=0)

    def expert_block(args):
        tok, e = args
        xb = x_pad[tok]
        h = jax.nn.silu(xb @ w_gate[e]) * (xb @ w_up[e])
        return h @ w_down[e]

    out = lax.map(expert_block, (tok_pad.reshape(n_blocks, MOE_BLOCK), block_expert))
    out = out.reshape(-1, D) * w_pad[:, None].astype(out.dtype)
    y = jnp.zeros((N + 1, D), out.dtype).at[tok_pad].add(out)[:N]
    return y.reshape(B, S, D)


def setup_inputs(seed: int = 0) -> dict:
    key = jax.random.key(seed)
    ks = jax.random.split(key, 26)
    n = lambda k, shape, s: jax.random.normal(k, shape, jnp.float32) * s
    return {
        "x": n(ks[0], (BATCH, SEQ, D_MODEL), 1.0),
        "w_in": n(ks[1], (DEPTH, D_MODEL, D_IN), D_MODEL ** -0.5),
        "w_out": n(ks[2], (DEPTH, D_MIX, D_MODEL), D_MIX ** -0.5 * DN_BETA),
        "lambda_q1": n(ks[3], (DEPTH, ATTN_HEAD_DIM), 0.1),
        "lambda_k1": n(ks[4], (DEPTH, ATTN_HEAD_DIM), 0.1),
        "lambda_q2": n(ks[5], (DEPTH, ATTN_HEAD_DIM), 0.1),
        "lambda_k2": n(ks[6], (DEPTH, ATTN_HEAD_DIM), 0.1),
        "subln_g": 1.0 + n(ks[7], (DEPTH, ATTN_V_DIM), 0.1),
        "beta_attn": 1.0 + n(ks[8], (DEPTH, D_ATTN), 0.1),
        "gmlp_ln_g": 1.0 + n(ks[9], (DEPTH, D_GMLP), 0.1),
        "gmlp_ln_b": n(ks[10], (DEPTH, D_GMLP), 0.02),
        "spatial_w": n(ks[11], (DEPTH, GMLP_HEADS, CHUNK, CHUNK), CHUNK ** -0.5),
        "spatial_b": 1.0 + n(ks[12], (DEPTH, GMLP_HEADS, CHUNK), 0.1),
        "beta_gmlp": 1.0 + n(ks[13], (DEPTH, D_GMLP), 0.1),
        "rel_bias": n(ks[14], (REL_BUCKETS, ATTN_HEADS), 0.5),
        "ln1_g": 1.0 + n(ks[15], (DEPTH, D_MODEL), 0.1),
        "ln1_b": n(ks[16], (DEPTH, D_MODEL), 0.02),
        "w_group": n(ks[17], (DEPTH, D_MODEL, N_GROUPS), D_MODEL ** -0.5),
        "b_group": n(ks[18], (DEPTH, N_GROUPS), 0.01),
        "w_expert": n(ks[19], (DEPTH, D_MODEL, N_GROUPS, EXPERTS_PER_GROUP), D_MODEL ** -0.5),
        "b_expert": n(ks[20], (DEPTH, N_GROUPS, EXPERTS_PER_GROUP), 0.01),
        "w_gate": n(ks[21], (DEPTH, N_EXPERTS, D_MODEL, D_FF_EXPERT), D_MODEL ** -0.5),
        "w_up": n(ks[22], (DEPTH, N_EXPERTS, D_MODEL, D_FF_EXPERT), D_MODEL ** -0.5),
        "w_down": n(ks[23], (DEPTH, N_EXPERTS, D_FF_EXPERT, D_MODEL), D_FF_EXPERT ** -0.5 * DN_BETA),
        "ln2_g": 1.0 + n(ks[24], (DEPTH, D_MODEL), 0.1),
        "ln2_b": n(ks[25], (DEPTH, D_MODEL), 0.02),
    }


def reference(x, w_in, w_out, lambda_q1, lambda_k1, lambda_q2, lambda_k2, subln_g, beta_attn,
              gmlp_ln_g, gmlp_ln_b, spatial_w, spatial_b, beta_gmlp, rel_bias, ln1_g, ln1_b,
              w_group, b_group, w_expert, b_expert, w_gate, w_up, w_down, ln2_g, ln2_b):
    B, S, _ = x.shape
    for l in range(DEPTH):
        lambda_init = 0.8 - 0.6 * math.exp(-0.3 * l)
        proj = x @ w_in[l]
        q, k, v, u, g = jnp.split(proj, [D_ATTN, 2 * D_ATTN, 3 * D_ATTN, 3 * D_ATTN + D_GMLP], axis=-1)
        q = q.reshape(B, S, ATTN_HEADS, 2, ATTN_HEAD_DIM).transpose(0, 2, 3, 1, 4)
        k = k.reshape(B, S, ATTN_HEADS, 2, ATTN_HEAD_DIM).transpose(0, 2, 3, 1, 4)
        v = v.reshape(B, S, ATTN_HEADS, ATTN_V_DIM).transpose(0, 2, 1, 3)
        lam = (jnp.exp(jnp.sum(lambda_q1[l].astype(jnp.float32) * lambda_k1[l].astype(jnp.float32)))
               - jnp.exp(jnp.sum(lambda_q2[l].astype(jnp.float32) * lambda_k2[l].astype(jnp.float32)))
               + lambda_init)
        attn = diff_attention(q, k, v, lam, lambda_init, subln_g[l], rel_bias) * beta_attn[l]
        gm = rms_norm(chunked_spatial_gating(u, g, gmlp_ln_g[l], gmlp_ln_b[l], spatial_w[l], spatial_b[l]),
                      beta_gmlp[l])
        mix = jnp.concatenate([attn, gm], axis=-1) @ w_out[l]
        x = layer_norm(DN_ALPHA * x + mix, ln1_g[l], ln1_b[l])
        moe = hierarchical_moe(x, w_group[l], b_group[l], w_expert[l], b_expert[l],
                               w_gate[l], w_up[l], w_down[l])
        x = layer_norm(DN_ALPHA * x + moe, ln2_g[l], ln2_b[l])
    return x
```

```python
import functools
import math

import jax
import jax.numpy as jnp
from jax import lax
from jax.experimental import pallas as pl
from jax.experimental.pallas import tpu as pltpu

F32 = jnp.float32
BF16 = jnp.bfloat16

ATTN_HEADS = 8
ATTN_HEAD_DIM = 128
ATTN_V_DIM = 2 * ATTN_HEAD_DIM
GMLP_HEADS = 8
CHUNK = 128
REL_BUCKETS = 32
REL_MAX_DIST = 128
N_GROUPS = 8
EXPERTS_PER_GROUP = 8
N_EXPERTS = N_GROUPS * EXPERTS_PER_GROUP
TOP_K = 2
LN_EPS = 1e-5
DEPTH = 1
DN_ALPHA = (2 * DEPTH) ** 0.25
NEG_INF = -1e30
LANES = 128
VMEM_LIMIT = 56 * 1024 * 1024


def _params(*semantics):
    return pltpu.CompilerParams(dimension_semantics=semantics, vmem_limit_bytes=VMEM_LIMIT)


def _matmul_kernel(x_ref, w_ref, o_ref):
    o_ref[...] = jnp.dot(x_ref[...], w_ref[...], preferred_element_type=F32).astype(o_ref.dtype)


def _proj_matmul(xb, wb, tm=1024, tn=512):
    m, k = xb.shape
    n = wb.shape[1]
    return pl.pallas_call(
        _matmul_kernel,
        grid=(m // tm, n // tn),
        in_specs=[pl.BlockSpec((tm, k), lambda i, j: (i, 0)),
                  pl.BlockSpec((k, tn), lambda i, j: (0, j))],
        out_specs=pl.BlockSpec((tm, tn), lambda i, j: (i, j)),
        out_shape=jax.ShapeDtypeStruct((m, n), BF16),
        compiler_params=_params("arbitrary", "arbitrary"),
        name="proj_matmul",
    )(xb, wb)


def _rel_bucket(n):
    max_exact = REL_BUCKETS // 2
    nf = jnp.maximum(n, max_exact).astype(F32)
    large = max_exact + (jnp.log(nf / max_exact) / math.log(REL_MAX_DIST / max_exact)
                         * (REL_BUCKETS - max_exact)).astype(jnp.int32)
    large = jnp.minimum(large, REL_BUCKETS - 1)
    return jnp.where(n < max_exact, n, large)


def _rel_bias_tiles(rel_bias, blk):
    pos = jnp.arange(blk)
    tiles = []
    for d in (0, 1):
        n = jnp.maximum(pos[:, None] + d * blk - pos[None, :], 0)
        tiles.append(rel_bias[_rel_bucket(n)].transpose(2, 0, 1))
    return jnp.stack(tiles, axis=1).astype(F32)


def _attn_kernel(far_ref, q_ref, k_ref, v_ref, bias_ref, lq1_ref, lk1_ref, lq2_ref, lk2_ref,
                 sg_ref, ba_ref, o_ref, m_ref, l_ref, acc_ref, *, blk, lambda_init):
    h = pl.program_id(1)
    i = pl.program_id(2)
    dh = ATTN_HEAD_DIM
    scale = ATTN_HEAD_DIM ** -0.5
    q = q_ref[...]
    q1 = q[:, :dh]
    q2 = q[:, dh:]
    m_ref[...] = jnp.full(m_ref.shape, NEG_INF, F32)
    l_ref[...] = jnp.zeros(l_ref.shape, F32)
    acc_ref[...] = jnp.zeros(acc_ref.shape, F32)
    contract_last = (((1,), (1,)), ((), ()))

    def process(j, bias, mask):
        start = pl.multiple_of(j * blk, blk)
        kj = k_ref[pl.ds(start, blk), :]
        vj = v_ref[pl.ds(start, blk), :]
        ps = []
        alphas = []
        for mi, qm in enumerate((q1, q2)):
            s = lax.dot_general(qm, kj[:, mi * dh:(mi + 1) * dh], contract_last,
                                preferred_element_type=F32) * scale + bias
            if mask is not None:
                s = jnp.where(mask, s, NEG_INF)
            m_old = m_ref[mi]
            m_new = jnp.maximum(m_old, jnp.max(s, axis=-1, keepdims=True))
            a = jnp.exp(m_old - m_new)
            p = jnp.exp(s - m_new)
            l_ref[mi] = a * l_ref[mi] + jnp.sum(p, axis=-1, keepdims=True)
            m_ref[mi] = m_new
            ps.append(p.astype(BF16))
            alphas.append(a)
        pv = jnp.dot(jnp.concatenate(ps, axis=0), vj, preferred_element_type=F32)
        acc_ref[0] = alphas[0] * acc_ref[0] + pv[:blk]
        acc_ref[1] = alphas[1] * acc_ref[1] + pv[blk:]

    far = far_ref[h]

    def far_body(j, carry):
        process(j, far, None)
        return carry

    lax.fori_loop(0, jnp.maximum(i - 1, 0), far_body, 0)

    @pl.when(i >= 1)
    def _():
        process(i - 1, bias_ref[1], None)

    row = lax.broadcasted_iota(jnp.int32, (blk, blk), 0)
    col = lax.broadcasted_iota(jnp.int32, (blk, blk), 1)
    process(i, bias_ref[0], col <= row)

    lam = (jnp.exp(jnp.sum(lq1_ref[...] * lk1_ref[...], axis=-1, keepdims=True))
           - jnp.exp(jnp.sum(lq2_ref[...] * lk2_ref[...], axis=-1, keepdims=True)) + lambda_init)
    o = acc_ref[0] / l_ref[0] - lam * (acc_ref[1] / l_ref[1])
    o = o * lax.rsqrt(jnp.mean(jnp.square(o), axis=-1, keepdims=True) + LN_EPS) * sg_ref[...]
    o = o * (1.0 - lambda_init) * ba_ref[...]
    o_ref[...] = o.astype(o_ref.dtype)


def _diff_attention(proj3, rel_bias, lq1, lk1, lq2, lk2, subln_g, beta_attn, lambda_init, blk=256):
    b, s, _ = proj3.shape
    h = ATTN_HEADS
    dv = ATTN_V_DIM
    assert REL_BUCKETS // 2 + int(math.log((blk + 1) / (REL_BUCKETS // 2)) / math.log(REL_MAX_DIST / (REL_BUCKETS // 2))
                                  * (REL_BUCKETS // 2)) >= REL_BUCKETS
    tiles = _rel_bias_tiles(rel_bias, blk)
    far = rel_bias[REL_BUCKETS - 1].astype(F32)
    vec = lambda c: pl.BlockSpec((1, c), lambda bi, hi, qi: (0, 0))
    kernel = functools.partial(_attn_kernel, blk=blk, lambda_init=lambda_init)
    return pl.pallas_call(
        kernel,
        grid=(b, h, s // blk),
        in_specs=[
            pl.BlockSpec(memory_space=pltpu.SMEM),
            pl.BlockSpec((None, blk, dv), lambda bi, hi, qi: (bi, qi, hi)),
            pl.BlockSpec((None, s, dv), lambda bi, hi, qi: (bi, 0, h + hi)),
            pl.BlockSpec((None, s, dv), lambda bi, hi, qi: (bi, 0, 2 * h + hi)),
            pl.BlockSpec((None, 2, blk, blk), lambda bi, hi, qi: (hi, 0, 0, 0)),
            vec(ATTN_HEAD_DIM), vec(ATTN_HEAD_DIM), vec(ATTN_HEAD_DIM), vec(ATTN_HEAD_DIM),
            vec(dv),
            pl.BlockSpec((1, dv), lambda bi, hi, qi: (0, hi)),
        ],
        out_specs=pl.BlockSpec((None, blk, dv), lambda bi, hi, qi: (bi, qi, hi)),
        out_shape=jax.ShapeDtypeStruct((b, s, h * dv), BF16),
        scratch_shapes=[pltpu.VMEM((2, blk, 1), F32), pltpu.VMEM((2, blk, 1), F32),
                        pltpu.VMEM((2, blk, dv), F32)],
        compiler_params=_params("arbitrary", "arbitrary", "arbitrary"),
        name="diff_attention",
    )(far, proj3, proj3, proj3, tiles, lq1, lk1, lq2, lk2, subln_g, beta_attn)


def _gelu(x):
    return 0.5 * x * (1.0 + lax.erf(x * (2.0 ** -0.5)))


def _gating_kernel(u_ref, g_ref, lg_ref, lb_ref, ws_ref, bs_ref, bg_ref, o_ref, buf_ref):
    hd = u_ref.shape[1] // GMLP_HEADS
    g = _gelu(g_ref[...].astype(F32))
    mu = jnp.mean(g, axis=-1, keepdims=True)
    gc = g - mu
    var = jnp.mean(jnp.square(gc), axis=-1, keepdims=True)
    v = (gc * lax.rsqrt(var + LN_EPS) * lg_ref[...] + lb_ref[...]).astype(BF16)
    row = lax.broadcasted_iota(jnp.int32, (CHUNK, CHUNK), 0)
    col = lax.broadcasted_iota(jnp.int32, (CHUNK, CHUNK), 1)
    tril = col <= row
    ss = jnp.zeros((CHUNK, 1), F32)
    for hh in range(GMLP_HEADS):
        w = jnp.where(tril, ws_ref[hh], 0.0).astype(BF16)
        mixed = jnp.dot(w, v[:, hh * hd:(hh + 1) * hd], preferred_element_type=F32) + bs_ref[hh]
        out = _gelu(u_ref[:, hh * hd:(hh + 1) * hd].astype(F32)) * mixed
        ss = ss + jnp.sum(jnp.square(out), axis=-1, keepdims=True)
        buf_ref[:, hh * hd:(hh + 1) * hd] = out
    rstd = lax.rsqrt(ss / u_ref.shape[1] + LN_EPS)
    o_ref[...] = (buf_ref[...] * rstd * bg_ref[...]).astype(o_ref.dtype)


def _spatial_gating(proj2, ln_g, ln_b, w_s, b_s, beta_gmlp, d_gmlp, u_col, g_col):
    n = proj2.shape[0]
    row_vec = pl.BlockSpec((1, d_gmlp), lambda c: (0, 0))
    return pl.pallas_call(
        _gating_kernel,
        grid=(n // CHUNK,),
        in_specs=[
            pl.BlockSpec((CHUNK, d_gmlp), lambda c: (c, u_col)),
            pl.BlockSpec((CHUNK, d_gmlp), lambda c: (c, g_col)),
            row_vec, row_vec,
            pl.BlockSpec((GMLP_HEADS, CHUNK, CHUNK), lambda c: (0, 0, 0)),
            pl.BlockSpec((GMLP_HEADS, CHUNK, 1), lambda c: (0, 0, 0)),
            row_vec,
        ],
        out_specs=pl.BlockSpec((CHUNK, d_gmlp), lambda c: (c, 0)),
        out_shape=jax.ShapeDtypeStruct((n, d_gmlp), BF16),
        scratch_shapes=[pltpu.VMEM((CHUNK, d_gmlp), F32)],
        compiler_params=_params("arbitrary"),
        name="spatial_gating",
    )(proj2, proj2, ln_g, ln_b, w_s, b_s[:, :, None], beta_gmlp)


def _layer_norm_chunks(buf_ref, n_chunks, d):
    tot = jnp.sum(buf_ref[0], axis=-1, keepdims=True)
    for c in range(1, n_chunks):
        tot = tot + jnp.sum(buf_ref[c], axis=-1, keepdims=True)
    mu = tot / d
    sq = jnp.sum(jnp.square(buf_ref[0] - mu), axis=-1, keepdims=True)
    for c in range(1, n_chunks):
        sq = sq + jnp.sum(jnp.square(buf_ref[c] - mu), axis=-1, keepdims=True)
    return mu, lax.rsqrt(sq / d + LN_EPS)


def _out_proj_kernel(a_ref, g_ref, w_ref, x_ref, lg_ref, lb_ref, wr_ref, br_ref,
                     x1_ref, x1b_ref, lo_ref, buf_ref, *, n_chunks, tn):
    j = pl.program_id(1)
    ka = a_ref.shape[1]
    mix = (jnp.dot(a_ref[...], w_ref[:ka, :], preferred_element_type=F32)
           + jnp.dot(g_ref[...], w_ref[ka:, :], preferred_element_type=F32))
    buf_ref[j] = DN_ALPHA * x_ref[...] + mix

    @pl.when(j == n_chunks - 1)
    def _():
        d = n_chunks * tn
        mu, rstd = _layer_norm_chunks(buf_ref, n_chunks, d)
        logits = jnp.zeros(lo_ref.shape, F32) + br_ref[...]
        for c in range(n_chunks):
            cs = slice(c * tn, (c + 1) * tn)
            y = (buf_ref[c] - mu) * rstd * lg_ref[:, cs] + lb_ref[:, cs]
            x1_ref[:, cs] = y
            x1b_ref[:, cs] = y.astype(BF16)
            logits = logits + jnp.dot(y, wr_ref[cs, :], preferred_element_type=F32,
                                      precision=lax.Precision.HIGHEST)
        lo_ref[...] = logits


def _out_proj_ln_router(attn2, gm2, wb, x2, ln_g, ln_b, w_route, b_route, tm=512, tn=256):
    m, d = x2.shape
    ka = attn2.shape[1]
    kg = gm2.shape[1]
    n_chunks = d // tn
    row_vec = pl.BlockSpec((1, d), lambda i, j: (0, 0))
    kernel = functools.partial(_out_proj_kernel, n_chunks=n_chunks, tn=tn)
    return pl.pallas_call(
        kernel,
        grid=(m // tm, n_chunks),
        in_specs=[
            pl.BlockSpec((tm, ka), lambda i, j: (i, 0)),
            pl.BlockSpec((tm, kg), lambda i, j: (i, 0)),
            pl.BlockSpec((ka + kg, tn), lambda i, j: (0, j)),
            pl.BlockSpec((tm, tn), lambda i, j: (i, j)),
            row_vec, row_vec,
            pl.BlockSpec((d, LANES), lambda i, j: (0, 0)),
            pl.BlockSpec((1, LANES), lambda i, j: (0, 0)),
        ],
        out_specs=[
            pl.BlockSpec((tm, d), lambda i, j: (i, 0)),
            pl.BlockSpec((tm, d), lambda i, j: (i, 0)),
            pl.BlockSpec((tm, LANES), lambda i, j: (i, 0)),
        ],
        out_shape=[jax.ShapeDtypeStruct((m, d), F32), jax.ShapeDtypeStruct((m, d), BF16),
                   jax.ShapeDtypeStruct((m, LANES), F32)],
        scratch_shapes=[pltpu.VMEM((n_chunks, tm, tn), F32)],
        compiler_params=_params("arbitrary", "arbitrary"),
        name="out_proj_ln_router",
    )(attn2, gm2, wb, x2, ln_g, ln_b, w_route, b_route)


def _route_kernel(lo_ref, eid_ref, wt_ref):
    lg = lo_ref[...]
    lane = lax.broadcasted_iota(jnp.int32, lg.shape, 1)
    lane_f = lane.astype(F32)
    none = float(LANES)
    first = lambda hit: jnp.min(jnp.where(hit, lane_f, none), axis=-1, keepdims=True)

    in_groups = lane < N_GROUPS
    g_logits = jnp.where(in_groups, lg, -jnp.inf)
    g_max = jnp.max(g_logits, axis=-1, keepdims=True)
    g_idx = first(g_logits == g_max)
    g_w = 1.0 / jnp.sum(jnp.where(in_groups, jnp.exp(lg - g_max), 0.0), axis=-1, keepdims=True)

    lo = N_GROUPS + g_idx * EXPERTS_PER_GROUP
    in_group = (lane_f >= lo) & (lane_f < lo + EXPERTS_PER_GROUP)
    e_logits = jnp.where(in_group, lg, -jnp.inf)
    t1 = jnp.max(e_logits, axis=-1, keepdims=True)
    i1 = first(e_logits == t1)
    e_rest = jnp.where(lane_f == i1, -jnp.inf, e_logits)
    t2 = jnp.max(e_rest, axis=-1, keepdims=True)
    i2 = first(e_rest == t2)
    ex = jnp.exp(t2 - t1)
    w1 = g_w * (1.0 / (1.0 + ex))
    w2 = g_w * (ex / (1.0 + ex))
    e1 = (i1 - N_GROUPS).astype(jnp.int32)
    e2 = (i2 - N_GROUPS).astype(jnp.int32)
    eid_ref[...] = jnp.where(lane == 0, e1, jnp.where(lane == 1, e2, 0))
    wt_ref[...] = jnp.where(lane == 0, w1, jnp.where(lane == 1, w2, 0.0))


def _route(logits, tm=1024):
    m = logits.shape[0]
    spec = pl.BlockSpec((tm, LANES), lambda i: (i, 0))
    return pl.pallas_call(
        _route_kernel,
        grid=(m // tm,),
        in_specs=[spec],
        out_specs=[spec, spec],
        out_shape=[jax.ShapeDtypeStruct((m, LANES), jnp.int32), jax.ShapeDtypeStruct((m, LANES), F32)],
        compiler_params=_params("arbitrary"),
        name="route",
    )(logits)


def _expert_kernel(be_ref, nreal_ref, x_ref, wg_ref, wu_ref, wd_ref, o_ref):
    b = pl.program_id(0)
    f = pl.program_id(1)

    @pl.when(b < nreal_ref[0])
    def _():
        x = x_ref[...]
        g = jnp.dot(x, wg_ref[...].astype(BF16), preferred_element_type=F32)
        u = jnp.dot(x, wu_ref[...].astype(BF16), preferred_element_type=F32)
        hidden = (g * (1.0 / (1.0 + jnp.exp(-g))) * u).astype(BF16)
        contrib = jnp.dot(hidden, wd_ref[...].astype(BF16), preferred_element_type=F32)

        @pl.when(f == 0)
        def _():
            o_ref[...] = contrib

        @pl.when(f > 0)
        def _():
            o_ref[...] += contrib


def _expert_mlp(xs, block_expert, n_real, w_gate, w_up, w_down, tm, fc=256):
    rows, d = xs.shape
    n_blocks = rows // tm
    d_ff = w_gate.shape[2]
    nf = d_ff // fc

    def live(b, f, nreal):
        is_real = b < nreal[0]
        return jnp.where(is_real, b, nreal[0] - 1), jnp.where(is_real, f, nf - 1)

    def x_map(b, f, be, nreal):
        bb, _ = live(b, f, nreal)
        return bb, 0

    def w_in_map(b, f, be, nreal):
        bb, ff = live(b, f, nreal)
        return be[bb], 0, ff

    def w_out_map(b, f, be, nreal):
        bb, ff = live(b, f, nreal)
        return be[bb], ff, 0

    grid_spec = pltpu.PrefetchScalarGridSpec(
        num_scalar_prefetch=2,
        grid=(n_blocks, nf),
        in_specs=[
            pl.BlockSpec((tm, d), x_map),
            pl.BlockSpec((None, d, fc), w_in_map),
            pl.BlockSpec((None, d, fc), w_in_map),
            pl.BlockSpec((None, fc, d), w_out_map),
        ],
        out_specs=pl.BlockSpec((tm, d), x_map),
    )
    return pl.pallas_call(
        _expert_kernel,
        grid_spec=grid_spec,
        out_shape=jax.ShapeDtypeStruct((rows, d), F32),
        compiler_params=_params("arbitrary", "arbitrary"),
        name="expert_mlp",
    )(block_expert, n_real, xs, w_gate, w_up, w_down)


def _combine_kernel(x1_ref, y0_ref, y1_ref, wt_ref, lg_ref, lb_ref, o_ref):
    wt = wt_ref[...]
    hsum = DN_ALPHA * x1_ref[...] + (wt[:, 0:1] * y0_ref[...] + wt[:, 1:2] * y1_ref[...])
    mu = jnp.mean(hsum, axis=-1, keepdims=True)
    hc = hsum - mu
    var = jnp.mean(jnp.square(hc), axis=-1, keepdims=True)
    o_ref[...] = hc * lax.rsqrt(var + LN_EPS) * lg_ref[...] + lb_ref[...]


def _combine_ln(x1, y0, y1, wts, ln_g, ln_b, tm=256):
    m, d = x1.shape
    big = pl.BlockSpec((tm, d), lambda i: (i, 0))
    row_vec = pl.BlockSpec((1, d), lambda i: (0, 0))
    return pl.pallas_call(
        _combine_kernel,
        grid=(m // tm,),
        in_specs=[big, big, big, pl.BlockSpec((tm, LANES), lambda i: (i, 0)), row_vec, row_vec],
        out_specs=big,
        out_shape=jax.ShapeDtypeStruct((m, d), F32),
        compiler_params=_params("arbitrary"),
        name="combine_ln",
    )(x1, y0, y1, wts, ln_g, ln_b)


def _plan_blocks(eid, tm):
    n = eid.shape[0]
    a = n * TOP_K
    expert_id = eid.reshape(-1)
    onehot = (expert_id[:, None] == jnp.arange(N_EXPERTS, dtype=jnp.int32)[None, :]).astype(jnp.int32)
    csum = jnp.cumsum(onehot, axis=0)
    counts = csum[-1]
    rank = jnp.take_along_axis(csum, expert_id[:, None], axis=1)[:, 0] - 1
    padded = (counts + tm - 1) // tm * tm
    pad_ends = jnp.cumsum(padded)
    pad_starts = pad_ends - padded
    n_blocks = a // tm + N_EXPERTS
    pos = pad_starts[expert_id] + rank
    token_id = jnp.arange(a, dtype=jnp.int32) // TOP_K
    tok_pad = jnp.full((n_blocks * tm,), n, jnp.int32).at[pos].set(token_id)
    block_expert = jnp.minimum(
        jnp.searchsorted(pad_ends, jnp.arange(n_blocks, dtype=jnp.int32) * tm, side='right'),
        N_EXPERTS - 1).astype(jnp.int32)
    n_real = (pad_ends[-1] // tm).astype(jnp.int32).reshape(1)
    return pos.reshape(n, TOP_K), tok_pad, block_expert, n_real


def kernel(x, w_in, w_out, lambda_q1, lambda_k1, lambda_q2, lambda_k2, subln_g, beta_attn, gmlp_ln_g, gmlp_ln_b,
           spatial_w, spatial_b, beta_gmlp, rel_bias, ln1_g, ln1_b, w_group, b_group, w_expert, b_expert,
           w_gate, w_up, w_down, ln2_g, ln2_b):
    b, s, d = x.shape
    n = b * s
    d_attn = ATTN_HEADS * ATTN_V_DIM
    d_gmlp = d - d_attn
    moe_tm = 256
    for l in range(DEPTH):
        lambda_init = 0.8 - 0.6 * math.exp(-0.3 * l)
        x2 = x.reshape(n, d)
        proj = _proj_matmul(x2.astype(BF16), w_in[l].astype(BF16))
        attn = _diff_attention(proj.reshape(b, s, -1), rel_bias, lambda_q1[l][None], lambda_k1[l][None],
                               lambda_q2[l][None], lambda_k2[l][None], subln_g[l][None], beta_attn[l][None],
                               lambda_init)
        gm = _spatial_gating(proj, gmlp_ln_g[l][None], gmlp_ln_b[l][None], spatial_w[l], spatial_b[l],
                             beta_gmlp[l][None], d_gmlp, u_col=3 * d_attn // d_gmlp, g_col=3 * d_attn // d_gmlp + 1)
        n_route = N_GROUPS + N_EXPERTS
        w_route = jnp.concatenate([w_group[l], w_expert[l].reshape(d, N_EXPERTS),
                                   jnp.zeros((d, LANES - n_route), F32)], axis=1)
        b_route = jnp.concatenate([b_group[l], b_expert[l].reshape(-1), jnp.zeros((LANES - n_route,), F32)])[None]
        x1, x1b, logits = _out_proj_ln_router(attn.reshape(n, d_attn), gm, w_out[l].astype(BF16), x2,
                                              ln1_g[l][None], ln1_b[l][None], w_route, b_route)
        eid, wts = _route(logits)
        pos, tok_pad, block_expert, n_real = _plan_blocks(eid[:, :TOP_K], moe_tm)
        xs = jnp.concatenate([x1b, jnp.zeros((1, d), BF16)], axis=0)[tok_pad]
        ys = _expert_mlp(xs, block_expert, n_real, w_gate[l], w_up[l], w_down[l], moe_tm)
        x = _combine_ln(x1, ys[pos[:, 0]], ys[pos[:, 1]], wts, ln2_g[l][None], ln2_b[l][None]).reshape(b, s, d)
    return x
```

```python
import functools
import math

import jax
import jax.numpy as jnp
from jax import lax
from jax.experimental import pallas as pl
from jax.experimental.pallas import tpu as pltpu

F32 = jnp.float32
BF16 = jnp.bfloat16

ATTN_HEADS = 8
ATTN_HEAD_DIM = 128
ATTN_V_DIM = 2 * ATTN_HEAD_DIM
GMLP_HEADS = 8
CHUNK = 128
REL_BUCKETS = 32
REL_MAX_DIST = 128
N_GROUPS = 8
EXPERTS_PER_GROUP = 8
N_EXPERTS = N_GROUPS * EXPERTS_PER_GROUP
TOP_K = 2
LN_EPS = 1e-5
DEPTH = 1
DN_ALPHA = (2 * DEPTH) ** 0.25
NEG_INF = -1e30
LANES = 128
VMEM_LIMIT = 56 * 1024 * 1024


def _params(*semantics):
    return pltpu.CompilerParams(dimension_semantics=semantics, vmem_limit_bytes=VMEM_LIMIT)


def _matmul_kernel(x_ref, w_ref, o_ref):
    o_ref[...] = jnp.dot(x_ref[...], w_ref[...], preferred_element_type=F32).astype(o_ref.dtype)


def _proj_matmul(xb, wb, tm=1024, tn=512):
    m, k = xb.shape
    n = wb.shape[1]
    return pl.pallas_call(
        _matmul_kernel,
        grid=(m // tm, n // tn),
        in_specs=[pl.BlockSpec((tm, k), lambda i, j: (i, 0)),
                  pl.BlockSpec((k, tn), lambda i, j: (0, j))],
        out_specs=pl.BlockSpec((tm, tn), lambda i, j: (i, j)),
        out_shape=jax.ShapeDtypeStruct((m, n), BF16),
        compiler_params=_params("arbitrary", "arbitrary"),
        name="proj_matmul",
    )(xb, wb)


def _rel_bucket(n):
    max_exact = REL_BUCKETS // 2
    nf = jnp.maximum(n, max_exact).astype(F32)
    large = max_exact + (jnp.log(nf / max_exact) / math.log(REL_MAX_DIST / max_exact)
                         * (REL_BUCKETS - max_exact)).astype(jnp.int32)
    large = jnp.minimum(large, REL_BUCKETS - 1)
    return jnp.where(n < max_exact, n, large)


def _rel_bias_tiles(rel_bias, blk):
    pos = jnp.arange(blk)
    tiles = []
    for d in (0, 1):
        n = jnp.maximum(pos[:, None] + d * blk - pos[None, :], 0)
        tiles.append(rel_bias[_rel_bucket(n)].transpose(2, 0, 1))
    return jnp.stack(tiles, axis=1).astype(F32)


def _attn_kernel(far_ref, q_ref, k_ref, v_ref, bias_ref, lq1_ref, lk1_ref, lq2_ref, lk2_ref,
                 sg_ref, ba_ref, o_ref, m_ref, l_ref, acc_ref, *, blk, lambda_init):
    h = pl.program_id(1)
    i = pl.program_id(2)
    dh = ATTN_HEAD_DIM
    scale = ATTN_HEAD_DIM ** -0.5
    q = q_ref[...]
    q1 = q[:, :dh]
    q2 = q[:, dh:]
    m_ref[...] = jnp.full(m_ref.shape, NEG_INF, F32)
    l_ref[...] = jnp.zeros(l_ref.shape, F32)
    acc_ref[...] = jnp.zeros(acc_ref.shape, F32)
    contract_last = (((1,), (1,)), ((), ()))

    def process(j, bias, mask):
        start = pl.multiple_of(j * blk, blk)
        kj = k_ref[pl.ds(start, blk), :]
        vj = v_ref[pl.ds(start, blk), :]
        ps = []
        alphas = []
        for mi, qm in enumerate((q1, q2)):
            s = lax.dot_general(qm, kj[:, mi * dh:(mi + 1) * dh], contract_last,
                                preferred_element_type=F32) * scale + bias
            if mask is not None:
                s = jnp.where(mask, s, NEG_INF)
            m_old = m_ref[mi]
            m_new = jnp.maximum(m_old, jnp.max(s, axis=-1, keepdims=True))
            a = jnp.exp(m_old - m_new)
            p = jnp.exp(s - m_new)
            l_ref[mi] = a * l_ref[mi] + jnp.sum(p, axis=-1, keepdims=True)
            m_ref[mi] = m_new
            ps.append(p.astype(BF16))
            alphas.append(a)
        pv = jnp.dot(jnp.concatenate(ps, axis=0), vj, preferred_element_type=F32)
        acc_ref[0] = alphas[0] * acc_ref[0] + pv[:blk]
        acc_ref[1] = alphas[1] * acc_ref[1] + pv[blk:]

    far = far_ref[h]

    def far_body(j, carry):
        process(j, far, None)
        return carry

    lax.fori_loop(0, jnp.maximum(i - 1, 0), far_body, 0)

    @pl.when(i >= 1)
    def _():
        process(i - 1, bias_ref[1], None)

    row = lax.broadcasted_iota(jnp.int32, (blk, blk), 0)
    col = lax.broadcasted_iota(jnp.int32, (blk, blk), 1)
    process(i, bias_ref[0], col <= row)

    lam = (jnp.exp(jnp.sum(lq1_ref[...] * lk1_ref[...], axis=-1, keepdims=True))
           - jnp.exp(jnp.sum(lq2_ref[...] * lk2_ref[...], axis=-1, keepdims=True)) + lambda_init)
    o = acc_ref[0] / l_ref[0] - lam * (acc_ref[1] / l_ref[1])
    o = o * lax.rsqrt(jnp.mean(jnp.square(o), axis=-1, keepdims=True) + LN_EPS) * sg_ref[...]
    o = o * (1.0 - lambda_init) * ba_ref[...]
    o_ref[...] = o.astype(o_ref.dtype)


def _diff_attention(proj3, rel_bias, lq1, lk1, lq2, lk2, subln_g, beta_attn, lambda_init, blk=256):
    b, s, _ = proj3.shape
    h = ATTN_HEADS
    dv = ATTN_V_DIM
    assert REL_BUCKETS // 2 + int(math.log((blk + 1) / (REL_BUCKETS // 2)) / math.log(REL_MAX_DIST / (REL_BUCKETS // 2))
                                  * (REL_BUCKETS // 2)) >= REL_BUCKETS
    tiles = _rel_bias_tiles(rel_bias, blk)
    far = rel_bias[REL_BUCKETS - 1].astype(F32)
    vec = lambda c: pl.BlockSpec((1, c), lambda bi, hi, qi: (0, 0))
    kernel = functools.partial(_attn_kernel, blk=blk, lambda_init=lambda_init)
    return pl.pallas_call(
        kernel,
        grid=(b, h, s // blk),
        in_specs=[
            pl.BlockSpec(memory_space=pltpu.SMEM),
            pl.BlockSpec((None, blk, dv), lambda bi, hi, qi: (bi, qi, hi)),
            pl.BlockSpec((None, s, dv), lambda bi, hi, qi: (bi, 0, h + hi)),
            pl.BlockSpec((None, s, dv), lambda bi, hi, qi: (bi, 0, 2 * h + hi)),
            pl.BlockSpec((None, 2, blk, blk), lambda bi, hi, qi: (hi, 0, 0, 0)),
            vec(ATTN_HEAD_DIM), vec(ATTN_HEAD_DIM), vec(ATTN_HEAD_DIM), vec(ATTN_HEAD_DIM),
            vec(dv),
            pl.BlockSpec((1, dv), lambda bi, hi, qi: (0, hi)),
        ],
        out_specs=pl.BlockSpec((None, blk, dv), lambda bi, hi, qi: (bi, qi, hi)),
        out_shape=jax.ShapeDtypeStruct((b, s, h * dv), BF16),
        scratch_shapes=[pltpu.VMEM((2, blk, 1), F32), pltpu.VMEM((2, blk, 1), F32),
                        pltpu.VMEM((2, blk, dv), F32)],
        compiler_params=_params("arbitrary", "arbitrary", "arbitrary"),
        name="diff_attention",
    )(far, proj3, proj3, proj3, tiles, lq1, lk1, lq2, lk2, subln_g, beta_attn)


def _gelu(x):
    return 0.5 * x * (1.0 + lax.erf(x * (2.0 ** -0.5)))


def _gating_kernel(u_ref, g_ref, lg_ref, lb_ref, ws_ref, bs_ref, bg_ref, o_ref, buf_ref):
    hd = u_ref.shape[1] // GMLP_HEADS
    g = _gelu(g_ref[...].astype(F32))
    mu = jnp.mean(g, axis=-1, keepdims=True)
    gc = g - mu
    var = jnp.mean(jnp.square(gc), axis=-1, keepdims=True)
    v = (gc * lax.rsqrt(var + LN_EPS) * lg_ref[...] + lb_ref[...]).astype(BF16)
    row = lax.broadcasted_iota(jnp.int32, (CHUNK, CHUNK), 0)
    col = lax.broadcasted_iota(jnp.int32, (CHUNK, CHUNK), 1)
    tril = col <= row
    ss = jnp.zeros((CHUNK, 1), F32)
    for hh in range(GMLP_HEADS):
        w = jnp.where(tril, ws_ref[hh], 0.0).astype(BF16)
        mixed = jnp.dot(w, v[:, hh * hd:(hh + 1) * hd], preferred_element_type=F32) + bs_ref[hh]
        out = _gelu(u_ref[:, hh * hd:(hh + 1) * hd].astype(F32)) * mixed
        ss = ss + jnp.sum(jnp.square(out), axis=-1, keepdims=True)
        buf_ref[:, hh * hd:(hh + 1) * hd] = out
    rstd = lax.rsqrt(ss / u_ref.shape[1] + LN_EPS)
    o_ref[...] = (buf_ref[...] * rstd * bg_ref[...]).astype(o_ref.dtype)


def _spatial_gating(proj2, ln_g, ln_b, w_s, b_s, beta_gmlp, d_gmlp, u_col, g_col):
    n = proj2.shape[0]
    row_vec = pl.BlockSpec((1, d_gmlp), lambda c: (0, 0))
    return pl.pallas_call(
        _gating_kernel,
        grid=(n // CHUNK,),
        in_specs=[
            pl.BlockSpec((CHUNK, d_gmlp), lambda c: (c, u_col)),
            pl.BlockSpec((CHUNK, d_gmlp), lambda c: (c, g_col)),
            row_vec, row_vec,
            pl.BlockSpec((GMLP_HEADS, CHUNK, CHUNK), lambda c: (0, 0, 0)),
            pl.BlockSpec((GMLP_HEADS, CHUNK, 1), lambda c: (0, 0, 0)),
            row_vec,
        ],
        out_specs=pl.BlockSpec((CHUNK, d_gmlp), lambda c: (c, 0)),
        out_shape=jax.ShapeDtypeStruct((n, d_gmlp), BF16),
        scratch_shapes=[pltpu.VMEM((CHUNK, d_gmlp), F32)],
        compiler_params=_params("arbitrary"),
        name="spatial_gating",
    )(proj2, proj2, ln_g, ln_b, w_s, b_s[:, :, None], beta_gmlp)


def _layer_norm_chunks(buf_ref, n_chunks, d):
    tot = jnp.sum(buf_ref[0], axis=-1, keepdims=True)
    for c in range(1, n_chunks):
        tot = tot + jnp.sum(buf_ref[c], axis=-1, keepdims=True)
    mu = tot / d
    sq = jnp.sum(jnp.square(buf_ref[0] - mu), axis=-1, keepdims=True)
    for c in range(1, n_chunks):
        sq = sq + jnp.sum(jnp.square(buf_ref[c] - mu), axis=-1, keepdims=True)
    return mu, lax.rsqrt(sq / d + LN_EPS)


def _out_proj_kernel(a_ref, g_ref, w_ref, x_ref, lg_ref, lb_ref, wr_ref, br_ref,
                     x1_ref, lo_ref, buf_ref, *, n_chunks, tn):
    j = pl.program_id(1)
    ka = a_ref.shape[1]
    mix = (jnp.dot(a_ref[...], w_ref[:ka, :], preferred_element_type=F32)
           + jnp.dot(g_ref[...], w_ref[ka:, :], preferred_element_type=F32))
    buf_ref[j] = DN_ALPHA * x_ref[...] + mix

    @pl.when(j == n_chunks - 1)
    def _():
        d = n_chunks * tn
        mu, rstd = _layer_norm_chunks(buf_ref, n_chunks, d)
        logits = jnp.zeros(lo_ref.shape, F32) + br_ref[...]
        for c in range(n_chunks):
            cs = slice(c * tn, (c + 1) * tn)
            y = (buf_ref[c] - mu) * rstd * lg_ref[:, cs] + lb_ref[:, cs]
            x1_ref[:, cs] = y
            logits = logits + jnp.dot(y, wr_ref[cs, :], preferred_element_type=F32,
                                      precision=lax.Precision.HIGHEST)
        lo_ref[...] = logits


def _out_proj_ln_router(attn2, gm2, wb, x2, ln_g, ln_b, w_route, b_route, tm=512, tn=512):
    m, d = x2.shape
    ka = attn2.shape[1]
    kg = gm2.shape[1]
    n_chunks = d // tn
    row_vec = pl.BlockSpec((1, d), lambda i, j: (0, 0))
    kernel = functools.partial(_out_proj_kernel, n_chunks=n_chunks, tn=tn)
    return pl.pallas_call(
        kernel,
        grid=(m // tm, n_chunks),
        in_specs=[
            pl.BlockSpec((tm, ka), lambda i, j: (i, 0)),
            pl.BlockSpec((tm, kg), lambda i, j: (i, 0)),
            pl.BlockSpec((ka + kg, tn), lambda i, j: (0, j)),
            pl.BlockSpec((tm, tn), lambda i, j: (i, j)),
            row_vec, row_vec,
            pl.BlockSpec((d, LANES), lambda i, j: (0, 0)),
            pl.BlockSpec((1, LANES), lambda i, j: (0, 0)),
        ],
        out_specs=[
            pl.BlockSpec((tm, d), lambda i, j: (i, 0)),
            pl.BlockSpec((tm, LANES), lambda i, j: (i, 0)),
        ],
        out_shape=[jax.ShapeDtypeStruct((m, d), F32), jax.ShapeDtypeStruct((m, LANES), F32)],
        scratch_shapes=[pltpu.VMEM((n_chunks, tm, tn), F32)],
        compiler_params=_params("arbitrary", "arbitrary"),
        name="out_proj_ln_router",
    )(attn2, gm2, wb, x2, ln_g, ln_b, w_route, b_route)


def _route_kernel(lo_ref, eid_ref, wt_ref):
    lg = lo_ref[...]
    lane = lax.broadcasted_iota(jnp.int32, lg.shape, 1)
    lane_f = lane.astype(F32)
    none = float(LANES)
    first = lambda hit: jnp.min(jnp.where(hit, lane_f, none), axis=-1, keepdims=True)

    in_groups = lane < N_GROUPS
    g_logits = jnp.where(in_groups, lg, -jnp.inf)
    g_max = jnp.max(g_logits, axis=-1, keepdims=True)
    g_idx = first(g_logits == g_max)
    g_w = 1.0 / jnp.sum(jnp.where(in_groups, jnp.exp(lg - g_max), 0.0), axis=-1, keepdims=True)

    lo = N_GROUPS + g_idx * EXPERTS_PER_GROUP
    in_group = (lane_f >= lo) & (lane_f < lo + EXPERTS_PER_GROUP)
    e_logits = jnp.where(in_group, lg, -jnp.inf)
    t1 = jnp.max(e_logits, axis=-1, keepdims=True)
    i1 = first(e_logits == t1)
    e_rest = jnp.where(lane_f == i1, -jnp.inf, e_logits)
    t2 = jnp.max(e_rest, axis=-1, keepdims=True)
    i2 = first(e_rest == t2)
    ex = jnp.exp(t2 - t1)
    w1 = g_w * (1.0 / (1.0 + ex))
    w2 = g_w * (ex / (1.0 + ex))
    e1 = (i1 - N_GROUPS).astype(jnp.int32)
    e2 = (i2 - N_GROUPS).astype(jnp.int32)
    eid_ref[...] = jnp.where(lane == 0, e1, jnp.where(lane == 1, e2, 0))
    wt_ref[...] = jnp.where(lane == 0, w1, jnp.where(lane == 1, w2, 0.0))


def _route(logits, tm=1024):
    m = logits.shape[0]
    spec = pl.BlockSpec((tm, LANES), lambda i: (i, 0))
    return pl.pallas_call(
        _route_kernel,
        grid=(m // tm,),
        in_specs=[spec],
        out_specs=[spec, spec],
        out_shape=[jax.ShapeDtypeStruct((m, LANES), jnp.int32), jax.ShapeDtypeStruct((m, LANES), F32)],
        compiler_params=_params("arbitrary"),
        name="route",
    )(logits)


def _expert_kernel(be_ref, nreal_ref, tok_ref, aid_ref, x_hbm, wg_ref, wu_ref, wd_ref, ys_hbm,
                   xbuf, xb, obuf, gsem, ssem, *, tm, nf):
    b = pl.program_id(0)
    f = pl.program_id(1)
    nreal = nreal_ref[0]
    slot = lax.rem(b, 2)

    def start_gather(slot_):
        def body(r, carry):
            pltpu.make_async_copy(x_hbm.at[pl.ds(tok_ref[0, r], 1)], xbuf.at[slot_, pl.ds(r, 1)],
                                  gsem.at[slot_]).start()
            return carry
        lax.fori_loop(0, tm, body, 0, unroll=8)

    def wait_gather(slot_):
        pltpu.make_async_copy(x_hbm.at[pl.ds(0, tm)], xbuf.at[slot_], gsem.at[slot_]).wait()

    def start_scatter(slot_):
        def body(r, carry):
            pltpu.make_async_copy(obuf.at[slot_, pl.ds(r, 1)], ys_hbm.at[pl.ds(aid_ref[0, r], 1)],
                                  ssem.at[slot_]).start()
            return carry
        lax.fori_loop(0, tm, body, 0, unroll=8)

    def wait_scatter(slot_):
        pltpu.make_async_copy(obuf.at[slot_], ys_hbm.at[pl.ds(0, tm)], ssem.at[slot_]).wait()

    @pl.when(b < nreal)
    def _():
        @pl.when(f == 0)
        def _():
            @pl.when(b == 0)
            def _():
                start_gather(0)
            wait_gather(slot)
            xb[...] = xbuf[slot].astype(BF16)

        x = xb[...]
        g = jnp.dot(x, wg_ref[...].astype(BF16), preferred_element_type=F32)
        u = jnp.dot(x, wu_ref[...].astype(BF16), preferred_element_type=F32)
        hidden = (g * (1.0 / (1.0 + jnp.exp(-g))) * u).astype(BF16)
        contrib = jnp.dot(hidden, wd_ref[...].astype(BF16), preferred_element_type=F32)

        @pl.when(f == 0)
        def _():
            obuf[slot] = contrib

        @pl.when(f > 0)
        def _():
            obuf[slot] += contrib

        @pl.when(f == nf - 1)
        def _():
            @pl.when(b >= 1)
            def _():
                wait_scatter(1 - slot)
            start_scatter(slot)

            @pl.when(b + 1 < nreal)
            def _():
                start_gather(1 - slot)

            @pl.when(b + 1 == nreal)
            def _():
                wait_scatter(slot)


def _expert_mlp(x1, plan, w_gate, w_up, w_down, tm, fc=256):
    tok_blocks, aid_blocks, block_expert, n_real = plan
    n, d = x1.shape
    n_blocks = tok_blocks.shape[0]
    d_ff = w_gate.shape[2]
    nf = d_ff // fc
    assert nf >= 2

    def live(b, f, nreal):
        is_real = b < nreal[0]
        return jnp.where(is_real, b, nreal[0] - 1), jnp.where(is_real, f, nf - 1)

    def tok_map(b, f, be, nreal):
        bb, ff = live(b, f, nreal)
        return jnp.minimum(bb + (ff == nf - 1).astype(jnp.int32), nreal[0] - 1), 0, 0

    def aid_map(b, f, be, nreal):
        bb, _ = live(b, f, nreal)
        return bb, 0, 0

    def w_in_map(b, f, be, nreal):
        bb, ff = live(b, f, nreal)
        return be[bb], 0, ff

    def w_out_map(b, f, be, nreal):
        bb, ff = live(b, f, nreal)
        return be[bb], ff, 0

    grid_spec = pltpu.PrefetchScalarGridSpec(
        num_scalar_prefetch=2,
        grid=(n_blocks, nf),
        in_specs=[
            pl.BlockSpec((None, 1, tm), tok_map, memory_space=pltpu.SMEM),
            pl.BlockSpec((None, 1, tm), aid_map, memory_space=pltpu.SMEM),
            pl.BlockSpec(memory_space=pl.ANY),
            pl.BlockSpec((None, d, fc), w_in_map),
            pl.BlockSpec((None, d, fc), w_in_map),
            pl.BlockSpec((None, fc, d), w_out_map),
        ],
        out_specs=pl.BlockSpec(memory_space=pl.ANY),
        scratch_shapes=[pltpu.VMEM((2, tm, d), F32), pltpu.VMEM((tm, d), BF16), pltpu.VMEM((2, tm, d), F32),
                        pltpu.SemaphoreType.DMA((2,)), pltpu.SemaphoreType.DMA((2,))],
    )
    kernel = functools.partial(_expert_kernel, tm=tm, nf=nf)
    return pl.pallas_call(
        kernel,
        grid_spec=grid_spec,
        out_shape=jax.ShapeDtypeStruct((n * TOP_K + 2 * tm, d), F32),
        compiler_params=_params("arbitrary", "arbitrary"),
        name="expert_mlp",
    )(block_expert, n_real, tok_blocks, aid_blocks, x1, w_gate, w_up, w_down)


def _combine_kernel(x1_ref, ys_ref, wt_ref, lg_ref, lb_ref, o_ref):
    d = x1_ref.shape[1]
    wt = wt_ref[...]
    hsum = DN_ALPHA * x1_ref[...] + (wt[:, 0:1] * ys_ref[:, :d] + wt[:, 1:2] * ys_ref[:, d:])
    mu = jnp.mean(hsum, axis=-1, keepdims=True)
    hc = hsum - mu
    var = jnp.mean(jnp.square(hc), axis=-1, keepdims=True)
    o_ref[...] = hc * lax.rsqrt(var + LN_EPS) * lg_ref[...] + lb_ref[...]


def _combine_ln(x1, ys2, wts, ln_g, ln_b, tm=256):
    m, d = x1.shape
    big = pl.BlockSpec((tm, d), lambda i: (i, 0))
    row_vec = pl.BlockSpec((1, d), lambda i: (0, 0))
    return pl.pallas_call(
        _combine_kernel,
        grid=(m // tm,),
        in_specs=[big, pl.BlockSpec((tm, TOP_K * d), lambda i: (i, 0)),
                  pl.BlockSpec((tm, LANES), lambda i: (i, 0)), row_vec, row_vec],
        out_specs=big,
        out_shape=jax.ShapeDtypeStruct((m, d), F32),
        compiler_params=_params("arbitrary"),
        name="combine_ln",
    )(x1, ys2, wts, ln_g, ln_b)


def _plan_blocks(eid, tm):
    n = eid.shape[0]
    a = n * TOP_K
    expert_id = eid.reshape(-1)
    onehot = (expert_id[:, None] == jnp.arange(N_EXPERTS, dtype=jnp.int32)[None, :]).astype(jnp.int32)
    csum = jnp.cumsum(onehot, axis=0)
    counts = csum[-1]
    rank = jnp.take_along_axis(csum, expert_id[:, None], axis=1)[:, 0] - 1
    padded = (counts + tm - 1) // tm * tm
    pad_ends = jnp.cumsum(padded)
    pad_starts = pad_ends - padded
    n_blocks = a // tm + N_EXPERTS
    pos = pad_starts[expert_id] + rank
    spare = a + jnp.arange(n_blocks * tm, dtype=jnp.int32) % (2 * tm)
    aid_pad = spare.at[pos].set(jnp.arange(a, dtype=jnp.int32))
    tok_pad = jnp.where(aid_pad < a, aid_pad // TOP_K, 0)
    block_start = jnp.arange(n_blocks, dtype=jnp.int32) * tm
    block_expert = jnp.minimum(jnp.searchsorted(pad_ends, block_start, side='right'), N_EXPERTS - 1).astype(jnp.int32)
    n_real = (pad_ends[-1] // tm).astype(jnp.int32).reshape(1)
    return tok_pad.reshape(n_blocks, 1, tm), aid_pad.reshape(n_blocks, 1, tm), block_expert, n_real


def kernel(x, w_in, w_out, lambda_q1, lambda_k1, lambda_q2, lambda_k2, subln_g, beta_attn, gmlp_ln_g, gmlp_ln_b,
           spatial_w, spatial_b, beta_gmlp, rel_bias, ln1_g, ln1_b, w_group, b_group, w_expert, b_expert,
           w_gate, w_up, w_down, ln2_g, ln2_b):
    b, s, d = x.shape
    n = b * s
    d_attn = ATTN_HEADS * ATTN_V_DIM
    d_gmlp = d - d_attn
    moe_tm = 256
    for l in range(DEPTH):
        lambda_init = 0.8 - 0.6 * math.exp(-0.3 * l)
        x2 = x.reshape(n, d)
        proj = _proj_matmul(x2.astype(BF16), w_in[l].astype(BF16))
        attn = _diff_attention(proj.reshape(b, s, -1), rel_bias, lambda_q1[l][None], lambda_k1[l][None],
                               lambda_q2[l][None], lambda_k2[l][None], subln_g[l][None], beta_attn[l][None],
                               lambda_init)
        gm = _spatial_gating(proj, gmlp_ln_g[l][None], gmlp_ln_b[l][None], spatial_w[l], spatial_b[l],
                             beta_gmlp[l][None], d_gmlp, u_col=3 * d_attn // d_gmlp, g_col=3 * d_attn // d_gmlp + 1)
        n_route = N_GROUPS + N_EXPERTS
        w_route = jnp.concatenate([w_group[l], w_expert[l].reshape(d, N_EXPERTS),
                                   jnp.zeros((d, LANES - n_route), F32)], axis=1)
        b_route = jnp.concatenate([b_group[l], b_expert[l].reshape(-1), jnp.zeros((LANES - n_route,), F32)])[None]
        x1, logits = _out_proj_ln_router(attn.reshape(n, d_attn), gm, w_out[l].astype(BF16), x2,
                                         ln1_g[l][None], ln1_b[l][None], w_route, b_route)
        eid, wts = _route(logits)
        plan = _plan_blocks(eid[:, :TOP_K], moe_tm)
        ys = _expert_mlp(x1, plan, w_gate[l], w_up[l], w_down[l], moe_tm)
        x = _combine_ln(x1, ys.reshape(-1, TOP_K * d), wts, ln2_g[l][None], ln2_b[l][None]).reshape(b, s, d)
    return x
```

```python
import functools
import math

import jax
import jax.numpy as jnp
from jax import lax
from jax.experimental import pallas as pl
from jax.experimental.pallas import tpu as pltpu

F32 = jnp.float32
BF16 = jnp.bfloat16

ATTN_HEADS = 8
ATTN_HEAD_DIM = 128
ATTN_V_DIM = 2 * ATTN_HEAD_DIM
GMLP_HEADS = 8
CHUNK = 128
REL_BUCKETS = 32
REL_MAX_DIST = 128
N_GROUPS = 8
EXPERTS_PER_GROUP = 8
N_EXPERTS = N_GROUPS * EXPERTS_PER_GROUP
TOP_K = 2
LN_EPS = 1e-5
DEPTH = 1
DN_ALPHA = (2 * DEPTH) ** 0.25
NEG_INF = -1e30
LANES = 128
VMEM_LIMIT = 56 * 1024 * 1024


def _params(*semantics):
    return pltpu.CompilerParams(dimension_semantics=semantics, vmem_limit_bytes=VMEM_LIMIT)


def _matmul_kernel(x_ref, w_ref, o_ref):
    o_ref[...] = jnp.dot(x_ref[...], w_ref[...], preferred_element_type=F32).astype(o_ref.dtype)


def _proj_matmul(xb, wb, tm=1024, tn=512):
    m, k = xb.shape
    n = wb.shape[1]
    return pl.pallas_call(
        _matmul_kernel,
        grid=(m // tm, n // tn),
        in_specs=[pl.BlockSpec((tm, k), lambda i, j: (i, 0)),
                  pl.BlockSpec((k, tn), lambda i, j: (0, j))],
        out_specs=pl.BlockSpec((tm, tn), lambda i, j: (i, j)),
        out_shape=jax.ShapeDtypeStruct((m, n), BF16),
        compiler_params=_params("arbitrary", "arbitrary"),
        name="proj_matmul",
    )(xb, wb)


def _rel_bucket(n):
    max_exact = REL_BUCKETS // 2
    nf = jnp.maximum(n, max_exact).astype(F32)
    large = max_exact + (jnp.log(nf / max_exact) / math.log(REL_MAX_DIST / max_exact)
                         * (REL_BUCKETS - max_exact)).astype(jnp.int32)
    large = jnp.minimum(large, REL_BUCKETS - 1)
    return jnp.where(n < max_exact, n, large)


def _rel_bias_tiles(rel_bias, blk):
    pos = jnp.arange(blk)
    tiles = []
    for d in (0, 1):
        bucket = _rel_bucket(jnp.maximum(pos[:, None] + d * blk - pos[None, :], 0))
        hit = bucket[None] == jnp.arange(REL_BUCKETS)[:, None, None]
        tiles.append(jnp.sum(jnp.where(hit[:, None], rel_bias[:, :, None, None], 0.0), axis=0))
    return jnp.stack(tiles, axis=1).astype(F32)


def _attn_kernel(far_ref, q_ref, k_ref, v_ref, bias_ref, lq1_ref, lk1_ref, lq2_ref, lk2_ref,
                 sg_ref, ba_ref, o_ref, mx_ref, sh_ref, l_ref, s_ref, acc_ref, *, blk, lambda_init):
    h = pl.program_id(1)
    i = pl.program_id(2)
    dh = ATTN_HEAD_DIM
    half = blk // 2
    scale = ATTN_HEAD_DIM ** -0.5
    q = q_ref[...]
    qs = (q[:, :dh], q[:, dh:])
    contract_last = (((1,), (1,)), ((), ()))
    far = far_ref[h]
    n_far = jnp.maximum(i - 1, 0)

    def raw_scores(j):
        start = pl.multiple_of(j * blk, blk)
        kj = k_ref[pl.ds(start, blk), :]
        return [lax.dot_general(qs[mi], kj[:, mi * dh:(mi + 1) * dh], contract_last, preferred_element_type=F32)
                for mi in range(2)]

    mx_ref[...] = jnp.full(mx_ref.shape, -jnp.inf, F32)

    def max_body(j, carry):
        s = raw_scores(j)
        for mi in range(2):
            mx_ref[mi] = jnp.maximum(mx_ref[mi], jnp.maximum(s[mi][:, :half], s[mi][:, half:]))
        return carry

    lax.fori_loop(0, n_far, max_body, 0)

    row = lax.broadcasted_iota(jnp.int32, (blk, blk), 0)
    col = lax.broadcasted_iota(jnp.int32, (blk, blk), 1)
    near = raw_scores(jnp.maximum(i - 1, 0))
    diag = raw_scores(i)
    m = []
    for mi in range(2):
        s_near = jnp.where(i >= 1, near[mi] * scale + bias_ref[1], NEG_INF)
        s_diag = jnp.where(col <= row, diag[mi] * scale + bias_ref[0], NEG_INF)
        s_ref[mi, 0] = s_near
        s_ref[mi, 1] = s_diag
        m_far = jnp.max(mx_ref[mi], axis=-1, keepdims=True) * scale + far
        m_mi = jnp.maximum(m_far, jnp.maximum(jnp.max(s_near, axis=-1, keepdims=True),
                                              jnp.max(s_diag, axis=-1, keepdims=True)))
        sh_ref[mi] = jnp.broadcast_to(far - m_mi, (blk, half))
        m.append(m_mi)

    l_ref[...] = jnp.zeros(l_ref.shape, F32)
    acc_ref[...] = jnp.zeros(acc_ref.shape, F32)

    def pv_body(j, carry):
        s = raw_scores(j)
        vj = v_ref[pl.ds(pl.multiple_of(j * blk, blk), blk), :]
        ps = []
        for mi in range(2):
            sh = sh_ref[mi]
            p_lo = jnp.exp(s[mi][:, :half] * scale + sh)
            p_hi = jnp.exp(s[mi][:, half:] * scale + sh)
            l_ref[mi] += p_lo + p_hi
            ps.append(jnp.concatenate([p_lo, p_hi], axis=1).astype(BF16))
        acc_ref[...] += jnp.dot(jnp.concatenate(ps, axis=0), vj, preferred_element_type=F32)
        return carry

    lax.fori_loop(0, n_far, pv_body, 0)

    l = [jnp.sum(l_ref[mi], axis=-1, keepdims=True) for mi in range(2)]
    for t, j in ((0, jnp.maximum(i - 1, 0)), (1, i)):
        vj = v_ref[pl.ds(pl.multiple_of(j * blk, blk), blk), :]
        ps = []
        for mi in range(2):
            p = jnp.exp(s_ref[mi, t] - m[mi])
            l[mi] = l[mi] + jnp.sum(p, axis=-1, keepdims=True)
            ps.append(p.astype(BF16))
        acc_ref[...] += jnp.dot(jnp.concatenate(ps, axis=0), vj, preferred_element_type=F32)

    lam = (jnp.exp(jnp.sum(lq1_ref[...] * lk1_ref[...], axis=-1, keepdims=True))
           - jnp.exp(jnp.sum(lq2_ref[...] * lk2_ref[...], axis=-1, keepdims=True)) + lambda_init)
    o = acc_ref[:blk] / l[0] - lam * (acc_ref[blk:] / l[1])
    o = o * lax.rsqrt(jnp.mean(jnp.square(o), axis=-1, keepdims=True) + LN_EPS) * sg_ref[...]
    o = o * (1.0 - lambda_init) * ba_ref[...]
    o_ref[...] = o.astype(o_ref.dtype)


def _diff_attention(proj3, rel_bias, lq1, lk1, lq2, lk2, subln_g, beta_attn, lambda_init, blk=256):
    b, s, _ = proj3.shape
    h = ATTN_HEADS
    dv = ATTN_V_DIM
    assert REL_BUCKETS // 2 + int(math.log((blk + 1) / (REL_BUCKETS // 2)) / math.log(REL_MAX_DIST / (REL_BUCKETS // 2))
                                  * (REL_BUCKETS // 2)) >= REL_BUCKETS
    tiles = _rel_bias_tiles(rel_bias, blk)
    far = rel_bias[REL_BUCKETS - 1].astype(F32)
    vec = lambda c: pl.BlockSpec((1, c), lambda bi, hi, qi: (0, 0))
    kernel = functools.partial(_attn_kernel, blk=blk, lambda_init=lambda_init)
    return pl.pallas_call(
        kernel,
        grid=(b, h, s // blk),
        in_specs=[
            pl.BlockSpec(memory_space=pltpu.SMEM),
            pl.BlockSpec((None, blk, dv), lambda bi, hi, qi: (bi, qi, hi)),
            pl.BlockSpec((None, s, dv), lambda bi, hi, qi: (bi, 0, h + hi)),
            pl.BlockSpec((None, s, dv), lambda bi, hi, qi: (bi, 0, 2 * h + hi)),
            pl.BlockSpec((None, 2, blk, blk), lambda bi, hi, qi: (hi, 0, 0, 0)),
            vec(ATTN_HEAD_DIM), vec(ATTN_HEAD_DIM), vec(ATTN_HEAD_DIM), vec(ATTN_HEAD_DIM),
            vec(dv),
            pl.BlockSpec((1, dv), lambda bi, hi, qi: (0, hi)),
        ],
        out_specs=pl.BlockSpec((None, blk, dv), lambda bi, hi, qi: (bi, qi, hi)),
        out_shape=jax.ShapeDtypeStruct((b, s, h * dv), BF16),
        scratch_shapes=[pltpu.VMEM((2, blk, blk // 2), F32), pltpu.VMEM((2, blk, blk // 2), F32),
                        pltpu.VMEM((2, blk, blk // 2), F32), pltpu.VMEM((2, 2, blk, blk), F32),
                        pltpu.VMEM((2 * blk, dv), F32)],
        compiler_params=_params("arbitrary", "arbitrary", "arbitrary"),
        name="diff_attention",
    )(far, proj3, proj3, proj3, tiles, lq1, lk1, lq2, lk2, subln_g, beta_attn)


def _gelu(x):
    return 0.5 * x * (1.0 + lax.erf(x * (2.0 ** -0.5)))


def _gating_kernel(u_ref, g_ref, lg_ref, lb_ref, ws_ref, bs_ref, bg_ref, o_ref, buf_ref):
    hd = u_ref.shape[1] // GMLP_HEADS
    g = _gelu(g_ref[...].astype(F32))
    mu = jnp.mean(g, axis=-1, keepdims=True)
    gc = g - mu
    var = jnp.mean(jnp.square(gc), axis=-1, keepdims=True)
    v = (gc * lax.rsqrt(var + LN_EPS) * lg_ref[...] + lb_ref[...]).astype(BF16)
    row = lax.broadcasted_iota(jnp.int32, (CHUNK, CHUNK), 0)
    col = lax.broadcasted_iota(jnp.int32, (CHUNK, CHUNK), 1)
    tril = col <= row
    ss = jnp.zeros((CHUNK, 1), F32)
    for hh in range(GMLP_HEADS):
        w = jnp.where(tril, ws_ref[hh], 0.0).astype(BF16)
        mixed = jnp.dot(w, v[:, hh * hd:(hh + 1) * hd], preferred_element_type=F32) + bs_ref[hh]
        out = _gelu(u_ref[:, hh * hd:(hh + 1) * hd].astype(F32)) * mixed
        ss = ss + jnp.sum(jnp.square(out), axis=-1, keepdims=True)
        buf_ref[:, hh * hd:(hh + 1) * hd] = out
    rstd = lax.rsqrt(ss / u_ref.shape[1] + LN_EPS)
    o_ref[...] = (buf_ref[...] * rstd * bg_ref[...]).astype(o_ref.dtype)


def _spatial_gating(proj2, ln_g, ln_b, w_s, b_s, beta_gmlp, d_gmlp, u_col, g_col):
    n = proj2.shape[0]
    row_vec = pl.BlockSpec((1, d_gmlp), lambda c: (0, 0))
    return pl.pallas_call(
        _gating_kernel,
        grid=(n // CHUNK,),
        in_specs=[
            pl.BlockSpec((CHUNK, d_gmlp), lambda c: (c, u_col)),
            pl.BlockSpec((CHUNK, d_gmlp), lambda c: (c, g_col)),
            row_vec, row_vec,
            pl.BlockSpec((GMLP_HEADS, CHUNK, CHUNK), lambda c: (0, 0, 0)),
            pl.BlockSpec((GMLP_HEADS, CHUNK, 1), lambda c: (0, 0, 0)),
            row_vec,
        ],
        out_specs=pl.BlockSpec((CHUNK, d_gmlp), lambda c: (c, 0)),
        out_shape=jax.ShapeDtypeStruct((n, d_gmlp), BF16),
        scratch_shapes=[pltpu.VMEM((CHUNK, d_gmlp), F32)],
        compiler_params=_params("arbitrary"),
        name="spatial_gating",
    )(proj2, proj2, ln_g, ln_b, w_s, b_s[:, :, None], beta_gmlp)


def _layer_norm_chunks(buf_ref, n_chunks, d):
    tot = jnp.sum(buf_ref[0], axis=-1, keepdims=True)
    for c in range(1, n_chunks):
        tot = tot + jnp.sum(buf_ref[c], axis=-1, keepdims=True)
    mu = tot / d
    sq = jnp.sum(jnp.square(buf_ref[0] - mu), axis=-1, keepdims=True)
    for c in range(1, n_chunks):
        sq = sq + jnp.sum(jnp.square(buf_ref[c] - mu), axis=-1, keepdims=True)
    return mu, lax.rsqrt(sq / d + LN_EPS)


def _out_proj_kernel(a_ref, g_ref, w_ref, x_ref, lg_ref, lb_ref, wr_ref, br_ref,
                     x1_ref, lo_ref, buf_ref, *, n_chunks, tn):
    j = pl.program_id(1)
    ka = a_ref.shape[1]
    mix = (jnp.dot(a_ref[...], w_ref[:ka, :], preferred_element_type=F32)
           + jnp.dot(g_ref[...], w_ref[ka:, :], preferred_element_type=F32))
    buf_ref[j] = DN_ALPHA * x_ref[...] + mix

    @pl.when(j == n_chunks - 1)
    def _():
        d = n_chunks * tn
        mu, rstd = _layer_norm_chunks(buf_ref, n_chunks, d)
        logits = jnp.zeros(lo_ref.shape, F32) + br_ref[...]
        for c in range(n_chunks):
            cs = slice(c * tn, (c + 1) * tn)
            y = (buf_ref[c] - mu) * rstd * lg_ref[:, cs] + lb_ref[:, cs]
            x1_ref[:, cs] = y
            logits = logits + jnp.dot(y, wr_ref[cs, :], preferred_element_type=F32,
                                      precision=lax.Precision.HIGHEST)
        lo_ref[...] = logits


def _out_proj_ln_router(attn2, gm2, wb, x2, ln_g, ln_b, w_route, b_route, tm=512, tn=512):
    m, d = x2.shape
    ka = attn2.shape[1]
    kg = gm2.shape[1]
    n_chunks = d // tn
    row_vec = pl.BlockSpec((1, d), lambda i, j: (0, 0))
    kernel = functools.partial(_out_proj_kernel, n_chunks=n_chunks, tn=tn)
    return pl.pallas_call(
        kernel,
        grid=(m // tm, n_chunks),
        in_specs=[
            pl.BlockSpec((tm, ka), lambda i, j: (i, 0)),
            pl.BlockSpec((tm, kg), lambda i, j: (i, 0)),
            pl.BlockSpec((ka + kg, tn), lambda i, j: (0, j)),
            pl.BlockSpec((tm, tn), lambda i, j: (i, j)),
            row_vec, row_vec,
            pl.BlockSpec((d, LANES), lambda i, j: (0, 0)),
            pl.BlockSpec((1, LANES), lambda i, j: (0, 0)),
        ],
        out_specs=[
            pl.BlockSpec((tm, d), lambda i, j: (i, 0)),
            pl.BlockSpec((tm, LANES), lambda i, j: (i, 0)),
        ],
        out_shape=[jax.ShapeDtypeStruct((m, d), F32), jax.ShapeDtypeStruct((m, LANES), F32)],
        scratch_shapes=[pltpu.VMEM((n_chunks, tm, tn), F32)],
        compiler_params=_params("arbitrary", "arbitrary"),
        name="out_proj_ln_router",
    )(attn2, gm2, wb, x2, ln_g, ln_b, w_route, b_route)


def _route_kernel(lo_ref, eid_ref, wt_ref):
    lg = lo_ref[...]
    lane = lax.broadcasted_iota(jnp.int32, lg.shape, 1)
    lane_f = lane.astype(F32)
    none = float(LANES)
    first = lambda hit: jnp.min(jnp.where(hit, lane_f, none), axis=-1, keepdims=True)

    in_groups = lane < N_GROUPS
    g_logits = jnp.where(in_groups, lg, -jnp.inf)
    g_max = jnp.max(g_logits, axis=-1, keepdims=True)
    g_idx = first(g_logits == g_max)
    g_w = 1.0 / jnp.sum(jnp.where(in_groups, jnp.exp(lg - g_max), 0.0), axis=-1, keepdims=True)

    lo = N_GROUPS + g_idx * EXPERTS_PER_GROUP
    in_group = (lane_f >= lo) & (lane_f < lo + EXPERTS_PER_GROUP)
    e_logits = jnp.where(in_group, lg, -jnp.inf)
    t1 = jnp.max(e_logits, axis=-1, keepdims=True)
    i1 = first(e_logits == t1)
    e_rest = jnp.where(lane_f == i1, -jnp.inf, e_logits)
    t2 = jnp.max(e_rest, axis=-1, keepdims=True)
    i2 = first(e_rest == t2)
    ex = jnp.exp(t2 - t1)
    w1 = g_w * (1.0 / (1.0 + ex))
    w2 = g_w * (ex / (1.0 + ex))
    e1 = (i1 - N_GROUPS).astype(jnp.int32)
    e2 = (i2 - N_GROUPS).astype(jnp.int32)
    eid_ref[...] = jnp.where(lane == 0, e1, jnp.where(lane == 1, e2, 0))
    wt_ref[...] = jnp.where(lane == 0, w1, jnp.where(lane == 1, w2, 0.0))


def _route(logits, tm=1024):
    m = logits.shape[0]
    spec = pl.BlockSpec((tm, LANES), lambda i: (i, 0))
    return pl.pallas_call(
        _route_kernel,
        grid=(m // tm,),
        in_specs=[spec],
        out_specs=[spec, spec],
        out_shape=[jax.ShapeDtypeStruct((m, LANES), jnp.int32), jax.ShapeDtypeStruct((m, LANES), F32)],
        compiler_params=_params("arbitrary"),
        name="route",
    )(logits)


def _expert_kernel(be_ref, nreal_ref, tok_ref, aid_ref, x_hbm, wg_ref, wu_ref, wd_ref, ys_hbm,
                   xbuf, xb, obuf, gsem, ssem, *, tm, nf):
    b = pl.program_id(0)
    f = pl.program_id(1)
    nreal = nreal_ref[0]
    slot = lax.rem(b, 2)

    def start_gather(slot_):
        def body(r, carry):
            pltpu.make_async_copy(x_hbm.at[pl.ds(tok_ref[0, r], 1)], xbuf.at[slot_, pl.ds(r, 1)],
                                  gsem.at[slot_]).start()
            return carry
        lax.fori_loop(0, tm, body, 0, unroll=8)

    def wait_gather(slot_):
        pltpu.make_async_copy(x_hbm.at[pl.ds(0, tm)], xbuf.at[slot_], gsem.at[slot_]).wait()

    def start_scatter(slot_):
        def body(r, carry):
            pltpu.make_async_copy(obuf.at[slot_, pl.ds(r, 1)], ys_hbm.at[pl.ds(aid_ref[0, r], 1)],
                                  ssem.at[slot_]).start()
            return carry
        lax.fori_loop(0, tm, body, 0, unroll=8)

    def wait_scatter(slot_):
        pltpu.make_async_copy(obuf.at[slot_], ys_hbm.at[pl.ds(0, tm)], ssem.at[slot_]).wait()

    @pl.when(b < nreal)
    def _():
        @pl.when(f == 0)
        def _():
            @pl.when(b == 0)
            def _():
                start_gather(0)
            wait_gather(slot)
            xb[...] = xbuf[slot].astype(BF16)

        x = xb[...]
        g = jnp.dot(x, wg_ref[...].astype(BF16), preferred_element_type=F32)
        u = jnp.dot(x, wu_ref[...].astype(BF16), preferred_element_type=F32)
        hidden = (g * (1.0 / (1.0 + jnp.exp(-g))) * u).astype(BF16)
        contrib = jnp.dot(hidden, wd_ref[...].astype(BF16), preferred_element_type=F32)

        @pl.when(f == 0)
        def _():
            obuf[slot] = contrib

        @pl.when(f > 0)
        def _():
            obuf[slot] += contrib

        @pl.when(f == nf - 1)
        def _():
            @pl.when(b >= 1)
            def _():
                wait_scatter(1 - slot)
            start_scatter(slot)

            @pl.when(b + 1 < nreal)
            def _():
                start_gather(1 - slot)

            @pl.when(b + 1 == nreal)
            def _():
                wait_scatter(slot)


def _expert_mlp(x1, plan, w_gate, w_up, w_down, tm, fc=256):
    tok_blocks, aid_blocks, block_expert, n_real = plan
    n, d = x1.shape
    n_blocks = tok_blocks.shape[0]
    d_ff = w_gate.shape[2]
    nf = d_ff // fc
    assert nf >= 2

    def live(b, f, nreal):
        is_real = b < nreal[0]
        return jnp.where(is_real, b, nreal[0] - 1), jnp.where(is_real, f, nf - 1)

    def tok_map(b, f, be, nreal):
        bb, ff = live(b, f, nreal)
        return jnp.minimum(bb + (ff == nf - 1).astype(jnp.int32), nreal[0] - 1), 0, 0

    def aid_map(b, f, be, nreal):
        bb, _ = live(b, f, nreal)
        return bb, 0, 0

    def w_in_map(b, f, be, nreal):
        bb, ff = live(b, f, nreal)
        return be[bb], 0, ff

    def w_out_map(b, f, be, nreal):
        bb, ff = live(b, f, nreal)
        return be[bb], ff, 0

    grid_spec = pltpu.PrefetchScalarGridSpec(
        num_scalar_prefetch=2,
        grid=(n_blocks, nf),
        in_specs=[
            pl.BlockSpec((None, 1, tm), tok_map, memory_space=pltpu.SMEM),
            pl.BlockSpec((None, 1, tm), aid_map, memory_space=pltpu.SMEM),
            pl.BlockSpec(memory_space=pl.ANY),
            pl.BlockSpec((None, d, fc), w_in_map),
            pl.BlockSpec((None, d, fc), w_in_map),
            pl.BlockSpec((None, fc, d), w_out_map),
        ],
        out_specs=pl.BlockSpec(memory_space=pl.ANY),
        scratch_shapes=[pltpu.VMEM((2, tm, d), F32), pltpu.VMEM((tm, d), BF16), pltpu.VMEM((2, tm, d), F32),
                        pltpu.SemaphoreType.DMA((2,)), pltpu.SemaphoreType.DMA((2,))],
    )
    kernel = functools.partial(_expert_kernel, tm=tm, nf=nf)
    return pl.pallas_call(
        kernel,
        grid_spec=grid_spec,
        out_shape=jax.ShapeDtypeStruct((n * TOP_K + 2 * tm, d), F32),
        compiler_params=_params("arbitrary", "arbitrary"),
        name="expert_mlp",
    )(block_expert, n_real, tok_blocks, aid_blocks, x1, w_gate, w_up, w_down)


def _combine_kernel(x1_ref, y0_ref, y1_ref, wt_ref, lg_ref, lb_ref, o_ref):
    wt = wt_ref[...]
    hsum = DN_ALPHA * x1_ref[...] + (wt[:, 0:1] * y0_ref[...] + wt[:, 1:2] * y1_ref[...])
    mu = jnp.mean(hsum, axis=-1, keepdims=True)
    hc = hsum - mu
    var = jnp.mean(jnp.square(hc), axis=-1, keepdims=True)
    o_ref[...] = hc * lax.rsqrt(var + LN_EPS) * lg_ref[...] + lb_ref[...]


def _combine_ln(x1, ys, wts, ln_g, ln_b, tm=256):
    m, d = x1.shape
    big = pl.BlockSpec((tm, d), lambda i: (i, 0))
    second = pl.BlockSpec((tm, d), lambda i: (m // tm + i, 0))
    row_vec = pl.BlockSpec((1, d), lambda i: (0, 0))
    return pl.pallas_call(
        _combine_kernel,
        grid=(m // tm,),
        in_specs=[big, big, second, pl.BlockSpec((tm, LANES), lambda i: (i, 0)), row_vec, row_vec],
        out_specs=big,
        out_shape=jax.ShapeDtypeStruct((m, d), F32),
        compiler_params=_params("arbitrary"),
        name="combine_ln",
    )(x1, ys, ys, wts, ln_g, ln_b)


def _plan_blocks(eid, tm):
    n = eid.shape[0]
    a = n * TOP_K
    expert_id = eid.reshape(-1)
    onehot = (expert_id[:, None] == jnp.arange(N_EXPERTS, dtype=jnp.int32)[None, :]).astype(jnp.int32)
    csum = jnp.cumsum(onehot, axis=0)
    counts = csum[-1]
    rank = jnp.take_along_axis(csum, expert_id[:, None], axis=1)[:, 0] - 1
    padded = (counts + tm - 1) // tm * tm
    pad_ends = jnp.cumsum(padded)
    pad_starts = pad_ends - padded
    n_blocks = a // tm + N_EXPERTS
    pos = pad_starts[expert_id] + rank
    spare = a + jnp.arange(n_blocks * tm, dtype=jnp.int32) % (2 * tm)
    flat = jnp.arange(a, dtype=jnp.int32)
    aid_pad = spare.at[pos].set((flat % TOP_K) * n + flat // TOP_K)
    tok_pad = jnp.where(aid_pad < a, aid_pad % n, 0)
    block_start = jnp.arange(n_blocks, dtype=jnp.int32) * tm
    block_expert = jnp.minimum(jnp.searchsorted(pad_ends, block_start, side='right'), N_EXPERTS - 1).astype(jnp.int32)
    n_real = (pad_ends[-1] // tm).astype(jnp.int32).reshape(1)
    return tok_pad.reshape(n_blocks, 1, tm), aid_pad.reshape(n_blocks, 1, tm), block_expert, n_real


def kernel(x, w_in, w_out, lambda_q1, lambda_k1, lambda_q2, lambda_k2, subln_g, beta_attn, gmlp_ln_g, gmlp_ln_b,
           spatial_w, spatial_b, beta_gmlp, rel_bias, ln1_g, ln1_b, w_group, b_group, w_expert, b_expert,
           w_gate, w_up, w_down, ln2_g, ln2_b):
    b, s, d = x.shape
    n = b * s
    d_attn = ATTN_HEADS * ATTN_V_DIM
    d_gmlp = d - d_attn
    moe_tm = 256
    for l in range(DEPTH):
        lambda_init = 0.8 - 0.6 * math.exp(-0.3 * l)
        x2 = x.reshape(n, d)
        proj = _proj_matmul(x2.astype(BF16), w_in[l].astype(BF16))
        attn = _diff_attention(proj.reshape(b, s, -1), rel_bias, lambda_q1[l][None], lambda_k1[l][None],
                               lambda_q2[l][None], lambda_k2[l][None], subln_g[l][None], beta_attn[l][None],
                               lambda_init)
        gm = _spatial_gating(proj, gmlp_ln_g[l][None], gmlp_ln_b[l][None], spatial_w[l], spatial_b[l],
                             beta_gmlp[l][None], d_gmlp, u_col=3 * d_attn // d_gmlp, g_col=3 * d_attn // d_gmlp + 1)
        n_route = N_GROUPS + N_EXPERTS
        w_route = jnp.concatenate([w_group[l], w_expert[l].reshape(d, N_EXPERTS),
                                   jnp.zeros((d, LANES - n_route), F32)], axis=1)
        b_route = jnp.concatenate([b_group[l], b_expert[l].reshape(-1), jnp.zeros((LANES - n_route,), F32)])[None]
        x1, logits = _out_proj_ln_router(attn.reshape(n, d_attn), gm, w_out[l].astype(BF16), x2,
                                         ln1_g[l][None], ln1_b[l][None], w_route, b_route)
        eid, wts = _route(logits)
        plan = _plan_blocks(eid[:, :TOP_K], moe_tm)
        ys = _expert_mlp(x1, plan, w_gate[l], w_up[l], w_down[l], moe_tm)
        x = _combine_ln(x1, ys, wts, ln2_g[l][None], ln2_b[l][None]).reshape(b, s, d)
    return x
```

```python
import functools
import math

import jax
import jax.numpy as jnp
from jax import lax
from jax.experimental import pallas as pl
from jax.experimental.pallas import tpu as pltpu

F32 = jnp.float32
BF16 = jnp.bfloat16

ATTN_HEADS = 8
ATTN_HEAD_DIM = 128
ATTN_V_DIM = 2 * ATTN_HEAD_DIM
GMLP_HEADS = 8
CHUNK = 128
REL_BUCKETS = 32
REL_MAX_DIST = 128
N_GROUPS = 8
EXPERTS_PER_GROUP = 8
N_EXPERTS = N_GROUPS * EXPERTS_PER_GROUP
TOP_K = 2
LN_EPS = 1e-5
DEPTH = 1
DN_ALPHA = (2 * DEPTH) ** 0.25
NEG_INF = -1e30
LANES = 128
VMEM_LIMIT = 56 * 1024 * 1024


def _params(*semantics):
    return pltpu.CompilerParams(dimension_semantics=semantics, vmem_limit_bytes=VMEM_LIMIT)


def _matmul_kernel(x_ref, w_ref, o_ref):
    o_ref[...] = jnp.dot(x_ref[...], w_ref[...], preferred_element_type=F32).astype(o_ref.dtype)


def _proj_matmul(xb, wb, tm=1024, tn=512):
    m, k = xb.shape
    n = wb.shape[1]
    return pl.pallas_call(
        _matmul_kernel,
        grid=(m // tm, n // tn),
        in_specs=[pl.BlockSpec((tm, k), lambda i, j: (i, 0)),
                  pl.BlockSpec((k, tn), lambda i, j: (0, j))],
        out_specs=pl.BlockSpec((tm, tn), lambda i, j: (i, j)),
        out_shape=jax.ShapeDtypeStruct((m, n), BF16),
        compiler_params=_params("arbitrary", "arbitrary"),
        name="proj_matmul",
    )(xb, wb)


def _rel_bucket(n):
    max_exact = REL_BUCKETS // 2
    nf = jnp.maximum(n, max_exact).astype(F32)
    large = max_exact + (jnp.log(nf / max_exact) / math.log(REL_MAX_DIST / max_exact)
                         * (REL_BUCKETS - max_exact)).astype(jnp.int32)
    large = jnp.minimum(large, REL_BUCKETS - 1)
    return jnp.where(n < max_exact, n, large)


def _rel_bias_tiles(rel_bias, blk):
    pos = jnp.arange(blk)
    tiles = []
    for d in (0, 1):
        bucket = _rel_bucket(jnp.maximum(pos[:, None] + d * blk - pos[None, :], 0))
        hit = bucket[None] == jnp.arange(REL_BUCKETS)[:, None, None]
        tiles.append(jnp.sum(jnp.where(hit[:, None], rel_bias[:, :, None, None], 0.0), axis=0))
    return jnp.stack(tiles, axis=1).astype(F32)


def _attn_kernel(far_ref, q_ref, k_ref, v_ref, bias_ref, lq1_ref, lk1_ref, lq2_ref, lk2_ref,
                 sg_ref, ba_ref, o_ref, mx_ref, sh_ref, l_ref, s_ref, acc_ref, *, blk, lambda_init):
    h = pl.program_id(1)
    i = pl.program_id(2)
    dh = ATTN_HEAD_DIM
    half = blk // 2
    scale = ATTN_HEAD_DIM ** -0.5
    q = q_ref[...]
    qs = (q[:, :dh], q[:, dh:])
    contract_last = (((1,), (1,)), ((), ()))
    far = far_ref[h]
    n_far = jnp.maximum(i - 1, 0)

    def raw_scores(j):
        start = pl.multiple_of(j * blk, blk)
        kj = k_ref[pl.ds(start, blk), :]
        return [lax.dot_general(qs[mi], kj[:, mi * dh:(mi + 1) * dh], contract_last, preferred_element_type=F32)
                for mi in range(2)]

    mx_ref[...] = jnp.full(mx_ref.shape, -jnp.inf, F32)

    def max_body(j, carry):
        s = raw_scores(j)
        for mi in range(2):
            mx_ref[mi] = jnp.maximum(mx_ref[mi], jnp.maximum(s[mi][:, :half], s[mi][:, half:]))
        return carry

    lax.fori_loop(0, n_far, max_body, 0)

    row = lax.broadcasted_iota(jnp.int32, (blk, blk), 0)
    col = lax.broadcasted_iota(jnp.int32, (blk, blk), 1)
    near = raw_scores(jnp.maximum(i - 1, 0))
    diag = raw_scores(i)
    m = []
    for mi in range(2):
        s_near = jnp.where(i >= 1, near[mi] * scale + bias_ref[1], NEG_INF)
        s_diag = jnp.where(col <= row, diag[mi] * scale + bias_ref[0], NEG_INF)
        s_ref[mi, 0] = s_near
        s_ref[mi, 1] = s_diag
        m_far = jnp.max(mx_ref[mi], axis=-1, keepdims=True) * scale + far
        m_mi = jnp.maximum(m_far, jnp.maximum(jnp.max(s_near, axis=-1, keepdims=True),
                                              jnp.max(s_diag, axis=-1, keepdims=True)))
        sh_ref[mi] = jnp.broadcast_to(far - m_mi, (blk, half))
        m.append(m_mi)

    l_ref[...] = jnp.zeros(l_ref.shape, F32)
    acc_ref[...] = jnp.zeros(acc_ref.shape, F32)

    def pv_body(j, carry):
        s = raw_scores(j)
        vj = v_ref[pl.ds(pl.multiple_of(j * blk, blk), blk), :]
        ps = []
        for mi in range(2):
            sh = sh_ref[mi]
            p_lo = jnp.exp(s[mi][:, :half] * scale + sh)
            p_hi = jnp.exp(s[mi][:, half:] * scale + sh)
            l_ref[mi] += p_lo + p_hi
            ps.append(jnp.concatenate([p_lo, p_hi], axis=1).astype(BF16))
        acc_ref[...] += jnp.dot(jnp.concatenate(ps, axis=0), vj, preferred_element_type=F32)
        return carry

    lax.fori_loop(0, n_far, pv_body, 0)

    l = [jnp.sum(l_ref[mi], axis=-1, keepdims=True) for mi in range(2)]
    for t, j in ((0, jnp.maximum(i - 1, 0)), (1, i)):
        vj = v_ref[pl.ds(pl.multiple_of(j * blk, blk), blk), :]
        ps = []
        for mi in range(2):
            p = jnp.exp(s_ref[mi, t] - m[mi])
            l[mi] = l[mi] + jnp.sum(p, axis=-1, keepdims=True)
            ps.append(p.astype(BF16))
        acc_ref[...] += jnp.dot(jnp.concatenate(ps, axis=0), vj, preferred_element_type=F32)

    lam = (jnp.exp(jnp.sum(lq1_ref[...] * lk1_ref[...], axis=-1, keepdims=True))
           - jnp.exp(jnp.sum(lq2_ref[...] * lk2_ref[...], axis=-1, keepdims=True)) + lambda_init)
    o = acc_ref[:blk] / l[0] - lam * (acc_ref[blk:] / l[1])
    o = o * lax.rsqrt(jnp.mean(jnp.square(o), axis=-1, keepdims=True) + LN_EPS) * sg_ref[...]
    o = o * (1.0 - lambda_init) * ba_ref[...]
    o_ref[...] = o.astype(o_ref.dtype)


def _diff_attention(proj3, rel_bias, lq1, lk1, lq2, lk2, subln_g, beta_attn, lambda_init, blk=256):
    b, s, _ = proj3.shape
    h = ATTN_HEADS
    dv = ATTN_V_DIM
    assert REL_BUCKETS // 2 + int(math.log((blk + 1) / (REL_BUCKETS // 2)) / math.log(REL_MAX_DIST / (REL_BUCKETS // 2))
                                  * (REL_BUCKETS // 2)) >= REL_BUCKETS
    tiles = _rel_bias_tiles(rel_bias, blk)
    far = rel_bias[REL_BUCKETS - 1].astype(F32)
    vec = lambda c: pl.BlockSpec((1, c), lambda bi, hi, qi: (0, 0))
    kernel = functools.partial(_attn_kernel, blk=blk, lambda_init=lambda_init)
    return pl.pallas_call(
        kernel,
        grid=(b, h, s // blk),
        in_specs=[
            pl.BlockSpec(memory_space=pltpu.SMEM),
            pl.BlockSpec((None, blk, dv), lambda bi, hi, qi: (bi, qi, hi)),
            pl.BlockSpec((None, s, dv), lambda bi, hi, qi: (bi, 0, h + hi)),
            pl.BlockSpec((None, s, dv), lambda bi, hi, qi: (bi, 0, 2 * h + hi)),
            pl.BlockSpec((None, 2, blk, blk), lambda bi, hi, qi: (hi, 0, 0, 0)),
            vec(ATTN_HEAD_DIM), vec(ATTN_HEAD_DIM), vec(ATTN_HEAD_DIM), vec(ATTN_HEAD_DIM),
            vec(dv),
            pl.BlockSpec((1, dv), lambda bi, hi, qi: (0, hi)),
        ],
        out_specs=pl.BlockSpec((None, blk, dv), lambda bi, hi, qi: (bi, qi, hi)),
        out_shape=jax.ShapeDtypeStruct((b, s, h * dv), BF16),
        scratch_shapes=[pltpu.VMEM((2, blk, blk // 2), F32), pltpu.VMEM((2, blk, blk // 2), F32),
                        pltpu.VMEM((2, blk, blk // 2), F32), pltpu.VMEM((2, 2, blk, blk), F32),
                        pltpu.VMEM((2 * blk, dv), F32)],
        compiler_params=_params("arbitrary", "arbitrary", "arbitrary"),
        name="diff_attention",
    )(far, proj3, proj3, proj3, tiles, lq1, lk1, lq2, lk2, subln_g, beta_attn)


def _gelu(x):
    return 0.5 * x * (1.0 + lax.erf(x * (2.0 ** -0.5)))


def _gating_kernel(u_ref, g_ref, lg_ref, lb_ref, ws_ref, bs_ref, bg_ref, o_ref, buf_ref):
    hd = u_ref.shape[1] // GMLP_HEADS
    g = _gelu(g_ref[...].astype(F32))
    mu = jnp.mean(g, axis=-1, keepdims=True)
    gc = g - mu
    var = jnp.mean(jnp.square(gc), axis=-1, keepdims=True)
    v = (gc * lax.rsqrt(var + LN_EPS) * lg_ref[...] + lb_ref[...]).astype(BF16)
    row = lax.broadcasted_iota(jnp.int32, (CHUNK, CHUNK), 0)
    col = lax.broadcasted_iota(jnp.int32, (CHUNK, CHUNK), 1)
    tril = col <= row
    ss = jnp.zeros((CHUNK, 1), F32)
    for hh in range(GMLP_HEADS):
        w = jnp.where(tril, ws_ref[hh], 0.0).astype(BF16)
        mixed = jnp.dot(w, v[:, hh * hd:(hh + 1) * hd], preferred_element_type=F32) + bs_ref[hh]
        out = _gelu(u_ref[:, hh * hd:(hh + 1) * hd].astype(F32)) * mixed
        ss = ss + jnp.sum(jnp.square(out), axis=-1, keepdims=True)
        buf_ref[:, hh * hd:(hh + 1) * hd] = out
    rstd = lax.rsqrt(ss / u_ref.shape[1] + LN_EPS)
    o_ref[...] = (buf_ref[...] * rstd * bg_ref[...]).astype(o_ref.dtype)


def _spatial_gating(proj2, ln_g, ln_b, w_s, b_s, beta_gmlp, d_gmlp, u_col, g_col):
    n = proj2.shape[0]
    row_vec = pl.BlockSpec((1, d_gmlp), lambda c: (0, 0))
    return pl.pallas_call(
        _gating_kernel,
        grid=(n // CHUNK,),
        in_specs=[
            pl.BlockSpec((CHUNK, d_gmlp), lambda c: (c, u_col)),
            pl.BlockSpec((CHUNK, d_gmlp), lambda c: (c, g_col)),
            row_vec, row_vec,
            pl.BlockSpec((GMLP_HEADS, CHUNK, CHUNK), lambda c: (0, 0, 0)),
            pl.BlockSpec((GMLP_HEADS, CHUNK, 1), lambda c: (0, 0, 0)),
            row_vec,
        ],
        out_specs=pl.BlockSpec((CHUNK, d_gmlp), lambda c: (c, 0)),
        out_shape=jax.ShapeDtypeStruct((n, d_gmlp), BF16),
        scratch_shapes=[pltpu.VMEM((CHUNK, d_gmlp), F32)],
        compiler_params=_params("arbitrary"),
        name="spatial_gating",
    )(proj2, proj2, ln_g, ln_b, w_s, b_s[:, :, None], beta_gmlp)


def _layer_norm_chunks(buf_ref, n_chunks, d):
    tot = jnp.sum(buf_ref[0], axis=-1, keepdims=True)
    for c in range(1, n_chunks):
        tot = tot + jnp.sum(buf_ref[c], axis=-1, keepdims=True)
    mu = tot / d
    sq = jnp.sum(jnp.square(buf_ref[0] - mu), axis=-1, keepdims=True)
    for c in range(1, n_chunks):
        sq = sq + jnp.sum(jnp.square(buf_ref[c] - mu), axis=-1, keepdims=True)
    return mu, lax.rsqrt(sq / d + LN_EPS)


def _pack_bf16_pair(lo, hi):
    lo_bits = lax.bitcast_convert_type(lo.astype(BF16).astype(F32), jnp.uint32) >> 16
    hi_bits = lax.bitcast_convert_type(hi.astype(BF16).astype(F32), jnp.uint32) & jnp.uint32(0xFFFF0000)
    return hi_bits | lo_bits


def _unpack_bf16_pair(words):
    lo = lax.bitcast_convert_type(words << 16, F32)
    hi = lax.bitcast_convert_type(words & jnp.uint32(0xFFFF0000), F32)
    return lo, hi


def _out_proj_kernel(a_ref, g_ref, w_ref, x_ref, lg_ref, lb_ref, wr_ref, br_ref,
                     x1_ref, x1p_ref, lo_ref, buf_ref, *, n_chunks, tn):
    j = pl.program_id(1)
    ka = a_ref.shape[1]
    mix = (jnp.dot(a_ref[...], w_ref[:ka, :], preferred_element_type=F32)
           + jnp.dot(g_ref[...], w_ref[ka:, :], preferred_element_type=F32))
    buf_ref[j] = DN_ALPHA * x_ref[...] + mix

    @pl.when(j == n_chunks - 1)
    def _():
        d = n_chunks * tn
        mu, rstd = _layer_norm_chunks(buf_ref, n_chunks, d)
        logits = jnp.zeros(lo_ref.shape, F32) + br_ref[...]
        half_chunks = n_chunks // 2
        for c in range(half_chunks):
            ys = []
            for cc in (c, c + half_chunks):
                cs = slice(cc * tn, (cc + 1) * tn)
                y = (buf_ref[cc] - mu) * rstd * lg_ref[:, cs] + lb_ref[:, cs]
                x1_ref[:, cs] = y
                logits = logits + jnp.dot(y, wr_ref[cs, :], preferred_element_type=F32,
                                          precision=lax.Precision.HIGHEST)
                ys.append(y)
            x1p_ref[:, c * tn:(c + 1) * tn] = _pack_bf16_pair(ys[0], ys[1])
        lo_ref[...] = logits


def _out_proj_ln_router(attn2, gm2, wb, x2, ln_g, ln_b, w_route, b_route, tm=512, tn=256):
    m, d = x2.shape
    ka = attn2.shape[1]
    kg = gm2.shape[1]
    n_chunks = d // tn
    row_vec = pl.BlockSpec((1, d), lambda i, j: (0, 0))
    kernel = functools.partial(_out_proj_kernel, n_chunks=n_chunks, tn=tn)
    return pl.pallas_call(
        kernel,
        grid=(m // tm, n_chunks),
        in_specs=[
            pl.BlockSpec((tm, ka), lambda i, j: (i, 0)),
            pl.BlockSpec((tm, kg), lambda i, j: (i, 0)),
            pl.BlockSpec((ka + kg, tn), lambda i, j: (0, j)),
            pl.BlockSpec((tm, tn), lambda i, j: (i, j)),
            row_vec, row_vec,
            pl.BlockSpec((d, LANES), lambda i, j: (0, 0)),
            pl.BlockSpec((1, LANES), lambda i, j: (0, 0)),
        ],
        out_specs=[
            pl.BlockSpec((tm, d), lambda i, j: (i, 0)),
            pl.BlockSpec((tm, d // 2), lambda i, j: (i, 0)),
            pl.BlockSpec((tm, LANES), lambda i, j: (i, 0)),
        ],
        out_shape=[jax.ShapeDtypeStruct((m, d), F32), jax.ShapeDtypeStruct((m, d // 2), jnp.uint32),
                   jax.ShapeDtypeStruct((m, LANES), F32)],
        scratch_shapes=[pltpu.VMEM((n_chunks, tm, tn), F32)],
        compiler_params=_params("arbitrary", "arbitrary"),
        name="out_proj_ln_router",
    )(attn2, gm2, wb, x2, ln_g, ln_b, w_route, b_route)


def _route_kernel(lo_ref, eid_ref, wt_ref):
    lg = lo_ref[...]
    lane = lax.broadcasted_iota(jnp.int32, lg.shape, 1)
    lane_f = lane.astype(F32)
    none = float(LANES)
    first = lambda hit: jnp.min(jnp.where(hit, lane_f, none), axis=-1, keepdims=True)

    in_groups = lane < N_GROUPS
    g_logits = jnp.where(in_groups, lg, -jnp.inf)
    g_max = jnp.max(g_logits, axis=-1, keepdims=True)
    g_idx = first(g_logits == g_max)
    g_w = 1.0 / jnp.sum(jnp.where(in_groups, jnp.exp(lg - g_max), 0.0), axis=-1, keepdims=True)

    lo = N_GROUPS + g_idx * EXPERTS_PER_GROUP
    in_group = (lane_f >= lo) & (lane_f < lo + EXPERTS_PER_GROUP)
    e_logits = jnp.where(in_group, lg, -jnp.inf)
    t1 = jnp.max(e_logits, axis=-1, keepdims=True)
    i1 = first(e_logits == t1)
    e_rest = jnp.where(lane_f == i1, -jnp.inf, e_logits)
    t2 = jnp.max(e_rest, axis=-1, keepdims=True)
    i2 = first(e_rest == t2)
    ex = jnp.exp(t2 - t1)
    w1 = g_w * (1.0 / (1.0 + ex))
    w2 = g_w * (ex / (1.0 + ex))
    e1 = (i1 - N_GROUPS).astype(jnp.int32)
    e2 = (i2 - N_GROUPS).astype(jnp.int32)
    eid_ref[...] = jnp.where(lane == 0, e1, jnp.where(lane == 1, e2, 0))
    wt_ref[...] = jnp.where(lane == 0, w1, jnp.where(lane == 1, w2, 0.0))


def _route(logits, tm=1024):
    m = logits.shape[0]
    spec = pl.BlockSpec((tm, LANES), lambda i: (i, 0))
    return pl.pallas_call(
        _route_kernel,
        grid=(m // tm,),
        in_specs=[spec],
        out_specs=[spec, spec],
        out_shape=[jax.ShapeDtypeStruct((m, LANES), jnp.int32), jax.ShapeDtypeStruct((m, LANES), F32)],
        compiler_params=_params("arbitrary"),
        name="route",
    )(logits)


def _expert_kernel(be_ref, nreal_ref, tok_ref, aid_ref, x_hbm, wg_ref, wu_ref, wd_ref, ys_hbm,
                   xbuf, xb, acc, obuf, gsem, ssem, *, tm, nf):
    half = xb.shape[1] // 2
    b = pl.program_id(0)
    f = pl.program_id(1)
    nreal = nreal_ref[0]
    slot = lax.rem(b, 2)

    def start_gather(slot_):
        def body(r, carry):
            pltpu.make_async_copy(x_hbm.at[pl.ds(tok_ref[0, r], 1)], xbuf.at[slot_, pl.ds(r, 1)],
                                  gsem.at[slot_]).start()
            return carry
        lax.fori_loop(0, tm, body, 0, unroll=8)

    def wait_gather(slot_):
        pltpu.make_async_copy(x_hbm.at[pl.ds(0, tm)], xbuf.at[slot_], gsem.at[slot_]).wait()

    def start_scatter(slot_):
        def body(r, carry):
            pltpu.make_async_copy(obuf.at[slot_, pl.ds(r, 1)], ys_hbm.at[pl.ds(aid_ref[0, r], 1)],
                                  ssem.at[slot_]).start()
            return carry
        lax.fori_loop(0, tm, body, 0, unroll=8)

    def wait_scatter(slot_):
        pltpu.make_async_copy(obuf.at[slot_], ys_hbm.at[pl.ds(0, tm)], ssem.at[slot_]).wait()

    @pl.when(b < nreal)
    def _():
        @pl.when(f == 0)
        def _():
            @pl.when(b == 0)
            def _():
                start_gather(0)
            wait_gather(slot)
            lo, hi = _unpack_bf16_pair(xbuf[slot])
            xb[:, :half] = lo.astype(BF16)
            xb[:, half:] = hi.astype(BF16)

        x = xb[...]
        g = jnp.dot(x, wg_ref[...].astype(BF16), preferred_element_type=F32)
        u = jnp.dot(x, wu_ref[...].astype(BF16), preferred_element_type=F32)
        hidden = (g * (1.0 / (1.0 + jnp.exp(-g))) * u).astype(BF16)
        contrib = jnp.dot(hidden, wd_ref[...].astype(BF16), preferred_element_type=F32)

        @pl.when(f == 0)
        def _():
            acc[...] = contrib

        @pl.when((f > 0) & (f < nf - 1))
        def _():
            acc[...] += contrib

        @pl.when(f == nf - 1)
        def _():
            @pl.when(b >= 1)
            def _():
                wait_scatter(1 - slot)
            total = acc[...] + contrib
            obuf[slot] = _pack_bf16_pair(total[:, :half], total[:, half:])
            start_scatter(slot)

            @pl.when(b + 1 < nreal)
            def _():
                start_gather(1 - slot)

            @pl.when(b + 1 == nreal)
            def _():
                wait_scatter(slot)


def _expert_mlp(x1p, plan, w_gate, w_up, w_down, tm, fc=256):
    tok_blocks, aid_blocks, block_expert, n_real = plan
    n, dp = x1p.shape
    d = 2 * dp
    n_blocks = tok_blocks.shape[0]
    d_ff = w_gate.shape[2]
    nf = d_ff // fc
    assert nf >= 2

    def live(b, f, nreal):
        is_real = b < nreal[0]
        return jnp.where(is_real, b, nreal[0] - 1), jnp.where(is_real, f, nf - 1)

    def tok_map(b, f, be, nreal):
        bb, ff = live(b, f, nreal)
        return jnp.minimum(bb + (ff == nf - 1).astype(jnp.int32), nreal[0] - 1), 0, 0

    def aid_map(b, f, be, nreal):
        bb, _ = live(b, f, nreal)
        return bb, 0, 0

    def w_in_map(b, f, be, nreal):
        bb, ff = live(b, f, nreal)
        return be[bb], 0, ff

    def w_out_map(b, f, be, nreal):
        bb, ff = live(b, f, nreal)
        return be[bb], ff, 0

    grid_spec = pltpu.PrefetchScalarGridSpec(
        num_scalar_prefetch=2,
        grid=(n_blocks, nf),
        in_specs=[
            pl.BlockSpec((None, 1, tm), tok_map, memory_space=pltpu.SMEM),
            pl.BlockSpec((None, 1, tm), aid_map, memory_space=pltpu.SMEM),
            pl.BlockSpec(memory_space=pl.ANY),
            pl.BlockSpec((None, d, fc), w_in_map),
            pl.BlockSpec((None, d, fc), w_in_map),
            pl.BlockSpec((None, fc, d), w_out_map),
        ],
        out_specs=pl.BlockSpec(memory_space=pl.ANY),
        scratch_shapes=[pltpu.VMEM((2, tm, dp), jnp.uint32), pltpu.VMEM((tm, d), BF16), pltpu.VMEM((tm, d), F32),
                        pltpu.VMEM((2, tm, dp), jnp.uint32),
                        pltpu.SemaphoreType.DMA((2,)), pltpu.SemaphoreType.DMA((2,))],
    )
    kernel = functools.partial(_expert_kernel, tm=tm, nf=nf)
    return pl.pallas_call(
        kernel,
        grid_spec=grid_spec,
        out_shape=jax.ShapeDtypeStruct((n * TOP_K + 2 * tm, dp), jnp.uint32),
        compiler_params=_params("arbitrary", "arbitrary"),
        name="expert_mlp",
    )(block_expert, n_real, tok_blocks, aid_blocks, x1p, w_gate, w_up, w_down)


def _combine_kernel(x1_ref, y0_ref, y1_ref, wt_ref, lg_ref, lb_ref, o_ref, buf_ref):
    half = x1_ref.shape[1] // 2
    wt = wt_ref[...]
    y0 = _unpack_bf16_pair(y0_ref[...])
    y1 = _unpack_bf16_pair(y1_ref[...])
    for p, cs in enumerate((slice(0, half), slice(half, 2 * half))):
        buf_ref[:, cs] = DN_ALPHA * x1_ref[:, cs] + (wt[:, 0:1] * y0[p] + wt[:, 1:2] * y1[p])
    hsum = buf_ref[...]
    mu = jnp.mean(hsum, axis=-1, keepdims=True)
    hc = hsum - mu
    var = jnp.mean(jnp.square(hc), axis=-1, keepdims=True)
    o_ref[...] = hc * lax.rsqrt(var + LN_EPS) * lg_ref[...] + lb_ref[...]


def _combine_ln(x1, ys, wts, ln_g, ln_b, tm=256):
    m, d = x1.shape
    big = pl.BlockSpec((tm, d), lambda i: (i, 0))
    first = pl.BlockSpec((tm, d // 2), lambda i: (i, 0))
    second = pl.BlockSpec((tm, d // 2), lambda i: (m // tm + i, 0))
    row_vec = pl.BlockSpec((1, d), lambda i: (0, 0))
    return pl.pallas_call(
        _combine_kernel,
        grid=(m // tm,),
        in_specs=[big, first, second, pl.BlockSpec((tm, LANES), lambda i: (i, 0)), row_vec, row_vec],
        out_specs=big,
        out_shape=jax.ShapeDtypeStruct((m, d), F32),
        scratch_shapes=[pltpu.VMEM((tm, d), F32)],
        compiler_params=_params("arbitrary"),
        name="combine_ln",
    )(x1, ys, ys, wts, ln_g, ln_b)


def _plan_blocks(eid, tm):
    n = eid.shape[0]
    a = n * TOP_K
    expert_id = eid.reshape(-1)
    onehot = (expert_id[:, None] == jnp.arange(N_EXPERTS, dtype=jnp.int32)[None, :]).astype(jnp.int32)
    csum = jnp.cumsum(onehot, axis=0)
    counts = csum[-1]
    rank = jnp.take_along_axis(csum, expert_id[:, None], axis=1)[:, 0] - 1
    padded = (counts + tm - 1) // tm * tm
    pad_ends = jnp.cumsum(padded)
    pad_starts = pad_ends - padded
    n_blocks = a // tm + N_EXPERTS
    pos = pad_starts[expert_id] + rank
    spare = a + jnp.arange(n_blocks * tm, dtype=jnp.int32) % (2 * tm)
    flat = jnp.arange(a, dtype=jnp.int32)
    aid_pad = spare.at[pos].set((flat % TOP_K) * n + flat // TOP_K)
    tok_pad = jnp.where(aid_pad < a, aid_pad % n, 0)
    block_start = jnp.arange(n_blocks, dtype=jnp.int32) * tm
    block_expert = jnp.minimum(jnp.searchsorted(pad_ends, block_start, side='right'), N_EXPERTS - 1).astype(jnp.int32)
    n_real = (pad_ends[-1] // tm).astype(jnp.int32).reshape(1)
    return tok_pad.reshape(n_blocks, 1, tm), aid_pad.reshape(n_blocks, 1, tm), block_expert, n_real


def kernel(x, w_in, w_out, lambda_q1, lambda_k1, lambda_q2, lambda_k2, subln_g, beta_attn, gmlp_ln_g, gmlp_ln_b,
           spatial_w, spatial_b, beta_gmlp, rel_bias, ln1_g, ln1_b, w_group, b_group, w_expert, b_expert,
           w_gate, w_up, w_down, ln2_g, ln2_b):
    b, s, d = x.shape
    n = b * s
    d_attn = ATTN_HEADS * ATTN_V_DIM
    d_gmlp = d - d_attn
    moe_tm = 320
    for l in range(DEPTH):
        lambda_init = 0.8 - 0.6 * math.exp(-0.3 * l)
        x2 = x.reshape(n, d)
        proj = _proj_matmul(x2.astype(BF16), w_in[l].astype(BF16))
        attn = _diff_attention(proj.reshape(b, s, -1), rel_bias, lambda_q1[l][None], lambda_k1[l][None],
                               lambda_q2[l][None], lambda_k2[l][None], subln_g[l][None], beta_attn[l][None],
                               lambda_init)
        gm = _spatial_gating(proj, gmlp_ln_g[l][None], gmlp_ln_b[l][None], spatial_w[l], spatial_b[l],
                             beta_gmlp[l][None], d_gmlp, u_col=3 * d_attn // d_gmlp, g_col=3 * d_attn // d_gmlp + 1)
        n_route = N_GROUPS + N_EXPERTS
        w_route = jnp.concatenate([w_group[l], w_expert[l].reshape(d, N_EXPERTS),
                                   jnp.zeros((d, LANES - n_route), F32)], axis=1)
        b_route = jnp.concatenate([b_group[l], b_expert[l].reshape(-1), jnp.zeros((LANES - n_route,), F32)])[None]
        x1, x1p, logits = _out_proj_ln_router(attn.reshape(n, d_attn), gm, w_out[l].astype(BF16), x2,
                                         ln1_g[l][None], ln1_b[l][None], w_route, b_route)
        eid, wts = _route(logits)
        plan = _plan_blocks(eid[:, :TOP_K], moe_tm)
        ys = _expert_mlp(x1p, plan, w_gate[l], w_up[l], w_down[l], moe_tm)
        x = _combine_ln(x1, ys, wts, ln2_g[l][None], ln2_b[l][None]).reshape(b, s, d)
    return x
```

```python
import functools
import math

import jax
import jax.numpy as jnp
from jax import lax
from jax.experimental import pallas as pl
from jax.experimental.pallas import tpu as pltpu

F32 = jnp.float32
BF16 = jnp.bfloat16

ATTN_HEADS = 8
ATTN_HEAD_DIM = 128
ATTN_V_DIM = 2 * ATTN_HEAD_DIM
GMLP_HEADS = 8
CHUNK = 128
REL_BUCKETS = 32
REL_MAX_DIST = 128
N_GROUPS = 8
EXPERTS_PER_GROUP = 8
N_EXPERTS = N_GROUPS * EXPERTS_PER_GROUP
TOP_K = 2
LN_EPS = 1e-5
DEPTH = 1
DN_ALPHA = (2 * DEPTH) ** 0.25
NEG_INF = -1e30
LANES = 128
SUBLANES = 8
VMEM_LIMIT = 56 * 1024 * 1024


def _params(*semantics):
    return pltpu.CompilerParams(dimension_semantics=semantics, vmem_limit_bytes=VMEM_LIMIT)


def _matmul_kernel(x_ref, w_ref, o_ref):
    o_ref[...] = jnp.dot(x_ref[...], w_ref[...], preferred_element_type=F32).astype(o_ref.dtype)


def _proj_matmul(xb, wb, tm=1024, tn=512):
    m, k = xb.shape
    n = wb.shape[1]
    return pl.pallas_call(
        _matmul_kernel,
        grid=(m // tm, n // tn),
        in_specs=[pl.BlockSpec((tm, k), lambda i, j: (i, 0)),
                  pl.BlockSpec((k, tn), lambda i, j: (0, j))],
        out_specs=pl.BlockSpec((tm, tn), lambda i, j: (i, j)),
        out_shape=jax.ShapeDtypeStruct((m, n), BF16),
        compiler_params=_params("arbitrary", "arbitrary"),
        name="proj_matmul",
    )(xb, wb)


def _rel_bucket(n):
    max_exact = REL_BUCKETS // 2
    nf = jnp.maximum(n, max_exact).astype(F32)
    large = max_exact + (jnp.log(nf / max_exact) / math.log(REL_MAX_DIST / max_exact)
                         * (REL_BUCKETS - max_exact)).astype(jnp.int32)
    large = jnp.minimum(large, REL_BUCKETS - 1)
    return jnp.where(n < max_exact, n, large)


def _rel_bias_tiles(rel_bias, blk):
    pos = jnp.arange(blk)
    tiles = []
    for d in (0, 1):
        bucket = _rel_bucket(jnp.maximum(pos[:, None] + d * blk - pos[None, :], 0))
        hit = bucket[None] == jnp.arange(REL_BUCKETS)[:, None, None]
        tiles.append(jnp.sum(jnp.where(hit[:, None], rel_bias[:, :, None, None], 0.0), axis=0))
    return jnp.stack(tiles, axis=1).astype(F32)


def _attn_kernel(far_ref, q_ref, k_ref, v_ref, bias_ref, lq1_ref, lk1_ref, lq2_ref, lk2_ref,
                 sg_ref, ba_ref, o_ref, mx_ref, sh_ref, l_ref, s_ref, acc_ref, *, blk, lambda_init):
    h = pl.program_id(1)
    i = pl.program_id(2)
    dh = ATTN_HEAD_DIM
    half = blk // 2
    scale = ATTN_HEAD_DIM ** -0.5
    q = q_ref[...]
    qs = (q[:, :dh], q[:, dh:])
    contract_last = (((1,), (1,)), ((), ()))
    far = far_ref[h]
    n_far = jnp.maximum(i - 1, 0)

    def raw_scores(j):
        start = pl.multiple_of(j * blk, blk)
        kj = k_ref[pl.ds(start, blk), :]
        return [lax.dot_general(qs[mi], kj[:, mi * dh:(mi + 1) * dh], contract_last, preferred_element_type=F32)
                for mi in range(2)]

    mx_ref[...] = jnp.full(mx_ref.shape, -jnp.inf, F32)

    def max_body(j, carry):
        s = raw_scores(j)
        for mi in range(2):
            mx_ref[mi] = jnp.maximum(mx_ref[mi], jnp.maximum(s[mi][:, :half], s[mi][:, half:]))
        return carry

    lax.fori_loop(0, n_far, max_body, 0)

    row = lax.broadcasted_iota(jnp.int32, (blk, blk), 0)
    col = lax.broadcasted_iota(jnp.int32, (blk, blk), 1)
    near = raw_scores(jnp.maximum(i - 1, 0))
    diag = raw_scores(i)
    m = []
    for mi in range(2):
        s_near = jnp.where(i >= 1, near[mi] * scale + bias_ref[1], NEG_INF)
        s_diag = jnp.where(col <= row, diag[mi] * scale + bias_ref[0], NEG_INF)
        s_ref[mi, 0] = s_near
        s_ref[mi, 1] = s_diag
        m_far = jnp.max(mx_ref[mi], axis=-1, keepdims=True) * scale + far
        m_mi = jnp.maximum(m_far, jnp.maximum(jnp.max(s_near, axis=-1, keepdims=True),
                                              jnp.max(s_diag, axis=-1, keepdims=True)))
        sh_ref[mi] = jnp.broadcast_to(far - m_mi, (blk, half))
        m.append(m_mi)

    l_ref[...] = jnp.zeros(l_ref.shape, F32)
    acc_ref[...] = jnp.zeros(acc_ref.shape, F32)

    def pv_body(j, carry):
        s = raw_scores(j)
        vj = v_ref[pl.ds(pl.multiple_of(j * blk, blk), blk), :]
        ps = []
        for mi in range(2):
            sh = sh_ref[mi]
            p_lo = jnp.exp(s[mi][:, :half] * scale + sh)
            p_hi = jnp.exp(s[mi][:, half:] * scale + sh)
            l_ref[mi] += p_lo + p_hi
            ps.append(jnp.concatenate([p_lo, p_hi], axis=1).astype(BF16))
        acc_ref[...] += jnp.dot(jnp.concatenate(ps, axis=0), vj, preferred_element_type=F32)
        return carry

    lax.fori_loop(0, n_far, pv_body, 0)

    l = [jnp.sum(l_ref[mi], axis=-1, keepdims=True) for mi in range(2)]
    for t, j in ((0, jnp.maximum(i - 1, 0)), (1, i)):
        vj = v_ref[pl.ds(pl.multiple_of(j * blk, blk), blk), :]
        ps = []
        for mi in range(2):
            p = jnp.exp(s_ref[mi, t] - m[mi])
            l[mi] = l[mi] + jnp.sum(p, axis=-1, keepdims=True)
            ps.append(p.astype(BF16))
        acc_ref[...] += jnp.dot(jnp.concatenate(ps, axis=0), vj, preferred_element_type=F32)

    lam = (jnp.exp(jnp.sum(lq1_ref[...] * lk1_ref[...], axis=-1, keepdims=True))
           - jnp.exp(jnp.sum(lq2_ref[...] * lk2_ref[...], axis=-1, keepdims=True)) + lambda_init)
    o = acc_ref[:blk] / l[0] - lam * (acc_ref[blk:] / l[1])
    o = o * lax.rsqrt(jnp.mean(jnp.square(o), axis=-1, keepdims=True) + LN_EPS) * sg_ref[...]
    o = o * (1.0 - lambda_init) * ba_ref[...]
    o_ref[...] = o.astype(o_ref.dtype)


def _diff_attention(proj3, rel_bias, lq1, lk1, lq2, lk2, subln_g, beta_attn, lambda_init, blk=256):
    b, s, _ = proj3.shape
    h = ATTN_HEADS
    dv = ATTN_V_DIM
    assert REL_BUCKETS // 2 + int(math.log((blk + 1) / (REL_BUCKETS // 2)) / math.log(REL_MAX_DIST / (REL_BUCKETS // 2))
                                  * (REL_BUCKETS // 2)) >= REL_BUCKETS
    tiles = _rel_bias_tiles(rel_bias, blk)
    far = rel_bias[REL_BUCKETS - 1].astype(F32)
    vec = lambda c: pl.BlockSpec((1, c), lambda bi, hi, qi: (0, 0))
    kernel = functools.partial(_attn_kernel, blk=blk, lambda_init=lambda_init)
    return pl.pallas_call(
        kernel,
        grid=(b, h, s // blk),
        in_specs=[
            pl.BlockSpec(memory_space=pltpu.SMEM),
            pl.BlockSpec((None, blk, dv), lambda bi, hi, qi: (bi, qi, hi)),
            pl.BlockSpec((None, s, dv), lambda bi, hi, qi: (bi, 0, h + hi)),
            pl.BlockSpec((None, s, dv), lambda bi, hi, qi: (bi, 0, 2 * h + hi)),
            pl.BlockSpec((None, 2, blk, blk), lambda bi, hi, qi: (hi, 0, 0, 0)),
            vec(ATTN_HEAD_DIM), vec(ATTN_HEAD_DIM), vec(ATTN_HEAD_DIM), vec(ATTN_HEAD_DIM),
            vec(dv),
            pl.BlockSpec((1, dv), lambda bi, hi, qi: (0, hi)),
        ],
        out_specs=pl.BlockSpec((None, blk, dv), lambda bi, hi, qi: (bi, qi, hi)),
        out_shape=jax.ShapeDtypeStruct((b, s, h * dv), BF16),
        scratch_shapes=[pltpu.VMEM((2, blk, blk // 2), F32), pltpu.VMEM((2, blk, blk // 2), F32),
                        pltpu.VMEM((2, blk, blk // 2), F32), pltpu.VMEM((2, 2, blk, blk), F32),
                        pltpu.VMEM((2 * blk, dv), F32)],
        compiler_params=_params("arbitrary", "arbitrary", "arbitrary"),
        name="diff_attention",
    )(far, proj3, proj3, proj3, tiles, lq1, lk1, lq2, lk2, subln_g, beta_attn)


def _gelu(x):
    return 0.5 * x * (1.0 + lax.erf(x * (2.0 ** -0.5)))


def _gating_kernel(u_ref, g_ref, lg_ref, lb_ref, ws_ref, bs_ref, bg_ref, o_ref, buf_ref):
    hd = u_ref.shape[1] // GMLP_HEADS
    g = _gelu(g_ref[...].astype(F32))
    mu = jnp.mean(g, axis=-1, keepdims=True)
    gc = g - mu
    var = jnp.mean(jnp.square(gc), axis=-1, keepdims=True)
    v = (gc * lax.rsqrt(var + LN_EPS) * lg_ref[...] + lb_ref[...]).astype(BF16)
    row = lax.broadcasted_iota(jnp.int32, (CHUNK, CHUNK), 0)
    col = lax.broadcasted_iota(jnp.int32, (CHUNK, CHUNK), 1)
    tril = col <= row
    ss = jnp.zeros((CHUNK, 1), F32)
    for hh in range(GMLP_HEADS):
        w = jnp.where(tril, ws_ref[hh], 0.0).astype(BF16)
        mixed = jnp.dot(w, v[:, hh * hd:(hh + 1) * hd], preferred_element_type=F32) + bs_ref[hh]
        out = _gelu(u_ref[:, hh * hd:(hh + 1) * hd].astype(F32)) * mixed
        ss = ss + jnp.sum(jnp.square(out), axis=-1, keepdims=True)
        buf_ref[:, hh * hd:(hh + 1) * hd] = out
    rstd = lax.rsqrt(ss / u_ref.shape[1] + LN_EPS)
    o_ref[...] = (buf_ref[...] * rstd * bg_ref[...]).astype(o_ref.dtype)


def _spatial_gating(proj2, ln_g, ln_b, w_s, b_s, beta_gmlp, d_gmlp, u_col, g_col):
    n = proj2.shape[0]
    row_vec = pl.BlockSpec((1, d_gmlp), lambda c: (0, 0))
    return pl.pallas_call(
        _gating_kernel,
        grid=(n // CHUNK,),
        in_specs=[
            pl.BlockSpec((CHUNK, d_gmlp), lambda c: (c, u_col)),
            pl.BlockSpec((CHUNK, d_gmlp), lambda c: (c, g_col)),
            row_vec, row_vec,
            pl.BlockSpec((GMLP_HEADS, CHUNK, CHUNK), lambda c: (0, 0, 0)),
            pl.BlockSpec((GMLP_HEADS, CHUNK, 1), lambda c: (0, 0, 0)),
            row_vec,
        ],
        out_specs=pl.BlockSpec((CHUNK, d_gmlp), lambda c: (c, 0)),
        out_shape=jax.ShapeDtypeStruct((n, d_gmlp), BF16),
        scratch_shapes=[pltpu.VMEM((CHUNK, d_gmlp), F32)],
        compiler_params=_params("arbitrary"),
        name="spatial_gating",
    )(proj2, proj2, ln_g, ln_b, w_s, b_s[:, :, None], beta_gmlp)


def _layer_norm_chunks(buf_ref, n_chunks, d):
    tot = jnp.sum(buf_ref[0], axis=-1, keepdims=True)
    for c in range(1, n_chunks):
        tot = tot + jnp.sum(buf_ref[c], axis=-1, keepdims=True)
    mu = tot / d
    sq = jnp.sum(jnp.square(buf_ref[0] - mu), axis=-1, keepdims=True)
    for c in range(1, n_chunks):
        sq = sq + jnp.sum(jnp.square(buf_ref[c] - mu), axis=-1, keepdims=True)
    return mu, lax.rsqrt(sq / d + LN_EPS)


def _pack_bf16_pair(lo, hi):
    lo_bits = lax.bitcast_convert_type(lo.astype(BF16).astype(F32), jnp.uint32) >> 16
    hi_bits = lax.bitcast_convert_type(hi.astype(BF16).astype(F32), jnp.uint32) & jnp.uint32(0xFFFF0000)
    return hi_bits | lo_bits


def _unpack_bf16_pair(words):
    lo = lax.bitcast_convert_type(words << 16, F32)
    hi = lax.bitcast_convert_type(words & jnp.uint32(0xFFFF0000), F32)
    return lo, hi


def _out_proj_kernel(a_ref, g_ref, w_ref, x_ref, lg_ref, lb_ref, wr_ref, br_ref,
                     x1_ref, x1p_ref, lo_ref, buf_ref, *, n_chunks, tn):
    j = pl.program_id(1)
    ka = a_ref.shape[1]
    mix = (jnp.dot(a_ref[...], w_ref[:ka, :], preferred_element_type=F32)
           + jnp.dot(g_ref[...], w_ref[ka:, :], preferred_element_type=F32))
    buf_ref[j] = DN_ALPHA * x_ref[...] + mix

    @pl.when(j == n_chunks - 1)
    def _():
        d = n_chunks * tn
        mu, rstd = _layer_norm_chunks(buf_ref, n_chunks, d)
        logits = jnp.zeros(lo_ref.shape, F32) + br_ref[...]
        half_chunks = n_chunks // 2
        for c in range(half_chunks):
            ys = []
            for cc in (c, c + half_chunks):
                cs = slice(cc * tn, (cc + 1) * tn)
                y = (buf_ref[cc] - mu) * rstd * lg_ref[:, cs] + lb_ref[:, cs]
                x1_ref[:, cs] = y
                logits = logits + jnp.dot(y, wr_ref[cs, :], preferred_element_type=F32,
                                          precision=lax.Precision.HIGHEST)
                ys.append(y)
            x1p_ref[:, c * tn:(c + 1) * tn] = _pack_bf16_pair(ys[0], ys[1])
        lo_ref[...] = logits


def _out_proj_ln_router(attn2, gm2, wb, x2, ln_g, ln_b, w_route, b_route, tm=512, tn=256):
    m, d = x2.shape
    ka = attn2.shape[1]
    kg = gm2.shape[1]
    n_chunks = d // tn
    row_vec = pl.BlockSpec((1, d), lambda i, j: (0, 0))
    kernel = functools.partial(_out_proj_kernel, n_chunks=n_chunks, tn=tn)
    return pl.pallas_call(
        kernel,
        grid=(m // tm, n_chunks),
        in_specs=[
            pl.BlockSpec((tm, ka), lambda i, j: (i, 0)),
            pl.BlockSpec((tm, kg), lambda i, j: (i, 0)),
            pl.BlockSpec((ka + kg, tn), lambda i, j: (0, j)),
            pl.BlockSpec((tm, tn), lambda i, j: (i, j)),
            row_vec, row_vec,
            pl.BlockSpec((d, LANES), lambda i, j: (0, 0)),
            pl.BlockSpec((1, LANES), lambda i, j: (0, 0)),
        ],
        out_specs=[
            pl.BlockSpec((tm, d), lambda i, j: (i, 0)),
            pl.BlockSpec((tm, d // 2), lambda i, j: (i, 0)),
            pl.BlockSpec((tm, LANES), lambda i, j: (i, 0)),
        ],
        out_shape=[jax.ShapeDtypeStruct((m, d), F32), jax.ShapeDtypeStruct((m, d // 2), jnp.uint32),
                   jax.ShapeDtypeStruct((m, LANES), F32)],
        scratch_shapes=[pltpu.VMEM((n_chunks, tm, tn), F32)],
        compiler_params=_params("arbitrary", "arbitrary"),
        name="out_proj_ln_router",
    )(attn2, gm2, wb, x2, ln_g, ln_b, w_route, b_route)


def _route_kernel(lo_ref, eid_ref, wt_ref):
    lg = lo_ref[...]
    lane = lax.broadcasted_iota(jnp.int32, lg.shape, 1)
    lane_f = lane.astype(F32)
    none = float(LANES)
    first = lambda hit: jnp.min(jnp.where(hit, lane_f, none), axis=-1, keepdims=True)

    in_groups = lane < N_GROUPS
    g_logits = jnp.where(in_groups, lg, -jnp.inf)
    g_max = jnp.max(g_logits, axis=-1, keepdims=True)
    g_idx = first(g_logits == g_max)
    g_w = 1.0 / jnp.sum(jnp.where(in_groups, jnp.exp(lg - g_max), 0.0), axis=-1, keepdims=True)

    lo = N_GROUPS + g_idx * EXPERTS_PER_GROUP
    in_group = (lane_f >= lo) & (lane_f < lo + EXPERTS_PER_GROUP)
    e_logits = jnp.where(in_group, lg, -jnp.inf)
    t1 = jnp.max(e_logits, axis=-1, keepdims=True)
    i1 = first(e_logits == t1)
    e_rest = jnp.where(lane_f == i1, -jnp.inf, e_logits)
    t2 = jnp.max(e_rest, axis=-1, keepdims=True)
    i2 = first(e_rest == t2)
    ex = jnp.exp(t2 - t1)
    w1 = g_w * (1.0 / (1.0 + ex))
    w2 = g_w * (ex / (1.0 + ex))
    e1 = (i1 - N_GROUPS).astype(jnp.int32)
    e2 = (i2 - N_GROUPS).astype(jnp.int32)
    eid_ref[...] = jnp.where(lane == 0, e1, jnp.where(lane == 1, e2, 0))
    wt_ref[...] = jnp.where(lane == 0, w1, jnp.where(lane == 1, w2, 0.0))


def _route(logits, tm=1024):
    m = logits.shape[0]
    spec = pl.BlockSpec((tm, LANES), lambda i: (i, 0))
    return pl.pallas_call(
        _route_kernel,
        grid=(m // tm,),
        in_specs=[spec],
        out_specs=[spec, spec],
        out_shape=[jax.ShapeDtypeStruct((m, LANES), jnp.int32), jax.ShapeDtypeStruct((m, LANES), F32)],
        compiler_params=_params("arbitrary"),
        name="route",
    )(logits)


def _expert_kernel(be_ref, cnt_ref, nreal_ref, tok_ref, aid_ref, x_hbm, wg_ref, wu_ref, wd_ref, ys_hbm,
                   xbuf, xb, acc, obuf, gsem, ssem, *, tm, nf):
    half = xb.shape[1] // 2
    b = pl.program_id(0)
    f = pl.program_id(1)
    nreal = nreal_ref[0]
    slot = lax.rem(b, 2)

    def rows_of(blk):
        return pl.multiple_of(cnt_ref[blk], SUBLANES)

    def for_each_row(n_rows, copy_of_row):
        def body(grp, carry):
            base = pl.multiple_of(grp * SUBLANES, SUBLANES)
            for k in range(SUBLANES):
                copy_of_row(base + k).start(priority=k % 2)
            return carry
        lax.fori_loop(0, lax.shift_right_logical(n_rows, SUBLANES.bit_length() - 1), body, 0)

    def start_gather(slot_, n_rows):
        for_each_row(n_rows, lambda r: pltpu.make_async_copy(
            x_hbm.at[pl.ds(tok_ref[0, r], 1)], xbuf.at[slot_, pl.ds(r, 1)], gsem.at[slot_]))

    def wait_gather(slot_, n_rows):
        pltpu.make_async_copy(x_hbm.at[pl.ds(0, n_rows)], xbuf.at[slot_, pl.ds(0, n_rows)], gsem.at[slot_]).wait()

    def start_scatter(slot_, n_rows):
        for_each_row(n_rows, lambda r: pltpu.make_async_copy(
            obuf.at[slot_, pl.ds(r, 1)], ys_hbm.at[pl.ds(aid_ref[0, r], 1)], ssem.at[slot_]))

    def wait_scatter(slot_, n_rows):
        pltpu.make_async_copy(obuf.at[slot_, pl.ds(0, n_rows)], ys_hbm.at[pl.ds(0, n_rows)], ssem.at[slot_]).wait()

    @pl.when(b < nreal)
    def _():
        @pl.when(f == 0)
        def _():
            @pl.when(b == 0)
            def _():
                xbuf[...] = jnp.zeros(xbuf.shape, xbuf.dtype)
                start_gather(0, rows_of(0))
            wait_gather(slot, rows_of(b))
            lo, hi = _unpack_bf16_pair(xbuf[slot])
            xb[:, :half] = lo.astype(BF16)
            xb[:, half:] = hi.astype(BF16)

        x = xb[...]
        g = jnp.dot(x, wg_ref[...].astype(BF16), preferred_element_type=F32)
        u = jnp.dot(x, wu_ref[...].astype(BF16), preferred_element_type=F32)
        hidden = (g * (1.0 / (1.0 + jnp.exp(-g))) * u).astype(BF16)
        contrib = jnp.dot(hidden, wd_ref[...].astype(BF16), preferred_element_type=F32)

        @pl.when(f == 0)
        def _():
            acc[...] = contrib

        @pl.when((f > 0) & (f < nf - 1))
        def _():
            acc[...] += contrib

        @pl.when(f == nf - 1)
        def _():
            @pl.when(b >= 1)
            def _():
                wait_scatter(1 - slot, rows_of(b - 1))
            total = acc[...] + contrib
            obuf[slot] = _pack_bf16_pair(total[:, :half], total[:, half:])
            start_scatter(slot, rows_of(b))

            @pl.when(b + 1 < nreal)
            def _():
                start_gather(1 - slot, rows_of(b + 1))

            @pl.when(b + 1 == nreal)
            def _():
                wait_scatter(slot, rows_of(b))


def _expert_mlp(x1p, plan, w_gate, w_up, w_down, tm, fc=256):
    tok_blocks, aid_blocks, block_expert, block_rows, n_real = plan
    n, dp = x1p.shape
    d = 2 * dp
    n_blocks = tok_blocks.shape[0]
    d_ff = w_gate.shape[2]
    nf = d_ff // fc
    assert nf >= 2

    def live(b, f, nreal):
        is_real = b < nreal[0]
        return jnp.where(is_real, b, nreal[0] - 1), jnp.where(is_real, f, nf - 1)

    def tok_map(b, f, be, cnt, nreal):
        bb, ff = live(b, f, nreal)
        return jnp.minimum(bb + (ff == nf - 1).astype(jnp.int32), nreal[0] - 1), 0, 0

    def aid_map(b, f, be, cnt, nreal):
        bb, _ = live(b, f, nreal)
        return bb, 0, 0

    def w_in_map(b, f, be, cnt, nreal):
        bb, ff = live(b, f, nreal)
        return be[bb], 0, ff

    def w_out_map(b, f, be, cnt, nreal):
        bb, ff = live(b, f, nreal)
        return be[bb], ff, 0

    grid_spec = pltpu.PrefetchScalarGridSpec(
        num_scalar_prefetch=3,
        grid=(n_blocks, nf),
        in_specs=[
            pl.BlockSpec((None, 1, tm), tok_map, memory_space=pltpu.SMEM),
            pl.BlockSpec((None, 1, tm), aid_map, memory_space=pltpu.SMEM),
            pl.BlockSpec(memory_space=pl.ANY),
            pl.BlockSpec((None, d, fc), w_in_map),
            pl.BlockSpec((None, d, fc), w_in_map),
            pl.BlockSpec((None, fc, d), w_out_map),
        ],
        out_specs=pl.BlockSpec(memory_space=pl.ANY),
        scratch_shapes=[pltpu.VMEM((2, tm, dp), jnp.uint32), pltpu.VMEM((tm, d), BF16), pltpu.VMEM((tm, d), F32),
                        pltpu.VMEM((2, tm, dp), jnp.uint32),
                        pltpu.SemaphoreType.DMA((2,)), pltpu.SemaphoreType.DMA((2,))],
    )
    kernel = functools.partial(_expert_kernel, tm=tm, nf=nf)
    return pl.pallas_call(
        kernel,
        grid_spec=grid_spec,
        out_shape=jax.ShapeDtypeStruct((n * TOP_K + 2 * tm, dp), jnp.uint32),
        compiler_params=_params("arbitrary", "arbitrary"),
        name="expert_mlp",
    )(block_expert, block_rows, n_real, tok_blocks, aid_blocks, x1p, w_gate, w_up, w_down)


def _combine_kernel(x1_ref, y0_ref, y1_ref, wt_ref, lg_ref, lb_ref, o_ref, buf_ref):
    half = x1_ref.shape[1] // 2
    wt = wt_ref[...]
    y0 = _unpack_bf16_pair(y0_ref[...])
    y1 = _unpack_bf16_pair(y1_ref[...])
    for p, cs in enumerate((slice(0, half), slice(half, 2 * half))):
        buf_ref[:, cs] = DN_ALPHA * x1_ref[:, cs] + (wt[:, 0:1] * y0[p] + wt[:, 1:2] * y1[p])
    hsum = buf_ref[...]
    mu = jnp.mean(hsum, axis=-1, keepdims=True)
    hc = hsum - mu
    var = jnp.mean(jnp.square(hc), axis=-1, keepdims=True)
    o_ref[...] = hc * lax.rsqrt(var + LN_EPS) * lg_ref[...] + lb_ref[...]


def _combine_ln(x1, ys, wts, ln_g, ln_b, tm=256):
    m, d = x1.shape
    big = pl.BlockSpec((tm, d), lambda i: (i, 0))
    first = pl.BlockSpec((tm, d // 2), lambda i: (i, 0))
    second = pl.BlockSpec((tm, d // 2), lambda i: (m // tm + i, 0))
    row_vec = pl.BlockSpec((1, d), lambda i: (0, 0))
    return pl.pallas_call(
        _combine_kernel,
        grid=(m // tm,),
        in_specs=[big, first, second, pl.BlockSpec((tm, LANES), lambda i: (i, 0)), row_vec, row_vec],
        out_specs=big,
        out_shape=jax.ShapeDtypeStruct((m, d), F32),
        scratch_shapes=[pltpu.VMEM((tm, d), F32)],
        compiler_params=_params("arbitrary"),
        name="combine_ln",
    )(x1, ys, ys, wts, ln_g, ln_b)


def _plan_blocks(eid, tm):
    n = eid.shape[0]
    a = n * TOP_K
    expert_id = eid.reshape(-1)
    onehot = (expert_id[:, None] == jnp.arange(N_EXPERTS, dtype=jnp.int32)[None, :]).astype(jnp.int32)
    csum = jnp.cumsum(onehot, axis=0)
    counts = csum[-1]
    rank = jnp.take_along_axis(csum, expert_id[:, None], axis=1)[:, 0] - 1
    padded = (counts + tm - 1) // tm * tm
    pad_ends = jnp.cumsum(padded)
    pad_starts = pad_ends - padded
    n_blocks = a // tm + N_EXPERTS
    pos = pad_starts[expert_id] + rank
    spare = a + jnp.arange(n_blocks * tm, dtype=jnp.int32) % (2 * tm)
    flat = jnp.arange(a, dtype=jnp.int32)
    aid_pad = spare.at[pos].set((flat % TOP_K) * n + flat // TOP_K)
    tok_pad = jnp.where(aid_pad < a, aid_pad % n, 0)
    block_start = jnp.arange(n_blocks, dtype=jnp.int32) * tm
    block_expert = jnp.minimum(jnp.searchsorted(pad_ends, block_start, side='right'), N_EXPERTS - 1).astype(jnp.int32)
    block_count = jnp.clip(counts[block_expert] - (block_start - pad_starts[block_expert]), 0, tm)
    block_rows = ((block_count + SUBLANES - 1) // SUBLANES * SUBLANES).astype(jnp.int32)
    n_real = (pad_ends[-1] // tm).astype(jnp.int32).reshape(1)
    return tok_pad.reshape(n_blocks, 1, tm), aid_pad.reshape(n_blocks, 1, tm), block_expert, block_rows, n_real


def kernel(x, w_in, w_out, lambda_q1, lambda_k1, lambda_q2, lambda_k2, subln_g, beta_attn, gmlp_ln_g, gmlp_ln_b,
           spatial_w, spatial_b, beta_gmlp, rel_bias, ln1_g, ln1_b, w_group, b_group, w_expert, b_expert,
           w_gate, w_up, w_down, ln2_g, ln2_b):
    b, s, d = x.shape
    n = b * s
    d_attn = ATTN_HEADS * ATTN_V_DIM
    d_gmlp = d - d_attn
    moe_tm = 320
    for l in range(DEPTH):
        lambda_init = 0.8 - 0.6 * math.exp(-0.3 * l)
        x2 = x.reshape(n, d)
        proj = _proj_matmul(x2.astype(BF16), w_in[l].astype(BF16))
        attn = _diff_attention(proj.reshape(b, s, -1), rel_bias, lambda_q1[l][None], lambda_k1[l][None],
                               lambda_q2[l][None], lambda_k2[l][None], subln_g[l][None], beta_attn[l][None],
                               lambda_init)
        gm = _spatial_gating(proj, gmlp_ln_g[l][None], gmlp_ln_b[l][None], spatial_w[l], spatial_b[l],
                             beta_gmlp[l][None], d_gmlp, u_col=3 * d_attn // d_gmlp, g_col=3 * d_attn // d_gmlp + 1)
        n_route = N_GROUPS + N_EXPERTS
        w_route = jnp.concatenate([w_group[l], w_expert[l].reshape(d, N_EXPERTS),
                                   jnp.zeros((d, LANES - n_route), F32)], axis=1)
        b_route = jnp.concatenate([b_group[l], b_expert[l].reshape(-1), jnp.zeros((LANES - n_route,), F32)])[None]
        x1, x1p, logits = _out_proj_ln_router(attn.reshape(n, d_attn), gm, w_out[l].astype(BF16), x2,
                                         ln1_g[l][None], ln1_b[l][None], w_route, b_route)
        eid, wts = _route(logits)
        plan = _plan_blocks(eid[:, :TOP_K], moe_tm)
        ys = _expert_mlp(x1p, plan, w_gate[l], w_up[l], w_down[l], moe_tm)
        x = _combine_ln(x1, ys, wts, ln2_g[l][None], ln2_b[l][None]).reshape(b, s, d)
    return x
```

```python
import functools
import math

import jax
import jax.numpy as jnp
from jax import lax
from jax.experimental import pallas as pl
from jax.experimental.pallas import tpu as pltpu

F32 = jnp.float32
BF16 = jnp.bfloat16

ATTN_HEADS = 8
ATTN_HEAD_DIM = 128
ATTN_V_DIM = 2 * ATTN_HEAD_DIM
GMLP_HEADS = 8
CHUNK = 128
REL_BUCKETS = 32
REL_MAX_DIST = 128
N_GROUPS = 8
EXPERTS_PER_GROUP = 8
N_EXPERTS = N_GROUPS * EXPERTS_PER_GROUP
TOP_K = 2
LN_EPS = 1e-5
DEPTH = 1
DN_ALPHA = (2 * DEPTH) ** 0.25
NEG_INF = -1e30
LOG2E = math.log2(math.e)
LANES = 128
SUBLANES = 8
VMEM_LIMIT = 56 * 1024 * 1024


def _params(*semantics):
    return pltpu.CompilerParams(dimension_semantics=semantics, vmem_limit_bytes=VMEM_LIMIT)


def _matmul_kernel(x_ref, w_ref, o_ref):
    o_ref[...] = jnp.dot(x_ref[...], w_ref[...], preferred_element_type=F32).astype(o_ref.dtype)


def _proj_matmul(xb, wb, tm=1024, tn=512):
    m, k = xb.shape
    n = wb.shape[1]
    return pl.pallas_call(
        _matmul_kernel,
        grid=(m // tm, n // tn),
        in_specs=[pl.BlockSpec((tm, k), lambda i, j: (i, 0)),
                  pl.BlockSpec((k, tn), lambda i, j: (0, j))],
        out_specs=pl.BlockSpec((tm, tn), lambda i, j: (i, j)),
        out_shape=jax.ShapeDtypeStruct((m, n), BF16),
        compiler_params=_params("arbitrary", "arbitrary"),
        name="proj_matmul",
    )(xb, wb)


def _rel_bucket(n):
    max_exact = REL_BUCKETS // 2
    nf = jnp.maximum(n, max_exact).astype(F32)
    large = max_exact + (jnp.log(nf / max_exact) / math.log(REL_MAX_DIST / max_exact)
                         * (REL_BUCKETS - max_exact)).astype(jnp.int32)
    large = jnp.minimum(large, REL_BUCKETS - 1)
    return jnp.where(n < max_exact, n, large)


def _rel_bias_tiles(rel_bias, blk):
    pos = jnp.arange(blk)
    tiles = []
    for d in (0, 1):
        bucket = _rel_bucket(jnp.maximum(pos[:, None] + d * blk - pos[None, :], 0))
        hit = bucket[None] == jnp.arange(REL_BUCKETS)[:, None, None]
        tiles.append(jnp.sum(jnp.where(hit[:, None], rel_bias[:, :, None, None], 0.0), axis=0))
    return jnp.stack(tiles, axis=1).astype(F32)


def _attn_kernel(far_ref, q_ref, k_ref, v_ref, bias_ref, lq1_ref, lk1_ref, lq2_ref, lk2_ref,
                 sg_ref, ba_ref, o_ref, mx_ref, sh_ref, mrow_ref, l_ref, s_ref, acc_ref, *, blk, lambda_init):
    h = pl.program_id(1)
    i = pl.program_id(2)
    dh = ATTN_HEAD_DIM
    c1 = ATTN_HEAD_DIM ** -0.5 * LOG2E
    q = q_ref[...]
    qs = (q[:, :dh], q[:, dh:])
    contract_last = (((1,), (1,)), ((), ()))
    far2 = far_ref[h] * LOG2E
    n_far = jnp.maximum(i - 1, 0)

    def lane_chunks(x):
        return [x[:, c * LANES:(c + 1) * LANES] for c in range(x.shape[1] // LANES)]

    def raw_scores(start, width):
        kj = k_ref[pl.ds(start, width), :]
        return [lax.dot_general(qs[mi], kj[:, mi * dh:(mi + 1) * dh], contract_last, preferred_element_type=F32)
                for mi in range(2)]

    def over_far_blocks(body):
        def pair(t, carry):
            body(pl.multiple_of(t * (2 * blk), 2 * blk), 2 * blk)
            return carry
        lax.fori_loop(0, lax.shift_right_logical(n_far, 1), pair, 0)

        @pl.when(lax.rem(n_far, 2) == 1)
        def _():
            body(pl.multiple_of((n_far - 1) * blk, blk), blk)

    mx_ref[...] = jnp.full(mx_ref.shape, -jnp.inf, F32)

    def max_body(start, width):
        s = raw_scores(start, width)
        for mi in range(2):
            best = mx_ref[mi]
            for chunk in lane_chunks(s[mi]):
                best = jnp.maximum(best, chunk)
            mx_ref[mi] = best

    over_far_blocks(max_body)

    row = lax.broadcasted_iota(jnp.int32, (blk, blk), 0)
    col = lax.broadcasted_iota(jnp.int32, (blk, blk), 1)
    near = raw_scores(pl.multiple_of(n_far * blk, blk), blk)
    diag = raw_scores(pl.multiple_of(i * blk, blk), blk)
    for mi in range(2):
        s_near = jnp.where(i >= 1, near[mi] * c1 + bias_ref[1], NEG_INF)
        s_diag = jnp.where(col <= row, diag[mi] * c1 + bias_ref[0], NEG_INF)
        s_ref[mi, 0] = s_near
        s_ref[mi, 1] = s_diag
        best = mx_ref[mi] * c1 + far2
        for chunk in lane_chunks(s_near) + lane_chunks(s_diag):
            best = jnp.maximum(best, chunk)
        m_rows = jnp.broadcast_to(jnp.max(best, axis=-1, keepdims=True), (blk, LANES))
        mrow_ref[mi] = m_rows
        sh_ref[mi] = far2 - m_rows

    l_ref[...] = jnp.zeros(l_ref.shape, F32)
    acc_ref[...] = jnp.zeros(acc_ref.shape, F32)

    def accumulate(p_chunks, vj):
        ps = []
        for mi in range(2):
            tot = l_ref[mi]
            for chunk in p_chunks[mi]:
                tot = tot + chunk
            l_ref[mi] = tot
            ps.append(jnp.concatenate(p_chunks[mi], axis=1).astype(BF16))
        acc_ref[...] += jnp.dot(jnp.concatenate(ps, axis=0), vj, preferred_element_type=F32)

    def pv_body(start, width):
        s = raw_scores(start, width)
        accumulate([[jnp.exp2(chunk * c1 + sh_ref[mi]) for chunk in lane_chunks(s[mi])] for mi in range(2)],
                   v_ref[pl.ds(start, width), :])

    over_far_blocks(pv_body)

    for t, start in ((0, pl.multiple_of(n_far * blk, blk)), (1, pl.multiple_of(i * blk, blk))):
        accumulate([[jnp.exp2(chunk - mrow_ref[mi]) for chunk in lane_chunks(s_ref[mi, t])] for mi in range(2)],
                   v_ref[pl.ds(start, blk), :])

    l = [jnp.sum(l_ref[mi], axis=-1, keepdims=True) for mi in range(2)]
    lam = (jnp.exp(jnp.sum(lq1_ref[...] * lk1_ref[...], axis=-1, keepdims=True))
           - jnp.exp(jnp.sum(lq2_ref[...] * lk2_ref[...], axis=-1, keepdims=True)) + lambda_init)
    o = acc_ref[:blk] / l[0] - lam * (acc_ref[blk:] / l[1])
    o = o * lax.rsqrt(jnp.mean(jnp.square(o), axis=-1, keepdims=True) + LN_EPS) * sg_ref[...]
    o = o * (1.0 - lambda_init) * ba_ref[...]
    o_ref[...] = o.astype(o_ref.dtype)


def _diff_attention(proj3, rel_bias, lq1, lk1, lq2, lk2, subln_g, beta_attn, lambda_init, blk=256):
    b, s, _ = proj3.shape
    h = ATTN_HEADS
    dv = ATTN_V_DIM
    assert REL_BUCKETS // 2 + int(math.log((blk + 1) / (REL_BUCKETS // 2)) / math.log(REL_MAX_DIST / (REL_BUCKETS // 2))
                                  * (REL_BUCKETS // 2)) >= REL_BUCKETS
    tiles = _rel_bias_tiles(rel_bias, blk) * LOG2E
    far = rel_bias[REL_BUCKETS - 1].astype(F32)
    vec = lambda c: pl.BlockSpec((1, c), lambda bi, hi, qi: (0, 0))
    kernel = functools.partial(_attn_kernel, blk=blk, lambda_init=lambda_init)
    return pl.pallas_call(
        kernel,
        grid=(b, h, s // blk),
        in_specs=[
            pl.BlockSpec(memory_space=pltpu.SMEM),
            pl.BlockSpec((None, blk, dv), lambda bi, hi, qi: (bi, qi, hi)),
            pl.BlockSpec((None, s, dv), lambda bi, hi, qi: (bi, 0, h + hi)),
            pl.BlockSpec((None, s, dv), lambda bi, hi, qi: (bi, 0, 2 * h + hi)),
            pl.BlockSpec((None, 2, blk, blk), lambda bi, hi, qi: (hi, 0, 0, 0)),
            vec(ATTN_HEAD_DIM), vec(ATTN_HEAD_DIM), vec(ATTN_HEAD_DIM), vec(ATTN_HEAD_DIM),
            vec(dv),
            pl.BlockSpec((1, dv), lambda bi, hi, qi: (0, hi)),
        ],
        out_specs=pl.BlockSpec((None, blk, dv), lambda bi, hi, qi: (bi, qi, hi)),
        out_shape=jax.ShapeDtypeStruct((b, s, h * dv), BF16),
        scratch_shapes=[pltpu.VMEM((2, blk, LANES), F32), pltpu.VMEM((2, blk, LANES), F32),
                        pltpu.VMEM((2, blk, LANES), F32), pltpu.VMEM((2, blk, LANES), F32),
                        pltpu.VMEM((2, 2, blk, blk), F32), pltpu.VMEM((2 * blk, dv), F32)],
        compiler_params=_params("arbitrary", "arbitrary", "arbitrary"),
        name="diff_attention",
    )(far, proj3, proj3, proj3, tiles, lq1, lk1, lq2, lk2, subln_g, beta_attn)


def _gelu(x):
    return 0.5 * x * (1.0 + lax.erf(x * (2.0 ** -0.5)))


def _gating_kernel(u_ref, g_ref, lg_ref, lb_ref, ws_ref, bs_ref, bg_ref, o_ref, buf_ref):
    hd = u_ref.shape[1] // GMLP_HEADS
    g = _gelu(g_ref[...].astype(F32))
    mu = jnp.mean(g, axis=-1, keepdims=True)
    gc = g - mu
    var = jnp.mean(jnp.square(gc), axis=-1, keepdims=True)
    v = (gc * lax.rsqrt(var + LN_EPS) * lg_ref[...] + lb_ref[...]).astype(BF16)
    row = lax.broadcasted_iota(jnp.int32, (CHUNK, CHUNK), 0)
    col = lax.broadcasted_iota(jnp.int32, (CHUNK, CHUNK), 1)
    tril = col <= row
    ss = jnp.zeros((CHUNK, 1), F32)
    for hh in range(GMLP_HEADS):
        w = jnp.where(tril, ws_ref[hh], 0.0).astype(BF16)
        mixed = jnp.dot(w, v[:, hh * hd:(hh + 1) * hd], preferred_element_type=F32) + bs_ref[hh]
        out = _gelu(u_ref[:, hh * hd:(hh + 1) * hd].astype(F32)) * mixed
        ss = ss + jnp.sum(jnp.square(out), axis=-1, keepdims=True)
        buf_ref[:, hh * hd:(hh + 1) * hd] = out
    rstd = lax.rsqrt(ss / u_ref.shape[1] + LN_EPS)
    o_ref[...] = (buf_ref[...] * rstd * bg_ref[...]).astype(o_ref.dtype)


def _spatial_gating(proj2, ln_g, ln_b, w_s, b_s, beta_gmlp, d_gmlp, u_col, g_col):
    n = proj2.shape[0]
    row_vec = pl.BlockSpec((1, d_gmlp), lambda c: (0, 0))
    return pl.pallas_call(
        _gating_kernel,
        grid=(n // CHUNK,),
        in_specs=[
            pl.BlockSpec((CHUNK, d_gmlp), lambda c: (c, u_col)),
            pl.BlockSpec((CHUNK, d_gmlp), lambda c: (c, g_col)),
            row_vec, row_vec,
            pl.BlockSpec((GMLP_HEADS, CHUNK, CHUNK), lambda c: (0, 0, 0)),
            pl.BlockSpec((GMLP_HEADS, CHUNK, 1), lambda c: (0, 0, 0)),
            row_vec,
        ],
        out_specs=pl.BlockSpec((CHUNK, d_gmlp), lambda c: (c, 0)),
        out_shape=jax.ShapeDtypeStruct((n, d_gmlp), BF16),
        scratch_shapes=[pltpu.VMEM((CHUNK, d_gmlp), F32)],
        compiler_params=_params("arbitrary"),
        name="spatial_gating",
    )(proj2, proj2, ln_g, ln_b, w_s, b_s[:, :, None], beta_gmlp)


def _layer_norm_chunks(buf_ref, n_chunks, d):
    tot = jnp.sum(buf_ref[0], axis=-1, keepdims=True)
    for c in range(1, n_chunks):
        tot = tot + jnp.sum(buf_ref[c], axis=-1, keepdims=True)
    mu = tot / d
    sq = jnp.sum(jnp.square(buf_ref[0] - mu), axis=-1, keepdims=True)
    for c in range(1, n_chunks):
        sq = sq + jnp.sum(jnp.square(buf_ref[c] - mu), axis=-1, keepdims=True)
    return mu, lax.rsqrt(sq / d + LN_EPS)


def _pack_bf16_pair(lo, hi):
    lo_bits = lax.bitcast_convert_type(lo.astype(BF16).astype(F32), jnp.uint32) >> 16
    hi_bits = lax.bitcast_convert_type(hi.astype(BF16).astype(F32), jnp.uint32) & jnp.uint32(0xFFFF0000)
    return hi_bits | lo_bits


def _unpack_bf16_pair(words):
    lo = lax.bitcast_convert_type(words << 16, F32)
    hi = lax.bitcast_convert_type(words & jnp.uint32(0xFFFF0000), F32)
    return lo, hi


def _out_proj_kernel(a_ref, g_ref, w_ref, x_ref, lg_ref, lb_ref, wr_ref, br_ref,
                     x1_ref, x1p_ref, lo_ref, buf_ref, *, n_chunks, tn):
    j = pl.program_id(1)
    ka = a_ref.shape[1]
    mix = (jnp.dot(a_ref[...], w_ref[:ka, :], preferred_element_type=F32)
           + jnp.dot(g_ref[...], w_ref[ka:, :], preferred_element_type=F32))
    buf_ref[j] = DN_ALPHA * x_ref[...] + mix

    @pl.when(j == n_chunks - 1)
    def _():
        d = n_chunks * tn
        mu, rstd = _layer_norm_chunks(buf_ref, n_chunks, d)
        logits = jnp.zeros(lo_ref.shape, F32) + br_ref[...]
        half_chunks = n_chunks // 2
        for c in range(half_chunks):
            ys = []
            for cc in (c, c + half_chunks):
                cs = slice(cc * tn, (cc + 1) * tn)
                y = (buf_ref[cc] - mu) * rstd * lg_ref[:, cs] + lb_ref[:, cs]
                x1_ref[:, cs] = y
                logits = logits + jnp.dot(y, wr_ref[cs, :], preferred_element_type=F32,
                                          precision=lax.Precision.HIGHEST)
                ys.append(y)
            x1p_ref[:, c * tn:(c + 1) * tn] = _pack_bf16_pair(ys[0], ys[1])
        lo_ref[...] = logits


def _out_proj_ln_router(attn2, gm2, wb, x2, ln_g, ln_b, w_route, b_route, tm=512, tn=256):
    m, d = x2.shape
    ka = attn2.shape[1]
    kg = gm2.shape[1]
    n_chunks = d // tn
    row_vec = pl.BlockSpec((1, d), lambda i, j: (0, 0))
    kernel = functools.partial(_out_proj_kernel, n_chunks=n_chunks, tn=tn)
    return pl.pallas_call(
        kernel,
        grid=(m // tm, n_chunks),
        in_specs=[
            pl.BlockSpec((tm, ka), lambda i, j: (i, 0)),
            pl.BlockSpec((tm, kg), lambda i, j: (i, 0)),
            pl.BlockSpec((ka + kg, tn), lambda i, j: (0, j)),
            pl.BlockSpec((tm, tn), lambda i, j: (i, j)),
            row_vec, row_vec,
            pl.BlockSpec((d, LANES), lambda i, j: (0, 0)),
            pl.BlockSpec((1, LANES), lambda i, j: (0, 0)),
        ],
        out_specs=[
            pl.BlockSpec((tm, d), lambda i, j: (i, 0)),
            pl.BlockSpec((tm, d // 2), lambda i, j: (i, 0)),
            pl.BlockSpec((tm, LANES), lambda i, j: (i, 0)),
        ],
        out_shape=[jax.ShapeDtypeStruct((m, d), F32), jax.ShapeDtypeStruct((m, d // 2), jnp.uint32),
                   jax.ShapeDtypeStruct((m, LANES), F32)],
        scratch_shapes=[pltpu.VMEM((n_chunks, tm, tn), F32)],
        compiler_params=_params("arbitrary", "arbitrary"),
        name="out_proj_ln_router",
    )(attn2, gm2, wb, x2, ln_g, ln_b, w_route, b_route)


def _route_kernel(lo_ref, eid_ref, wt_ref):
    lg = lo_ref[...]
    lane = lax.broadcasted_iota(jnp.int32, lg.shape, 1)
    lane_f = lane.astype(F32)
    none = float(LANES)
    first = lambda hit: jnp.min(jnp.where(hit, lane_f, none), axis=-1, keepdims=True)

    in_groups = lane < N_GROUPS
    g_logits = jnp.where(in_groups, lg, -jnp.inf)
    g_max = jnp.max(g_logits, axis=-1, keepdims=True)
    g_idx = first(g_logits == g_max)
    g_w = 1.0 / jnp.sum(jnp.where(in_groups, jnp.exp(lg - g_max), 0.0), axis=-1, keepdims=True)

    lo = N_GROUPS + g_idx * EXPERTS_PER_GROUP
    in_group = (lane_f >= lo) & (lane_f < lo + EXPERTS_PER_GROUP)
    e_logits = jnp.where(in_group, lg, -jnp.inf)
    t1 = jnp.max(e_logits, axis=-1, keepdims=True)
    i1 = first(e_logits == t1)
    e_rest = jnp.where(lane_f == i1, -jnp.inf, e_logits)
    t2 = jnp.max(e_rest, axis=-1, keepdims=True)
    i2 = first(e_rest == t2)
    ex = jnp.exp(t2 - t1)
    w1 = g_w * (1.0 / (1.0 + ex))
    w2 = g_w * (ex / (1.0 + ex))
    e1 = (i1 - N_GROUPS).astype(jnp.int32)
    e2 = (i2 - N_GROUPS).astype(jnp.int32)
    eid_ref[...] = jnp.where(lane == 0, e1, jnp.where(lane == 1, e2, 0))
    wt_ref[...] = jnp.where(lane == 0, w1, jnp.where(lane == 1, w2, 0.0))


def _route(logits, tm=1024):
    m = logits.shape[0]
    spec = pl.BlockSpec((tm, LANES), lambda i: (i, 0))
    return pl.pallas_call(
        _route_kernel,
        grid=(m // tm,),
        in_specs=[spec],
        out_specs=[spec, spec],
        out_shape=[jax.ShapeDtypeStruct((m, LANES), jnp.int32), jax.ShapeDtypeStruct((m, LANES), F32)],
        compiler_params=_params("arbitrary"),
        name="route",
    )(logits)


def _expert_kernel(be_ref, cnt_ref, nreal_ref, tok_ref, aid_ref, x_hbm, wg_ref, wu_ref, wd_ref, ys_hbm,
                   xbuf, xb, acc, obuf, gsem, ssem, *, tm, nf):
    half = xb.shape[1] // 2
    b = pl.program_id(0)
    f = pl.program_id(1)
    nreal = nreal_ref[0]
    slot = lax.rem(b, 2)

    def rows_of(blk):
        return pl.multiple_of(cnt_ref[blk], SUBLANES)

    def for_each_row(n_rows, copy_of_row):
        def body(grp, carry):
            base = pl.multiple_of(grp * SUBLANES, SUBLANES)
            for k in range(SUBLANES):
                copy_of_row(base + k).start(priority=k % 2)
            return carry
        lax.fori_loop(0, lax.shift_right_logical(n_rows, SUBLANES.bit_length() - 1), body, 0)

    def start_gather(slot_, n_rows):
        for_each_row(n_rows, lambda r: pltpu.make_async_copy(
            x_hbm.at[pl.ds(tok_ref[0, r], 1)], xbuf.at[slot_, pl.ds(r, 1)], gsem.at[slot_]))

    def wait_gather(slot_, n_rows):
        pltpu.make_async_copy(x_hbm.at[pl.ds(0, n_rows)], xbuf.at[slot_, pl.ds(0, n_rows)], gsem.at[slot_]).wait()

    def start_scatter(slot_, n_rows):
        for_each_row(n_rows, lambda r: pltpu.make_async_copy(
            obuf.at[slot_, pl.ds(r, 1)], ys_hbm.at[pl.ds(aid_ref[0, r], 1)], ssem.at[slot_]))

    def wait_scatter(slot_, n_rows):
        pltpu.make_async_copy(obuf.at[slot_, pl.ds(0, n_rows)], ys_hbm.at[pl.ds(0, n_rows)], ssem.at[slot_]).wait()

    @pl.when(b < nreal)
    def _():
        @pl.when(f == 0)
        def _():
            @pl.when(b == 0)
            def _():
                xbuf[...] = jnp.zeros(xbuf.shape, xbuf.dtype)
                start_gather(0, rows_of(0))
            wait_gather(slot, rows_of(b))
            lo, hi = _unpack_bf16_pair(xbuf[slot])
            xb[:, :half] = lo.astype(BF16)
            xb[:, half:] = hi.astype(BF16)

        x = xb[...]
        g = jnp.dot(x, wg_ref[...].astype(BF16), preferred_element_type=F32)
        u = jnp.dot(x, wu_ref[...].astype(BF16), preferred_element_type=F32)
        hidden = (g * (1.0 / (1.0 + jnp.exp(-g))) * u).astype(BF16)
        contrib = jnp.dot(hidden, wd_ref[...].astype(BF16), preferred_element_type=F32)

        @pl.when(f == 0)
        def _():
            acc[...] = contrib

        @pl.when((f > 0) & (f < nf - 1))
        def _():
            acc[...] += contrib

        @pl.when(f == nf - 1)
        def _():
            @pl.when(b >= 1)
            def _():
                wait_scatter(1 - slot, rows_of(b - 1))
            total = acc[...] + contrib
            obuf[slot] = _pack_bf16_pair(total[:, :half], total[:, half:])
            start_scatter(slot, rows_of(b))

            @pl.when(b + 1 < nreal)
            def _():
                start_gather(1 - slot, rows_of(b + 1))

            @pl.when(b + 1 == nreal)
            def _():
                wait_scatter(slot, rows_of(b))


def _expert_mlp(x1p, plan, w_gate, w_up, w_down, tm, fc=256):
    tok_blocks, aid_blocks, block_expert, block_rows, n_real = plan
    n, dp = x1p.shape
    d = 2 * dp
    n_blocks = tok_blocks.shape[0]
    d_ff = w_gate.shape[2]
    nf = d_ff // fc
    assert nf >= 2

    def live(b, f, nreal):
        is_real = b < nreal[0]
        return jnp.where(is_real, b, nreal[0] - 1), jnp.where(is_real, f, nf - 1)

    def tok_map(b, f, be, cnt, nreal):
        bb, ff = live(b, f, nreal)
        return jnp.minimum(bb + (ff == nf - 1).astype(jnp.int32), nreal[0] - 1), 0, 0

    def aid_map(b, f, be, cnt, nreal):
        bb, _ = live(b, f, nreal)
        return bb, 0, 0

    def w_in_map(b, f, be, cnt, nreal):
        bb, ff = live(b, f, nreal)
        return be[bb], 0, ff

    def w_out_map(b, f, be, cnt, nreal):
        bb, ff = live(b, f, nreal)
        return be[bb], ff, 0

    grid_spec = pltpu.PrefetchScalarGridSpec(
        num_scalar_prefetch=3,
        grid=(n_blocks, nf),
        in_specs=[
            pl.BlockSpec((None, 1, tm), tok_map, memory_space=pltpu.SMEM),
            pl.BlockSpec((None, 1, tm), aid_map, memory_space=pltpu.SMEM),
            pl.BlockSpec(memory_space=pl.ANY),
            pl.BlockSpec((None, d, fc), w_in_map),
            pl.BlockSpec((None, d, fc), w_in_map),
            pl.BlockSpec((None, fc, d), w_out_map),
        ],
        out_specs=pl.BlockSpec(memory_space=pl.ANY),
        scratch_shapes=[pltpu.VMEM((2, tm, dp), jnp.uint32), pltpu.VMEM((tm, d), BF16), pltpu.VMEM((tm, d), F32),
                        pltpu.VMEM((2, tm, dp), jnp.uint32),
                        pltpu.SemaphoreType.DMA((2,)), pltpu.SemaphoreType.DMA((2,))],
    )
    kernel = functools.partial(_expert_kernel, tm=tm, nf=nf)
    return pl.pallas_call(
        kernel,
        grid_spec=grid_spec,
        out_shape=jax.ShapeDtypeStruct((n * TOP_K + 2 * tm, dp), jnp.uint32),
        compiler_params=_params("arbitrary", "arbitrary"),
        name="expert_mlp",
    )(block_expert, block_rows, n_real, tok_blocks, aid_blocks, x1p, w_gate, w_up, w_down)


def _combine_kernel(x1_ref, y0_ref, y1_ref, wt_ref, lg_ref, lb_ref, o_ref, buf_ref):
    half = x1_ref.shape[1] // 2
    wt = wt_ref[...]
    y0 = _unpack_bf16_pair(y0_ref[...])
    y1 = _unpack_bf16_pair(y1_ref[...])
    for p, cs in enumerate((slice(0, half), slice(half, 2 * half))):
        buf_ref[:, cs] = DN_ALPHA * x1_ref[:, cs] + (wt[:, 0:1] * y0[p] + wt[:, 1:2] * y1[p])
    hsum = buf_ref[...]
    mu = jnp.mean(hsum, axis=-1, keepdims=True)
    hc = hsum - mu
    var = jnp.mean(jnp.square(hc), axis=-1, keepdims=True)
    o_ref[...] = hc * lax.rsqrt(var + LN_EPS) * lg_ref[...] + lb_ref[...]


def _combine_ln(x1, ys, wts, ln_g, ln_b, tm=256):
    m, d = x1.shape
    big = pl.BlockSpec((tm, d), lambda i: (i, 0))
    first = pl.BlockSpec((tm, d // 2), lambda i: (i, 0))
    second = pl.BlockSpec((tm, d // 2), lambda i: (m // tm + i, 0))
    row_vec = pl.BlockSpec((1, d), lambda i: (0, 0))
    return pl.pallas_call(
        _combine_kernel,
        grid=(m // tm,),
        in_specs=[big, first, second, pl.BlockSpec((tm, LANES), lambda i: (i, 0)), row_vec, row_vec],
        out_specs=big,
        out_shape=jax.ShapeDtypeStruct((m, d), F32),
        scratch_shapes=[pltpu.VMEM((tm, d), F32)],
        compiler_params=_params("arbitrary"),
        name="combine_ln",
    )(x1, ys, ys, wts, ln_g, ln_b)


def _plan_blocks(eid, tm):
    n = eid.shape[0]
    a = n * TOP_K
    expert_id = eid.reshape(-1)
    onehot = (expert_id[:, None] == jnp.arange(N_EXPERTS, dtype=jnp.int32)[None, :]).astype(jnp.int32)
    csum = jnp.cumsum(onehot, axis=0)
    counts = csum[-1]
    rank = jnp.take_along_axis(csum, expert_id[:, None], axis=1)[:, 0] - 1
    padded = (counts + tm - 1) // tm * tm
    pad_ends = jnp.cumsum(padded)
    pad_starts = pad_ends - padded
    n_blocks = a // tm + N_EXPERTS
    pos = pad_starts[expert_id] + rank
    spare = a + jnp.arange(n_blocks * tm, dtype=jnp.int32) % (2 * tm)
    flat = jnp.arange(a, dtype=jnp.int32)
    aid_pad = spare.at[pos].set((flat % TOP_K) * n + flat // TOP_K)
    tok_pad = jnp.where(aid_pad < a, aid_pad % n, 0)
    block_start = jnp.arange(n_blocks, dtype=jnp.int32) * tm
    block_expert = jnp.minimum(jnp.searchsorted(pad_ends, block_start, side='right'), N_EXPERTS - 1).astype(jnp.int32)
    block_count = jnp.clip(counts[block_expert] - (block_start - pad_starts[block_expert]), 0, tm)
    block_rows = ((block_count + SUBLANES - 1) // SUBLANES * SUBLANES).astype(jnp.int32)
    n_real = (pad_ends[-1] // tm).astype(jnp.int32).reshape(1)
    return tok_pad.reshape(n_blocks, 1, tm), aid_pad.reshape(n_blocks, 1, tm), block_expert, block_rows, n_real


def kernel(x, w_in, w_out, lambda_q1, lambda_k1, lambda_q2, lambda_k2, subln_g, beta_attn, gmlp_ln_g, gmlp_ln_b,
           spatial_w, spatial_b, beta_gmlp, rel_bias, ln1_g, ln1_b, w_group, b_group, w_expert, b_expert,
           w_gate, w_up, w_down, ln2_g, ln2_b):
    b, s, d = x.shape
    n = b * s
    d_attn = ATTN_HEADS * ATTN_V_DIM
    d_gmlp = d - d_attn
    moe_tm = 320
    for l in range(DEPTH):
        lambda_init = 0.8 - 0.6 * math.exp(-0.3 * l)
        x2 = x.reshape(n, d)
        proj = _proj_matmul(x2.astype(BF16), w_in[l].astype(BF16))
        attn = _diff_attention(proj.reshape(b, s, -1), rel_bias, lambda_q1[l][None], lambda_k1[l][None],
                               lambda_q2[l][None], lambda_k2[l][None], subln_g[l][None], beta_attn[l][None],
                               lambda_init)
        gm = _spatial_gating(proj, gmlp_ln_g[l][None], gmlp_ln_b[l][None], spatial_w[l], spatial_b[l],
                             beta_gmlp[l][None], d_gmlp, u_col=3 * d_attn // d_gmlp, g_col=3 * d_attn // d_gmlp + 1)
        n_route = N_GROUPS + N_EXPERTS
        w_route = jnp.concatenate([w_group[l], w_expert[l].reshape(d, N_EXPERTS),
                                   jnp.zeros((d, LANES - n_route), F32)], axis=1)
        b_route = jnp.concatenate([b_group[l], b_expert[l].reshape(-1), jnp.zeros((LANES - n_route,), F32)])[None]
        x1, x1p, logits = _out_proj_ln_router(attn.reshape(n, d_attn), gm, w_out[l].astype(BF16), x2,
                                         ln1_g[l][None], ln1_b[l][None], w_route, b_route)
        eid, wts = _route(logits)
        plan = _plan_blocks(eid[:, :TOP_K], moe_tm)
        ys = _expert_mlp(x1p, plan, w_gate[l], w_up[l], w_down[l], moe_tm)
        x = _combine_ln(x1, ys, wts, ln2_g[l][None], ln2_b[l][None]).reshape(b, s, d)
    return x
```

```python
import functools
import math

import jax
import jax.numpy as jnp
from jax import lax
from jax.experimental import pallas as pl
from jax.experimental.pallas import tpu as pltpu

F32 = jnp.float32
BF16 = jnp.bfloat16

ATTN_HEADS = 8
ATTN_HEAD_DIM = 128
ATTN_V_DIM = 2 * ATTN_HEAD_DIM
GMLP_HEADS = 8
CHUNK = 128
REL_BUCKETS = 32
REL_MAX_DIST = 128
N_GROUPS = 8
EXPERTS_PER_GROUP = 8
N_EXPERTS = N_GROUPS * EXPERTS_PER_GROUP
TOP_K = 2
LN_EPS = 1e-5
DEPTH = 1
DN_ALPHA = (2 * DEPTH) ** 0.25
NEG_INF = -1e30
LOG2E = math.log2(math.e)
LANES = 128
SUBLANES = 8
VMEM_LIMIT = 56 * 1024 * 1024


def _params(*semantics):
    return pltpu.CompilerParams(dimension_semantics=semantics, vmem_limit_bytes=VMEM_LIMIT)


def _matmul_kernel(x_ref, w_ref, o_ref, wb_ref):
    @pl.when(pl.program_id(1) == 0)
    def _():
        wb_ref[...] = w_ref[...].astype(BF16)

    o_ref[...] = jnp.dot(x_ref[...], wb_ref[...], preferred_element_type=F32).astype(o_ref.dtype)


def _proj_matmul(xb, w, tm=512, tn=1024):
    m, k = xb.shape
    n = w.shape[1]
    return pl.pallas_call(
        _matmul_kernel,
        grid=(n // tn, m // tm),
        in_specs=[pl.BlockSpec((tm, k), lambda j, i: (i, 0)),
                  pl.BlockSpec((k, tn), lambda j, i: (0, j))],
        out_specs=pl.BlockSpec((tm, tn), lambda j, i: (i, j)),
        out_shape=jax.ShapeDtypeStruct((m, n), BF16),
        scratch_shapes=[pltpu.VMEM((k, tn), BF16)],
        compiler_params=_params("arbitrary", "arbitrary"),
        name="proj_matmul",
    )(xb, w)


def _rel_bucket(n):
    max_exact = REL_BUCKETS // 2
    nf = jnp.maximum(n, max_exact).astype(F32)
    large = max_exact + (jnp.log(nf / max_exact) / math.log(REL_MAX_DIST / max_exact)
                         * (REL_BUCKETS - max_exact)).astype(jnp.int32)
    large = jnp.minimum(large, REL_BUCKETS - 1)
    return jnp.where(n < max_exact, n, large)


def _rel_bias_tiles(rel_bias, blk):
    pos = jnp.arange(blk)
    tiles = []
    for d in (0, 1):
        bucket = _rel_bucket(jnp.maximum(pos[:, None] + d * blk - pos[None, :], 0))
        hit = bucket[None] == jnp.arange(REL_BUCKETS)[:, None, None]
        tiles.append(jnp.sum(jnp.where(hit[:, None], rel_bias[:, :, None, None], 0.0), axis=0))
    return jnp.stack(tiles, axis=1).astype(F32)


def _attn_kernel(far_ref, q_ref, k_ref, v_ref, bias_ref, lq1_ref, lk1_ref, lq2_ref, lk2_ref,
                 sg_ref, ba_ref, o_ref, mx_ref, sh_ref, mrow_ref, l_ref, s_ref, acc_ref, *, blk, lambda_init):
    h = pl.program_id(1)
    i = pl.program_id(2)
    dh = ATTN_HEAD_DIM
    c1 = ATTN_HEAD_DIM ** -0.5 * LOG2E
    q = q_ref[...]
    qs = (q[:, :dh], q[:, dh:])
    contract_last = (((1,), (1,)), ((), ()))
    far2 = far_ref[h] * LOG2E
    n_far = jnp.maximum(i - 1, 0)

    def lane_chunks(x):
        return [x[:, c * LANES:(c + 1) * LANES] for c in range(x.shape[1] // LANES)]

    def raw_scores(start, width):
        kj = k_ref[pl.ds(start, width), :]
        return [lax.dot_general(qs[mi], kj[:, mi * dh:(mi + 1) * dh], contract_last, preferred_element_type=F32)
                for mi in range(2)]

    def over_far_blocks(body):
        def pair(t, carry):
            body(pl.multiple_of(t * (2 * blk), 2 * blk), 2 * blk)
            return carry
        lax.fori_loop(0, lax.shift_right_logical(n_far, 1), pair, 0)

        @pl.when(lax.rem(n_far, 2) == 1)
        def _():
            body(pl.multiple_of((n_far - 1) * blk, blk), blk)

    mx_ref[...] = jnp.full(mx_ref.shape, -jnp.inf, F32)

    def max_body(start, width):
        s = raw_scores(start, width)
        for mi in range(2):
            best = mx_ref[mi]
            for chunk in lane_chunks(s[mi]):
                best = jnp.maximum(best, chunk)
            mx_ref[mi] = best

    over_far_blocks(max_body)

    row = lax.broadcasted_iota(jnp.int32, (blk, blk), 0)
    col = lax.broadcasted_iota(jnp.int32, (blk, blk), 1)
    near = raw_scores(pl.multiple_of(n_far * blk, blk), blk)
    diag = raw_scores(pl.multiple_of(i * blk, blk), blk)
    for mi in range(2):
        s_near = jnp.where(i >= 1, near[mi] * c1 + bias_ref[1], NEG_INF)
        s_diag = jnp.where(col <= row, diag[mi] * c1 + bias_ref[0], NEG_INF)
        s_ref[mi, 0] = s_near
        s_ref[mi, 1] = s_diag
        best = mx_ref[mi] * c1 + far2
        for chunk in lane_chunks(s_near) + lane_chunks(s_diag):
            best = jnp.maximum(best, chunk)
        m_rows = jnp.broadcast_to(jnp.max(best, axis=-1, keepdims=True), (blk, LANES))
        mrow_ref[mi] = m_rows
        sh_ref[mi] = far2 - m_rows

    l_ref[...] = jnp.zeros(l_ref.shape, F32)
    acc_ref[...] = jnp.zeros(acc_ref.shape, F32)

    def accumulate(p_chunks, vj):
        ps = []
        for mi in range(2):
            tot = l_ref[mi]
            for chunk in p_chunks[mi]:
                tot = tot + chunk
            l_ref[mi] = tot
            ps.append(jnp.concatenate(p_chunks[mi], axis=1).astype(BF16))
        acc_ref[...] += jnp.dot(jnp.concatenate(ps, axis=0), vj, preferred_element_type=F32)

    def pv_body(start, width):
        s = raw_scores(start, width)
        accumulate([[jnp.exp2(chunk * c1 + sh_ref[mi]) for chunk in lane_chunks(s[mi])] for mi in range(2)],
                   v_ref[pl.ds(start, width), :])

    over_far_blocks(pv_body)

    for t, start in ((0, pl.multiple_of(n_far * blk, blk)), (1, pl.multiple_of(i * blk, blk))):
        accumulate([[jnp.exp2(chunk - mrow_ref[mi]) for chunk in lane_chunks(s_ref[mi, t])] for mi in range(2)],
                   v_ref[pl.ds(start, blk), :])

    l = [jnp.sum(l_ref[mi], axis=-1, keepdims=True) for mi in range(2)]
    lam = (jnp.exp(jnp.sum(lq1_ref[...] * lk1_ref[...], axis=-1, keepdims=True))
           - jnp.exp(jnp.sum(lq2_ref[...] * lk2_ref[...], axis=-1, keepdims=True)) + lambda_init)
    o = acc_ref[:blk] / l[0] - lam * (acc_ref[blk:] / l[1])
    o = o * lax.rsqrt(jnp.mean(jnp.square(o), axis=-1, keepdims=True) + LN_EPS) * sg_ref[...]
    o = o * (1.0 - lambda_init) * ba_ref[...]
    o_ref[...] = o.astype(o_ref.dtype)


def _diff_attention(proj3, rel_bias, lq1, lk1, lq2, lk2, subln_g, beta_attn, lambda_init, blk=256):
    b, s, _ = proj3.shape
    h = ATTN_HEADS
    dv = ATTN_V_DIM
    assert REL_BUCKETS // 2 + int(math.log((blk + 1) / (REL_BUCKETS // 2)) / math.log(REL_MAX_DIST / (REL_BUCKETS // 2))
                                  * (REL_BUCKETS // 2)) >= REL_BUCKETS
    tiles = _rel_bias_tiles(rel_bias, blk) * LOG2E
    far = rel_bias[REL_BUCKETS - 1].astype(F32)
    vec = lambda c: pl.BlockSpec((1, c), lambda bi, hi, qi: (0, 0))
    kernel = functools.partial(_attn_kernel, blk=blk, lambda_init=lambda_init)
    return pl.pallas_call(
        kernel,
        grid=(b, h, s // blk),
        in_specs=[
            pl.BlockSpec(memory_space=pltpu.SMEM),
            pl.BlockSpec((None, blk, dv), lambda bi, hi, qi: (bi, qi, hi)),
            pl.BlockSpec((None, s, dv), lambda bi, hi, qi: (bi, 0, h + hi)),
            pl.BlockSpec((None, s, dv), lambda bi, hi, qi: (bi, 0, 2 * h + hi)),
            pl.BlockSpec((None, 2, blk, blk), lambda bi, hi, qi: (hi, 0, 0, 0)),
            vec(ATTN_HEAD_DIM), vec(ATTN_HEAD_DIM), vec(ATTN_HEAD_DIM), vec(ATTN_HEAD_DIM),
            vec(dv),
            pl.BlockSpec((1, dv), lambda bi, hi, qi: (0, hi)),
        ],
        out_specs=pl.BlockSpec((None, blk, dv), lambda bi, hi, qi: (bi, qi, hi)),
        out_shape=jax.ShapeDtypeStruct((b, s, h * dv), BF16),
        scratch_shapes=[pltpu.VMEM((2, blk, LANES), F32), pltpu.VMEM((2, blk, LANES), F32),
                        pltpu.VMEM((2, blk, LANES), F32), pltpu.VMEM((2, blk, LANES), F32),
                        pltpu.VMEM((2, 2, blk, blk), F32), pltpu.VMEM((2 * blk, dv), F32)],
        compiler_params=_params("arbitrary", "arbitrary", "arbitrary"),
        name="diff_attention",
    )(far, proj3, proj3, proj3, tiles, lq1, lk1, lq2, lk2, subln_g, beta_attn)


def _gelu(x):
    return 0.5 * x * (1.0 + lax.erf(x * (2.0 ** -0.5)))


def _gating_kernel(u_ref, g_ref, lg_ref, lb_ref, ws_ref, bs_ref, bg_ref, o_ref, buf_ref):
    hd = u_ref.shape[1] // GMLP_HEADS
    g = _gelu(g_ref[...].astype(F32))
    mu = jnp.mean(g, axis=-1, keepdims=True)
    gc = g - mu
    var = jnp.mean(jnp.square(gc), axis=-1, keepdims=True)
    v = (gc * lax.rsqrt(var + LN_EPS) * lg_ref[...] + lb_ref[...]).astype(BF16)
    row = lax.broadcasted_iota(jnp.int32, (CHUNK, CHUNK), 0)
    col = lax.broadcasted_iota(jnp.int32, (CHUNK, CHUNK), 1)
    tril = col <= row
    ss = jnp.zeros((CHUNK, 1), F32)
    for hh in range(GMLP_HEADS):
        w = jnp.where(tril, ws_ref[hh], 0.0).astype(BF16)
        mixed = jnp.dot(w, v[:, hh * hd:(hh + 1) * hd], preferred_element_type=F32) + bs_ref[hh]
        out = _gelu(u_ref[:, hh * hd:(hh + 1) * hd].astype(F32)) * mixed
        ss = ss + jnp.sum(jnp.square(out), axis=-1, keepdims=True)
        buf_ref[:, hh * hd:(hh + 1) * hd] = out
    rstd = lax.rsqrt(ss / u_ref.shape[1] + LN_EPS)
    o_ref[...] = (buf_ref[...] * rstd * bg_ref[...]).astype(o_ref.dtype)


def _spatial_gating(proj2, ln_g, ln_b, w_s, b_s, beta_gmlp, d_gmlp, u_col, g_col):
    n = proj2.shape[0]
    row_vec = pl.BlockSpec((1, d_gmlp), lambda c: (0, 0))
    return pl.pallas_call(
        _gating_kernel,
        grid=(n // CHUNK,),
        in_specs=[
            pl.BlockSpec((CHUNK, d_gmlp), lambda c: (c, u_col)),
            pl.BlockSpec((CHUNK, d_gmlp), lambda c: (c, g_col)),
            row_vec, row_vec,
            pl.BlockSpec((GMLP_HEADS, CHUNK, CHUNK), lambda c: (0, 0, 0)),
            pl.BlockSpec((GMLP_HEADS, CHUNK, 1), lambda c: (0, 0, 0)),
            row_vec,
        ],
        out_specs=pl.BlockSpec((CHUNK, d_gmlp), lambda c: (c, 0)),
        out_shape=jax.ShapeDtypeStruct((n, d_gmlp), BF16),
        scratch_shapes=[pltpu.VMEM((CHUNK, d_gmlp), F32)],
        compiler_params=_params("arbitrary"),
        name="spatial_gating",
    )(proj2, proj2, ln_g, ln_b, w_s, b_s[:, :, None], beta_gmlp)


def _layer_norm_chunks(buf_ref, n_chunks, d):
    tot = jnp.sum(buf_ref[0], axis=-1, keepdims=True)
    for c in range(1, n_chunks):
        tot = tot + jnp.sum(buf_ref[c], axis=-1, keepdims=True)
    mu = tot / d
    sq = jnp.sum(jnp.square(buf_ref[0] - mu), axis=-1, keepdims=True)
    for c in range(1, n_chunks):
        sq = sq + jnp.sum(jnp.square(buf_ref[c] - mu), axis=-1, keepdims=True)
    return mu, lax.rsqrt(sq / d + LN_EPS)


def _pack_bf16_pair(lo, hi):
    lo_bits = lax.bitcast_convert_type(lo.astype(BF16).astype(F32), jnp.uint32) >> 16
    hi_bits = lax.bitcast_convert_type(hi.astype(BF16).astype(F32), jnp.uint32) & jnp.uint32(0xFFFF0000)
    return hi_bits | lo_bits


def _unpack_bf16_pair(words):
    lo = lax.bitcast_convert_type(words << 16, F32)
    hi = lax.bitcast_convert_type(words & jnp.uint32(0xFFFF0000), F32)
    return lo, hi


def _out_proj_kernel(a_ref, g_ref, w_ref, x_ref, lg_ref, lb_ref, wr_ref, br_ref,
                     x1_ref, x1p_ref, lo_ref, buf_ref, *, n_chunks, tn):
    j = pl.program_id(1)
    ka = a_ref.shape[1]
    mix = (jnp.dot(a_ref[...], w_ref[:ka, :], preferred_element_type=F32)
           + jnp.dot(g_ref[...], w_ref[ka:, :], preferred_element_type=F32))
    buf_ref[j] = DN_ALPHA * x_ref[...] + mix

    @pl.when(j == n_chunks - 1)
    def _():
        d = n_chunks * tn
        mu, rstd = _layer_norm_chunks(buf_ref, n_chunks, d)
        logits = jnp.zeros(lo_ref.shape, F32) + br_ref[...]
        half_chunks = n_chunks // 2
        for c in range(half_chunks):
            ys = []
            for cc in (c, c + half_chunks):
                cs = slice(cc * tn, (cc + 1) * tn)
                y = (buf_ref[cc] - mu) * rstd * lg_ref[:, cs] + lb_ref[:, cs]
                x1_ref[:, cs] = y
                y_hi = y.astype(BF16)
                y_lo = (y - y_hi.astype(F32)).astype(BF16)
                logits = logits + (jnp.dot(y_hi, wr_ref[0, cs, :], preferred_element_type=F32)
                                   + (jnp.dot(y_lo, wr_ref[0, cs, :], preferred_element_type=F32)
                                      + jnp.dot(y_hi, wr_ref[1, cs, :], preferred_element_type=F32)))
                ys.append(y)
            x1p_ref[:, c * tn:(c + 1) * tn] = _pack_bf16_pair(ys[0], ys[1])
        lo_ref[...] = logits


def _out_proj_ln_router(attn2, gm2, wb, x2, ln_g, ln_b, w_route, b_route, tm=512, tn=256):
    m, d = x2.shape
    ka = attn2.shape[1]
    kg = gm2.shape[1]
    n_chunks = d // tn
    row_vec = pl.BlockSpec((1, d), lambda i, j: (0, 0))
    kernel = functools.partial(_out_proj_kernel, n_chunks=n_chunks, tn=tn)
    return pl.pallas_call(
        kernel,
        grid=(m // tm, n_chunks),
        in_specs=[
            pl.BlockSpec((tm, ka), lambda i, j: (i, 0)),
            pl.BlockSpec((tm, kg), lambda i, j: (i, 0)),
            pl.BlockSpec((ka + kg, tn), lambda i, j: (0, j)),
            pl.BlockSpec((tm, tn), lambda i, j: (i, j)),
            row_vec, row_vec,
            pl.BlockSpec((2, d, LANES), lambda i, j: (0, 0, 0)),
            pl.BlockSpec((1, LANES), lambda i, j: (0, 0)),
        ],
        out_specs=[
            pl.BlockSpec((tm, d), lambda i, j: (i, 0)),
            pl.BlockSpec((tm, d // 2), lambda i, j: (i, 0)),
            pl.BlockSpec((tm, LANES), lambda i, j: (i, 0)),
        ],
        out_shape=[jax.ShapeDtypeStruct((m, d), F32), jax.ShapeDtypeStruct((m, d // 2), jnp.uint32),
                   jax.ShapeDtypeStruct((m, LANES), F32)],
        scratch_shapes=[pltpu.VMEM((n_chunks, tm, tn), F32)],
        compiler_params=_params("arbitrary", "arbitrary"),
        name="out_proj_ln_router",
    )(attn2, gm2, wb, x2, ln_g, ln_b, w_route, b_route)


def _route_kernel(lo_ref, eid_ref, wt_ref):
    lg = lo_ref[...]
    lane = lax.broadcasted_iota(jnp.int32, lg.shape, 1)
    lane_f = lane.astype(F32)
    none = float(LANES)
    first = lambda hit: jnp.min(jnp.where(hit, lane_f, none), axis=-1, keepdims=True)

    in_groups = lane < N_GROUPS
    g_logits = jnp.where(in_groups, lg, -jnp.inf)
    g_max = jnp.max(g_logits, axis=-1, keepdims=True)
    g_idx = first(g_logits == g_max)
    g_w = 1.0 / jnp.sum(jnp.where(in_groups, jnp.exp(lg - g_max), 0.0), axis=-1, keepdims=True)

    lo = N_GROUPS + g_idx * EXPERTS_PER_GROUP
    in_group = (lane_f >= lo) & (lane_f < lo + EXPERTS_PER_GROUP)
    e_logits = jnp.where(in_group, lg, -jnp.inf)
    t1 = jnp.max(e_logits, axis=-1, keepdims=True)
    i1 = first(e_logits == t1)
    e_rest = jnp.where(lane_f == i1, -jnp.inf, e_logits)
    t2 = jnp.max(e_rest, axis=-1, keepdims=True)
    i2 = first(e_rest == t2)
    ex = jnp.exp(t2 - t1)
    w1 = g_w * (1.0 / (1.0 + ex))
    w2 = g_w * (ex / (1.0 + ex))
    e1 = (i1 - N_GROUPS).astype(jnp.int32)
    e2 = (i2 - N_GROUPS).astype(jnp.int32)
    eid_ref[...] = jnp.where(lane == 0, e1, jnp.where(lane == 1, e2, 0))
    wt_ref[...] = jnp.where(lane == 0, w1, jnp.where(lane == 1, w2, 0.0))


def _route(logits, tm=1024):
    m = logits.shape[0]
    spec = pl.BlockSpec((tm, LANES), lambda i: (i, 0))
    return pl.pallas_call(
        _route_kernel,
        grid=(m // tm,),
        in_specs=[spec],
        out_specs=[spec, spec],
        out_shape=[jax.ShapeDtypeStruct((m, LANES), jnp.int32), jax.ShapeDtypeStruct((m, LANES), F32)],
        compiler_params=_params("arbitrary"),
        name="route",
    )(logits)


def _expert_kernel(be_ref, cnt_ref, nreal_ref, tok_ref, aid_ref, x_hbm, wg_ref, wu_ref, wd_ref, ys_hbm,
                   xbuf, xb, gp, up, hb, acc, obuf, gsem, ssem, *, tm, nf):
    half = xb.shape[1] // 2
    fh = hb.shape[1] // 2
    b = pl.program_id(0)
    f = pl.program_id(1)
    nreal = nreal_ref[0]
    slot = lax.rem(b, 2)

    def rows_of(blk):
        return pl.multiple_of(cnt_ref[blk], SUBLANES)

    def for_each_row(n_rows, copy_of_row):
        def body(grp, carry):
            base = pl.multiple_of(grp * SUBLANES, SUBLANES)
            for k in range(SUBLANES):
                copy_of_row(base + k).start(priority=k % 2)
            return carry
        lax.fori_loop(0, lax.shift_right_logical(n_rows, SUBLANES.bit_length() - 1), body, 0)

    def start_gather(slot_, n_rows):
        for_each_row(n_rows, lambda r: pltpu.make_async_copy(
            x_hbm.at[pl.ds(tok_ref[0, r], 1)], xbuf.at[slot_, pl.ds(r, 1)], gsem.at[slot_]))

    def wait_gather(slot_, n_rows):
        pltpu.make_async_copy(x_hbm.at[pl.ds(0, n_rows)], xbuf.at[slot_, pl.ds(0, n_rows)], gsem.at[slot_]).wait()

    def start_scatter(slot_, n_rows):
        for_each_row(n_rows, lambda r: pltpu.make_async_copy(
            obuf.at[slot_, pl.ds(r, 1)], ys_hbm.at[pl.ds(aid_ref[0, r], 1)], ssem.at[slot_]))

    def wait_scatter(slot_, n_rows):
        pltpu.make_async_copy(obuf.at[slot_, pl.ds(0, n_rows)], ys_hbm.at[pl.ds(0, n_rows)], ssem.at[slot_]).wait()

    @pl.when(b < nreal)
    def _():
        @pl.when(f == 0)
        def _():
            @pl.when(b == 0)
            def _():
                xbuf[...] = jnp.zeros(xbuf.shape, xbuf.dtype)
                start_gather(0, rows_of(0))
            wait_gather(slot, rows_of(b))
            lo, hi = _unpack_bf16_pair(xbuf[slot])
            xb[:, :half] = lo.astype(BF16)
            xb[:, half:] = hi.astype(BF16)
            gp[...] = jnp.dot(xb[:, :half], wg_ref[...].astype(BF16), preferred_element_type=F32)
            up[...] = jnp.dot(xb[:, :half], wu_ref[...].astype(BF16), preferred_element_type=F32)

        @pl.when(f == 1)
        def _():
            g = gp[...] + jnp.dot(xb[:, half:], wg_ref[...].astype(BF16), preferred_element_type=F32)
            u = up[...] + jnp.dot(xb[:, half:], wu_ref[...].astype(BF16), preferred_element_type=F32)
            hb[...] = (g * (1.0 / (1.0 + jnp.exp(-g))) * u).astype(BF16)
            acc[...] = jnp.dot(hb[:, :fh], wd_ref[...].astype(BF16), preferred_element_type=F32)

        @pl.when(f == nf - 1)
        def _():
            @pl.when(b >= 1)
            def _():
                wait_scatter(1 - slot, rows_of(b - 1))
            total = acc[...] + jnp.dot(hb[:, fh:], wd_ref[...].astype(BF16), preferred_element_type=F32)
            obuf[slot] = _pack_bf16_pair(total[:, :half], total[:, half:])
            start_scatter(slot, rows_of(b))

            @pl.when(b + 1 < nreal)
            def _():
                start_gather(1 - slot, rows_of(b + 1))

            @pl.when(b + 1 == nreal)
            def _():
                wait_scatter(slot, rows_of(b))


def _expert_mlp(x1p, plan, w_gate, w_up, w_down, tm):
    tok_blocks, aid_blocks, block_expert, block_rows, n_real = plan
    n, dp = x1p.shape
    d = 2 * dp
    n_blocks = tok_blocks.shape[0]
    d_ff = w_gate.shape[2]
    nf = 3

    def live(b, f, nreal):
        is_real = b < nreal[0]
        return jnp.where(is_real, b, nreal[0] - 1), jnp.where(is_real, f, nf - 1)

    def tok_map(b, f, be, cnt, nreal):
        bb, ff = live(b, f, nreal)
        return jnp.minimum(bb + (ff == nf - 1).astype(jnp.int32), nreal[0] - 1), 0, 0

    def aid_map(b, f, be, cnt, nreal):
        bb, _ = live(b, f, nreal)
        return bb, 0, 0

    def w_in_map(b, f, be, cnt, nreal):
        bb, ff = live(b, f, nreal)
        return be[bb], jnp.minimum(ff, 1), 0

    def w_out_map(b, f, be, cnt, nreal):
        bb, ff = live(b, f, nreal)
        return be[bb], jnp.maximum(ff - 1, 0), 0

    grid_spec = pltpu.PrefetchScalarGridSpec(
        num_scalar_prefetch=3,
        grid=(n_blocks, nf),
        in_specs=[
            pl.BlockSpec((None, 1, tm), tok_map, memory_space=pltpu.SMEM),
            pl.BlockSpec((None, 1, tm), aid_map, memory_space=pltpu.SMEM),
            pl.BlockSpec(memory_space=pl.ANY),
            pl.BlockSpec((None, dp, d_ff), w_in_map),
            pl.BlockSpec((None, dp, d_ff), w_in_map),
            pl.BlockSpec((None, d_ff // 2, d), w_out_map),
        ],
        out_specs=pl.BlockSpec(memory_space=pl.ANY),
        scratch_shapes=[pltpu.VMEM((2, tm, dp), jnp.uint32), pltpu.VMEM((tm, d), BF16),
                        pltpu.VMEM((tm, d_ff), F32), pltpu.VMEM((tm, d_ff), F32), pltpu.VMEM((tm, d_ff), BF16),
                        pltpu.VMEM((tm, d), F32), pltpu.VMEM((2, tm, dp), jnp.uint32),
                        pltpu.SemaphoreType.DMA((2,)), pltpu.SemaphoreType.DMA((2,))],
    )
    kernel = functools.partial(_expert_kernel, tm=tm, nf=nf)
    return pl.pallas_call(
        kernel,
        grid_spec=grid_spec,
        out_shape=jax.ShapeDtypeStruct((n * TOP_K + 2 * tm, dp), jnp.uint32),
        compiler_params=_params("arbitrary", "arbitrary"),
        name="expert_mlp",
    )(block_expert, block_rows, n_real, tok_blocks, aid_blocks, x1p, w_gate, w_up, w_down)


def _combine_kernel(x1_ref, y0_ref, y1_ref, wt_ref, lg_ref, lb_ref, o_ref, buf_ref):
    half = x1_ref.shape[1] // 2
    wt = wt_ref[...]
    y0 = _unpack_bf16_pair(y0_ref[...])
    y1 = _unpack_bf16_pair(y1_ref[...])
    for p, cs in enumerate((slice(0, half), slice(half, 2 * half))):
        buf_ref[:, cs] = DN_ALPHA * x1_ref[:, cs] + (wt[:, 0:1] * y0[p] + wt[:, 1:2] * y1[p])
    hsum = buf_ref[...]
    mu = jnp.mean(hsum, axis=-1, keepdims=True)
    hc = hsum - mu
    var = jnp.mean(jnp.square(hc), axis=-1, keepdims=True)
    o_ref[...] = hc * lax.rsqrt(var + LN_EPS) * lg_ref[...] + lb_ref[...]


def _combine_ln(x1, ys, wts, ln_g, ln_b, tm=256):
    m, d = x1.shape
    big = pl.BlockSpec((tm, d), lambda i: (i, 0))
    first = pl.BlockSpec((tm, d // 2), lambda i: (i, 0))
    second = pl.BlockSpec((tm, d // 2), lambda i: (m // tm + i, 0))
    row_vec = pl.BlockSpec((1, d), lambda i: (0, 0))
    return pl.pallas_call(
        _combine_kernel,
        grid=(m // tm,),
        in_specs=[big, first, second, pl.BlockSpec((tm, LANES), lambda i: (i, 0)), row_vec, row_vec],
        out_specs=big,
        out_shape=jax.ShapeDtypeStruct((m, d), F32),
        scratch_shapes=[pltpu.VMEM((tm, d), F32)],
        compiler_params=_params("arbitrary"),
        name="combine_ln",
    )(x1, ys, ys, wts, ln_g, ln_b)


def _plan_blocks(eid, tm):
    n = eid.shape[0]
    a = n * TOP_K
    expert_id = eid.reshape(-1)
    onehot = (expert_id[:, None] == jnp.arange(N_EXPERTS, dtype=jnp.int32)[None, :]).astype(jnp.int32)
    csum = jnp.cumsum(onehot, axis=0)
    counts = csum[-1]
    rank = jnp.take_along_axis(csum, expert_id[:, None], axis=1)[:, 0] - 1
    padded = (counts + tm - 1) // tm * tm
    pad_ends = jnp.cumsum(padded)
    pad_starts = pad_ends - padded
    n_blocks = a // tm + N_EXPERTS
    pos = pad_starts[expert_id] + rank
    spare = a + jnp.arange(n_blocks * tm, dtype=jnp.int32) % (2 * tm)
    flat = jnp.arange(a, dtype=jnp.int32)
    aid_pad = spare.at[pos].set((flat % TOP_K) * n + flat // TOP_K)
    tok_pad = jnp.where(aid_pad < a, aid_pad % n, 0)
    block_start = jnp.arange(n_blocks, dtype=jnp.int32) * tm
    block_expert = jnp.minimum(jnp.searchsorted(pad_ends, block_start, side='right'), N_EXPERTS - 1).astype(jnp.int32)
    block_count = jnp.clip(counts[block_expert] - (block_start - pad_starts[block_expert]), 0, tm)
    block_rows = ((block_count + SUBLANES - 1) // SUBLANES * SUBLANES).astype(jnp.int32)
    n_real = (pad_ends[-1] // tm).astype(jnp.int32).reshape(1)
    return tok_pad.reshape(n_blocks, 1, tm), aid_pad.reshape(n_blocks, 1, tm), block_expert, block_rows, n_real


def kernel(x, w_in, w_out, lambda_q1, lambda_k1, lambda_q2, lambda_k2, subln_g, beta_attn, gmlp_ln_g, gmlp_ln_b,
           spatial_w, spatial_b, beta_gmlp, rel_bias, ln1_g, ln1_b, w_group, b_group, w_expert, b_expert,
           w_gate, w_up, w_down, ln2_g, ln2_b):
    b, s, d = x.shape
    n = b * s
    d_attn = ATTN_HEADS * ATTN_V_DIM
    d_gmlp = d - d_attn
    moe_tm = 320
    for l in range(DEPTH):
        lambda_init = 0.8 - 0.6 * math.exp(-0.3 * l)
        x2 = x.reshape(n, d)
        proj = _proj_matmul(x2.astype(BF16), w_in[l])
        attn = _diff_attention(proj.reshape(b, s, -1), rel_bias, lambda_q1[l][None], lambda_k1[l][None],
                               lambda_q2[l][None], lambda_k2[l][None], subln_g[l][None], beta_attn[l][None],
                               lambda_init)
        gm = _spatial_gating(proj, gmlp_ln_g[l][None], gmlp_ln_b[l][None], spatial_w[l], spatial_b[l],
                             beta_gmlp[l][None], d_gmlp, u_col=3 * d_attn // d_gmlp, g_col=3 * d_attn // d_gmlp + 1)
        n_route = N_GROUPS + N_EXPERTS
        w_route = jnp.concatenate([w_group[l], w_expert[l].reshape(d, N_EXPERTS),
                                   jnp.zeros((d, LANES - n_route), F32)], axis=1)
        w_route_hi = w_route.astype(BF16)
        w_route = jnp.stack([w_route_hi, (w_route - w_route_hi.astype(F32)).astype(BF16)])
        b_route = jnp.concatenate([b_group[l], b_expert[l].reshape(-1), jnp.zeros((LANES - n_route,), F32)])[None]
        x1, x1p, logits = _out_proj_ln_router(attn.reshape(n, d_attn), gm, w_out[l].astype(BF16), x2,
                                         ln1_g[l][None], ln1_b[l][None], w_route, b_route)
        eid, wts = _route(logits)
        plan = _plan_blocks(eid[:, :TOP_K], moe_tm)
        ys = _expert_mlp(x1p, plan, w_gate[l], w_up[l], w_down[l], moe_tm)
        x = _combine_ln(x1, ys, wts, ln2_g[l][None], ln2_b[l][None]).reshape(b, s, d)
    return x
```

```python
import functools
import math

import jax
import jax.numpy as jnp
from jax import lax
from jax.experimental import pallas as pl
from jax.experimental.pallas import tpu as pltpu

F32 = jnp.float32
BF16 = jnp.bfloat16

ATTN_HEADS = 8
ATTN_HEAD_DIM = 128
ATTN_V_DIM = 2 * ATTN_HEAD_DIM
GMLP_HEADS = 8
CHUNK = 128
REL_BUCKETS = 32
REL_MAX_DIST = 128
N_GROUPS = 8
EXPERTS_PER_GROUP = 8
N_EXPERTS = N_GROUPS * EXPERTS_PER_GROUP
TOP_K = 2
LN_EPS = 1e-5
DEPTH = 1
DN_ALPHA = (2 * DEPTH) ** 0.25
NEG_INF = -1e30
LOG2E = math.log2(math.e)
LANES = 128
SUBLANES = 8
VMEM_LIMIT = 56 * 1024 * 1024
X1_PACK_GROUP = 256


def _params(*semantics):
    return pltpu.CompilerParams(dimension_semantics=semantics, vmem_limit_bytes=VMEM_LIMIT)


def _matmul_kernel(x_ref, w_ref, o_ref, wb_ref):
    @pl.when(pl.program_id(1) == 0)
    def _():
        wb_ref[...] = w_ref[...].astype(BF16)

    o_ref[...] = jnp.dot(x_ref[...], wb_ref[...], preferred_element_type=F32).astype(o_ref.dtype)


def _proj_matmul(xb, w, tm=512, tn=1024):
    m, k = xb.shape
    n = w.shape[1]
    return pl.pallas_call(
        _matmul_kernel,
        grid=(n // tn, m // tm),
        in_specs=[pl.BlockSpec((tm, k), lambda j, i: (i, 0)),
                  pl.BlockSpec((k, tn), lambda j, i: (0, j))],
        out_specs=pl.BlockSpec((tm, tn), lambda j, i: (i, j)),
        out_shape=jax.ShapeDtypeStruct((m, n), BF16),
        scratch_shapes=[pltpu.VMEM((k, tn), BF16)],
        compiler_params=_params("arbitrary", "arbitrary"),
        name="proj_matmul",
    )(xb, w)


def _rel_bucket(n):
    max_exact = REL_BUCKETS // 2
    nf = jnp.maximum(n, max_exact).astype(F32)
    large = max_exact + (jnp.log(nf / max_exact) / math.log(REL_MAX_DIST / max_exact)
                         * (REL_BUCKETS - max_exact)).astype(jnp.int32)
    large = jnp.minimum(large, REL_BUCKETS - 1)
    return jnp.where(n < max_exact, n, large)


def _rel_bias_tiles(rel_bias, blk):
    pos = jnp.arange(blk)
    tiles = []
    for d in (0, 1):
        bucket = _rel_bucket(jnp.maximum(pos[:, None] + d * blk - pos[None, :], 0))
        hit = bucket[None] == jnp.arange(REL_BUCKETS)[:, None, None]
        tiles.append(jnp.sum(jnp.where(hit[:, None], rel_bias[:, :, None, None], 0.0), axis=0))
    return jnp.stack(tiles, axis=1).astype(F32)


def _attn_kernel(far_ref, q_ref, k_ref, v_ref, bias_ref, lq1_ref, lk1_ref, lq2_ref, lk2_ref,
                 sg_ref, ba_ref, o_ref, mx_ref, sh_ref, mrow_ref, l_ref, s_ref, acc_ref, *, blk, lambda_init):
    h = pl.program_id(1)
    i = pl.program_id(2)
    dh = ATTN_HEAD_DIM
    c1 = ATTN_HEAD_DIM ** -0.5 * LOG2E
    q = q_ref[...]
    qs = (q[:, :dh], q[:, dh:])
    contract_last = (((1,), (1,)), ((), ()))
    far2 = far_ref[h] * LOG2E
    n_far = jnp.maximum(i - 1, 0)

    def lane_chunks(x):
        return [x[:, c * LANES:(c + 1) * LANES] for c in range(x.shape[1] // LANES)]

    def raw_scores(start, width):
        kj = k_ref[pl.ds(start, width), :]
        return [lax.dot_general(qs[mi], kj[:, mi * dh:(mi + 1) * dh], contract_last, preferred_element_type=F32)
                for mi in range(2)]

    def over_far_blocks(body):
        def pair(t, carry):
            body(pl.multiple_of(t * (2 * blk), 2 * blk), 2 * blk)
            return carry
        lax.fori_loop(0, lax.shift_right_logical(n_far, 1), pair, 0)

        @pl.when(lax.rem(n_far, 2) == 1)
        def _():
            body(pl.multiple_of((n_far - 1) * blk, blk), blk)

    mx_ref[...] = jnp.full(mx_ref.shape, -jnp.inf, F32)

    def max_body(start, width):
        s = raw_scores(start, width)
        for mi in range(2):
            best = mx_ref[mi]
            for chunk in lane_chunks(s[mi]):
                best = jnp.maximum(best, chunk)
            mx_ref[mi] = best

    over_far_blocks(max_body)

    row = lax.broadcasted_iota(jnp.int32, (blk, blk), 0)
    col = lax.broadcasted_iota(jnp.int32, (blk, blk), 1)
    near = raw_scores(pl.multiple_of(n_far * blk, blk), blk)
    diag = raw_scores(pl.multiple_of(i * blk, blk), blk)
    for mi in range(2):
        s_near = jnp.where(i >= 1, near[mi] * c1 + bias_ref[1], NEG_INF)
        s_diag = jnp.where(col <= row, diag[mi] * c1 + bias_ref[0], NEG_INF)
        s_ref[mi, 0] = s_near
        s_ref[mi, 1] = s_diag
        best = mx_ref[mi] * c1 + far2
        for chunk in lane_chunks(s_near) + lane_chunks(s_diag):
            best = jnp.maximum(best, chunk)
        m_rows = jnp.broadcast_to(jnp.max(best, axis=-1, keepdims=True), (blk, LANES))
        mrow_ref[mi] = m_rows
        sh_ref[mi] = far2 - m_rows

    l_ref[...] = jnp.zeros(l_ref.shape, F32)
    acc_ref[...] = jnp.zeros(acc_ref.shape, F32)

    def accumulate(p_chunks, vj):
        ps = []
        for mi in range(2):
            tot = l_ref[mi]
            for chunk in p_chunks[mi]:
                tot = tot + chunk
            l_ref[mi] = tot
            ps.append(jnp.concatenate(p_chunks[mi], axis=1).astype(BF16))
        acc_ref[...] += jnp.dot(jnp.concatenate(ps, axis=0), vj, preferred_element_type=F32)

    def pv_body(start, width):
        s = raw_scores(start, width)
        accumulate([[jnp.exp2(chunk * c1 + sh_ref[mi]) for chunk in lane_chunks(s[mi])] for mi in range(2)],
                   v_ref[pl.ds(start, width), :])

    over_far_blocks(pv_body)

    for t, start in ((0, pl.multiple_of(n_far * blk, blk)), (1, pl.multiple_of(i * blk, blk))):
        accumulate([[jnp.exp2(chunk - mrow_ref[mi]) for chunk in lane_chunks(s_ref[mi, t])] for mi in range(2)],
                   v_ref[pl.ds(start, blk), :])

    l = [jnp.sum(l_ref[mi], axis=-1, keepdims=True) for mi in range(2)]
    lam = (jnp.exp(jnp.sum(lq1_ref[...] * lk1_ref[...], axis=-1, keepdims=True))
           - jnp.exp(jnp.sum(lq2_ref[...] * lk2_ref[...], axis=-1, keepdims=True)) + lambda_init)
    o = acc_ref[:blk] / l[0] - lam * (acc_ref[blk:] / l[1])
    o = o * lax.rsqrt(jnp.mean(jnp.square(o), axis=-1, keepdims=True) + LN_EPS) * sg_ref[...]
    o = o * (1.0 - lambda_init) * ba_ref[...]
    o_ref[...] = o.astype(o_ref.dtype)


def _diff_attention(proj3, rel_bias, lq1, lk1, lq2, lk2, subln_g, beta_attn, lambda_init, blk=256):
    b, s, _ = proj3.shape
    h = ATTN_HEADS
    dv = ATTN_V_DIM
    assert REL_BUCKETS // 2 + int(math.log((blk + 1) / (REL_BUCKETS // 2)) / math.log(REL_MAX_DIST / (REL_BUCKETS // 2))
                                  * (REL_BUCKETS // 2)) >= REL_BUCKETS
    tiles = _rel_bias_tiles(rel_bias, blk) * LOG2E
    far = rel_bias[REL_BUCKETS - 1].astype(F32)
    vec = lambda c: pl.BlockSpec((1, c), lambda bi, hi, qi: (0, 0))
    kernel = functools.partial(_attn_kernel, blk=blk, lambda_init=lambda_init)
    return pl.pallas_call(
        kernel,
        grid=(b, h, s // blk),
        in_specs=[
            pl.BlockSpec(memory_space=pltpu.SMEM),
            pl.BlockSpec((None, blk, dv), lambda bi, hi, qi: (bi, qi, hi)),
            pl.BlockSpec((None, s, dv), lambda bi, hi, qi: (bi, 0, h + hi)),
            pl.BlockSpec((None, s, dv), lambda bi, hi, qi: (bi, 0, 2 * h + hi)),
            pl.BlockSpec((None, 2, blk, blk), lambda bi, hi, qi: (hi, 0, 0, 0)),
            vec(ATTN_HEAD_DIM), vec(ATTN_HEAD_DIM), vec(ATTN_HEAD_DIM), vec(ATTN_HEAD_DIM),
            vec(dv),
            pl.BlockSpec((1, dv), lambda bi, hi, qi: (0, hi)),
        ],
        out_specs=pl.BlockSpec((None, blk, dv), lambda bi, hi, qi: (bi, qi, hi)),
        out_shape=jax.ShapeDtypeStruct((b, s, h * dv), BF16),
        scratch_shapes=[pltpu.VMEM((2, blk, LANES), F32), pltpu.VMEM((2, blk, LANES), F32),
                        pltpu.VMEM((2, blk, LANES), F32), pltpu.VMEM((2, blk, LANES), F32),
                        pltpu.VMEM((2, 2, blk, blk), F32), pltpu.VMEM((2 * blk, dv), F32)],
        compiler_params=_params("arbitrary", "arbitrary", "arbitrary"),
        name="diff_attention",
    )(far, proj3, proj3, proj3, tiles, lq1, lk1, lq2, lk2, subln_g, beta_attn)


def _gelu(x):
    return 0.5 * x * (1.0 + lax.erf(x * (2.0 ** -0.5)))


def _gating_kernel(u_ref, g_ref, lg_ref, lb_ref, ws_ref, bs_ref, bg_ref, o_ref, buf_ref):
    hd = u_ref.shape[1] // GMLP_HEADS
    g = _gelu(g_ref[...].astype(F32))
    mu = jnp.mean(g, axis=-1, keepdims=True)
    gc = g - mu
    var = jnp.mean(jnp.square(gc), axis=-1, keepdims=True)
    v = (gc * lax.rsqrt(var + LN_EPS) * lg_ref[...] + lb_ref[...]).astype(BF16)
    row = lax.broadcasted_iota(jnp.int32, (CHUNK, CHUNK), 0)
    col = lax.broadcasted_iota(jnp.int32, (CHUNK, CHUNK), 1)
    tril = col <= row
    ss = jnp.zeros((CHUNK, 1), F32)
    for hh in range(GMLP_HEADS):
        w = jnp.where(tril, ws_ref[hh], 0.0).astype(BF16)
        mixed = jnp.dot(w, v[:, hh * hd:(hh + 1) * hd], preferred_element_type=F32) + bs_ref[hh]
        out = _gelu(u_ref[:, hh * hd:(hh + 1) * hd].astype(F32)) * mixed
        ss = ss + jnp.sum(jnp.square(out), axis=-1, keepdims=True)
        buf_ref[:, hh * hd:(hh + 1) * hd] = out
    rstd = lax.rsqrt(ss / u_ref.shape[1] + LN_EPS)
    o_ref[...] = (buf_ref[...] * rstd * bg_ref[...]).astype(o_ref.dtype)


def _spatial_gating(proj2, ln_g, ln_b, w_s, b_s, beta_gmlp, d_gmlp, u_col, g_col):
    n = proj2.shape[0]
    row_vec = pl.BlockSpec((1, d_gmlp), lambda c: (0, 0))
    return pl.pallas_call(
        _gating_kernel,
        grid=(n // CHUNK,),
        in_specs=[
            pl.BlockSpec((CHUNK, d_gmlp), lambda c: (c, u_col)),
            pl.BlockSpec((CHUNK, d_gmlp), lambda c: (c, g_col)),
            row_vec, row_vec,
            pl.BlockSpec((GMLP_HEADS, CHUNK, CHUNK), lambda c: (0, 0, 0)),
            pl.BlockSpec((GMLP_HEADS, CHUNK, 1), lambda c: (0, 0, 0)),
            row_vec,
        ],
        out_specs=pl.BlockSpec((CHUNK, d_gmlp), lambda c: (c, 0)),
        out_shape=jax.ShapeDtypeStruct((n, d_gmlp), BF16),
        scratch_shapes=[pltpu.VMEM((CHUNK, d_gmlp), F32)],
        compiler_params=_params("arbitrary"),
        name="spatial_gating",
    )(proj2, proj2, ln_g, ln_b, w_s, b_s[:, :, None], beta_gmlp)


def _layer_norm_chunks(buf_ref, n_chunks, d):
    tot = jnp.sum(buf_ref[0], axis=-1, keepdims=True)
    for c in range(1, n_chunks):
        tot = tot + jnp.sum(buf_ref[c], axis=-1, keepdims=True)
    mu = tot / d
    sq = jnp.sum(jnp.square(buf_ref[0] - mu), axis=-1, keepdims=True)
    for c in range(1, n_chunks):
        sq = sq + jnp.sum(jnp.square(buf_ref[c] - mu), axis=-1, keepdims=True)
    return mu, lax.rsqrt(sq / d + LN_EPS)


def _pack_bf16_pair(lo, hi):
    lo_bits = lax.bitcast_convert_type(lo.astype(BF16).astype(F32), jnp.uint32) >> 16
    hi_bits = lax.bitcast_convert_type(hi.astype(BF16).astype(F32), jnp.uint32) & jnp.uint32(0xFFFF0000)
    return hi_bits | lo_bits


def _unpack_bf16_pair(words):
    lo = lax.bitcast_convert_type(words << 16, F32)
    hi = lax.bitcast_convert_type(words & jnp.uint32(0xFFFF0000), F32)
    return lo, hi


def _out_proj_kernel(a_ref, g_ref, w_ref, x_ref, lg_ref, lb_ref, wr_ref, br_ref,
                     x1_ref, x1p_ref, lo_ref, buf_ref, mu_ref, rstd_ref, *, n_chunks, tn):
    j = pl.program_id(1)
    ka = a_ref.shape[1]

    @pl.when(j < n_chunks)
    def _():
        mix = (jnp.dot(a_ref[...], w_ref[:ka, :], preferred_element_type=F32)
               + jnp.dot(g_ref[...], w_ref[ka:, :], preferred_element_type=F32))
        buf_ref[j] = DN_ALPHA * x_ref[...] + mix

    @pl.when(j == n_chunks)
    def _():
        mu, rstd = _layer_norm_chunks(buf_ref, n_chunks, n_chunks * tn)
        mu_ref[...] = mu
        rstd_ref[...] = rstd
        lo_ref[...] = jnp.zeros(lo_ref.shape, F32) + br_ref[...]

    @pl.when(j >= n_chunks)
    def _():
        y = (buf_ref[j - n_chunks] - mu_ref[...]) * rstd_ref[...] * lg_ref[...] + lb_ref[...]
        x1_ref[...] = y
        x1p_ref[...] = _pack_bf16_pair(y[:, :tn // 2], y[:, tn // 2:])
        y_hi = y.astype(BF16)
        y_lo = (y - y_hi.astype(F32)).astype(BF16)
        lo_ref[...] += (jnp.dot(y_hi, wr_ref[0], preferred_element_type=F32)
                        + (jnp.dot(y_lo, wr_ref[0], preferred_element_type=F32)
                           + jnp.dot(y_hi, wr_ref[1], preferred_element_type=F32)))


def _out_proj_ln_router(attn2, gm2, wb, x2, ln_g, ln_b, w_route, b_route, tm=1024, tn=X1_PACK_GROUP):
    m, d = x2.shape
    ka = attn2.shape[1]
    kg = gm2.shape[1]
    n_chunks = d // tn
    mat = lambda j: jnp.minimum(j, n_chunks - 1)
    fin = lambda j: jnp.maximum(j - n_chunks, 0)
    kernel = functools.partial(_out_proj_kernel, n_chunks=n_chunks, tn=tn)
    return pl.pallas_call(
        kernel,
        grid=(m // tm, 2 * n_chunks),
        in_specs=[
            pl.BlockSpec((tm, ka), lambda i, j: (i, 0)),
            pl.BlockSpec((tm, kg), lambda i, j: (i, 0)),
            pl.BlockSpec((ka + kg, tn), lambda i, j: (0, mat(j))),
            pl.BlockSpec((tm, tn), lambda i, j: (i, mat(j))),
            pl.BlockSpec((1, tn), lambda i, j: (0, fin(j))),
            pl.BlockSpec((1, tn), lambda i, j: (0, fin(j))),
            pl.BlockSpec((2, tn, LANES), lambda i, j: (0, fin(j), 0)),
            pl.BlockSpec((1, LANES), lambda i, j: (0, 0)),
        ],
        out_specs=[
            pl.BlockSpec((tm, tn), lambda i, j: (i, fin(j))),
            pl.BlockSpec((tm, tn // 2), lambda i, j: (i, fin(j))),
            pl.BlockSpec((tm, LANES), lambda i, j: (i, 0)),
        ],
        out_shape=[jax.ShapeDtypeStruct((m, d), F32), jax.ShapeDtypeStruct((m, d // 2), jnp.uint32),
                   jax.ShapeDtypeStruct((m, LANES), F32)],
        scratch_shapes=[pltpu.VMEM((n_chunks, tm, tn), F32), pltpu.VMEM((tm, 1), F32), pltpu.VMEM((tm, 1), F32)],
        compiler_params=_params("arbitrary", "arbitrary"),
        name="out_proj_ln_router",
    )(attn2, gm2, wb, x2, ln_g, ln_b, w_route, b_route)


def _route_kernel(lo_ref, eid_ref, wt_ref):
    lg = lo_ref[...]
    lane = lax.broadcasted_iota(jnp.int32, lg.shape, 1)
    lane_f = lane.astype(F32)
    none = float(LANES)
    first = lambda hit: jnp.min(jnp.where(hit, lane_f, none), axis=-1, keepdims=True)

    in_groups = lane < N_GROUPS
    g_logits = jnp.where(in_groups, lg, -jnp.inf)
    g_max = jnp.max(g_logits, axis=-1, keepdims=True)
    g_idx = first(g_logits == g_max)
    g_w = 1.0 / jnp.sum(jnp.where(in_groups, jnp.exp(lg - g_max), 0.0), axis=-1, keepdims=True)

    lo = N_GROUPS + g_idx * EXPERTS_PER_GROUP
    in_group = (lane_f >= lo) & (lane_f < lo + EXPERTS_PER_GROUP)
    e_logits = jnp.where(in_group, lg, -jnp.inf)
    t1 = jnp.max(e_logits, axis=-1, keepdims=True)
    i1 = first(e_logits == t1)
    e_rest = jnp.where(lane_f == i1, -jnp.inf, e_logits)
    t2 = jnp.max(e_rest, axis=-1, keepdims=True)
    i2 = first(e_rest == t2)
    ex = jnp.exp(t2 - t1)
    w1 = g_w * (1.0 / (1.0 + ex))
    w2 = g_w * (ex / (1.0 + ex))
    e1 = (i1 - N_GROUPS).astype(jnp.int32)
    e2 = (i2 - N_GROUPS).astype(jnp.int32)
    eid_ref[...] = jnp.where(lane == 0, e1, jnp.where(lane == 1, e2, 0))
    wt_ref[...] = jnp.where(lane == 0, w1, jnp.where(lane == 1, w2, 0.0))


def _route(logits, tm=1024):
    m = logits.shape[0]
    spec = pl.BlockSpec((tm, LANES), lambda i: (i, 0))
    return pl.pallas_call(
        _route_kernel,
        grid=(m // tm,),
        in_specs=[spec],
        out_specs=[spec, spec],
        out_shape=[jax.ShapeDtypeStruct((m, LANES), jnp.int32), jax.ShapeDtypeStruct((m, LANES), F32)],
        compiler_params=_params("arbitrary"),
        name="route",
    )(logits)


def _expert_kernel(be_ref, cnt_ref, nreal_ref, tok_ref, aid_ref, x_hbm, wg_ref, wu_ref, wd_ref, ys_hbm,
                   xbuf, xb, acc, obuf, gsem, ssem, *, tm, nf):
    half = xb.shape[1] // 2
    b = pl.program_id(0)
    f = pl.program_id(1)
    nreal = nreal_ref[0]
    slot = lax.rem(b, 2)

    def rows_of(blk):
        return pl.multiple_of(cnt_ref[blk], SUBLANES)

    def for_each_row(n_rows, copy_of_row):
        def body(grp, carry):
            base = pl.multiple_of(grp * SUBLANES, SUBLANES)
            for k in range(SUBLANES):
                copy_of_row(base + k).start(priority=k % 2)
            return carry
        lax.fori_loop(0, lax.shift_right_logical(n_rows, SUBLANES.bit_length() - 1), body, 0)

    def start_gather(slot_, n_rows):
        for_each_row(n_rows, lambda r: pltpu.make_async_copy(
            x_hbm.at[pl.ds(tok_ref[0, r], 1)], xbuf.at[slot_, pl.ds(r, 1)], gsem.at[slot_]))

    def wait_gather(slot_, n_rows):
        pltpu.make_async_copy(x_hbm.at[pl.ds(0, n_rows)], xbuf.at[slot_, pl.ds(0, n_rows)], gsem.at[slot_]).wait()

    def start_scatter(slot_, n_rows):
        for_each_row(n_rows, lambda r: pltpu.make_async_copy(
            obuf.at[slot_, pl.ds(r, 1)], ys_hbm.at[pl.ds(aid_ref[0, r], 1)], ssem.at[slot_]))

    def wait_scatter(slot_, n_rows):
        pltpu.make_async_copy(obuf.at[slot_, pl.ds(0, n_rows)], ys_hbm.at[pl.ds(0, n_rows)], ssem.at[slot_]).wait()

    @pl.when(b < nreal)
    def _():
        @pl.when(f == 0)
        def _():
            @pl.when(b == 0)
            def _():
                xbuf[...] = jnp.zeros(xbuf.shape, xbuf.dtype)
                start_gather(0, rows_of(0))
            wait_gather(slot, rows_of(b))
            lo, hi = _unpack_bf16_pair(xbuf[slot])
            gw = X1_PACK_GROUP // 2
            for grp in range(half // gw):
                xb[:, 2 * grp * gw:(2 * grp + 1) * gw] = lo[:, grp * gw:(grp + 1) * gw].astype(BF16)
                xb[:, (2 * grp + 1) * gw:(2 * grp + 2) * gw] = hi[:, grp * gw:(grp + 1) * gw].astype(BF16)

        x = xb[...]
        g = jnp.dot(x, wg_ref[...].astype(BF16), preferred_element_type=F32)
        u = jnp.dot(x, wu_ref[...].astype(BF16), preferred_element_type=F32)
        hidden = (g * (1.0 / (1.0 + jnp.exp(-g))) * u).astype(BF16)
        contrib = jnp.dot(hidden, wd_ref[...].astype(BF16), preferred_element_type=F32)

        @pl.when(f == 0)
        def _():
            acc[...] = contrib

        @pl.when((f > 0) & (f < nf - 1))
        def _():
            acc[...] += contrib

        @pl.when(f == nf - 1)
        def _():
            @pl.when(b >= 1)
            def _():
                wait_scatter(1 - slot, rows_of(b - 1))
            total = acc[...] + contrib
            obuf[slot] = _pack_bf16_pair(total[:, :half], total[:, half:])
            start_scatter(slot, rows_of(b))

            @pl.when(b + 1 < nreal)
            def _():
                start_gather(1 - slot, rows_of(b + 1))

            @pl.when(b + 1 == nreal)
            def _():
                wait_scatter(slot, rows_of(b))


def _expert_mlp(x1p, plan, w_gate, w_up, w_down, tm, fc=256):
    tok_blocks, aid_blocks, block_expert, block_rows, n_real = plan
    n, dp = x1p.shape
    d = 2 * dp
    n_blocks = tok_blocks.shape[0]
    d_ff = w_gate.shape[2]
    nf = d_ff // fc
    assert nf >= 2

    def live(b, f, nreal):
        is_real = b < nreal[0]
        return jnp.where(is_real, b, nreal[0] - 1), jnp.where(is_real, f, nf - 1)

    def tok_map(b, f, be, cnt, nreal):
        bb, ff = live(b, f, nreal)
        return jnp.minimum(bb + (ff == nf - 1).astype(jnp.int32), nreal[0] - 1), 0, 0

    def aid_map(b, f, be, cnt, nreal):
        bb, _ = live(b, f, nreal)
        return bb, 0, 0

    def w_in_map(b, f, be, cnt, nreal):
        bb, ff = live(b, f, nreal)
        return be[bb], 0, ff

    def w_out_map(b, f, be, cnt, nreal):
        bb, ff = live(b, f, nreal)
        return be[bb], ff, 0

    grid_spec = pltpu.PrefetchScalarGridSpec(
        num_scalar_prefetch=3,
        grid=(n_blocks, nf),
        in_specs=[
            pl.BlockSpec((None, 1, tm), tok_map, memory_space=pltpu.SMEM),
            pl.BlockSpec((None, 1, tm), aid_map, memory_space=pltpu.SMEM),
            pl.BlockSpec(memory_space=pl.ANY),
            pl.BlockSpec((None, d, fc), w_in_map),
            pl.BlockSpec((None, d, fc), w_in_map),
            pl.BlockSpec((None, fc, d), w_out_map),
        ],
        out_specs=pl.BlockSpec(memory_space=pl.ANY),
        scratch_shapes=[pltpu.VMEM((2, tm, dp), jnp.uint32), pltpu.VMEM((tm, d), BF16), pltpu.VMEM((tm, d), F32),
                        pltpu.VMEM((2, tm, dp), jnp.uint32),
                        pltpu.SemaphoreType.DMA((2,)), pltpu.SemaphoreType.DMA((2,))],
    )
    kernel = functools.partial(_expert_kernel, tm=tm, nf=nf)
    return pl.pallas_call(
        kernel,
        grid_spec=grid_spec,
        out_shape=jax.ShapeDtypeStruct((n * TOP_K + 2 * tm, dp), jnp.uint32),
        compiler_params=_params("arbitrary", "arbitrary"),
        name="expert_mlp",
    )(block_expert, block_rows, n_real, tok_blocks, aid_blocks, x1p, w_gate, w_up, w_down)


def _combine_kernel(x1_ref, y0_ref, y1_ref, wt_ref, lg_ref, lb_ref, o_ref, buf_ref):
    half = x1_ref.shape[1] // 2
    wt = wt_ref[...]
    y0 = _unpack_bf16_pair(y0_ref[...])
    y1 = _unpack_bf16_pair(y1_ref[...])
    for p, cs in enumerate((slice(0, half), slice(half, 2 * half))):
        buf_ref[:, cs] = DN_ALPHA * x1_ref[:, cs] + (wt[:, 0:1] * y0[p] + wt[:, 1:2] * y1[p])
    hsum = buf_ref[...]
    mu = jnp.mean(hsum, axis=-1, keepdims=True)
    hc = hsum - mu
    var = jnp.mean(jnp.square(hc), axis=-1, keepdims=True)
    o_ref[...] = hc * lax.rsqrt(var + LN_EPS) * lg_ref[...] + lb_ref[...]


def _combine_ln(x1, ys, wts, ln_g, ln_b, tm=256):
    m, d = x1.shape
    big = pl.BlockSpec((tm, d), lambda i: (i, 0))
    first = pl.BlockSpec((tm, d // 2), lambda i: (i, 0))
    second = pl.BlockSpec((tm, d // 2), lambda i: (m // tm + i, 0))
    row_vec = pl.BlockSpec((1, d), lambda i: (0, 0))
    return pl.pallas_call(
        _combine_kernel,
        grid=(m // tm,),
        in_specs=[big, first, second, pl.BlockSpec((tm, LANES), lambda i: (i, 0)), row_vec, row_vec],
        out_specs=big,
        out_shape=jax.ShapeDtypeStruct((m, d), F32),
        scratch_shapes=[pltpu.VMEM((tm, d), F32)],
        compiler_params=_params("arbitrary"),
        name="combine_ln",
    )(x1, ys, ys, wts, ln_g, ln_b)


def _plan_blocks(eid, tm):
    n = eid.shape[0]
    a = n * TOP_K
    expert_id = eid.reshape(-1)
    onehot = (expert_id[:, None] == jnp.arange(N_EXPERTS, dtype=jnp.int32)[None, :]).astype(jnp.int32)
    csum = jnp.cumsum(onehot, axis=0)
    counts = csum[-1]
    rank = jnp.take_along_axis(csum, expert_id[:, None], axis=1)[:, 0] - 1
    padded = (counts + tm - 1) // tm * tm
    pad_ends = jnp.cumsum(padded)
    pad_starts = pad_ends - padded
    n_blocks = a // tm + N_EXPERTS
    pos = pad_starts[expert_id] + rank
    spare = a + jnp.arange(n_blocks * tm, dtype=jnp.int32) % (2 * tm)
    flat = jnp.arange(a, dtype=jnp.int32)
    aid_pad = spare.at[pos].set((flat % TOP_K) * n + flat // TOP_K)
    tok_pad = jnp.where(aid_pad < a, aid_pad % n, 0)
    block_start = jnp.arange(n_blocks, dtype=jnp.int32) * tm
    block_expert = jnp.minimum(jnp.searchsorted(pad_ends, block_start, side='right'), N_EXPERTS - 1).astype(jnp.int32)
    block_count = jnp.clip(counts[block_expert] - (block_start - pad_starts[block_expert]), 0, tm)
    block_rows = ((block_count + SUBLANES - 1) // SUBLANES * SUBLANES).astype(jnp.int32)
    n_real = (pad_ends[-1] // tm).astype(jnp.int32).reshape(1)
    return tok_pad.reshape(n_blocks, 1, tm), aid_pad.reshape(n_blocks, 1, tm), block_expert, block_rows, n_real


def kernel(x, w_in, w_out, lambda_q1, lambda_k1, lambda_q2, lambda_k2, subln_g, beta_attn, gmlp_ln_g, gmlp_ln_b,
           spatial_w, spatial_b, beta_gmlp, rel_bias, ln1_g, ln1_b, w_group, b_group, w_expert, b_expert,
           w_gate, w_up, w_down, ln2_g, ln2_b):
    b, s, d = x.shape
    n = b * s
    d_attn = ATTN_HEADS * ATTN_V_DIM
    d_gmlp = d - d_attn
    moe_tm = 320
    for l in range(DEPTH):
        lambda_init = 0.8 - 0.6 * math.exp(-0.3 * l)
        x2 = x.reshape(n, d)
        proj = _proj_matmul(x2.astype(BF16), w_in[l])
        attn = _diff_attention(proj.reshape(b, s, -1), rel_bias, lambda_q1[l][None], lambda_k1[l][None],
                               lambda_q2[l][None], lambda_k2[l][None], subln_g[l][None], beta_attn[l][None],
                               lambda_init)
        gm = _spatial_gating(proj, gmlp_ln_g[l][None], gmlp_ln_b[l][None], spatial_w[l], spatial_b[l],
                             beta_gmlp[l][None], d_gmlp, u_col=3 * d_attn // d_gmlp, g_col=3 * d_attn // d_gmlp + 1)
        n_route = N_GROUPS + N_EXPERTS
        w_route = jnp.concatenate([w_group[l], w_expert[l].reshape(d, N_EXPERTS),
                                   jnp.zeros((d, LANES - n_route), F32)], axis=1)
        w_route_hi = w_route.astype(BF16)
        w_route = jnp.stack([w_route_hi, (w_route - w_route_hi.astype(F32)).astype(BF16)])
        b_route = jnp.concatenate([b_group[l], b_expert[l].reshape(-1), jnp.zeros((LANES - n_route,), F32)])[None]
        x1, x1p, logits = _out_proj_ln_router(attn.reshape(n, d_attn), gm, w_out[l].astype(BF16), x2,
                                         ln1_g[l][None], ln1_b[l][None], w_route, b_route)
        eid, wts = _route(logits)
        plan = _plan_blocks(eid[:, :TOP_K], moe_tm)
        ys = _expert_mlp(x1p, plan, w_gate[l], w_up[l], w_down[l], moe_tm)
        x = _combine_ln(x1, ys, wts, ln2_g[l][None], ln2_b[l][None]).reshape(b, s, d)
    return x
```

```python
import functools
import math

import jax
import jax.numpy as jnp
from jax import lax
from jax.experimental import pallas as pl
from jax.experimental.pallas import tpu as pltpu

F32 = jnp.float32
BF16 = jnp.bfloat16

ATTN_HEADS = 8
ATTN_HEAD_DIM = 128
ATTN_V_DIM = 2 * ATTN_HEAD_DIM
GMLP_HEADS = 8
CHUNK = 128
REL_BUCKETS = 32
REL_MAX_DIST = 128
N_GROUPS = 8
EXPERTS_PER_GROUP = 8
N_EXPERTS = N_GROUPS * EXPERTS_PER_GROUP
TOP_K = 2
LN_EPS = 1e-5
DEPTH = 1
DN_ALPHA = (2 * DEPTH) ** 0.25
NEG_INF = -1e30
LOG2E = math.log2(math.e)
LANES = 128
SUBLANES = 8
VMEM_LIMIT = 56 * 1024 * 1024
X1_PACK_GROUP = 256


def _params(*semantics):
    return pltpu.CompilerParams(dimension_semantics=semantics, vmem_limit_bytes=VMEM_LIMIT)


def _matmul_kernel(x_ref, w_ref, o_ref, wb_ref):
    @pl.when(pl.program_id(1) == 0)
    def _():
        wb_ref[...] = w_ref[...].astype(BF16)

    o_ref[...] = jnp.dot(x_ref[...], wb_ref[...], preferred_element_type=F32).astype(o_ref.dtype)


def _proj_matmul(xb, w, tm=512, tn=1024):
    m, k = xb.shape
    n = w.shape[1]
    return pl.pallas_call(
        _matmul_kernel,
        grid=(n // tn, m // tm),
        in_specs=[pl.BlockSpec((tm, k), lambda j, i: (i, 0)),
                  pl.BlockSpec((k, tn), lambda j, i: (0, j))],
        out_specs=pl.BlockSpec((tm, tn), lambda j, i: (i, j)),
        out_shape=jax.ShapeDtypeStruct((m, n), BF16),
        scratch_shapes=[pltpu.VMEM((k, tn), BF16)],
        compiler_params=_params("arbitrary", "arbitrary"),
        name="proj_matmul",
    )(xb, w)


def _rel_bucket(n):
    max_exact = REL_BUCKETS // 2
    nf = jnp.maximum(n, max_exact).astype(F32)
    large = max_exact + (jnp.log(nf / max_exact) / math.log(REL_MAX_DIST / max_exact)
                         * (REL_BUCKETS - max_exact)).astype(jnp.int32)
    large = jnp.minimum(large, REL_BUCKETS - 1)
    return jnp.where(n < max_exact, n, large)


def _rel_bias_tiles(rel_bias, blk):
    pos = jnp.arange(blk)
    tiles = []
    for d in (0, 1):
        bucket = _rel_bucket(jnp.maximum(pos[:, None] + d * blk - pos[None, :], 0))
        hit = bucket[None] == jnp.arange(REL_BUCKETS)[:, None, None]
        tiles.append(jnp.sum(jnp.where(hit[:, None], rel_bias[:, :, None, None], 0.0), axis=0))
    return jnp.stack(tiles, axis=1).astype(F32)


def _attn_kernel(far_ref, q_ref, k_ref, v_ref, bias_ref, lq1_ref, lk1_ref, lq2_ref, lk2_ref,
                 sg_ref, ba_ref, o_ref, mx_ref, sh_ref, mrow_ref, l_ref, s_ref, acc_ref, *, blk, hp, lambda_init):
    g = pl.program_id(1)
    i = pl.program_id(2)
    dh = ATTN_HEAD_DIM
    dv = ATTN_V_DIM
    c1 = ATTN_HEAD_DIM ** -0.5 * LOG2E
    streams = [(hh, mi) for hh in range(hp) for mi in range(2)]
    q = q_ref[...]
    qs = [q[:, hh * dv + mi * dh:hh * dv + (mi + 1) * dh] for hh, mi in streams]
    contract_last = (((1,), (1,)), ((), ()))
    far2 = [far_ref[g * hp + hh] * LOG2E for hh in range(hp)]
    n_far = jnp.maximum(i - 1, 0)

    def lane_chunks(x):
        return [x[:, c * LANES:(c + 1) * LANES] for c in range(x.shape[1] // LANES)]

    def raw_scores(start, width):
        kj = k_ref[pl.ds(start, width), :]
        return [lax.dot_general(qs[sid], kj[:, hh * dv + mi * dh:hh * dv + (mi + 1) * dh], contract_last,
                                preferred_element_type=F32) for sid, (hh, mi) in enumerate(streams)]

    def over_far_blocks(body):
        def pair(t, carry):
            body(pl.multiple_of(t * (2 * blk), 2 * blk), 2 * blk)
            return carry
        lax.fori_loop(0, lax.shift_right_logical(n_far, 1), pair, 0)

        @pl.when(lax.rem(n_far, 2) == 1)
        def _():
            body(pl.multiple_of((n_far - 1) * blk, blk), blk)

    mx_ref[...] = jnp.full(mx_ref.shape, -jnp.inf, F32)

    def max_body(start, width):
        s = raw_scores(start, width)
        for sid in range(len(streams)):
            best = mx_ref[sid]
            for chunk in lane_chunks(s[sid]):
                best = jnp.maximum(best, chunk)
            mx_ref[sid] = best

    over_far_blocks(max_body)

    row = lax.broadcasted_iota(jnp.int32, (blk, blk), 0)
    col = lax.broadcasted_iota(jnp.int32, (blk, blk), 1)
    near = raw_scores(pl.multiple_of(n_far * blk, blk), blk)
    diag = raw_scores(pl.multiple_of(i * blk, blk), blk)
    for sid, (hh, mi) in enumerate(streams):
        s_near = jnp.where(i >= 1, near[sid] * c1 + bias_ref[hh, 1], NEG_INF)
        s_diag = jnp.where(col <= row, diag[sid] * c1 + bias_ref[hh, 0], NEG_INF)
        s_ref[sid, 0] = s_near
        s_ref[sid, 1] = s_diag
        best = mx_ref[sid] * c1 + far2[hh]
        for chunk in lane_chunks(s_near) + lane_chunks(s_diag):
            best = jnp.maximum(best, chunk)
        m_rows = jnp.broadcast_to(jnp.max(best, axis=-1, keepdims=True), (blk, LANES))
        mrow_ref[sid] = m_rows
        sh_ref[sid] = far2[hh] - m_rows

    l_ref[...] = jnp.zeros(l_ref.shape, F32)
    acc_ref[...] = jnp.zeros(acc_ref.shape, F32)

    def accumulate(p_chunks, vj):
        ps = []
        for sid in range(len(streams)):
            tot = l_ref[sid]
            for chunk in p_chunks[sid]:
                tot = tot + chunk
            l_ref[sid] = tot
            ps.append(jnp.concatenate(p_chunks[sid], axis=1).astype(BF16))
        for hh in range(hp):
            acc_ref[hh] += jnp.dot(jnp.concatenate(ps[2 * hh:2 * hh + 2], axis=0), vj[:, hh * dv:(hh + 1) * dv],
                                   preferred_element_type=F32)

    def pv_body(start, width):
        s = raw_scores(start, width)
        accumulate([[jnp.exp2(chunk * c1 + sh_ref[sid]) for chunk in lane_chunks(s[sid])]
                    for sid in range(len(streams))], v_ref[pl.ds(start, width), :])

    over_far_blocks(pv_body)

    for t, start in ((0, pl.multiple_of(n_far * blk, blk)), (1, pl.multiple_of(i * blk, blk))):
        accumulate([[jnp.exp2(chunk - mrow_ref[sid]) for chunk in lane_chunks(s_ref[sid, t])]
                    for sid in range(len(streams))], v_ref[pl.ds(start, blk), :])

    lam = (jnp.exp(jnp.sum(lq1_ref[...] * lk1_ref[...], axis=-1, keepdims=True))
           - jnp.exp(jnp.sum(lq2_ref[...] * lk2_ref[...], axis=-1, keepdims=True)) + lambda_init)
    for hh in range(hp):
        l1 = jnp.sum(l_ref[2 * hh], axis=-1, keepdims=True)
        l2 = jnp.sum(l_ref[2 * hh + 1], axis=-1, keepdims=True)
        o = acc_ref[hh, :blk] / l1 - lam * (acc_ref[hh, blk:] / l2)
        o = o * lax.rsqrt(jnp.mean(jnp.square(o), axis=-1, keepdims=True) + LN_EPS) * sg_ref[...]
        o = o * (1.0 - lambda_init) * ba_ref[:, hh * dv:(hh + 1) * dv]
        o_ref[:, hh * dv:(hh + 1) * dv] = o.astype(o_ref.dtype)


def _diff_attention(proj3, rel_bias, lq1, lk1, lq2, lk2, subln_g, beta_attn, lambda_init, blk=256, hp=2):
    b, s, _ = proj3.shape
    h = ATTN_HEADS
    dv = ATTN_V_DIM
    gw = hp * dv
    groups = h // hp
    assert REL_BUCKETS // 2 + int(math.log((blk + 1) / (REL_BUCKETS // 2)) / math.log(REL_MAX_DIST / (REL_BUCKETS // 2))
                                  * (REL_BUCKETS // 2)) >= REL_BUCKETS
    tiles = _rel_bias_tiles(rel_bias, blk) * LOG2E
    far = rel_bias[REL_BUCKETS - 1].astype(F32)
    vec = lambda c: pl.BlockSpec((1, c), lambda bi, gi, qi: (0, 0))
    kernel = functools.partial(_attn_kernel, blk=blk, hp=hp, lambda_init=lambda_init)
    n_streams = 2 * hp
    stat = pltpu.VMEM((n_streams, blk, LANES), F32)
    return pl.pallas_call(
        kernel,
        grid=(b, groups, s // blk),
        in_specs=[
            pl.BlockSpec(memory_space=pltpu.SMEM),
            pl.BlockSpec((None, blk, gw), lambda bi, gi, qi: (bi, qi, gi)),
            pl.BlockSpec((None, s, gw), lambda bi, gi, qi: (bi, 0, groups + gi)),
            pl.BlockSpec((None, s, gw), lambda bi, gi, qi: (bi, 0, 2 * groups + gi)),
            pl.BlockSpec((hp, 2, blk, blk), lambda bi, gi, qi: (gi, 0, 0, 0)),
            vec(ATTN_HEAD_DIM), vec(ATTN_HEAD_DIM), vec(ATTN_HEAD_DIM), vec(ATTN_HEAD_DIM),
            vec(dv),
            pl.BlockSpec((1, gw), lambda bi, gi, qi: (0, gi)),
        ],
        out_specs=pl.BlockSpec((None, blk, gw), lambda bi, gi, qi: (bi, qi, gi)),
        out_shape=jax.ShapeDtypeStruct((b, s, h * dv), BF16),
        scratch_shapes=[stat, stat, stat, stat,
                        pltpu.VMEM((n_streams, 2, blk, blk), F32), pltpu.VMEM((hp, 2 * blk, dv), F32)],
        compiler_params=_params("arbitrary", "arbitrary", "arbitrary"),
        name="diff_attention",
    )(far, proj3, proj3, proj3, tiles, lq1, lk1, lq2, lk2, subln_g, beta_attn)


def _gelu(x):
    return 0.5 * x * (1.0 + lax.erf(x * (2.0 ** -0.5)))


def _gating_kernel(u_ref, g_ref, lg_ref, lb_ref, ws_ref, bs_ref, bg_ref, o_ref, buf_ref):
    hd = u_ref.shape[1] // GMLP_HEADS
    g = _gelu(g_ref[...].astype(F32))
    mu = jnp.mean(g, axis=-1, keepdims=True)
    gc = g - mu
    var = jnp.mean(jnp.square(gc), axis=-1, keepdims=True)
    v = (gc * lax.rsqrt(var + LN_EPS) * lg_ref[...] + lb_ref[...]).astype(BF16)
    row = lax.broadcasted_iota(jnp.int32, (CHUNK, CHUNK), 0)
    col = lax.broadcasted_iota(jnp.int32, (CHUNK, CHUNK), 1)
    tril = col <= row
    ss = jnp.zeros((CHUNK, 1), F32)
    for hh in range(GMLP_HEADS):
        w = jnp.where(tril, ws_ref[hh], 0.0).astype(BF16)
        mixed = jnp.dot(w, v[:, hh * hd:(hh + 1) * hd], preferred_element_type=F32) + bs_ref[hh]
        out = _gelu(u_ref[:, hh * hd:(hh + 1) * hd].astype(F32)) * mixed
        ss = ss + jnp.sum(jnp.square(out), axis=-1, keepdims=True)
        buf_ref[:, hh * hd:(hh + 1) * hd] = out
    rstd = lax.rsqrt(ss / u_ref.shape[1] + LN_EPS)
    o_ref[...] = (buf_ref[...] * rstd * bg_ref[...]).astype(o_ref.dtype)


def _spatial_gating(proj2, ln_g, ln_b, w_s, b_s, beta_gmlp, d_gmlp, u_col, g_col):
    n = proj2.shape[0]
    row_vec = pl.BlockSpec((1, d_gmlp), lambda c: (0, 0))
    return pl.pallas_call(
        _gating_kernel,
        grid=(n // CHUNK,),
        in_specs=[
            pl.BlockSpec((CHUNK, d_gmlp), lambda c: (c, u_col)),
            pl.BlockSpec((CHUNK, d_gmlp), lambda c: (c, g_col)),
            row_vec, row_vec,
            pl.BlockSpec((GMLP_HEADS, CHUNK, CHUNK), lambda c: (0, 0, 0)),
            pl.BlockSpec((GMLP_HEADS, CHUNK, 1), lambda c: (0, 0, 0)),
            row_vec,
        ],
        out_specs=pl.BlockSpec((CHUNK, d_gmlp), lambda c: (c, 0)),
        out_shape=jax.ShapeDtypeStruct((n, d_gmlp), BF16),
        scratch_shapes=[pltpu.VMEM((CHUNK, d_gmlp), F32)],
        compiler_params=_params("arbitrary"),
        name="spatial_gating",
    )(proj2, proj2, ln_g, ln_b, w_s, b_s[:, :, None], beta_gmlp)


def _layer_norm_chunks(buf_ref, n_chunks, d):
    tot = jnp.sum(buf_ref[0], axis=-1, keepdims=True)
    for c in range(1, n_chunks):
        tot = tot + jnp.sum(buf_ref[c], axis=-1, keepdims=True)
    mu = tot / d
    sq = jnp.sum(jnp.square(buf_ref[0] - mu), axis=-1, keepdims=True)
    for c in range(1, n_chunks):
        sq = sq + jnp.sum(jnp.square(buf_ref[c] - mu), axis=-1, keepdims=True)
    return mu, lax.rsqrt(sq / d + LN_EPS)


def _pack_bf16_pair(lo, hi):
    lo_bits = lax.bitcast_convert_type(lo.astype(BF16).astype(F32), jnp.uint32) >> 16
    hi_bits = lax.bitcast_convert_type(hi.astype(BF16).astype(F32), jnp.uint32) & jnp.uint32(0xFFFF0000)
    return hi_bits | lo_bits


def _unpack_bf16_pair(words):
    lo = lax.bitcast_convert_type(words << 16, F32)
    hi = lax.bitcast_convert_type(words & jnp.uint32(0xFFFF0000), F32)
    return lo, hi


def _out_proj_kernel(a_ref, g_ref, w_ref, x_ref, lg_ref, lb_ref, wr_ref, br_ref,
                     x1_ref, x1p_ref, lo_ref, buf_ref, mu_ref, rstd_ref, *, n_chunks, tn):
    j = pl.program_id(1)
    ka = a_ref.shape[1]

    @pl.when(j < n_chunks)
    def _():
        mix = (jnp.dot(a_ref[...], w_ref[:ka, :], preferred_element_type=F32)
               + jnp.dot(g_ref[...], w_ref[ka:, :], preferred_element_type=F32))
        buf_ref[j] = DN_ALPHA * x_ref[...] + mix

    @pl.when(j == n_chunks)
    def _():
        mu, rstd = _layer_norm_chunks(buf_ref, n_chunks, n_chunks * tn)
        mu_ref[...] = mu
        rstd_ref[...] = rstd
        lo_ref[...] = jnp.zeros(lo_ref.shape, F32) + br_ref[...]

    @pl.when(j >= n_chunks)
    def _():
        y = (buf_ref[j - n_chunks] - mu_ref[...]) * rstd_ref[...] * lg_ref[...] + lb_ref[...]
        x1_ref[...] = y
        x1p_ref[...] = _pack_bf16_pair(y[:, :tn // 2], y[:, tn // 2:])
        y_hi = y.astype(BF16)
        y_lo = (y - y_hi.astype(F32)).astype(BF16)
        lo_ref[...] += (jnp.dot(y_hi, wr_ref[0], preferred_element_type=F32)
                        + (jnp.dot(y_lo, wr_ref[0], preferred_element_type=F32)
                           + jnp.dot(y_hi, wr_ref[1], preferred_element_type=F32)))


def _out_proj_ln_router(attn2, gm2, wb, x2, ln_g, ln_b, w_route, b_route, tm=1024, tn=X1_PACK_GROUP):
    m, d = x2.shape
    ka = attn2.shape[1]
    kg = gm2.shape[1]
    n_chunks = d // tn
    mat = lambda j: jnp.minimum(j, n_chunks - 1)
    fin = lambda j: jnp.maximum(j - n_chunks, 0)
    kernel = functools.partial(_out_proj_kernel, n_chunks=n_chunks, tn=tn)
    return pl.pallas_call(
        kernel,
        grid=(m // tm, 2 * n_chunks),
        in_specs=[
            pl.BlockSpec((tm, ka), lambda i, j: (i, 0)),
            pl.BlockSpec((tm, kg), lambda i, j: (i, 0)),
            pl.BlockSpec((ka + kg, tn), lambda i, j: (0, mat(j))),
            pl.BlockSpec((tm, tn), lambda i, j: (i, mat(j))),
            pl.BlockSpec((1, tn), lambda i, j: (0, fin(j))),
            pl.BlockSpec((1, tn), lambda i, j: (0, fin(j))),
            pl.BlockSpec((2, tn, LANES), lambda i, j: (0, fin(j), 0)),
            pl.BlockSpec((1, LANES), lambda i, j: (0, 0)),
        ],
        out_specs=[
            pl.BlockSpec((tm, tn), lambda i, j: (i, fin(j))),
            pl.BlockSpec((tm, tn // 2), lambda i, j: (i, fin(j))),
            pl.BlockSpec((tm, LANES), lambda i, j: (i, 0)),
        ],
        out_shape=[jax.ShapeDtypeStruct((m, d), F32), jax.ShapeDtypeStruct((m, d // 2), jnp.uint32),
                   jax.ShapeDtypeStruct((m, LANES), F32)],
        scratch_shapes=[pltpu.VMEM((n_chunks, tm, tn), F32), pltpu.VMEM((tm, 1), F32), pltpu.VMEM((tm, 1), F32)],
        compiler_params=_params("arbitrary", "arbitrary"),
        name="out_proj_ln_router",
    )(attn2, gm2, wb, x2, ln_g, ln_b, w_route, b_route)


def _route_kernel(lo_ref, eid_ref, wt_ref):
    lg = lo_ref[...]
    lane = lax.broadcasted_iota(jnp.int32, lg.shape, 1)
    lane_f = lane.astype(F32)
    none = float(LANES)
    first = lambda hit: jnp.min(jnp.where(hit, lane_f, none), axis=-1, keepdims=True)

    in_groups = lane < N_GROUPS
    g_logits = jnp.where(in_groups, lg, -jnp.inf)
    g_max = jnp.max(g_logits, axis=-1, keepdims=True)
    g_idx = first(g_logits == g_max)
    g_w = 1.0 / jnp.sum(jnp.where(in_groups, jnp.exp(lg - g_max), 0.0), axis=-1, keepdims=True)

    lo = N_GROUPS + g_idx * EXPERTS_PER_GROUP
    in_group = (lane_f >= lo) & (lane_f < lo + EXPERTS_PER_GROUP)
    e_logits = jnp.where(in_group, lg, -jnp.inf)
    t1 = jnp.max(e_logits, axis=-1, keepdims=True)
    i1 = first(e_logits == t1)
    e_rest = jnp.where(lane_f == i1, -jnp.inf, e_logits)
    t2 = jnp.max(e_rest, axis=-1, keepdims=True)
    i2 = first(e_rest == t2)
    ex = jnp.exp(t2 - t1)
    w1 = g_w * (1.0 / (1.0 + ex))
    w2 = g_w * (ex / (1.0 + ex))
    e1 = (i1 - N_GROUPS).astype(jnp.int32)
    e2 = (i2 - N_GROUPS).astype(jnp.int32)
    eid_ref[...] = jnp.where(lane == 0, e1, jnp.where(lane == 1, e2, 0))
    wt_ref[...] = jnp.where(lane == 0, w1, jnp.where(lane == 1, w2, 0.0))


def _route(logits, tm=1024):
    m = logits.shape[0]
    spec = pl.BlockSpec((tm, LANES), lambda i: (i, 0))
    return pl.pallas_call(
        _route_kernel,
        grid=(m // tm,),
        in_specs=[spec],
        out_specs=[spec, spec],
        out_shape=[jax.ShapeDtypeStruct((m, LANES), jnp.int32), jax.ShapeDtypeStruct((m, LANES), F32)],
        compiler_params=_params("arbitrary"),
        name="route",
    )(logits)


def _expert_kernel(be_ref, cnt_ref, nreal_ref, tok_ref, aid_ref, x_hbm, wg_ref, wu_ref, wd_ref, ys_hbm,
                   xbuf, xb, acc, obuf, gsem, ssem, *, tm, nf):
    half = xb.shape[1] // 2
    b = pl.program_id(0)
    f = pl.program_id(1)
    nreal = nreal_ref[0]
    slot = lax.rem(b, 2)

    def rows_of(blk):
        return pl.multiple_of(cnt_ref[blk], SUBLANES)

    def for_each_row(n_rows, copy_of_row):
        def body(grp, carry):
            base = pl.multiple_of(grp * SUBLANES, SUBLANES)
            for k in range(SUBLANES):
                copy_of_row(base + k).start(priority=k % 2)
            return carry
        lax.fori_loop(0, lax.shift_right_logical(n_rows, SUBLANES.bit_length() - 1), body, 0)

    def start_gather(slot_, n_rows):
        for_each_row(n_rows, lambda r: pltpu.make_async_copy(
            x_hbm.at[pl.ds(tok_ref[0, r], 1)], xbuf.at[slot_, pl.ds(r, 1)], gsem.at[slot_]))

    def wait_gather(slot_, n_rows):
        pltpu.make_async_copy(x_hbm.at[pl.ds(0, n_rows)], xbuf.at[slot_, pl.ds(0, n_rows)], gsem.at[slot_]).wait()

    def start_scatter(slot_, n_rows):
        for_each_row(n_rows, lambda r: pltpu.make_async_copy(
            obuf.at[slot_, pl.ds(r, 1)], ys_hbm.at[pl.ds(aid_ref[0, r], 1)], ssem.at[slot_]))

    def wait_scatter(slot_, n_rows):
        pltpu.make_async_copy(obuf.at[slot_, pl.ds(0, n_rows)], ys_hbm.at[pl.ds(0, n_rows)], ssem.at[slot_]).wait()

    @pl.when(b < nreal)
    def _():
        @pl.when(f == 0)
        def _():
            @pl.when(b == 0)
            def _():
                xbuf[...] = jnp.zeros(xbuf.shape, xbuf.dtype)
                start_gather(0, rows_of(0))
            wait_gather(slot, rows_of(b))
            lo, hi = _unpack_bf16_pair(xbuf[slot])
            gw = X1_PACK_GROUP // 2
            for grp in range(half // gw):
                xb[:, 2 * grp * gw:(2 * grp + 1) * gw] = lo[:, grp * gw:(grp + 1) * gw].astype(BF16)
                xb[:, (2 * grp + 1) * gw:(2 * grp + 2) * gw] = hi[:, grp * gw:(grp + 1) * gw].astype(BF16)

        x = xb[...]
        g = jnp.dot(x, wg_ref[...].astype(BF16), preferred_element_type=F32)
        u = jnp.dot(x, wu_ref[...].astype(BF16), preferred_element_type=F32)
        hidden = (g * (1.0 / (1.0 + jnp.exp(-g))) * u).astype(BF16)
        contrib = jnp.dot(hidden, wd_ref[...].astype(BF16), preferred_element_type=F32)

        @pl.when(f == 0)
        def _():
            acc[...] = contrib

        @pl.when((f > 0) & (f < nf - 1))
        def _():
            acc[...] += contrib

        @pl.when(f == nf - 1)
        def _():
            @pl.when(b >= 1)
            def _():
                wait_scatter(1 - slot, rows_of(b - 1))
            total = acc[...] + contrib
            obuf[slot] = _pack_bf16_pair(total[:, :half], total[:, half:])
            start_scatter(slot, rows_of(b))

            @pl.when(b + 1 < nreal)
            def _():
                start_gather(1 - slot, rows_of(b + 1))

            @pl.when(b + 1 == nreal)
            def _():
                wait_scatter(slot, rows_of(b))


def _expert_mlp(x1p, plan, w_gate, w_up, w_down, tm, fc=256):
    tok_blocks, aid_blocks, block_expert, block_rows, n_real = plan
    n, dp = x1p.shape
    d = 2 * dp
    n_blocks = tok_blocks.shape[0]
    d_ff = w_gate.shape[2]
    nf = d_ff // fc
    assert nf >= 2

    def live(b, f, nreal):
        is_real = b < nreal[0]
        return jnp.where(is_real, b, nreal[0] - 1), jnp.where(is_real, f, nf - 1)

    def tok_map(b, f, be, cnt, nreal):
        bb, ff = live(b, f, nreal)
        return jnp.minimum(bb + (ff == nf - 1).astype(jnp.int32), nreal[0] - 1), 0, 0

    def aid_map(b, f, be, cnt, nreal):
        bb, _ = live(b, f, nreal)
        return bb, 0, 0

    def w_in_map(b, f, be, cnt, nreal):
        bb, ff = live(b, f, nreal)
        return be[bb], 0, ff

    def w_out_map(b, f, be, cnt, nreal):
        bb, ff = live(b, f, nreal)
        return be[bb], ff, 0

    grid_spec = pltpu.PrefetchScalarGridSpec(
        num_scalar_prefetch=3,
        grid=(n_blocks, nf),
        in_specs=[
            pl.BlockSpec((None, 1, tm), tok_map, memory_space=pltpu.SMEM),
            pl.BlockSpec((None, 1, tm), aid_map, memory_space=pltpu.SMEM),
            pl.BlockSpec(memory_space=pl.ANY),
            pl.BlockSpec((None, d, fc), w_in_map),
            pl.BlockSpec((None, d, fc), w_in_map),
            pl.BlockSpec((None, fc, d), w_out_map),
        ],
        out_specs=pl.BlockSpec(memory_space=pl.ANY),
        scratch_shapes=[pltpu.VMEM((2, tm, dp), jnp.uint32), pltpu.VMEM((tm, d), BF16), pltpu.VMEM((tm, d), F32),
                        pltpu.VMEM((2, tm, dp), jnp.uint32),
                        pltpu.SemaphoreType.DMA((2,)), pltpu.SemaphoreType.DMA((2,))],
    )
    kernel = functools.partial(_expert_kernel, tm=tm, nf=nf)
    return pl.pallas_call(
        kernel,
        grid_spec=grid_spec,
        out_shape=jax.ShapeDtypeStruct((n * TOP_K + 2 * tm, dp), jnp.uint32),
        compiler_params=_params("arbitrary", "arbitrary"),
        name="expert_mlp",
    )(block_expert, block_rows, n_real, tok_blocks, aid_blocks, x1p, w_gate, w_up, w_down)


def _combine_kernel(x1_ref, y0_ref, y1_ref, wt_ref, lg_ref, lb_ref, o_ref, buf_ref):
    half = x1_ref.shape[1] // 2
    wt = wt_ref[...]
    y0 = _unpack_bf16_pair(y0_ref[...])
    y1 = _unpack_bf16_pair(y1_ref[...])
    for p, cs in enumerate((slice(0, half), slice(half, 2 * half))):
        buf_ref[:, cs] = DN_ALPHA * x1_ref[:, cs] + (wt[:, 0:1] * y0[p] + wt[:, 1:2] * y1[p])
    hsum = buf_ref[...]
    mu = jnp.mean(hsum, axis=-1, keepdims=True)
    hc = hsum - mu
    var = jnp.mean(jnp.square(hc), axis=-1, keepdims=True)
    o_ref[...] = hc * lax.rsqrt(var + LN_EPS) * lg_ref[...] + lb_ref[...]


def _combine_ln(x1, ys, wts, ln_g, ln_b, tm=256):
    m, d = x1.shape
    big = pl.BlockSpec((tm, d), lambda i: (i, 0))
    first = pl.BlockSpec((tm, d // 2), lambda i: (i, 0))
    second = pl.BlockSpec((tm, d // 2), lambda i: (m // tm + i, 0))
    row_vec = pl.BlockSpec((1, d), lambda i: (0, 0))
    return pl.pallas_call(
        _combine_kernel,
        grid=(m // tm,),
        in_specs=[big, first, second, pl.BlockSpec((tm, LANES), lambda i: (i, 0)), row_vec, row_vec],
        out_specs=big,
        out_shape=jax.ShapeDtypeStruct((m, d), F32),
        scratch_shapes=[pltpu.VMEM((tm, d), F32)],
        compiler_params=_params("arbitrary"),
        name="combine_ln",
    )(x1, ys, ys, wts, ln_g, ln_b)


def _plan_blocks(eid, tm):
    n = eid.shape[0]
    a = n * TOP_K
    expert_id = eid.reshape(-1)
    onehot = (expert_id[:, None] == jnp.arange(N_EXPERTS, dtype=jnp.int32)[None, :]).astype(jnp.int32)
    csum = jnp.cumsum(onehot, axis=0)
    counts = csum[-1]
    rank = jnp.take_along_axis(csum, expert_id[:, None], axis=1)[:, 0] - 1
    padded = (counts + tm - 1) // tm * tm
    pad_ends = jnp.cumsum(padded)
    pad_starts = pad_ends - padded
    n_blocks = a // tm + N_EXPERTS
    pos = pad_starts[expert_id] + rank
    spare = a + jnp.arange(n_blocks * tm, dtype=jnp.int32) % (2 * tm)
    flat = jnp.arange(a, dtype=jnp.int32)
    aid_pad = spare.at[pos].set((flat % TOP_K) * n + flat // TOP_K)
    tok_pad = jnp.where(aid_pad < a, aid_pad % n, 0)
    block_start = jnp.arange(n_blocks, dtype=jnp.int32) * tm
    block_expert = jnp.minimum(jnp.searchsorted(pad_ends, block_start, side='right'), N_EXPERTS - 1).astype(jnp.int32)
    block_count = jnp.clip(counts[block_expert] - (block_start - pad_starts[block_expert]), 0, tm)
    block_rows = ((block_count + SUBLANES - 1) // SUBLANES * SUBLANES).astype(jnp.int32)
    n_real = (pad_ends[-1] // tm).astype(jnp.int32).reshape(1)
    return tok_pad.reshape(n_blocks, 1, tm), aid_pad.reshape(n_blocks, 1, tm), block_expert, block_rows, n_real


def kernel(x, w_in, w_out, lambda_q1, lambda_k1, lambda_q2, lambda_k2, subln_g, beta_attn, gmlp_ln_g, gmlp_ln_b,
           spatial_w, spatial_b, beta_gmlp, rel_bias, ln1_g, ln1_b, w_group, b_group, w_expert, b_expert,
           w_gate, w_up, w_down, ln2_g, ln2_b):
    b, s, d = x.shape
    n = b * s
    d_attn = ATTN_HEADS * ATTN_V_DIM
    d_gmlp = d - d_attn
    moe_tm = 320
    for l in range(DEPTH):
        lambda_init = 0.8 - 0.6 * math.exp(-0.3 * l)
        x2 = x.reshape(n, d)
        proj = _proj_matmul(x2.astype(BF16), w_in[l])
        attn = _diff_attention(proj.reshape(b, s, -1), rel_bias, lambda_q1[l][None], lambda_k1[l][None],
                               lambda_q2[l][None], lambda_k2[l][None], subln_g[l][None], beta_attn[l][None],
                               lambda_init)
        gm = _spatial_gating(proj, gmlp_ln_g[l][None], gmlp_ln_b[l][None], spatial_w[l], spatial_b[l],
                             beta_gmlp[l][None], d_gmlp, u_col=3 * d_attn // d_gmlp, g_col=3 * d_attn // d_gmlp + 1)
        n_route = N_GROUPS + N_EXPERTS
        w_route = jnp.concatenate([w_group[l], w_expert[l].reshape(d, N_EXPERTS),
                                   jnp.zeros((d, LANES - n_route), F32)], axis=1)
        w_route_hi = w_route.astype(BF16)
        w_route = jnp.stack([w_route_hi, (w_route - w_route_hi.astype(F32)).astype(BF16)])
        b_route = jnp.concatenate([b_group[l], b_expert[l].reshape(-1), jnp.zeros((LANES - n_route,), F32)])[None]
        x1, x1p, logits = _out_proj_ln_router(attn.reshape(n, d_attn), gm, w_out[l].astype(BF16), x2,
                                         ln1_g[l][None], ln1_b[l][None], w_route, b_route)
        eid, wts = _route(logits)
        plan = _plan_blocks(eid[:, :TOP_K], moe_tm)
        ys = _expert_mlp(x1p, plan, w_gate[l], w_up[l], w_down[l], moe_tm)
        x = _combine_ln(x1, ys, wts, ln2_g[l][None], ln2_b[l][None]).reshape(b, s, d)
    return x
```

```python
import functools
import math

import jax
import jax.numpy as jnp
from jax import lax
from jax.experimental import pallas as pl
from jax.experimental.pallas import tpu as pltpu

F32 = jnp.float32
BF16 = jnp.bfloat16

ATTN_HEADS = 8
ATTN_HEAD_DIM = 128
ATTN_V_DIM = 2 * ATTN_HEAD_DIM
GMLP_HEADS = 8
CHUNK = 128
REL_BUCKETS = 32
REL_MAX_DIST = 128
N_GROUPS = 8
EXPERTS_PER_GROUP = 8
N_EXPERTS = N_GROUPS * EXPERTS_PER_GROUP
TOP_K = 2
LN_EPS = 1e-5
DEPTH = 1
DN_ALPHA = (2 * DEPTH) ** 0.25
NEG_INF = -1e30
LOG2E = math.log2(math.e)
LANES = 128
SUBLANES = 8
VMEM_LIMIT = 56 * 1024 * 1024
X1_PACK_GROUP = 256


def _params(*semantics):
    return pltpu.CompilerParams(dimension_semantics=semantics, vmem_limit_bytes=VMEM_LIMIT)


def _matmul_kernel(x_ref, w_ref, o_ref, wb_ref):
    @pl.when(pl.program_id(1) == 0)
    def _():
        wb_ref[...] = w_ref[...].astype(BF16)

    o_ref[...] = jnp.dot(x_ref[...], wb_ref[...], preferred_element_type=F32).astype(o_ref.dtype)


def _proj_matmul(xb, w, tm=512, tn=1024):
    m, k = xb.shape
    n = w.shape[1]
    return pl.pallas_call(
        _matmul_kernel,
        grid=(n // tn, m // tm),
        in_specs=[pl.BlockSpec((tm, k), lambda j, i: (i, 0)),
                  pl.BlockSpec((k, tn), lambda j, i: (0, j))],
        out_specs=pl.BlockSpec((tm, tn), lambda j, i: (i, j)),
        out_shape=jax.ShapeDtypeStruct((m, n), BF16),
        scratch_shapes=[pltpu.VMEM((k, tn), BF16)],
        compiler_params=_params("arbitrary", "arbitrary"),
        name="proj_matmul",
    )(xb, w)


def _rel_bucket(n):
    max_exact = REL_BUCKETS // 2
    nf = jnp.maximum(n, max_exact).astype(F32)
    large = max_exact + (jnp.log(nf / max_exact) / math.log(REL_MAX_DIST / max_exact)
                         * (REL_BUCKETS - max_exact)).astype(jnp.int32)
    large = jnp.minimum(large, REL_BUCKETS - 1)
    return jnp.where(n < max_exact, n, large)


def _rel_bias_tiles(rel_bias, blk):
    pos = jnp.arange(blk)
    tiles = []
    for d in (0, 1):
        bucket = _rel_bucket(jnp.maximum(pos[:, None] + d * blk - pos[None, :], 0))
        hit = bucket[None] == jnp.arange(REL_BUCKETS)[:, None, None]
        tiles.append(jnp.sum(jnp.where(hit[:, None], rel_bias[:, :, None, None], 0.0), axis=0))
    return jnp.stack(tiles, axis=1).astype(F32)


def _attn_kernel(far_ref, q_ref, k_ref, v_ref, bias_ref, lq1_ref, lk1_ref, lq2_ref, lk2_ref,
                 sg_ref, ba_ref, o_ref, mx_ref, sh_ref, mrow_ref, l_ref, s_ref, acc_ref, *, blk, hp, lambda_init):
    g = pl.program_id(1)
    i = pl.program_id(2)
    dh = ATTN_HEAD_DIM
    dv = ATTN_V_DIM
    c1 = ATTN_HEAD_DIM ** -0.5 * LOG2E
    streams = [(hh, mi) for hh in range(hp) for mi in range(2)]
    q = q_ref[...]
    qs = [q[:, hh * dv + mi * dh:hh * dv + (mi + 1) * dh] for hh, mi in streams]
    contract_last = (((1,), (1,)), ((), ()))
    far2 = [far_ref[g * hp + hh] * LOG2E for hh in range(hp)]
    n_far = jnp.maximum(i - 1, 0)

    def lane_chunks(x):
        return [x[:, c * LANES:(c + 1) * LANES] for c in range(x.shape[1] // LANES)]

    def raw_scores(start, width):
        kj = k_ref[pl.ds(start, width), :]
        return [lax.dot_general(qs[sid], kj[:, hh * dv + mi * dh:hh * dv + (mi + 1) * dh], contract_last,
                                preferred_element_type=F32) for sid, (hh, mi) in enumerate(streams)]

    def over_far_blocks(body):
        def pair(t, carry):
            body(pl.multiple_of(t * (2 * blk), 2 * blk), 2 * blk)
            return carry
        lax.fori_loop(0, lax.shift_right_logical(n_far, 1), pair, 0)

        @pl.when(lax.rem(n_far, 2) == 1)
        def _():
            body(pl.multiple_of((n_far - 1) * blk, blk), blk)

    mx_ref[...] = jnp.full(mx_ref.shape, -jnp.inf, F32)

    def max_body(start, width):
        s = raw_scores(start, width)
        for sid in range(len(streams)):
            best = mx_ref[sid]
            for chunk in lane_chunks(s[sid]):
                best = jnp.maximum(best, chunk)
            mx_ref[sid] = best

    over_far_blocks(max_body)

    row = lax.broadcasted_iota(jnp.int32, (blk, blk), 0)
    col = lax.broadcasted_iota(jnp.int32, (blk, blk), 1)
    near = raw_scores(pl.multiple_of(n_far * blk, blk), blk)
    diag = raw_scores(pl.multiple_of(i * blk, blk), blk)
    for sid, (hh, mi) in enumerate(streams):
        s_near = jnp.where(i >= 1, near[sid] * c1 + bias_ref[hh, 1], NEG_INF)
        s_diag = jnp.where(col <= row, diag[sid] * c1 + bias_ref[hh, 0], NEG_INF)
        s_ref[sid, 0] = s_near
        s_ref[sid, 1] = s_diag
        best = mx_ref[sid] * c1 + far2[hh]
        for chunk in lane_chunks(s_near) + lane_chunks(s_diag):
            best = jnp.maximum(best, chunk)
        m_rows = jnp.broadcast_to(jnp.max(best, axis=-1, keepdims=True), (blk, LANES))
        mrow_ref[sid] = m_rows
        sh_ref[sid] = far2[hh] - m_rows

    l_ref[...] = jnp.zeros(l_ref.shape, F32)
    acc_ref[...] = jnp.zeros(acc_ref.shape, F32)

    def accumulate(p_chunks, vj):
        ps = []
        for sid in range(len(streams)):
            tot = l_ref[sid]
            for chunk in p_chunks[sid]:
                tot = tot + chunk
            l_ref[sid] = tot
            ps.append(jnp.concatenate(p_chunks[sid], axis=1).astype(BF16))
        for hh in range(hp):
            acc_ref[hh] += jnp.dot(jnp.concatenate(ps[2 * hh:2 * hh + 2], axis=0), vj[:, hh * dv:(hh + 1) * dv],
                                   preferred_element_type=F32)

    def pv_body(start, width):
        s = raw_scores(start, width)
        accumulate([[jnp.exp2(chunk * c1 + sh_ref[sid]) for chunk in lane_chunks(s[sid])]
                    for sid in range(len(streams))], v_ref[pl.ds(start, width), :])

    over_far_blocks(pv_body)

    for t, start in ((0, pl.multiple_of(n_far * blk, blk)), (1, pl.multiple_of(i * blk, blk))):
        accumulate([[jnp.exp2(chunk - mrow_ref[sid]) for chunk in lane_chunks(s_ref[sid, t])]
                    for sid in range(len(streams))], v_ref[pl.ds(start, blk), :])

    lam = (jnp.exp(jnp.sum(lq1_ref[...] * lk1_ref[...], axis=-1, keepdims=True))
           - jnp.exp(jnp.sum(lq2_ref[...] * lk2_ref[...], axis=-1, keepdims=True)) + lambda_init)
    for hh in range(hp):
        l1 = jnp.sum(l_ref[2 * hh], axis=-1, keepdims=True)
        l2 = jnp.sum(l_ref[2 * hh + 1], axis=-1, keepdims=True)
        o = acc_ref[hh, :blk] / l1 - lam * (acc_ref[hh, blk:] / l2)
        o = o * lax.rsqrt(jnp.mean(jnp.square(o), axis=-1, keepdims=True) + LN_EPS) * sg_ref[...]
        o = o * (1.0 - lambda_init) * ba_ref[:, hh * dv:(hh + 1) * dv]
        o_ref[:, hh * dv:(hh + 1) * dv] = o.astype(o_ref.dtype)


def _diff_attention(proj3, rel_bias, lq1, lk1, lq2, lk2, subln_g, beta_attn, lambda_init, blk=256, hp=4):
    b, s, _ = proj3.shape
    h = ATTN_HEADS
    dv = ATTN_V_DIM
    gw = hp * dv
    groups = h // hp
    assert REL_BUCKETS // 2 + int(math.log((blk + 1) / (REL_BUCKETS // 2)) / math.log(REL_MAX_DIST / (REL_BUCKETS // 2))
                                  * (REL_BUCKETS // 2)) >= REL_BUCKETS
    tiles = _rel_bias_tiles(rel_bias, blk) * LOG2E
    far = rel_bias[REL_BUCKETS - 1].astype(F32)
    vec = lambda c: pl.BlockSpec((1, c), lambda bi, gi, qi: (0, 0))
    kernel = functools.partial(_attn_kernel, blk=blk, hp=hp, lambda_init=lambda_init)
    n_streams = 2 * hp
    stat = pltpu.VMEM((n_streams, blk, LANES), F32)
    return pl.pallas_call(
        kernel,
        grid=(b, groups, s // blk),
        in_specs=[
            pl.BlockSpec(memory_space=pltpu.SMEM),
            pl.BlockSpec((None, blk, gw), lambda bi, gi, qi: (bi, qi, gi)),
            pl.BlockSpec((None, s, gw), lambda bi, gi, qi: (bi, 0, groups + gi)),
            pl.BlockSpec((None, s, gw), lambda bi, gi, qi: (bi, 0, 2 * groups + gi)),
            pl.BlockSpec((hp, 2, blk, blk), lambda bi, gi, qi: (gi, 0, 0, 0)),
            vec(ATTN_HEAD_DIM), vec(ATTN_HEAD_DIM), vec(ATTN_HEAD_DIM), vec(ATTN_HEAD_DIM),
            vec(dv),
            pl.BlockSpec((1, gw), lambda bi, gi, qi: (0, gi)),
        ],
        out_specs=pl.BlockSpec((None, blk, gw), lambda bi, gi, qi: (bi, qi, gi)),
        out_shape=jax.ShapeDtypeStruct((b, s, h * dv), BF16),
        scratch_shapes=[stat, stat, stat, stat,
                        pltpu.VMEM((n_streams, 2, blk, blk), F32), pltpu.VMEM((hp, 2 * blk, dv), F32)],
        compiler_params=_params("arbitrary", "arbitrary", "arbitrary"),
        name="diff_attention",
    )(far, proj3, proj3, proj3, tiles, lq1, lk1, lq2, lk2, subln_g, beta_attn)


def _gelu(x):
    return 0.5 * x * (1.0 + lax.erf(x * (2.0 ** -0.5)))


def _gating_kernel(u_ref, g_ref, lg_ref, lb_ref, ws_ref, bs_ref, bg_ref, o_ref, buf_ref):
    hd = u_ref.shape[1] // GMLP_HEADS
    g = _gelu(g_ref[...].astype(F32))
    mu = jnp.mean(g, axis=-1, keepdims=True)
    gc = g - mu
    var = jnp.mean(jnp.square(gc), axis=-1, keepdims=True)
    v = (gc * lax.rsqrt(var + LN_EPS) * lg_ref[...] + lb_ref[...]).astype(BF16)
    row = lax.broadcasted_iota(jnp.int32, (CHUNK, CHUNK), 0)
    col = lax.broadcasted_iota(jnp.int32, (CHUNK, CHUNK), 1)
    tril = col <= row
    ss = jnp.zeros((CHUNK, 1), F32)
    for hh in range(GMLP_HEADS):
        w = jnp.where(tril, ws_ref[hh], 0.0).astype(BF16)
        mixed = jnp.dot(w, v[:, hh * hd:(hh + 1) * hd], preferred_element_type=F32) + bs_ref[hh]
        out = _gelu(u_ref[:, hh * hd:(hh + 1) * hd].astype(F32)) * mixed
        ss = ss + jnp.sum(jnp.square(out), axis=-1, keepdims=True)
        buf_ref[:, hh * hd:(hh + 1) * hd] = out
    rstd = lax.rsqrt(ss / u_ref.shape[1] + LN_EPS)
    o_ref[...] = (buf_ref[...] * rstd * bg_ref[...]).astype(o_ref.dtype)


def _spatial_gating(proj2, ln_g, ln_b, w_s, b_s, beta_gmlp, d_gmlp, u_col, g_col):
    n = proj2.shape[0]
    row_vec = pl.BlockSpec((1, d_gmlp), lambda c: (0, 0))
    return pl.pallas_call(
        _gating_kernel,
        grid=(n // CHUNK,),
        in_specs=[
            pl.BlockSpec((CHUNK, d_gmlp), lambda c: (c, u_col)),
            pl.BlockSpec((CHUNK, d_gmlp), lambda c: (c, g_col)),
            row_vec, row_vec,
            pl.BlockSpec((GMLP_HEADS, CHUNK, CHUNK), lambda c: (0, 0, 0)),
            pl.BlockSpec((GMLP_HEADS, CHUNK, 1), lambda c: (0, 0, 0)),
            row_vec,
        ],
        out_specs=pl.BlockSpec((CHUNK, d_gmlp), lambda c: (c, 0)),
        out_shape=jax.ShapeDtypeStruct((n, d_gmlp), BF16),
        scratch_shapes=[pltpu.VMEM((CHUNK, d_gmlp), F32)],
        compiler_params=_params("arbitrary"),
        name="spatial_gating",
    )(proj2, proj2, ln_g, ln_b, w_s, b_s[:, :, None], beta_gmlp)


def _layer_norm_chunks(buf_ref, n_chunks, d):
    tot = jnp.sum(buf_ref[0], axis=-1, keepdims=True)
    for c in range(1, n_chunks):
        tot = tot + jnp.sum(buf_ref[c], axis=-1, keepdims=True)
    mu = tot / d
    sq = jnp.sum(jnp.square(buf_ref[0] - mu), axis=-1, keepdims=True)
    for c in range(1, n_chunks):
        sq = sq + jnp.sum(jnp.square(buf_ref[c] - mu), axis=-1, keepdims=True)
    return mu, lax.rsqrt(sq / d + LN_EPS)


def _pack_bf16_pair(lo, hi):
    lo_bits = lax.bitcast_convert_type(lo.astype(BF16).astype(F32), jnp.uint32) >> 16
    hi_bits = lax.bitcast_convert_type(hi.astype(BF16).astype(F32), jnp.uint32) & jnp.uint32(0xFFFF0000)
    return hi_bits | lo_bits


def _unpack_bf16_pair(words):
    lo = lax.bitcast_convert_type(words << 16, F32)
    hi = lax.bitcast_convert_type(words & jnp.uint32(0xFFFF0000), F32)
    return lo, hi


def _out_proj_kernel(a_ref, g_ref, w_ref, x_ref, lg_ref, lb_ref, wr_ref, br_ref,
                     x1_ref, x1p_ref, lo_ref, buf_ref, mu_ref, rstd_ref, *, n_chunks, tn):
    j = pl.program_id(1)
    ka = a_ref.shape[1]

    @pl.when(j < n_chunks)
    def _():
        mix = (jnp.dot(a_ref[...], w_ref[:ka, :], preferred_element_type=F32)
               + jnp.dot(g_ref[...], w_ref[ka:, :], preferred_element_type=F32))
        buf_ref[j] = DN_ALPHA * x_ref[...] + mix

    @pl.when(j == n_chunks)
    def _():
        mu, rstd = _layer_norm_chunks(buf_ref, n_chunks, n_chunks * tn)
        mu_ref[...] = mu
        rstd_ref[...] = rstd
        lo_ref[...] = jnp.zeros(lo_ref.shape, F32) + br_ref[...]

    @pl.when(j >= n_chunks)
    def _():
        y = (buf_ref[j - n_chunks] - mu_ref[...]) * rstd_ref[...] * lg_ref[...] + lb_ref[...]
        x1_ref[...] = y
        x1p_ref[...] = _pack_bf16_pair(y[:, :tn // 2], y[:, tn // 2:])
        y_hi = y.astype(BF16)
        y_lo = (y - y_hi.astype(F32)).astype(BF16)
        lo_ref[...] += (jnp.dot(y_hi, wr_ref[0], preferred_element_type=F32)
                        + (jnp.dot(y_lo, wr_ref[0], preferred_element_type=F32)
                           + jnp.dot(y_hi, wr_ref[1], preferred_element_type=F32)))


def _out_proj_ln_router(attn2, gm2, wb, x2, ln_g, ln_b, w_route, b_route, tm=1024, tn=X1_PACK_GROUP):
    m, d = x2.shape
    ka = attn2.shape[1]
    kg = gm2.shape[1]
    n_chunks = d // tn
    mat = lambda j: jnp.minimum(j, n_chunks - 1)
    fin = lambda j: jnp.maximum(j - n_chunks, 0)
    kernel = functools.partial(_out_proj_kernel, n_chunks=n_chunks, tn=tn)
    return pl.pallas_call(
        kernel,
        grid=(m // tm, 2 * n_chunks),
        in_specs=[
            pl.BlockSpec((tm, ka), lambda i, j: (i, 0)),
            pl.BlockSpec((tm, kg), lambda i, j: (i, 0)),
            pl.BlockSpec((ka + kg, tn), lambda i, j: (0, mat(j))),
            pl.BlockSpec((tm, tn), lambda i, j: (i, mat(j))),
            pl.BlockSpec((1, tn), lambda i, j: (0, fin(j))),
            pl.BlockSpec((1, tn), lambda i, j: (0, fin(j))),
            pl.BlockSpec((2, tn, LANES), lambda i, j: (0, fin(j), 0)),
            pl.BlockSpec((1, LANES), lambda i, j: (0, 0)),
        ],
        out_specs=[
            pl.BlockSpec((tm, tn), lambda i, j: (i, fin(j))),
            pl.BlockSpec((tm, tn // 2), lambda i, j: (i, fin(j))),
            pl.BlockSpec((tm, LANES), lambda i, j: (i, 0)),
        ],
        out_shape=[jax.ShapeDtypeStruct((m, d), F32), jax.ShapeDtypeStruct((m, d // 2), jnp.uint32),
                   jax.ShapeDtypeStruct((m, LANES), F32)],
        scratch_shapes=[pltpu.VMEM((n_chunks, tm, tn), F32), pltpu.VMEM((tm, 1), F32), pltpu.VMEM((tm, 1), F32)],
        compiler_params=_params("arbitrary", "arbitrary"),
        name="out_proj_ln_router",
    )(attn2, gm2, wb, x2, ln_g, ln_b, w_route, b_route)


def _route_kernel(lo_ref, eid_ref, wt_ref):
    lg = lo_ref[...]
    lane = lax.broadcasted_iota(jnp.int32, lg.shape, 1)
    lane_f = lane.astype(F32)
    none = float(LANES)
    first = lambda hit: jnp.min(jnp.where(hit, lane_f, none), axis=-1, keepdims=True)

    in_groups = lane < N_GROUPS
    g_logits = jnp.where(in_groups, lg, -jnp.inf)
    g_max = jnp.max(g_logits, axis=-1, keepdims=True)
    g_idx = first(g_logits == g_max)
    g_w = 1.0 / jnp.sum(jnp.where(in_groups, jnp.exp(lg - g_max), 0.0), axis=-1, keepdims=True)

    lo = N_GROUPS + g_idx * EXPERTS_PER_GROUP
    in_group = (lane_f >= lo) & (lane_f < lo + EXPERTS_PER_GROUP)
    e_logits = jnp.where(in_group, lg, -jnp.inf)
    t1 = jnp.max(e_logits, axis=-1, keepdims=True)
    i1 = first(e_logits == t1)
    e_rest = jnp.where(lane_f == i1, -jnp.inf, e_logits)
    t2 = jnp.max(e_rest, axis=-1, keepdims=True)
    i2 = first(e_rest == t2)
    ex = jnp.exp(t2 - t1)
    w1 = g_w * (1.0 / (1.0 + ex))
    w2 = g_w * (ex / (1.0 + ex))
    e1 = (i1 - N_GROUPS).astype(jnp.int32)
    e2 = (i2 - N_GROUPS).astype(jnp.int32)
    eid_ref[...] = jnp.where(lane == 0, e1, jnp.where(lane == 1, e2, 0))
    wt_ref[...] = jnp.where(lane == 0, w1, jnp.where(lane == 1, w2, 0.0))


def _route(logits, tm=1024):
    m = logits.shape[0]
    spec = pl.BlockSpec((tm, LANES), lambda i: (i, 0))
    return pl.pallas_call(
        _route_kernel,
        grid=(m // tm,),
        in_specs=[spec],
        out_specs=[spec, spec],
        out_shape=[jax.ShapeDtypeStruct((m, LANES), jnp.int32), jax.ShapeDtypeStruct((m, LANES), F32)],
        compiler_params=_params("arbitrary"),
        name="route",
    )(logits)


def _expert_kernel(be_ref, cnt_ref, nreal_ref, tok_ref, aid_ref, x_hbm, wg0_ref, wg1_ref, wu0_ref, wu1_ref,
                   wd0_ref, wd1_ref, ys_hbm,
                   xbuf, xb, acc, obuf, gsem, ssem, *, tm, nf):
    half = xb.shape[1] // 2
    b = pl.program_id(0)
    f = pl.program_id(1)
    nreal = nreal_ref[0]
    slot = lax.rem(b, 2)

    def rows_of(blk):
        return pl.multiple_of(cnt_ref[blk], SUBLANES)

    def for_each_row(n_rows, copy_of_row):
        def body(grp, carry):
            base = pl.multiple_of(grp * SUBLANES, SUBLANES)
            for k in range(SUBLANES):
                copy_of_row(base + k).start(priority=k % 2)
            return carry
        lax.fori_loop(0, lax.shift_right_logical(n_rows, SUBLANES.bit_length() - 1), body, 0)

    def start_gather(slot_, n_rows):
        for_each_row(n_rows, lambda r: pltpu.make_async_copy(
            x_hbm.at[pl.ds(tok_ref[0, r], 1)], xbuf.at[slot_, pl.ds(r, 1)], gsem.at[slot_]))

    def wait_gather(slot_, n_rows):
        pltpu.make_async_copy(x_hbm.at[pl.ds(0, n_rows)], xbuf.at[slot_, pl.ds(0, n_rows)], gsem.at[slot_]).wait()

    def start_scatter(slot_, n_rows):
        for_each_row(n_rows, lambda r: pltpu.make_async_copy(
            obuf.at[slot_, pl.ds(r, 1)], ys_hbm.at[pl.ds(aid_ref[0, r], 1)], ssem.at[slot_]))

    def wait_scatter(slot_, n_rows):
        pltpu.make_async_copy(obuf.at[slot_, pl.ds(0, n_rows)], ys_hbm.at[pl.ds(0, n_rows)], ssem.at[slot_]).wait()

    @pl.when(b < nreal)
    def _():
        @pl.when(f == 0)
        def _():
            @pl.when(b == 0)
            def _():
                xbuf[...] = jnp.zeros(xbuf.shape, xbuf.dtype)
                start_gather(0, rows_of(0))
            wait_gather(slot, rows_of(b))
            lo, hi = _unpack_bf16_pair(xbuf[slot])
            gw = X1_PACK_GROUP // 2
            for grp in range(half // gw):
                xb[:, 2 * grp * gw:(2 * grp + 1) * gw] = lo[:, grp * gw:(grp + 1) * gw].astype(BF16)
                xb[:, (2 * grp + 1) * gw:(2 * grp + 2) * gw] = hi[:, grp * gw:(grp + 1) * gw].astype(BF16)

        def dot2(lhs, w0_ref, w1_ref):
            k0 = w0_ref.shape[0]
            return (jnp.dot(lhs[:, :k0], w0_ref[...].astype(BF16), preferred_element_type=F32)
                    + jnp.dot(lhs[:, k0:], w1_ref[...].astype(BF16), preferred_element_type=F32))

        x = xb[...]
        g = dot2(x, wg0_ref, wg1_ref)
        u = dot2(x, wu0_ref, wu1_ref)
        hidden = (g * (1.0 / (1.0 + jnp.exp(-g))) * u).astype(BF16)
        contrib = dot2(hidden, wd0_ref, wd1_ref)

        @pl.when(f == 0)
        def _():
            acc[...] = contrib

        @pl.when((f > 0) & (f < nf - 1))
        def _():
            acc[...] += contrib

        @pl.when(f == nf - 1)
        def _():
            @pl.when(b >= 1)
            def _():
                wait_scatter(1 - slot, rows_of(b - 1))
            total = acc[...] + contrib
            obuf[slot] = _pack_bf16_pair(total[:, :half], total[:, half:])
            start_scatter(slot, rows_of(b))

            @pl.when(b + 1 < nreal)
            def _():
                start_gather(1 - slot, rows_of(b + 1))

            @pl.when(b + 1 == nreal)
            def _():
                wait_scatter(slot, rows_of(b))


def _expert_mlp(x1p, plan, w_gate, w_up, w_down, tm, fc=256):
    tok_blocks, aid_blocks, block_expert, block_rows, n_real = plan
    n, dp = x1p.shape
    d = 2 * dp
    n_blocks = tok_blocks.shape[0]
    d_ff = w_gate.shape[2]
    nf = d_ff // fc
    assert nf >= 2

    def live(b, f, nreal):
        is_real = b < nreal[0]
        return jnp.where(is_real, b, nreal[0] - 1), jnp.where(is_real, f, nf - 1)

    def tok_map(b, f, be, cnt, nreal):
        bb, ff = live(b, f, nreal)
        return jnp.minimum(bb + (ff == nf - 1).astype(jnp.int32), nreal[0] - 1), 0, 0

    def aid_map(b, f, be, cnt, nreal):
        bb, _ = live(b, f, nreal)
        return bb, 0, 0

    def w_in_map(part):
        def index(b, f, be, cnt, nreal):
            bb, ff = live(b, f, nreal)
            return be[bb], part, ff
        return index

    def w_out_map(part):
        def index(b, f, be, cnt, nreal):
            bb, ff = live(b, f, nreal)
            return be[bb], 2 * ff + part, 0
        return index

    grid_spec = pltpu.PrefetchScalarGridSpec(
        num_scalar_prefetch=3,
        grid=(n_blocks, nf),
        in_specs=[
            pl.BlockSpec((None, 1, tm), tok_map, memory_space=pltpu.SMEM),
            pl.BlockSpec((None, 1, tm), aid_map, memory_space=pltpu.SMEM),
            pl.BlockSpec(memory_space=pl.ANY),
            pl.BlockSpec((None, dp, fc), w_in_map(0)), pl.BlockSpec((None, dp, fc), w_in_map(1)),
            pl.BlockSpec((None, dp, fc), w_in_map(0)), pl.BlockSpec((None, dp, fc), w_in_map(1)),
            pl.BlockSpec((None, fc // 2, d), w_out_map(0)), pl.BlockSpec((None, fc // 2, d), w_out_map(1)),
        ],
        out_specs=pl.BlockSpec(memory_space=pl.ANY),
        scratch_shapes=[pltpu.VMEM((2, tm, dp), jnp.uint32), pltpu.VMEM((tm, d), BF16), pltpu.VMEM((tm, d), F32),
                        pltpu.VMEM((2, tm, dp), jnp.uint32),
                        pltpu.SemaphoreType.DMA((2,)), pltpu.SemaphoreType.DMA((2,))],
    )
    kernel = functools.partial(_expert_kernel, tm=tm, nf=nf)
    return pl.pallas_call(
        kernel,
        grid_spec=grid_spec,
        out_shape=jax.ShapeDtypeStruct((n * TOP_K + 2 * tm, dp), jnp.uint32),
        compiler_params=_params("arbitrary", "arbitrary"),
        name="expert_mlp",
    )(block_expert, block_rows, n_real, tok_blocks, aid_blocks, x1p, w_gate, w_gate, w_up, w_up, w_down, w_down)


def _combine_kernel(x1_ref, y0_ref, y1_ref, wt_ref, lg_ref, lb_ref, o_ref, buf_ref):
    half = x1_ref.shape[1] // 2
    wt = wt_ref[...]
    y0 = _unpack_bf16_pair(y0_ref[...])
    y1 = _unpack_bf16_pair(y1_ref[...])
    for p, cs in enumerate((slice(0, half), slice(half, 2 * half))):
        buf_ref[:, cs] = DN_ALPHA * x1_ref[:, cs] + (wt[:, 0:1] * y0[p] + wt[:, 1:2] * y1[p])
    hsum = buf_ref[...]
    mu = jnp.mean(hsum, axis=-1, keepdims=True)
    hc = hsum - mu
    var = jnp.mean(jnp.square(hc), axis=-1, keepdims=True)
    o_ref[...] = hc * lax.rsqrt(var + LN_EPS) * lg_ref[...] + lb_ref[...]


def _combine_ln(x1, ys, wts, ln_g, ln_b, tm=256):
    m, d = x1.shape
    big = pl.BlockSpec((tm, d), lambda i: (i, 0))
    first = pl.BlockSpec((tm, d // 2), lambda i: (i, 0))
    second = pl.BlockSpec((tm, d // 2), lambda i: (m // tm + i, 0))
    row_vec = pl.BlockSpec((1, d), lambda i: (0, 0))
    return pl.pallas_call(
        _combine_kernel,
        grid=(m // tm,),
        in_specs=[big, first, second, pl.BlockSpec((tm, LANES), lambda i: (i, 0)), row_vec, row_vec],
        out_specs=big,
        out_shape=jax.ShapeDtypeStruct((m, d), F32),
        scratch_shapes=[pltpu.VMEM((tm, d), F32)],
        compiler_params=_params("arbitrary"),
        name="combine_ln",
    )(x1, ys, ys, wts, ln_g, ln_b)


def _plan_blocks(eid, tm):
    n = eid.shape[0]
    a = n * TOP_K
    expert_id = eid.reshape(-1)
    onehot = (expert_id[:, None] == jnp.arange(N_EXPERTS, dtype=jnp.int32)[None, :]).astype(jnp.int32)
    csum = jnp.cumsum(onehot, axis=0)
    counts = csum[-1]
    rank = jnp.take_along_axis(csum, expert_id[:, None], axis=1)[:, 0] - 1
    padded = (counts + tm - 1) // tm * tm
    pad_ends = jnp.cumsum(padded)
    pad_starts = pad_ends - padded
    n_blocks = a // tm + N_EXPERTS
    pos = pad_starts[expert_id] + rank
    spare = a + jnp.arange(n_blocks * tm, dtype=jnp.int32) % (2 * tm)
    flat = jnp.arange(a, dtype=jnp.int32)
    aid_pad = spare.at[pos].set((flat % TOP_K) * n + flat // TOP_K)
    tok_pad = jnp.where(aid_pad < a, aid_pad % n, 0)
    block_start = jnp.arange(n_blocks, dtype=jnp.int32) * tm
    block_expert = jnp.minimum(jnp.searchsorted(pad_ends, block_start, side='right'), N_EXPERTS - 1).astype(jnp.int32)
    block_count = jnp.clip(counts[block_expert] - (block_start - pad_starts[block_expert]), 0, tm)
    block_rows = ((block_count + SUBLANES - 1) // SUBLANES * SUBLANES).astype(jnp.int32)
    n_real = (pad_ends[-1] // tm).astype(jnp.int32).reshape(1)
    return tok_pad.reshape(n_blocks, 1, tm), aid_pad.reshape(n_blocks, 1, tm), block_expert, block_rows, n_real


def kernel(x, w_in, w_out, lambda_q1, lambda_k1, lambda_q2, lambda_k2, subln_g, beta_attn, gmlp_ln_g, gmlp_ln_b,
           spatial_w, spatial_b, beta_gmlp, rel_bias, ln1_g, ln1_b, w_group, b_group, w_expert, b_expert,
           w_gate, w_up, w_down, ln2_g, ln2_b):
    b, s, d = x.shape
    n = b * s
    d_attn = ATTN_HEADS * ATTN_V_DIM
    d_gmlp = d - d_attn
    moe_tm = 320
    for l in range(DEPTH):
        lambda_init = 0.8 - 0.6 * math.exp(-0.3 * l)
        x2 = x.reshape(n, d)
        proj = _proj_matmul(x2.astype(BF16), w_in[l])
        attn = _diff_attention(proj.reshape(b, s, -1), rel_bias, lambda_q1[l][None], lambda_k1[l][None],
                               lambda_q2[l][None], lambda_k2[l][None], subln_g[l][None], beta_attn[l][None],
                               lambda_init)
        gm = _spatial_gating(proj, gmlp_ln_g[l][None], gmlp_ln_b[l][None], spatial_w[l], spatial_b[l],
                             beta_gmlp[l][None], d_gmlp, u_col=3 * d_attn // d_gmlp, g_col=3 * d_attn // d_gmlp + 1)
        n_route = N_GROUPS + N_EXPERTS
        w_route = jnp.concatenate([w_group[l], w_expert[l].reshape(d, N_EXPERTS),
                                   jnp.zeros((d, LANES - n_route), F32)], axis=1)
        w_route_hi = w_route.astype(BF16)
        w_route = jnp.stack([w_route_hi, (w_route - w_route_hi.astype(F32)).astype(BF16)])
        b_route = jnp.concatenate([b_group[l], b_expert[l].reshape(-1), jnp.zeros((LANES - n_route,), F32)])[None]
        x1, x1p, logits = _out_proj_ln_router(attn.reshape(n, d_attn), gm, w_out[l].astype(BF16), x2,
                                         ln1_g[l][None], ln1_b[l][None], w_route, b_route)
        eid, wts = _route(logits)
        plan = _plan_blocks(eid[:, :TOP_K], moe_tm)
        ys = _expert_mlp(x1p, plan, w_gate[l], w_up[l], w_down[l], moe_tm)
        x = _combine_ln(x1, ys, wts, ln2_g[l][None], ln2_b[l][None]).reshape(b, s, d)
    return x
```

```python
import functools
import math

import jax
import jax.numpy as jnp
from jax import lax
from jax.experimental import pallas as pl
from jax.experimental.pallas import tpu as pltpu

F32 = jnp.float32
BF16 = jnp.bfloat16

ATTN_HEADS = 8
ATTN_HEAD_DIM = 128
ATTN_V_DIM = 2 * ATTN_HEAD_DIM
GMLP_HEADS = 8
CHUNK = 128
REL_BUCKETS = 32
REL_MAX_DIST = 128
N_GROUPS = 8
EXPERTS_PER_GROUP = 8
N_EXPERTS = N_GROUPS * EXPERTS_PER_GROUP
TOP_K = 2
LN_EPS = 1e-5
DEPTH = 1
DN_ALPHA = (2 * DEPTH) ** 0.25
NEG_INF = -1e30
LOG2E = math.log2(math.e)
LANES = 128
SUBLANES = 8
VMEM_LIMIT = 56 * 1024 * 1024
X1_PACK_GROUP = 256


def _params(*semantics):
    return pltpu.CompilerParams(dimension_semantics=semantics, vmem_limit_bytes=VMEM_LIMIT)


def _matmul_kernel(x_ref, w_ref, o_ref, wb_ref):
    @pl.when(pl.program_id(1) == 0)
    def _():
        wb_ref[...] = w_ref[...].astype(BF16)

    o_ref[...] = jnp.dot(x_ref[...], wb_ref[...], preferred_element_type=F32).astype(o_ref.dtype)


def _proj_matmul(xb, w, tm=512, tn=1024):
    m, k = xb.shape
    n = w.shape[1]
    return pl.pallas_call(
        _matmul_kernel,
        grid=(n // tn, m // tm),
        in_specs=[pl.BlockSpec((tm, k), lambda j, i: (i, 0)),
                  pl.BlockSpec((k, tn), lambda j, i: (0, j))],
        out_specs=pl.BlockSpec((tm, tn), lambda j, i: (i, j)),
        out_shape=jax.ShapeDtypeStruct((m, n), BF16),
        scratch_shapes=[pltpu.VMEM((k, tn), BF16)],
        compiler_params=_params("arbitrary", "arbitrary"),
        name="proj_matmul",
    )(xb, w)


def _rel_bucket(n):
    max_exact = REL_BUCKETS // 2
    nf = jnp.maximum(n, max_exact).astype(F32)
    large = max_exact + (jnp.log(nf / max_exact) / math.log(REL_MAX_DIST / max_exact)
                         * (REL_BUCKETS - max_exact)).astype(jnp.int32)
    large = jnp.minimum(large, REL_BUCKETS - 1)
    return jnp.where(n < max_exact, n, large)


def _rel_bias_tiles(rel_bias, blk):
    pos = jnp.arange(blk)
    tiles = []
    for d in (0, 1):
        bucket = _rel_bucket(jnp.maximum(pos[:, None] + d * blk - pos[None, :], 0))
        hit = bucket[None] == jnp.arange(REL_BUCKETS)[:, None, None]
        tiles.append(jnp.sum(jnp.where(hit[:, None], rel_bias[:, :, None, None], 0.0), axis=0))
    return jnp.stack(tiles, axis=1).astype(F32)


def _attn_kernel(far_ref, q_ref, k_ref, v_ref, bias_ref, lq1_ref, lk1_ref, lq2_ref, lk2_ref,
                 sg_ref, ba_ref, o_ref, mx_ref, sh_ref, mrow_ref, l_ref, s_ref, acc_ref, *, blk, hp, lambda_init):
    g = pl.program_id(1)
    i = pl.program_id(2)
    dh = ATTN_HEAD_DIM
    dv = ATTN_V_DIM
    c1 = ATTN_HEAD_DIM ** -0.5 * LOG2E
    streams = [(hh, mi) for hh in range(hp) for mi in range(2)]
    q = q_ref[...]
    qs = [q[:, hh * dv + mi * dh:hh * dv + (mi + 1) * dh] for hh, mi in streams]
    contract_last = (((1,), (1,)), ((), ()))
    far2 = [far_ref[g * hp + hh] * LOG2E for hh in range(hp)]
    n_far = jnp.maximum(i - 1, 0)

    def lane_chunks(x):
        return [x[:, c * LANES:(c + 1) * LANES] for c in range(x.shape[1] // LANES)]

    def raw_scores(start, width):
        kj = k_ref[pl.ds(start, width), :]
        return [lax.dot_general(qs[sid], kj[:, hh * dv + mi * dh:hh * dv + (mi + 1) * dh], contract_last,
                                preferred_element_type=F32) for sid, (hh, mi) in enumerate(streams)]

    def over_far_blocks(body):
        def pair(t, carry):
            body(pl.multiple_of(t * (2 * blk), 2 * blk), 2 * blk)
            return carry
        lax.fori_loop(0, lax.shift_right_logical(n_far, 1), pair, 0)

        @pl.when(lax.rem(n_far, 2) == 1)
        def _():
            body(pl.multiple_of((n_far - 1) * blk, blk), blk)

    mx_ref[...] = jnp.full(mx_ref.shape, -jnp.inf, F32)

    def max_body(start, width):
        s = raw_scores(start, width)
        for sid in range(len(streams)):
            best = mx_ref[sid]
            for chunk in lane_chunks(s[sid]):
                best = jnp.maximum(best, chunk)
            mx_ref[sid] = best

    over_far_blocks(max_body)

    row = lax.broadcasted_iota(jnp.int32, (blk, blk), 0)
    col = lax.broadcasted_iota(jnp.int32, (blk, blk), 1)
    near = raw_scores(pl.multiple_of(n_far * blk, blk), blk)
    diag = raw_scores(pl.multiple_of(i * blk, blk), blk)
    for sid, (hh, mi) in enumerate(streams):
        s_near = jnp.where(i >= 1, near[sid] * c1 + bias_ref[hh, 1], NEG_INF)
        s_diag = jnp.where(col <= row, diag[sid] * c1 + bias_ref[hh, 0], NEG_INF)
        s_ref[sid, 0] = s_near
        s_ref[sid, 1] = s_diag
        best = mx_ref[sid] * c1 + far2[hh]
        for chunk in lane_chunks(s_near) + lane_chunks(s_diag):
            best = jnp.maximum(best, chunk)
        m_rows = jnp.broadcast_to(jnp.max(best, axis=-1, keepdims=True), (blk, LANES))
        mrow_ref[sid] = m_rows
        sh_ref[sid] = far2[hh] - m_rows

    l_ref[...] = jnp.zeros(l_ref.shape, F32)
    acc_ref[...] = jnp.zeros(acc_ref.shape, F32)

    def accumulate(p_chunks, vj):
        ps = []
        for sid in range(len(streams)):
            tot = l_ref[sid]
            for chunk in p_chunks[sid]:
                tot = tot + chunk
            l_ref[sid] = tot
            ps.append(jnp.concatenate(p_chunks[sid], axis=1).astype(BF16))
        for hh in range(hp):
            acc_ref[hh] += jnp.dot(jnp.concatenate(ps[2 * hh:2 * hh + 2], axis=0), vj[:, hh * dv:(hh + 1) * dv],
                                   preferred_element_type=F32)

    def pv_body(start, width):
        s = raw_scores(start, width)
        accumulate([[jnp.exp2(chunk * c1 + sh_ref[sid]) for chunk in lane_chunks(s[sid])]
                    for sid in range(len(streams))], v_ref[pl.ds(start, width), :])

    over_far_blocks(pv_body)

    for t, start in ((0, pl.multiple_of(n_far * blk, blk)), (1, pl.multiple_of(i * blk, blk))):
        accumulate([[jnp.exp2(chunk - mrow_ref[sid]) for chunk in lane_chunks(s_ref[sid, t])]
                    for sid in range(len(streams))], v_ref[pl.ds(start, blk), :])

    lam = (jnp.exp(jnp.sum(lq1_ref[...] * lk1_ref[...], axis=-1, keepdims=True))
           - jnp.exp(jnp.sum(lq2_ref[...] * lk2_ref[...], axis=-1, keepdims=True)) + lambda_init)
    for hh in range(hp):
        l1 = jnp.sum(l_ref[2 * hh], axis=-1, keepdims=True)
        l2 = jnp.sum(l_ref[2 * hh + 1], axis=-1, keepdims=True)
        o = acc_ref[hh, :blk] / l1 - lam * (acc_ref[hh, blk:] / l2)
        o = o * lax.rsqrt(jnp.mean(jnp.square(o), axis=-1, keepdims=True) + LN_EPS) * sg_ref[...]
        o = o * (1.0 - lambda_init) * ba_ref[:, hh * dv:(hh + 1) * dv]
        o_ref[:, hh * dv:(hh + 1) * dv] = o.astype(o_ref.dtype)


def _diff_attention(proj3, rel_bias, lq1, lk1, lq2, lk2, subln_g, beta_attn, lambda_init, blk=256, hp=4):
    b, s, _ = proj3.shape
    h = ATTN_HEADS
    dv = ATTN_V_DIM
    gw = hp * dv
    groups = h // hp
    assert REL_BUCKETS // 2 + int(math.log((blk + 1) / (REL_BUCKETS // 2)) / math.log(REL_MAX_DIST / (REL_BUCKETS // 2))
                                  * (REL_BUCKETS // 2)) >= REL_BUCKETS
    tiles = _rel_bias_tiles(rel_bias, blk) * LOG2E
    far = rel_bias[REL_BUCKETS - 1].astype(F32)
    vec = lambda c: pl.BlockSpec((1, c), lambda bi, gi, qi: (0, 0))
    kernel = functools.partial(_attn_kernel, blk=blk, hp=hp, lambda_init=lambda_init)
    n_streams = 2 * hp
    stat = pltpu.VMEM((n_streams, blk, LANES), F32)
    return pl.pallas_call(
        kernel,
        grid=(b, groups, s // blk),
        in_specs=[
            pl.BlockSpec(memory_space=pltpu.SMEM),
            pl.BlockSpec((None, blk, gw), lambda bi, gi, qi: (bi, qi, gi)),
            pl.BlockSpec((None, s, gw), lambda bi, gi, qi: (bi, 0, groups + gi)),
            pl.BlockSpec((None, s, gw), lambda bi, gi, qi: (bi, 0, 2 * groups + gi)),
            pl.BlockSpec((hp, 2, blk, blk), lambda bi, gi, qi: (gi, 0, 0, 0)),
            vec(ATTN_HEAD_DIM), vec(ATTN_HEAD_DIM), vec(ATTN_HEAD_DIM), vec(ATTN_HEAD_DIM),
            vec(dv),
            pl.BlockSpec((1, gw), lambda bi, gi, qi: (0, gi)),
        ],
        out_specs=pl.BlockSpec((None, blk, gw), lambda bi, gi, qi: (bi, qi, gi)),
        out_shape=jax.ShapeDtypeStruct((b, s, h * dv), BF16),
        scratch_shapes=[stat, stat, stat, stat,
                        pltpu.VMEM((n_streams, 2, blk, blk), F32), pltpu.VMEM((hp, 2 * blk, dv), F32)],
        compiler_params=_params("arbitrary", "arbitrary", "arbitrary"),
        name="diff_attention",
    )(far, proj3, proj3, proj3, tiles, lq1, lk1, lq2, lk2, subln_g, beta_attn)


def _gelu(x):
    return 0.5 * x * (1.0 + lax.erf(x * (2.0 ** -0.5)))


def _gating_kernel(u_ref, g_ref, lg_ref, lb_ref, ws_ref, bs_ref, bg_ref, o_ref, buf_ref):
    hd = u_ref.shape[1] // GMLP_HEADS
    g = _gelu(g_ref[...].astype(F32))
    mu = jnp.mean(g, axis=-1, keepdims=True)
    gc = g - mu
    var = jnp.mean(jnp.square(gc), axis=-1, keepdims=True)
    v = (gc * lax.rsqrt(var + LN_EPS) * lg_ref[...] + lb_ref[...]).astype(BF16)
    row = lax.broadcasted_iota(jnp.int32, (CHUNK, CHUNK), 0)
    col = lax.broadcasted_iota(jnp.int32, (CHUNK, CHUNK), 1)
    tril = col <= row
    ss = jnp.zeros((CHUNK, 1), F32)
    for hh in range(GMLP_HEADS):
        w = jnp.where(tril, ws_ref[hh], 0.0).astype(BF16)
        mixed = jnp.dot(w, v[:, hh * hd:(hh + 1) * hd], preferred_element_type=F32) + bs_ref[hh]
        out = _gelu(u_ref[:, hh * hd:(hh + 1) * hd].astype(F32)) * mixed
        ss = ss + jnp.sum(jnp.square(out), axis=-1, keepdims=True)
        buf_ref[:, hh * hd:(hh + 1) * hd] = out
    rstd = lax.rsqrt(ss / u_ref.shape[1] + LN_EPS)
    o_ref[...] = (buf_ref[...] * rstd * bg_ref[...]).astype(o_ref.dtype)


def _spatial_gating(proj2, ln_g, ln_b, w_s, b_s, beta_gmlp, d_gmlp, u_col, g_col):
    n = proj2.shape[0]
    row_vec = pl.BlockSpec((1, d_gmlp), lambda c: (0, 0))
    return pl.pallas_call(
        _gating_kernel,
        grid=(n // CHUNK,),
        in_specs=[
            pl.BlockSpec((CHUNK, d_gmlp), lambda c: (c, u_col)),
            pl.BlockSpec((CHUNK, d_gmlp), lambda c: (c, g_col)),
            row_vec, row_vec,
            pl.BlockSpec((GMLP_HEADS, CHUNK, CHUNK), lambda c: (0, 0, 0)),
            pl.BlockSpec((GMLP_HEADS, CHUNK, 1), lambda c: (0, 0, 0)),
            row_vec,
        ],
        out_specs=pl.BlockSpec((CHUNK, d_gmlp), lambda c: (c, 0)),
        out_shape=jax.ShapeDtypeStruct((n, d_gmlp), BF16),
        scratch_shapes=[pltpu.VMEM((CHUNK, d_gmlp), F32)],
        compiler_params=_params("arbitrary"),
        name="spatial_gating",
    )(proj2, proj2, ln_g, ln_b, w_s, b_s[:, :, None], beta_gmlp)


def _layer_norm_chunks(buf_ref, n_chunks, d):
    tot = jnp.sum(buf_ref[0], axis=-1, keepdims=True)
    for c in range(1, n_chunks):
        tot = tot + jnp.sum(buf_ref[c], axis=-1, keepdims=True)
    mu = tot / d
    sq = jnp.sum(jnp.square(buf_ref[0] - mu), axis=-1, keepdims=True)
    for c in range(1, n_chunks):
        sq = sq + jnp.sum(jnp.square(buf_ref[c] - mu), axis=-1, keepdims=True)
    return mu, lax.rsqrt(sq / d + LN_EPS)


def _pack_bf16_pair(lo, hi):
    lo_bits = lax.bitcast_convert_type(lo.astype(BF16).astype(F32), jnp.uint32) >> 16
    hi_bits = lax.bitcast_convert_type(hi.astype(BF16).astype(F32), jnp.uint32) & jnp.uint32(0xFFFF0000)
    return hi_bits | lo_bits


def _unpack_bf16_pair(words):
    lo = lax.bitcast_convert_type(words << 16, F32)
    hi = lax.bitcast_convert_type(words & jnp.uint32(0xFFFF0000), F32)
    return lo, hi


def _out_proj_kernel(a_ref, g_ref, w_ref, x_ref, lg_ref, lb_ref, wr_ref, br_ref,
                     x1_ref, x1p_ref, lo_ref, buf_ref, mu_ref, rstd_ref, *, n_tiles, n_chunks, tn):
    i = pl.program_id(0)
    j = pl.program_id(1)
    ka = a_ref.shape[1]

    def finish_chunk():
        y = (buf_ref[j] - mu_ref[...]) * rstd_ref[...] * lg_ref[...] + lb_ref[...]
        x1_ref[...] = y
        x1p_ref[...] = _pack_bf16_pair(y[:, :tn // 2], y[:, tn // 2:])
        y_hi = y.astype(BF16)
        y_lo = (y - y_hi.astype(F32)).astype(BF16)
        start = jnp.where(j == 0, jnp.broadcast_to(br_ref[...], lo_ref.shape), lo_ref[...])
        lo_ref[...] = start + (jnp.dot(y_hi, wr_ref[0], preferred_element_type=F32)
                               + (jnp.dot(y_lo, wr_ref[0], preferred_element_type=F32)
                                  + jnp.dot(y_hi, wr_ref[1], preferred_element_type=F32)))

    def matmul_chunk():
        mix = (jnp.dot(a_ref[...], w_ref[:ka, :], preferred_element_type=F32)
               + jnp.dot(g_ref[...], w_ref[ka:, :], preferred_element_type=F32))
        buf_ref[j] = DN_ALPHA * x_ref[...] + mix

        @pl.when(j == n_chunks - 1)
        def _():
            mu, rstd = _layer_norm_chunks(buf_ref, n_chunks, n_chunks * tn)
            mu_ref[...] = mu
            rstd_ref[...] = rstd

    @pl.when(i == 0)
    def _():
        matmul_chunk()

    @pl.when((i > 0) & (i < n_tiles))
    def _():
        finish_chunk()
        matmul_chunk()

    @pl.when(i == n_tiles)
    def _():
        finish_chunk()


def _out_proj_ln_router(attn2, gm2, wb, x2, ln_g, ln_b, w_route, b_route, tm=1024, tn=X1_PACK_GROUP):
    m, d = x2.shape
    ka = attn2.shape[1]
    kg = gm2.shape[1]
    n_chunks = d // tn
    n_tiles = m // tm
    tile_in = lambda i: jnp.minimum(i, n_tiles - 1)
    tile_out = lambda i: jnp.maximum(i - 1, 0)
    chunk_in = lambda i, j: jnp.where(i < n_tiles, j, n_chunks - 1)
    chunk_out = lambda i, j: jnp.where(i >= 1, j, 0)
    kernel = functools.partial(_out_proj_kernel, n_tiles=n_tiles, n_chunks=n_chunks, tn=tn)
    return pl.pallas_call(
        kernel,
        grid=(n_tiles + 1, n_chunks),
        in_specs=[
            pl.BlockSpec((tm, ka), lambda i, j: (tile_in(i), 0)),
            pl.BlockSpec((tm, kg), lambda i, j: (tile_in(i), 0)),
            pl.BlockSpec((ka + kg, tn), lambda i, j: (0, chunk_in(i, j))),
            pl.BlockSpec((tm, tn), lambda i, j: (tile_in(i), chunk_in(i, j))),
            pl.BlockSpec((1, tn), lambda i, j: (0, chunk_out(i, j))),
            pl.BlockSpec((1, tn), lambda i, j: (0, chunk_out(i, j))),
            pl.BlockSpec((2, tn, LANES), lambda i, j: (0, chunk_out(i, j), 0)),
            pl.BlockSpec((1, LANES), lambda i, j: (0, 0)),
        ],
        out_specs=[
            pl.BlockSpec((tm, tn), lambda i, j: (tile_out(i), chunk_out(i, j))),
            pl.BlockSpec((tm, tn // 2), lambda i, j: (tile_out(i), chunk_out(i, j))),
            pl.BlockSpec((tm, LANES), lambda i, j: (tile_out(i), 0)),
        ],
        out_shape=[jax.ShapeDtypeStruct((m, d), F32), jax.ShapeDtypeStruct((m, d // 2), jnp.uint32),
                   jax.ShapeDtypeStruct((m, LANES), F32)],
        scratch_shapes=[pltpu.VMEM((n_chunks, tm, tn), F32), pltpu.VMEM((tm, 1), F32), pltpu.VMEM((tm, 1), F32)],
        compiler_params=_params("arbitrary", "arbitrary"),
        name="out_proj_ln_router",
    )(attn2, gm2, wb, x2, ln_g, ln_b, w_route, b_route)


def _route_kernel(lo_ref, eid_ref, wt_ref):
    lg = lo_ref[...]
    lane = lax.broadcasted_iota(jnp.int32, lg.shape, 1)
    lane_f = lane.astype(F32)
    none = float(LANES)
    first = lambda hit: jnp.min(jnp.where(hit, lane_f, none), axis=-1, keepdims=True)

    in_groups = lane < N_GROUPS
    g_logits = jnp.where(in_groups, lg, -jnp.inf)
    g_max = jnp.max(g_logits, axis=-1, keepdims=True)
    g_idx = first(g_logits == g_max)
    g_w = 1.0 / jnp.sum(jnp.where(in_groups, jnp.exp(lg - g_max), 0.0), axis=-1, keepdims=True)

    lo = N_GROUPS + g_idx * EXPERTS_PER_GROUP
    in_group = (lane_f >= lo) & (lane_f < lo + EXPERTS_PER_GROUP)
    e_logits = jnp.where(in_group, lg, -jnp.inf)
    t1 = jnp.max(e_logits, axis=-1, keepdims=True)
    i1 = first(e_logits == t1)
    e_rest = jnp.where(lane_f == i1, -jnp.inf, e_logits)
    t2 = jnp.max(e_rest, axis=-1, keepdims=True)
    i2 = first(e_rest == t2)
    ex = jnp.exp(t2 - t1)
    w1 = g_w * (1.0 / (1.0 + ex))
    w2 = g_w * (ex / (1.0 + ex))
    e1 = (i1 - N_GROUPS).astype(jnp.int32)
    e2 = (i2 - N_GROUPS).astype(jnp.int32)
    eid_ref[...] = jnp.where(lane == 0, e1, jnp.where(lane == 1, e2, 0))
    wt_ref[...] = jnp.where(lane == 0, w1, jnp.where(lane == 1, w2, 0.0))


def _route(logits, tm=1024):
    m = logits.shape[0]
    spec = pl.BlockSpec((tm, LANES), lambda i: (i, 0))
    return pl.pallas_call(
        _route_kernel,
        grid=(m // tm,),
        in_specs=[spec],
        out_specs=[spec, spec],
        out_shape=[jax.ShapeDtypeStruct((m, LANES), jnp.int32), jax.ShapeDtypeStruct((m, LANES), F32)],
        compiler_params=_params("arbitrary"),
        name="route",
    )(logits)


def _expert_kernel(be_ref, cnt_ref, nreal_ref, tok_ref, aid_ref, x_hbm, wg_ref, wu_ref, wd_ref, ys_hbm,
                   xbuf, xb, acc, obuf, gsem, ssem, *, tm, nf):
    half = xb.shape[1] // 2
    b = pl.program_id(0)
    f = pl.program_id(1)
    nreal = nreal_ref[0]
    slot = lax.rem(b, 2)

    def rows_of(blk):
        return pl.multiple_of(cnt_ref[blk], SUBLANES)

    def for_each_row(n_rows, copy_of_row):
        def body(grp, carry):
            base = pl.multiple_of(grp * SUBLANES, SUBLANES)
            for k in range(SUBLANES):
                copy_of_row(base + k).start(priority=k % 2)
            return carry
        lax.fori_loop(0, lax.shift_right_logical(n_rows, SUBLANES.bit_length() - 1), body, 0)

    def start_gather(slot_, n_rows):
        for_each_row(n_rows, lambda r: pltpu.make_async_copy(
            x_hbm.at[pl.ds(tok_ref[0, r], 1)], xbuf.at[slot_, pl.ds(r, 1)], gsem.at[slot_]))

    def wait_gather(slot_, n_rows):
        pltpu.make_async_copy(x_hbm.at[pl.ds(0, n_rows)], xbuf.at[slot_, pl.ds(0, n_rows)], gsem.at[slot_]).wait()

    def start_scatter(slot_, n_rows):
        for_each_row(n_rows, lambda r: pltpu.make_async_copy(
            obuf.at[slot_, pl.ds(r, 1)], ys_hbm.at[pl.ds(aid_ref[0, r], 1)], ssem.at[slot_]))

    def wait_scatter(slot_, n_rows):
        pltpu.make_async_copy(obuf.at[slot_, pl.ds(0, n_rows)], ys_hbm.at[pl.ds(0, n_rows)], ssem.at[slot_]).wait()

    @pl.when(b < nreal)
    def _():
        @pl.when(f == 0)
        def _():
            @pl.when(b == 0)
            def _():
                xbuf[...] = jnp.zeros(xbuf.shape, xbuf.dtype)
                start_gather(0, rows_of(0))
            wait_gather(slot, rows_of(b))
            lo, hi = _unpack_bf16_pair(xbuf[slot])
            gw = X1_PACK_GROUP // 2
            for grp in range(half // gw):
                xb[:, 2 * grp * gw:(2 * grp + 1) * gw] = lo[:, grp * gw:(grp + 1) * gw].astype(BF16)
                xb[:, (2 * grp + 1) * gw:(2 * grp + 2) * gw] = hi[:, grp * gw:(grp + 1) * gw].astype(BF16)

        x = xb[...]
        g = jnp.dot(x, wg_ref[...].astype(BF16), preferred_element_type=F32)
        u = jnp.dot(x, wu_ref[...].astype(BF16), preferred_element_type=F32)
        hidden = (g * (1.0 / (1.0 + jnp.exp(-g))) * u).astype(BF16)
        contrib = jnp.dot(hidden, wd_ref[...].astype(BF16), preferred_element_type=F32)

        @pl.when(f == 0)
        def _():
            acc[...] = contrib

        @pl.when((f > 0) & (f < nf - 1))
        def _():
            acc[...] += contrib

        @pl.when(f == nf - 1)
        def _():
            @pl.when(b >= 1)
            def _():
                wait_scatter(1 - slot, rows_of(b - 1))
            total = acc[...] + contrib
            obuf[slot] = _pack_bf16_pair(total[:, :half], total[:, half:])
            start_scatter(slot, rows_of(b))

            @pl.when(b + 1 < nreal)
            def _():
                start_gather(1 - slot, rows_of(b + 1))

            @pl.when(b + 1 == nreal)
            def _():
                wait_scatter(slot, rows_of(b))


def _expert_mlp(x1p, plan, w_gate, w_up, w_down, tm, fc=256):
    tok_blocks, aid_blocks, block_expert, block_rows, n_real = plan
    n, dp = x1p.shape
    d = 2 * dp
    n_blocks = tok_blocks.shape[0]
    d_ff = w_gate.shape[2]
    nf = d_ff // fc
    assert nf >= 2

    def live(b, f, nreal):
        is_real = b < nreal[0]
        return jnp.where(is_real, b, nreal[0] - 1), jnp.where(is_real, f, nf - 1)

    def tok_map(b, f, be, cnt, nreal):
        bb, ff = live(b, f, nreal)
        return jnp.minimum(bb + (ff == nf - 1).astype(jnp.int32), nreal[0] - 1), 0, 0

    def aid_map(b, f, be, cnt, nreal):
        bb, _ = live(b, f, nreal)
        return bb, 0, 0

    def w_in_map(b, f, be, cnt, nreal):
        bb, ff = live(b, f, nreal)
        return be[bb], 0, ff

    def w_out_map(b, f, be, cnt, nreal):
        bb, ff = live(b, f, nreal)
        return be[bb], ff, 0

    grid_spec = pltpu.PrefetchScalarGridSpec(
        num_scalar_prefetch=3,
        grid=(n_blocks, nf),
        in_specs=[
            pl.BlockSpec((None, 1, tm), tok_map, memory_space=pltpu.SMEM),
            pl.BlockSpec((None, 1, tm), aid_map, memory_space=pltpu.SMEM),
            pl.BlockSpec(memory_space=pl.ANY),
            pl.BlockSpec((None, d, fc), w_in_map),
            pl.BlockSpec((None, d, fc), w_in_map),
            pl.BlockSpec((None, fc, d), w_out_map),
        ],
        out_specs=pl.BlockSpec(memory_space=pl.ANY),
        scratch_shapes=[pltpu.VMEM((2, tm, dp), jnp.uint32), pltpu.VMEM((tm, d), BF16), pltpu.VMEM((tm, d), F32),
                        pltpu.VMEM((2, tm, dp), jnp.uint32),
                        pltpu.SemaphoreType.DMA((2,)), pltpu.SemaphoreType.DMA((2,))],
    )
    kernel = functools.partial(_expert_kernel, tm=tm, nf=nf)
    return pl.pallas_call(
        kernel,
        grid_spec=grid_spec,
        out_shape=jax.ShapeDtypeStruct((n * TOP_K + 2 * tm, dp), jnp.uint32),
        compiler_params=_params("arbitrary", "arbitrary"),
        name="expert_mlp",
    )(block_expert, block_rows, n_real, tok_blocks, aid_blocks, x1p, w_gate, w_up, w_down)


def _combine_kernel(x1_ref, y0_ref, y1_ref, wt_ref, lg_ref, lb_ref, o_ref, buf_ref):
    half = x1_ref.shape[1] // 2
    wt = wt_ref[...]
    y0 = _unpack_bf16_pair(y0_ref[...])
    y1 = _unpack_bf16_pair(y1_ref[...])
    for p, cs in enumerate((slice(0, half), slice(half, 2 * half))):
        buf_ref[:, cs] = DN_ALPHA * x1_ref[:, cs] + (wt[:, 0:1] * y0[p] + wt[:, 1:2] * y1[p])
    hsum = buf_ref[...]
    mu = jnp.mean(hsum, axis=-1, keepdims=True)
    hc = hsum - mu
    var = jnp.mean(jnp.square(hc), axis=-1, keepdims=True)
    o_ref[...] = hc * lax.rsqrt(var + LN_EPS) * lg_ref[...] + lb_ref[...]


def _combine_ln(x1, ys, wts, ln_g, ln_b, tm=256):
    m, d = x1.shape
    big = pl.BlockSpec((tm, d), lambda i: (i, 0))
    first = pl.BlockSpec((tm, d // 2), lambda i: (i, 0))
    second = pl.BlockSpec((tm, d // 2), lambda i: (m // tm + i, 0))
    row_vec = pl.BlockSpec((1, d), lambda i: (0, 0))
    return pl.pallas_call(
        _combine_kernel,
        grid=(m // tm,),
        in_specs=[big, first, second, pl.BlockSpec((tm, LANES), lambda i: (i, 0)), row_vec, row_vec],
        out_specs=big,
        out_shape=jax.ShapeDtypeStruct((m, d), F32),
        scratch_shapes=[pltpu.VMEM((tm, d), F32)],
        compiler_params=_params("arbitrary"),
        name="combine_ln",
    )(x1, ys, ys, wts, ln_g, ln_b)


def _plan_blocks(eid, tm):
    n = eid.shape[0]
    a = n * TOP_K
    expert_id = eid.reshape(-1)
    onehot = (expert_id[:, None] == jnp.arange(N_EXPERTS, dtype=jnp.int32)[None, :]).astype(jnp.int32)
    csum = jnp.cumsum(onehot, axis=0)
    counts = csum[-1]
    rank = jnp.take_along_axis(csum, expert_id[:, None], axis=1)[:, 0] - 1
    padded = (counts + tm - 1) // tm * tm
    pad_ends = jnp.cumsum(padded)
    pad_starts = pad_ends - padded
    n_blocks = a // tm + N_EXPERTS
    pos = pad_starts[expert_id] + rank
    spare = a + jnp.arange(n_blocks * tm, dtype=jnp.int32) % (2 * tm)
    flat = jnp.arange(a, dtype=jnp.int32)
    aid_pad = spare.at[pos].set((flat % TOP_K) * n + flat // TOP_K)
    tok_pad = jnp.where(aid_pad < a, aid_pad % n, 0)
    block_start = jnp.arange(n_blocks, dtype=jnp.int32) * tm
    block_expert = jnp.minimum(jnp.searchsorted(pad_ends, block_start, side='right'), N_EXPERTS - 1).astype(jnp.int32)
    block_count = jnp.clip(counts[block_expert] - (block_start - pad_starts[block_expert]), 0, tm)
    block_rows = ((block_count + SUBLANES - 1) // SUBLANES * SUBLANES).astype(jnp.int32)
    n_real = (pad_ends[-1] // tm).astype(jnp.int32).reshape(1)
    return tok_pad.reshape(n_blocks, 1, tm), aid_pad.reshape(n_blocks, 1, tm), block_expert, block_rows, n_real


def kernel(x, w_in, w_out, lambda_q1, lambda_k1, lambda_q2, lambda_k2, subln_g, beta_attn, gmlp_ln_g, gmlp_ln_b,
           spatial_w, spatial_b, beta_gmlp, rel_bias, ln1_g, ln1_b, w_group, b_group, w_expert, b_expert,
           w_gate, w_up, w_down, ln2_g, ln2_b):
    b, s, d = x.shape
    n = b * s
    d_attn = ATTN_HEADS * ATTN_V_DIM
    d_gmlp = d - d_attn
    moe_tm = 320
    for l in range(DEPTH):
        lambda_init = 0.8 - 0.6 * math.exp(-0.3 * l)
        x2 = x.reshape(n, d)
        proj = _proj_matmul(x2.astype(BF16), w_in[l])
        attn = _diff_attention(proj.reshape(b, s, -1), rel_bias, lambda_q1[l][None], lambda_k1[l][None],
                               lambda_q2[l][None], lambda_k2[l][None], subln_g[l][None], beta_attn[l][None],
                               lambda_init)
        gm = _spatial_gating(proj, gmlp_ln_g[l][None], gmlp_ln_b[l][None], spatial_w[l], spatial_b[l],
                             beta_gmlp[l][None], d_gmlp, u_col=3 * d_attn // d_gmlp, g_col=3 * d_attn // d_gmlp + 1)
        n_route = N_GROUPS + N_EXPERTS
        w_route = jnp.concatenate([w_group[l], w_expert[l].reshape(d, N_EXPERTS),
                                   jnp.zeros((d, LANES - n_route), F32)], axis=1)
        w_route_hi = w_route.astype(BF16)
        w_route = jnp.stack([w_route_hi, (w_route - w_route_hi.astype(F32)).astype(BF16)])
        b_route = jnp.concatenate([b_group[l], b_expert[l].reshape(-1), jnp.zeros((LANES - n_route,), F32)])[None]
        x1, x1p, logits = _out_proj_ln_router(attn.reshape(n, d_attn), gm, w_out[l].astype(BF16), x2,
                                         ln1_g[l][None], ln1_b[l][None], w_route, b_route)
        eid, wts = _route(logits)
        plan = _plan_blocks(eid[:, :TOP_K], moe_tm)
        ys = _expert_mlp(x1p, plan, w_gate[l], w_up[l], w_down[l], moe_tm)
        x = _combine_ln(x1, ys, wts, ln2_g[l][None], ln2_b[l][None]).reshape(b, s, d)
    return x
```

```python
import functools
import math

import jax
import jax.numpy as jnp
from jax import lax
from jax.experimental import pallas as pl
from jax.experimental.pallas import tpu as pltpu

F32 = jnp.float32
BF16 = jnp.bfloat16

ATTN_HEADS = 8
ATTN_HEAD_DIM = 128
ATTN_V_DIM = 2 * ATTN_HEAD_DIM
GMLP_HEADS = 8
CHUNK = 128
REL_BUCKETS = 32
REL_MAX_DIST = 128
N_GROUPS = 8
EXPERTS_PER_GROUP = 8
N_EXPERTS = N_GROUPS * EXPERTS_PER_GROUP
TOP_K = 2
LN_EPS = 1e-5
DEPTH = 1
DN_ALPHA = (2 * DEPTH) ** 0.25
NEG_INF = -1e30
LOG2E = math.log2(math.e)
LANES = 128
SUBLANES = 8
VMEM_LIMIT = 56 * 1024 * 1024
X1_PACK_GROUP = 256


def _params(*semantics):
    return pltpu.CompilerParams(dimension_semantics=semantics, vmem_limit_bytes=VMEM_LIMIT)


def _matmul_kernel(x_ref, w_ref, o_ref, wb_ref):
    @pl.when(pl.program_id(1) == 0)
    def _():
        wb_ref[...] = w_ref[...].astype(BF16)

    o_ref[...] = jnp.dot(x_ref[...], wb_ref[...], preferred_element_type=F32).astype(o_ref.dtype)


def _proj_matmul(xb, w, tm=512, tn=1024):
    m, k = xb.shape
    n = w.shape[1]
    return pl.pallas_call(
        _matmul_kernel,
        grid=(n // tn, m // tm),
        in_specs=[pl.BlockSpec((tm, k), lambda j, i: (i, 0)),
                  pl.BlockSpec((k, tn), lambda j, i: (0, j))],
        out_specs=pl.BlockSpec((tm, tn), lambda j, i: (i, j)),
        out_shape=jax.ShapeDtypeStruct((m, n), BF16),
        scratch_shapes=[pltpu.VMEM((k, tn), BF16)],
        compiler_params=_params("arbitrary", "arbitrary"),
        name="proj_matmul",
    )(xb, w)


def _rel_bucket(n):
    max_exact = REL_BUCKETS // 2
    nf = jnp.maximum(n, max_exact).astype(F32)
    large = max_exact + (jnp.log(nf / max_exact) / math.log(REL_MAX_DIST / max_exact)
                         * (REL_BUCKETS - max_exact)).astype(jnp.int32)
    large = jnp.minimum(large, REL_BUCKETS - 1)
    return jnp.where(n < max_exact, n, large)


def _rel_bias_tiles(rel_bias, blk):
    pos = jnp.arange(blk)
    tiles = []
    for d in (0, 1):
        bucket = _rel_bucket(jnp.maximum(pos[:, None] + d * blk - pos[None, :], 0))
        hit = bucket[None] == jnp.arange(REL_BUCKETS)[:, None, None]
        tiles.append(jnp.sum(jnp.where(hit[:, None], rel_bias[:, :, None, None], 0.0), axis=0))
    return jnp.stack(tiles, axis=1).astype(F32)


def _attn_kernel(far_ref, q_ref, k_ref, v_ref, bias_ref, lq1_ref, lk1_ref, lq2_ref, lk2_ref,
                 sg_ref, ba_ref, o_ref, mx_ref, sh_ref, mrow_ref, l_ref, s_ref, acc_ref, *, blk, hp, lambda_init):
    g = pl.program_id(1)
    i = pl.program_id(2)
    dh = ATTN_HEAD_DIM
    dv = ATTN_V_DIM
    c1 = ATTN_HEAD_DIM ** -0.5 * LOG2E
    streams = [(hh, mi) for hh in range(hp) for mi in range(2)]
    q = q_ref[...]
    qs = [q[:, hh * dv + mi * dh:hh * dv + (mi + 1) * dh] for hh, mi in streams]
    contract_last = (((1,), (1,)), ((), ()))
    far2 = [far_ref[g * hp + hh] * LOG2E for hh in range(hp)]
    n_far = jnp.maximum(i - 1, 0)

    def lane_chunks(x):
        return [x[:, c * LANES:(c + 1) * LANES] for c in range(x.shape[1] // LANES)]

    def raw_scores(start, width):
        kj = k_ref[pl.ds(start, width), :]
        return [lax.dot_general(qs[sid], kj[:, hh * dv + mi * dh:hh * dv + (mi + 1) * dh], contract_last,
                                preferred_element_type=F32) for sid, (hh, mi) in enumerate(streams)]

    def over_far_blocks(body):
        def pair(t, carry):
            body(pl.multiple_of(t * (2 * blk), 2 * blk), 2 * blk)
            return carry
        lax.fori_loop(0, lax.shift_right_logical(n_far, 1), pair, 0)

        @pl.when(lax.rem(n_far, 2) == 1)
        def _():
            body(pl.multiple_of((n_far - 1) * blk, blk), blk)

    mx_ref[...] = jnp.full(mx_ref.shape, -jnp.inf, F32)

    def max_body(start, width):
        s = raw_scores(start, width)
        for sid in range(len(streams)):
            best = mx_ref[sid]
            for chunk in lane_chunks(s[sid]):
                best = jnp.maximum(best, chunk)
            mx_ref[sid] = best

    over_far_blocks(max_body)

    row = lax.broadcasted_iota(jnp.int32, (blk, blk), 0)
    col = lax.broadcasted_iota(jnp.int32, (blk, blk), 1)
    near = raw_scores(pl.multiple_of(n_far * blk, blk), blk)
    diag = raw_scores(pl.multiple_of(i * blk, blk), blk)
    for sid, (hh, mi) in enumerate(streams):
        s_near = jnp.where(i >= 1, near[sid] * c1 + bias_ref[hh, 1], NEG_INF)
        s_diag = jnp.where(col <= row, diag[sid] * c1 + bias_ref[hh, 0], NEG_INF)
        s_ref[sid, 0] = s_near
        s_ref[sid, 1] = s_diag
        best = mx_ref[sid] * c1 + far2[hh]
        for chunk in lane_chunks(s_near) + lane_chunks(s_diag):
            best = jnp.maximum(best, chunk)
        m_rows = jnp.broadcast_to(jnp.max(best, axis=-1, keepdims=True), (blk, LANES))
        mrow_ref[sid] = m_rows
        sh_ref[sid] = far2[hh] - m_rows

    l_ref[...] = jnp.zeros(l_ref.shape, F32)
    acc_ref[...] = jnp.zeros(acc_ref.shape, F32)

    def accumulate(p_chunks, vj):
        ps = []
        for sid in range(len(streams)):
            tot = l_ref[sid]
            for chunk in p_chunks[sid]:
                tot = tot + chunk
            l_ref[sid] = tot
            ps.append(jnp.concatenate(p_chunks[sid], axis=1).astype(BF16))
        for hh in range(hp):
            acc_ref[hh] += jnp.dot(jnp.concatenate(ps[2 * hh:2 * hh + 2], axis=0), vj[:, hh * dv:(hh + 1) * dv],
                                   preferred_element_type=F32)

    def pv_body(start, width):
        s = raw_scores(start, width)
        accumulate([[jnp.exp2(chunk * c1 + sh_ref[sid]) for chunk in lane_chunks(s[sid])]
                    for sid in range(len(streams))], v_ref[pl.ds(start, width), :])

    over_far_blocks(pv_body)

    for t, start in ((0, pl.multiple_of(n_far * blk, blk)), (1, pl.multiple_of(i * blk, blk))):
        accumulate([[jnp.exp2(chunk - mrow_ref[sid]) for chunk in lane_chunks(s_ref[sid, t])]
                    for sid in range(len(streams))], v_ref[pl.ds(start, blk), :])

    lam = (jnp.exp(jnp.sum(lq1_ref[...] * lk1_ref[...], axis=-1, keepdims=True))
           - jnp.exp(jnp.sum(lq2_ref[...] * lk2_ref[...], axis=-1, keepdims=True)) + lambda_init)
    for hh in range(hp):
        l1 = jnp.sum(l_ref[2 * hh], axis=-1, keepdims=True)
        l2 = jnp.sum(l_ref[2 * hh + 1], axis=-1, keepdims=True)
        o = acc_ref[hh, :blk] / l1 - lam * (acc_ref[hh, blk:] / l2)
        o = o * lax.rsqrt(jnp.mean(jnp.square(o), axis=-1, keepdims=True) + LN_EPS) * sg_ref[...]
        o = o * (1.0 - lambda_init) * ba_ref[:, hh * dv:(hh + 1) * dv]
        o_ref[:, hh * dv:(hh + 1) * dv] = o.astype(o_ref.dtype)


def _diff_attention(proj3, rel_bias, lq1, lk1, lq2, lk2, subln_g, beta_attn, lambda_init, blk=256, hp=4):
    b, s, _ = proj3.shape
    h = ATTN_HEADS
    dv = ATTN_V_DIM
    gw = hp * dv
    groups = h // hp
    assert REL_BUCKETS // 2 + int(math.log((blk + 1) / (REL_BUCKETS // 2)) / math.log(REL_MAX_DIST / (REL_BUCKETS // 2))
                                  * (REL_BUCKETS // 2)) >= REL_BUCKETS
    tiles = _rel_bias_tiles(rel_bias, blk) * LOG2E
    far = rel_bias[REL_BUCKETS - 1].astype(F32)
    vec = lambda c: pl.BlockSpec((1, c), lambda bi, gi, qi: (0, 0))
    kernel = functools.partial(_attn_kernel, blk=blk, hp=hp, lambda_init=lambda_init)
    n_streams = 2 * hp
    stat = pltpu.VMEM((n_streams, blk, LANES), F32)
    return pl.pallas_call(
        kernel,
        grid=(b, groups, s // blk),
        in_specs=[
            pl.BlockSpec(memory_space=pltpu.SMEM),
            pl.BlockSpec((None, blk, gw), lambda bi, gi, qi: (bi, qi, gi)),
            pl.BlockSpec((None, s, gw), lambda bi, gi, qi: (bi, 0, groups + gi)),
            pl.BlockSpec((None, s, gw), lambda bi, gi, qi: (bi, 0, 2 * groups + gi)),
            pl.BlockSpec((hp, 2, blk, blk), lambda bi, gi, qi: (gi, 0, 0, 0)),
            vec(ATTN_HEAD_DIM), vec(ATTN_HEAD_DIM), vec(ATTN_HEAD_DIM), vec(ATTN_HEAD_DIM),
            vec(dv),
            pl.BlockSpec((1, gw), lambda bi, gi, qi: (0, gi)),
        ],
        out_specs=pl.BlockSpec((None, blk, gw), lambda bi, gi, qi: (bi, qi, gi)),
        out_shape=jax.ShapeDtypeStruct((b, s, h * dv), BF16),
        scratch_shapes=[stat, stat, stat, stat,
                        pltpu.VMEM((n_streams, 2, blk, blk), F32), pltpu.VMEM((hp, 2 * blk, dv), F32)],
        compiler_params=_params("arbitrary", "arbitrary", "arbitrary"),
        name="diff_attention",
    )(far, proj3, proj3, proj3, tiles, lq1, lk1, lq2, lk2, subln_g, beta_attn)


def _gelu(x):
    return 0.5 * x * (1.0 + lax.erf(x * (2.0 ** -0.5)))


def _gating_kernel(u_ref, g_ref, lg_ref, lb_ref, ws_ref, bs_ref, bg_ref, o_ref, buf_ref):
    hd = u_ref.shape[1] // GMLP_HEADS
    g = _gelu(g_ref[...].astype(F32))
    mu = jnp.mean(g, axis=-1, keepdims=True)
    gc = g - mu
    var = jnp.mean(jnp.square(gc), axis=-1, keepdims=True)
    v = (gc * lax.rsqrt(var + LN_EPS) * lg_ref[...] + lb_ref[...]).astype(BF16)
    row = lax.broadcasted_iota(jnp.int32, (CHUNK, CHUNK), 0)
    col = lax.broadcasted_iota(jnp.int32, (CHUNK, CHUNK), 1)
    tril = col <= row
    ss = jnp.zeros((CHUNK, 1), F32)
    for hh in range(GMLP_HEADS):
        w = jnp.where(tril, ws_ref[hh], 0.0).astype(BF16)
        mixed = jnp.dot(w, v[:, hh * hd:(hh + 1) * hd], preferred_element_type=F32) + bs_ref[hh]
        out = _gelu(u_ref[:, hh * hd:(hh + 1) * hd].astype(F32)) * mixed
        ss = ss + jnp.sum(jnp.square(out), axis=-1, keepdims=True)
        buf_ref[:, hh * hd:(hh + 1) * hd] = out
    rstd = lax.rsqrt(ss / u_ref.shape[1] + LN_EPS)
    o_ref[...] = (buf_ref[...] * rstd * bg_ref[...]).astype(o_ref.dtype)


def _spatial_gating(proj2, ln_g, ln_b, w_s, b_s, beta_gmlp, d_gmlp, u_col, g_col):
    n = proj2.shape[0]
    row_vec = pl.BlockSpec((1, d_gmlp), lambda c: (0, 0))
    return pl.pallas_call(
        _gating_kernel,
        grid=(n // CHUNK,),
        in_specs=[
            pl.BlockSpec((CHUNK, d_gmlp), lambda c: (c, u_col)),
            pl.BlockSpec((CHUNK, d_gmlp), lambda c: (c, g_col)),
            row_vec, row_vec,
            pl.BlockSpec((GMLP_HEADS, CHUNK, CHUNK), lambda c: (0, 0, 0)),
            pl.BlockSpec((GMLP_HEADS, CHUNK, 1), lambda c: (0, 0, 0)),
            row_vec,
        ],
        out_specs=pl.BlockSpec((CHUNK, d_gmlp), lambda c: (c, 0)),
        out_shape=jax.ShapeDtypeStruct((n, d_gmlp), BF16),
        scratch_shapes=[pltpu.VMEM((CHUNK, d_gmlp), F32)],
        compiler_params=_params("arbitrary"),
        name="spatial_gating",
    )(proj2, proj2, ln_g, ln_b, w_s, b_s[:, :, None], beta_gmlp)


def _layer_norm_chunks(buf_ref, n_chunks, d):
    tot = jnp.sum(buf_ref[0], axis=-1, keepdims=True)
    for c in range(1, n_chunks):
        tot = tot + jnp.sum(buf_ref[c], axis=-1, keepdims=True)
    mu = tot / d
    sq = jnp.sum(jnp.square(buf_ref[0] - mu), axis=-1, keepdims=True)
    for c in range(1, n_chunks):
        sq = sq + jnp.sum(jnp.square(buf_ref[c] - mu), axis=-1, keepdims=True)
    return mu, lax.rsqrt(sq / d + LN_EPS)


def _pack_bf16_pair(lo, hi):
    lo_bits = lax.bitcast_convert_type(lo.astype(BF16).astype(F32), jnp.uint32) >> 16
    hi_bits = lax.bitcast_convert_type(hi.astype(BF16).astype(F32), jnp.uint32) & jnp.uint32(0xFFFF0000)
    return hi_bits | lo_bits


def _unpack_bf16_pair(words):
    lo = lax.bitcast_convert_type(words << 16, F32)
    hi = lax.bitcast_convert_type(words & jnp.uint32(0xFFFF0000), F32)
    return lo, hi


def _out_proj_kernel(a_ref, g_ref, w_ref, x_ref, lg_ref, lb_ref, wr_ref, br_ref,
                     x1_ref, x1p_ref, lo_ref, buf_ref, mu_ref, rstd_ref, *, n_tiles, n_chunks, tn):
    i = pl.program_id(0)
    j = pl.program_id(1)
    ka = a_ref.shape[1]

    def finish_chunk():
        y = (buf_ref[j] - mu_ref[...]) * rstd_ref[...] * lg_ref[...] + lb_ref[...]
        x1_ref[...] = y
        x1p_ref[...] = _pack_bf16_pair(y[:, :tn // 2], y[:, tn // 2:])
        y_hi = y.astype(BF16)
        y_lo = (y - y_hi.astype(F32)).astype(BF16)
        start = jnp.where(j == 0, jnp.broadcast_to(br_ref[...], lo_ref.shape), lo_ref[...])
        lo_ref[...] = start + (jnp.dot(y_hi, wr_ref[0], preferred_element_type=F32)
                               + (jnp.dot(y_lo, wr_ref[0], preferred_element_type=F32)
                                  + jnp.dot(y_hi, wr_ref[1], preferred_element_type=F32)))

    def matmul_chunk():
        mix = (jnp.dot(a_ref[...], w_ref[:ka, :], preferred_element_type=F32)
               + jnp.dot(g_ref[...], w_ref[ka:, :], preferred_element_type=F32))
        buf_ref[j] = DN_ALPHA * x_ref[...] + mix

        @pl.when(j == n_chunks - 1)
        def _():
            mu, rstd = _layer_norm_chunks(buf_ref, n_chunks, n_chunks * tn)
            mu_ref[...] = mu
            rstd_ref[...] = rstd

    @pl.when(i == 0)
    def _():
        matmul_chunk()

    @pl.when((i > 0) & (i < n_tiles))
    def _():
        finish_chunk()
        matmul_chunk()

    @pl.when(i == n_tiles)
    def _():
        finish_chunk()


def _out_proj_ln_router(attn2, gm2, wb, x2, ln_g, ln_b, w_route, b_route, tm=1024, tn=X1_PACK_GROUP):
    m, d = x2.shape
    ka = attn2.shape[1]
    kg = gm2.shape[1]
    n_chunks = d // tn
    n_tiles = m // tm
    tile_in = lambda i: jnp.minimum(i, n_tiles - 1)
    tile_out = lambda i: jnp.maximum(i - 1, 0)
    chunk_in = lambda i, j: jnp.where(i < n_tiles, j, n_chunks - 1)
    chunk_out = lambda i, j: jnp.where(i >= 1, j, 0)
    kernel = functools.partial(_out_proj_kernel, n_tiles=n_tiles, n_chunks=n_chunks, tn=tn)
    return pl.pallas_call(
        kernel,
        grid=(n_tiles + 1, n_chunks),
        in_specs=[
            pl.BlockSpec((tm, ka), lambda i, j: (tile_in(i), 0)),
            pl.BlockSpec((tm, kg), lambda i, j: (tile_in(i), 0)),
            pl.BlockSpec((ka + kg, tn), lambda i, j: (0, chunk_in(i, j))),
            pl.BlockSpec((tm, tn), lambda i, j: (tile_in(i), chunk_in(i, j))),
            pl.BlockSpec((1, tn), lambda i, j: (0, chunk_out(i, j))),
            pl.BlockSpec((1, tn), lambda i, j: (0, chunk_out(i, j))),
            pl.BlockSpec((2, tn, LANES), lambda i, j: (0, chunk_out(i, j), 0)),
            pl.BlockSpec((1, LANES), lambda i, j: (0, 0)),
        ],
        out_specs=[
            pl.BlockSpec((tm, tn), lambda i, j: (tile_out(i), chunk_out(i, j))),
            pl.BlockSpec((tm, tn // 2), lambda i, j: (tile_out(i), chunk_out(i, j))),
            pl.BlockSpec((tm, LANES), lambda i, j: (tile_out(i), 0)),
        ],
        out_shape=[jax.ShapeDtypeStruct((m, d), F32), jax.ShapeDtypeStruct((m, d // 2), jnp.uint32),
                   jax.ShapeDtypeStruct((m, LANES), F32)],
        scratch_shapes=[pltpu.VMEM((n_chunks, tm, tn), F32), pltpu.VMEM((tm, 1), F32), pltpu.VMEM((tm, 1), F32)],
        compiler_params=_params("arbitrary", "arbitrary"),
        name="out_proj_ln_router",
    )(attn2, gm2, wb, x2, ln_g, ln_b, w_route, b_route)


def _route_kernel(lo_ref, eid_ref, wt_ref, cnt_ref):
    lg = lo_ref[...]
    lane = lax.broadcasted_iota(jnp.int32, lg.shape, 1)
    lane_f = lane.astype(F32)
    none = float(LANES)
    first = lambda hit: jnp.min(jnp.where(hit, lane_f, none), axis=-1, keepdims=True)

    in_groups = lane < N_GROUPS
    g_logits = jnp.where(in_groups, lg, -jnp.inf)
    g_max = jnp.max(g_logits, axis=-1, keepdims=True)
    g_idx = first(g_logits == g_max)
    g_w = 1.0 / jnp.sum(jnp.where(in_groups, jnp.exp(lg - g_max), 0.0), axis=-1, keepdims=True)

    lo = N_GROUPS + g_idx * EXPERTS_PER_GROUP
    in_group = (lane_f >= lo) & (lane_f < lo + EXPERTS_PER_GROUP)
    e_logits = jnp.where(in_group, lg, -jnp.inf)
    t1 = jnp.max(e_logits, axis=-1, keepdims=True)
    i1 = first(e_logits == t1)
    e_rest = jnp.where(lane_f == i1, -jnp.inf, e_logits)
    t2 = jnp.max(e_rest, axis=-1, keepdims=True)
    i2 = first(e_rest == t2)
    ex = jnp.exp(t2 - t1)
    w1 = g_w * (1.0 / (1.0 + ex))
    w2 = g_w * (ex / (1.0 + ex))
    wt_ref[...] = jnp.where(lane == 0, w1, jnp.where(lane == 1, w2, 0.0))

    @pl.when(pl.program_id(0) == 0)
    def _():
        cnt_ref[...] = jnp.zeros(cnt_ref.shape, F32)

    tm = lg.shape[0]
    pick1 = lane_f == i1 - N_GROUPS
    pick2 = lane_f == i2 - N_GROUPS
    chosen = pick1.astype(F32) + pick2.astype(F32)
    earlier = (lax.broadcasted_iota(jnp.int32, (tm, tm), 1) < lax.broadcasted_iota(jnp.int32, (tm, tm), 0))
    before = jnp.dot(earlier.astype(BF16), chosen.astype(BF16), preferred_element_type=F32) + cnt_ref[...]
    r1 = jnp.sum(jnp.where(pick1, before, 0.0), axis=-1, keepdims=True)
    r2 = jnp.sum(jnp.where(pick2, before, 0.0), axis=-1, keepdims=True)
    cnt_ref[...] += jnp.sum(chosen, axis=0, keepdims=True)
    picks = (jnp.where(lane == 0, i1 - N_GROUPS, 0.0) + jnp.where(lane == 1, i2 - N_GROUPS, 0.0)
             + jnp.where(lane == 2, r1, 0.0) + jnp.where(lane == 3, r2, 0.0))
    eid_ref[...] = picks.astype(jnp.int32)


def _route(logits, tm=1024):
    m = logits.shape[0]
    spec = pl.BlockSpec((tm, LANES), lambda i: (i, 0))
    return pl.pallas_call(
        _route_kernel,
        grid=(m // tm,),
        in_specs=[spec],
        out_specs=[spec, spec, pl.BlockSpec((1, LANES), lambda i: (0, 0))],
        out_shape=[jax.ShapeDtypeStruct((m, LANES), jnp.int32), jax.ShapeDtypeStruct((m, LANES), F32),
                   jax.ShapeDtypeStruct((1, LANES), F32)],
        compiler_params=_params("arbitrary"),
        name="route",
    )(logits)


def _expert_kernel(be_ref, cnt_ref, nreal_ref, tok_ref, aid_ref, x_hbm, wg_ref, wu_ref, wd_ref, ys_hbm,
                   xbuf, xb, acc, obuf, gsem, ssem, *, tm, nf):
    half = xb.shape[1] // 2
    b = pl.program_id(0)
    f = pl.program_id(1)
    nreal = nreal_ref[0]
    slot = lax.rem(b, 2)

    def rows_of(blk):
        return pl.multiple_of(cnt_ref[blk], SUBLANES)

    def for_each_row(n_rows, copy_of_row):
        def body(grp, carry):
            base = pl.multiple_of(grp * SUBLANES, SUBLANES)
            for k in range(SUBLANES):
                copy_of_row(base + k).start(priority=k % 2)
            return carry
        lax.fori_loop(0, lax.shift_right_logical(n_rows, SUBLANES.bit_length() - 1), body, 0)

    def start_gather(slot_, n_rows):
        for_each_row(n_rows, lambda r: pltpu.make_async_copy(
            x_hbm.at[pl.ds(tok_ref[0, r], 1)], xbuf.at[slot_, pl.ds(r, 1)], gsem.at[slot_]))

    def wait_gather(slot_, n_rows):
        pltpu.make_async_copy(x_hbm.at[pl.ds(0, n_rows)], xbuf.at[slot_, pl.ds(0, n_rows)], gsem.at[slot_]).wait()

    def start_scatter(slot_, n_rows):
        for_each_row(n_rows, lambda r: pltpu.make_async_copy(
            obuf.at[slot_, pl.ds(r, 1)], ys_hbm.at[pl.ds(aid_ref[0, r], 1)], ssem.at[slot_]))

    def wait_scatter(slot_, n_rows):
        pltpu.make_async_copy(obuf.at[slot_, pl.ds(0, n_rows)], ys_hbm.at[pl.ds(0, n_rows)], ssem.at[slot_]).wait()

    @pl.when(b < nreal)
    def _():
        @pl.when(f == 0)
        def _():
            @pl.when(b == 0)
            def _():
                xbuf[...] = jnp.zeros(xbuf.shape, xbuf.dtype)
                start_gather(0, rows_of(0))
            wait_gather(slot, rows_of(b))
            lo, hi = _unpack_bf16_pair(xbuf[slot])
            gw = X1_PACK_GROUP // 2
            for grp in range(half // gw):
                xb[:, 2 * grp * gw:(2 * grp + 1) * gw] = lo[:, grp * gw:(grp + 1) * gw].astype(BF16)
                xb[:, (2 * grp + 1) * gw:(2 * grp + 2) * gw] = hi[:, grp * gw:(grp + 1) * gw].astype(BF16)

        x = xb[...]
        g = jnp.dot(x, wg_ref[...].astype(BF16), preferred_element_type=F32)
        u = jnp.dot(x, wu_ref[...].astype(BF16), preferred_element_type=F32)
        hidden = (g * (1.0 / (1.0 + jnp.exp(-g))) * u).astype(BF16)
        contrib = jnp.dot(hidden, wd_ref[...].astype(BF16), preferred_element_type=F32)

        @pl.when(f == 0)
        def _():
            acc[...] = contrib

        @pl.when((f > 0) & (f < nf - 1))
        def _():
            acc[...] += contrib

        @pl.when(f == nf - 1)
        def _():
            @pl.when(b >= 1)
            def _():
                wait_scatter(1 - slot, rows_of(b - 1))
            total = acc[...] + contrib
            obuf[slot] = _pack_bf16_pair(total[:, :half], total[:, half:])
            start_scatter(slot, rows_of(b))

            @pl.when(b + 1 < nreal)
            def _():
                start_gather(1 - slot, rows_of(b + 1))

            @pl.when(b + 1 == nreal)
            def _():
                wait_scatter(slot, rows_of(b))


def _expert_mlp(x1p, plan, w_gate, w_up, w_down, tm, fc=256):
    tok_blocks, aid_blocks, block_expert, block_rows, n_real = plan
    n, dp = x1p.shape
    d = 2 * dp
    n_blocks = tok_blocks.shape[0]
    d_ff = w_gate.shape[2]
    nf = d_ff // fc
    assert nf >= 2

    def live(b, f, nreal):
        is_real = b < nreal[0]
        return jnp.where(is_real, b, nreal[0] - 1), jnp.where(is_real, f, nf - 1)

    def tok_map(b, f, be, cnt, nreal):
        bb, ff = live(b, f, nreal)
        return jnp.minimum(bb + (ff == nf - 1).astype(jnp.int32), nreal[0] - 1), 0, 0

    def aid_map(b, f, be, cnt, nreal):
        bb, _ = live(b, f, nreal)
        return bb, 0, 0

    def w_in_map(b, f, be, cnt, nreal):
        bb, ff = live(b, f, nreal)
        return be[bb], 0, ff

    def w_out_map(b, f, be, cnt, nreal):
        bb, ff = live(b, f, nreal)
        return be[bb], ff, 0

    grid_spec = pltpu.PrefetchScalarGridSpec(
        num_scalar_prefetch=3,
        grid=(n_blocks, nf),
        in_specs=[
            pl.BlockSpec((None, 1, tm), tok_map, memory_space=pltpu.SMEM),
            pl.BlockSpec((None, 1, tm), aid_map, memory_space=pltpu.SMEM),
            pl.BlockSpec(memory_space=pl.ANY),
            pl.BlockSpec((None, d, fc), w_in_map),
            pl.BlockSpec((None, d, fc), w_in_map),
            pl.BlockSpec((None, fc, d), w_out_map),
        ],
        out_specs=pl.BlockSpec(memory_space=pl.ANY),
        scratch_shapes=[pltpu.VMEM((2, tm, dp), jnp.uint32), pltpu.VMEM((tm, d), BF16), pltpu.VMEM((tm, d), F32),
                        pltpu.VMEM((2, tm, dp), jnp.uint32),
                        pltpu.SemaphoreType.DMA((2,)), pltpu.SemaphoreType.DMA((2,))],
    )
    kernel = functools.partial(_expert_kernel, tm=tm, nf=nf)
    return pl.pallas_call(
        kernel,
        grid_spec=grid_spec,
        out_shape=jax.ShapeDtypeStruct((n * TOP_K + 2 * tm, dp), jnp.uint32),
        compiler_params=_params("arbitrary", "arbitrary"),
        name="expert_mlp",
    )(block_expert, block_rows, n_real, tok_blocks, aid_blocks, x1p, w_gate, w_up, w_down)


def _combine_kernel(x1_ref, y0_ref, y1_ref, wt_ref, lg_ref, lb_ref, o_ref, buf_ref):
    half = x1_ref.shape[1] // 2
    wt = wt_ref[...]
    y0 = _unpack_bf16_pair(y0_ref[...])
    y1 = _unpack_bf16_pair(y1_ref[...])
    for p, cs in enumerate((slice(0, half), slice(half, 2 * half))):
        buf_ref[:, cs] = DN_ALPHA * x1_ref[:, cs] + (wt[:, 0:1] * y0[p] + wt[:, 1:2] * y1[p])
    hsum = buf_ref[...]
    mu = jnp.mean(hsum, axis=-1, keepdims=True)
    hc = hsum - mu
    var = jnp.mean(jnp.square(hc), axis=-1, keepdims=True)
    o_ref[...] = hc * lax.rsqrt(var + LN_EPS) * lg_ref[...] + lb_ref[...]


def _combine_ln(x1, ys, wts, ln_g, ln_b, tm=256):
    m, d = x1.shape
    big = pl.BlockSpec((tm, d), lambda i: (i, 0))
    first = pl.BlockSpec((tm, d // 2), lambda i: (i, 0))
    second = pl.BlockSpec((tm, d // 2), lambda i: (m // tm + i, 0))
    row_vec = pl.BlockSpec((1, d), lambda i: (0, 0))
    return pl.pallas_call(
        _combine_kernel,
        grid=(m // tm,),
        in_specs=[big, first, second, pl.BlockSpec((tm, LANES), lambda i: (i, 0)), row_vec, row_vec],
        out_specs=big,
        out_shape=jax.ShapeDtypeStruct((m, d), F32),
        scratch_shapes=[pltpu.VMEM((tm, d), F32)],
        compiler_params=_params("arbitrary"),
        name="combine_ln",
    )(x1, ys, ys, wts, ln_g, ln_b)


def _plan_blocks(eid, rank, counts, tm):
    n = eid.shape[0]
    a = n * TOP_K
    expert_id = eid.reshape(-1)
    padded = (counts + tm - 1) // tm * tm
    pad_ends = jnp.cumsum(padded)
    pad_starts = pad_ends - padded
    n_blocks = a // tm + N_EXPERTS
    pos = pad_starts[expert_id] + rank.reshape(-1)
    spare = a + jnp.arange(n_blocks * tm, dtype=jnp.int32) % (2 * tm)
    flat = jnp.arange(a, dtype=jnp.int32)
    aid_pad = spare.at[pos].set((flat % TOP_K) * n + flat // TOP_K, unique_indices=True)
    tok_pad = jnp.where(aid_pad < a, aid_pad % n, 0)
    block_start = jnp.arange(n_blocks, dtype=jnp.int32) * tm
    block_expert = jnp.minimum(jnp.searchsorted(pad_ends, block_start, side='right'), N_EXPERTS - 1).astype(jnp.int32)
    block_count = jnp.clip(counts[block_expert] - (block_start - pad_starts[block_expert]), 0, tm)
    block_rows = ((block_count + SUBLANES - 1) // SUBLANES * SUBLANES).astype(jnp.int32)
    n_real = (pad_ends[-1] // tm).astype(jnp.int32).reshape(1)
    return tok_pad.reshape(n_blocks, 1, tm), aid_pad.reshape(n_blocks, 1, tm), block_expert, block_rows, n_real


def kernel(x, w_in, w_out, lambda_q1, lambda_k1, lambda_q2, lambda_k2, subln_g, beta_attn, gmlp_ln_g, gmlp_ln_b,
           spatial_w, spatial_b, beta_gmlp, rel_bias, ln1_g, ln1_b, w_group, b_group, w_expert, b_expert,
           w_gate, w_up, w_down, ln2_g, ln2_b):
    b, s, d = x.shape
    n = b * s
    d_attn = ATTN_HEADS * ATTN_V_DIM
    d_gmlp = d - d_attn
    moe_tm = 320
    for l in range(DEPTH):
        lambda_init = 0.8 - 0.6 * math.exp(-0.3 * l)
        x2 = x.reshape(n, d)
        proj = _proj_matmul(x2.astype(BF16), w_in[l])
        attn = _diff_attention(proj.reshape(b, s, -1), rel_bias, lambda_q1[l][None], lambda_k1[l][None],
                               lambda_q2[l][None], lambda_k2[l][None], subln_g[l][None], beta_attn[l][None],
                               lambda_init)
        gm = _spatial_gating(proj, gmlp_ln_g[l][None], gmlp_ln_b[l][None], spatial_w[l], spatial_b[l],
                             beta_gmlp[l][None], d_gmlp, u_col=3 * d_attn // d_gmlp, g_col=3 * d_attn // d_gmlp + 1)
        n_route = N_GROUPS + N_EXPERTS
        w_route = jnp.concatenate([w_group[l], w_expert[l].reshape(d, N_EXPERTS),
                                   jnp.zeros((d, LANES - n_route), F32)], axis=1)
        w_route_hi = w_route.astype(BF16)
        w_route = jnp.stack([w_route_hi, (w_route - w_route_hi.astype(F32)).astype(BF16)])
        b_route = jnp.concatenate([b_group[l], b_expert[l].reshape(-1), jnp.zeros((LANES - n_route,), F32)])[None]
        x1, x1p, logits = _out_proj_ln_router(attn.reshape(n, d_attn), gm, w_out[l].astype(BF16), x2,
                                         ln1_g[l][None], ln1_b[l][None], w_route, b_route)
        picks, wts, counts = _route(logits)
        plan = _plan_blocks(picks[:, :TOP_K], picks[:, TOP_K:2 * TOP_K], counts[0, :N_EXPERTS].astype(jnp.int32),
                            moe_tm)
        ys = _expert_mlp(x1p, plan, w_gate[l], w_up[l], w_down[l], moe_tm)
        x = _combine_ln(x1, ys, wts, ln2_g[l][None], ln2_b[l][None]).reshape(b, s, d)
    return x
```

```python
import functools
import math

import jax
import jax.numpy as jnp
from jax import lax
from jax.experimental import pallas as pl
from jax.experimental.pallas import tpu as pltpu

F32 = jnp.float32
BF16 = jnp.bfloat16

ATTN_HEADS = 8
ATTN_HEAD_DIM = 128
ATTN_V_DIM = 2 * ATTN_HEAD_DIM
GMLP_HEADS = 8
CHUNK = 128
REL_BUCKETS = 32
REL_MAX_DIST = 128
N_GROUPS = 8
EXPERTS_PER_GROUP = 8
N_EXPERTS = N_GROUPS * EXPERTS_PER_GROUP
TOP_K = 2
LN_EPS = 1e-5
DEPTH = 1
DN_ALPHA = (2 * DEPTH) ** 0.25
NEG_INF = -1e30
LOG2E = math.log2(math.e)
LANES = 128
SUBLANES = 8
VMEM_LIMIT = 56 * 1024 * 1024
X1_PACK_GROUP = 256


def _params(*semantics):
    return pltpu.CompilerParams(dimension_semantics=semantics, vmem_limit_bytes=VMEM_LIMIT)


def _matmul_kernel(x_ref, w_ref, o_ref, wb_ref):
    @pl.when(pl.program_id(1) == 0)
    def _():
        wb_ref[...] = w_ref[...].astype(BF16)

    o_ref[...] = jnp.dot(x_ref[...], wb_ref[...], preferred_element_type=F32).astype(o_ref.dtype)


def _proj_matmul(xb, w, tm=512, tn=1024):
    m, k = xb.shape
    n = w.shape[1]
    return pl.pallas_call(
        _matmul_kernel,
        grid=(n // tn, m // tm),
        in_specs=[pl.BlockSpec((tm, k), lambda j, i: (i, 0)),
                  pl.BlockSpec((k, tn), lambda j, i: (0, j))],
        out_specs=pl.BlockSpec((tm, tn), lambda j, i: (i, j)),
        out_shape=jax.ShapeDtypeStruct((m, n), BF16),
        scratch_shapes=[pltpu.VMEM((k, tn), BF16)],
        compiler_params=_params("arbitrary", "arbitrary"),
        name="proj_matmul",
    )(xb, w)


def _rel_bucket(n):
    max_exact = REL_BUCKETS // 2
    nf = jnp.maximum(n, max_exact).astype(F32)
    large = max_exact + (jnp.log(nf / max_exact) / math.log(REL_MAX_DIST / max_exact)
                         * (REL_BUCKETS - max_exact)).astype(jnp.int32)
    large = jnp.minimum(large, REL_BUCKETS - 1)
    return jnp.where(n < max_exact, n, large)


def _rel_bias_tiles(rel_bias, blk):
    pos = jnp.arange(blk)
    tiles = []
    for d in (0, 1):
        bucket = _rel_bucket(jnp.maximum(pos[:, None] + d * blk - pos[None, :], 0))
        hit = bucket[None] == jnp.arange(REL_BUCKETS)[:, None, None]
        tiles.append(jnp.sum(jnp.where(hit[:, None], rel_bias[:, :, None, None], 0.0), axis=0))
    return jnp.stack(tiles, axis=1).astype(F32)


def _attn_kernel(far_ref, q_ref, k_ref, v_ref, bias_ref, lq1_ref, lk1_ref, lq2_ref, lk2_ref,
                 sg_ref, ba_ref, o_ref, mx_ref, sh_ref, mrow_ref, l_ref, s_ref, acc_ref, *, blk, hp, lambda_init):
    g = pl.program_id(1)
    i = pl.program_id(2)
    dh = ATTN_HEAD_DIM
    dv = ATTN_V_DIM
    c1 = ATTN_HEAD_DIM ** -0.5 * LOG2E
    streams = [(hh, mi) for hh in range(hp) for mi in range(2)]
    q = q_ref[...]
    qs = [q[:, hh * dv + mi * dh:hh * dv + (mi + 1) * dh] for hh, mi in streams]
    contract_last = (((1,), (1,)), ((), ()))
    far2 = [far_ref[g * hp + hh] * LOG2E for hh in range(hp)]
    n_far = jnp.maximum(i - 1, 0)

    def lane_chunks(x):
        return [x[:, c * LANES:(c + 1) * LANES] for c in range(x.shape[1] // LANES)]

    def raw_scores(start, width):
        kj = k_ref[pl.ds(start, width), :]
        return [lax.dot_general(qs[sid], kj[:, hh * dv + mi * dh:hh * dv + (mi + 1) * dh], contract_last,
                                preferred_element_type=F32) for sid, (hh, mi) in enumerate(streams)]

    def over_far_blocks(body):
        def pair(t, carry):
            body(pl.multiple_of(t * (2 * blk), 2 * blk), 2 * blk)
            return carry
        lax.fori_loop(0, lax.shift_right_logical(n_far, 1), pair, 0)

        @pl.when(lax.rem(n_far, 2) == 1)
        def _():
            body(pl.multiple_of((n_far - 1) * blk, blk), blk)

    mx_ref[...] = jnp.full(mx_ref.shape, -jnp.inf, F32)

    def max_body(start, width):
        s = raw_scores(start, width)
        for sid in range(len(streams)):
            best = mx_ref[sid]
            for chunk in lane_chunks(s[sid]):
                best = jnp.maximum(best, chunk)
            mx_ref[sid] = best

    over_far_blocks(max_body)

    row = lax.broadcasted_iota(jnp.int32, (blk, blk), 0)
    col = lax.broadcasted_iota(jnp.int32, (blk, blk), 1)
    near = raw_scores(pl.multiple_of(n_far * blk, blk), blk)
    diag = raw_scores(pl.multiple_of(i * blk, blk), blk)
    for sid, (hh, mi) in enumerate(streams):
        s_near = jnp.where(i >= 1, near[sid] * c1 + bias_ref[hh, 1], NEG_INF)
        s_diag = jnp.where(col <= row, diag[sid] * c1 + bias_ref[hh, 0], NEG_INF)
        s_ref[sid, 0] = s_near
        s_ref[sid, 1] = s_diag
        best = mx_ref[sid] * c1 + far2[hh]
        for chunk in lane_chunks(s_near) + lane_chunks(s_diag):
            best = jnp.maximum(best, chunk)
        m_rows = jnp.broadcast_to(jnp.max(best, axis=-1, keepdims=True), (blk, LANES))
        mrow_ref[sid] = m_rows
        sh_ref[sid] = far2[hh] - m_rows

    l_ref[...] = jnp.zeros(l_ref.shape, F32)
    acc_ref[...] = jnp.zeros(acc_ref.shape, F32)

    def accumulate(p_chunks, vj):
        ps = []
        for sid in range(len(streams)):
            tot = l_ref[sid]
            for chunk in p_chunks[sid]:
                tot = tot + chunk
            l_ref[sid] = tot
            ps.append(jnp.concatenate(p_chunks[sid], axis=1).astype(BF16))
        for hh in range(hp):
            acc_ref[hh] += jnp.dot(jnp.concatenate(ps[2 * hh:2 * hh + 2], axis=0), vj[:, hh * dv:(hh + 1) * dv],
                                   preferred_element_type=F32)

    def pv_body(start, width):
        s = raw_scores(start, width)
        accumulate([[jnp.exp2(chunk * c1 + sh_ref[sid]) for chunk in lane_chunks(s[sid])]
                    for sid in range(len(streams))], v_ref[pl.ds(start, width), :])

    over_far_blocks(pv_body)

    for t, start in ((0, pl.multiple_of(n_far * blk, blk)), (1, pl.multiple_of(i * blk, blk))):
        accumulate([[jnp.exp2(chunk - mrow_ref[sid]) for chunk in lane_chunks(s_ref[sid, t])]
                    for sid in range(len(streams))], v_ref[pl.ds(start, blk), :])

    lam = (jnp.exp(jnp.sum(lq1_ref[...] * lk1_ref[...], axis=-1, keepdims=True))
           - jnp.exp(jnp.sum(lq2_ref[...] * lk2_ref[...], axis=-1, keepdims=True)) + lambda_init)
    for hh in range(hp):
        l1 = jnp.sum(l_ref[2 * hh], axis=-1, keepdims=True)
        l2 = jnp.sum(l_ref[2 * hh + 1], axis=-1, keepdims=True)
        o = acc_ref[hh, :blk] / l1 - lam * (acc_ref[hh, blk:] / l2)
        o = o * lax.rsqrt(jnp.mean(jnp.square(o), axis=-1, keepdims=True) + LN_EPS) * sg_ref[...]
        o = o * (1.0 - lambda_init) * ba_ref[:, hh * dv:(hh + 1) * dv]
        o_ref[:, hh * dv:(hh + 1) * dv] = o.astype(o_ref.dtype)


def _diff_attention(proj3, rel_bias, lq1, lk1, lq2, lk2, subln_g, beta_attn, lambda_init, blk=256, hp=4):
    b, s, _ = proj3.shape
    h = ATTN_HEADS
    dv = ATTN_V_DIM
    gw = hp * dv
    groups = h // hp
    assert REL_BUCKETS // 2 + int(math.log((blk + 1) / (REL_BUCKETS // 2)) / math.log(REL_MAX_DIST / (REL_BUCKETS // 2))
                                  * (REL_BUCKETS // 2)) >= REL_BUCKETS
    tiles = _rel_bias_tiles(rel_bias, blk) * LOG2E
    far = rel_bias[REL_BUCKETS - 1].astype(F32)
    vec = lambda c: pl.BlockSpec((1, c), lambda bi, gi, qi: (0, 0))
    kernel = functools.partial(_attn_kernel, blk=blk, hp=hp, lambda_init=lambda_init)
    n_streams = 2 * hp
    stat = pltpu.VMEM((n_streams, blk, LANES), F32)
    return pl.pallas_call(
        kernel,
        grid=(b, groups, s // blk),
        in_specs=[
            pl.BlockSpec(memory_space=pltpu.SMEM),
            pl.BlockSpec((None, blk, gw), lambda bi, gi, qi: (bi, qi, gi)),
            pl.BlockSpec((None, s, gw), lambda bi, gi, qi: (bi, 0, groups + gi)),
            pl.BlockSpec((None, s, gw), lambda bi, gi, qi: (bi, 0, 2 * groups + gi)),
            pl.BlockSpec((hp, 2, blk, blk), lambda bi, gi, qi: (gi, 0, 0, 0)),
            vec(ATTN_HEAD_DIM), vec(ATTN_HEAD_DIM), vec(ATTN_HEAD_DIM), vec(ATTN_HEAD_DIM),
            vec(dv),
            pl.BlockSpec((1, gw), lambda bi, gi, qi: (0, gi)),
        ],
        out_specs=pl.BlockSpec((None, blk, gw), lambda bi, gi, qi: (bi, qi, gi)),
        out_shape=jax.ShapeDtypeStruct((b, s, h * dv), BF16),
        scratch_shapes=[stat, stat, stat, stat,
                        pltpu.VMEM((n_streams, 2, blk, blk), F32), pltpu.VMEM((hp, 2 * blk, dv), F32)],
        compiler_params=_params("arbitrary", "arbitrary", "arbitrary"),
        name="diff_attention",
    )(far, proj3, proj3, proj3, tiles, lq1, lk1, lq2, lk2, subln_g, beta_attn)


def _gelu(x):
    return 0.5 * x * (1.0 + lax.erf(x * (2.0 ** -0.5)))


def _gating_kernel(u_ref, g_ref, lg_ref, lb_ref, ws_ref, bs_ref, bg_ref, o_ref, buf_ref):
    hd = u_ref.shape[1] // GMLP_HEADS
    row = lax.broadcasted_iota(jnp.int32, (CHUNK, CHUNK), 0)
    col = lax.broadcasted_iota(jnp.int32, (CHUNK, CHUNK), 1)
    tril = col <= row
    ws = [jnp.where(tril, ws_ref[hh], 0.0).astype(BF16) for hh in range(GMLP_HEADS)]
    for c in range(u_ref.shape[0] // CHUNK):
        rows = slice(c * CHUNK, (c + 1) * CHUNK)
        g = _gelu(g_ref[rows, :].astype(F32))
        mu = jnp.mean(g, axis=-1, keepdims=True)
        gc = g - mu
        var = jnp.mean(jnp.square(gc), axis=-1, keepdims=True)
        v = (gc * lax.rsqrt(var + LN_EPS) * lg_ref[...] + lb_ref[...]).astype(BF16)
        ss = jnp.zeros((CHUNK, 1), F32)
        for hh in range(GMLP_HEADS):
            cols = slice(hh * hd, (hh + 1) * hd)
            mixed = jnp.dot(ws[hh], v[:, cols], preferred_element_type=F32) + bs_ref[hh]
            out = _gelu(u_ref[rows, cols].astype(F32)) * mixed
            ss = ss + jnp.sum(jnp.square(out), axis=-1, keepdims=True)
            buf_ref[rows, cols] = out
        rstd = lax.rsqrt(ss / u_ref.shape[1] + LN_EPS)
        o_ref[rows, :] = (buf_ref[rows, :] * rstd * bg_ref[...]).astype(o_ref.dtype)


def _spatial_gating(proj2, ln_g, ln_b, w_s, b_s, beta_gmlp, d_gmlp, u_col, g_col, tm=2 * CHUNK):
    n = proj2.shape[0]
    row_vec = pl.BlockSpec((1, d_gmlp), lambda c: (0, 0))
    return pl.pallas_call(
        _gating_kernel,
        grid=(n // tm,),
        in_specs=[
            pl.BlockSpec((tm, d_gmlp), lambda c: (c, u_col)),
            pl.BlockSpec((tm, d_gmlp), lambda c: (c, g_col)),
            row_vec, row_vec,
            pl.BlockSpec((GMLP_HEADS, CHUNK, CHUNK), lambda c: (0, 0, 0)),
            pl.BlockSpec((GMLP_HEADS, CHUNK, 1), lambda c: (0, 0, 0)),
            row_vec,
        ],
        out_specs=pl.BlockSpec((tm, d_gmlp), lambda c: (c, 0)),
        out_shape=jax.ShapeDtypeStruct((n, d_gmlp), BF16),
        scratch_shapes=[pltpu.VMEM((tm, d_gmlp), F32)],
        compiler_params=_params("arbitrary"),
        name="spatial_gating",
    )(proj2, proj2, ln_g, ln_b, w_s, b_s[:, :, None], beta_gmlp)


def _layer_norm_chunks(buf_ref, n_chunks, d):
    tot = jnp.sum(buf_ref[0], axis=-1, keepdims=True)
    for c in range(1, n_chunks):
        tot = tot + jnp.sum(buf_ref[c], axis=-1, keepdims=True)
    mu = tot / d
    sq = jnp.sum(jnp.square(buf_ref[0] - mu), axis=-1, keepdims=True)
    for c in range(1, n_chunks):
        sq = sq + jnp.sum(jnp.square(buf_ref[c] - mu), axis=-1, keepdims=True)
    return mu, lax.rsqrt(sq / d + LN_EPS)


def _pack_bf16_pair(lo, hi):
    lo_bits = lax.bitcast_convert_type(lo.astype(BF16).astype(F32), jnp.uint32) >> 16
    hi_bits = lax.bitcast_convert_type(hi.astype(BF16).astype(F32), jnp.uint32) & jnp.uint32(0xFFFF0000)
    return hi_bits | lo_bits


def _unpack_bf16_pair(words):
    lo = lax.bitcast_convert_type(words << 16, F32)
    hi = lax.bitcast_convert_type(words & jnp.uint32(0xFFFF0000), F32)
    return lo, hi


def _out_proj_kernel(a_ref, g_ref, w_ref, x_ref, lg_ref, lb_ref, wr_ref, br_ref,
                     x1_ref, x1p_ref, lo_ref, buf_ref, mu_ref, rstd_ref, *, n_tiles, n_chunks, tn):
    i = pl.program_id(0)
    j = pl.program_id(1)
    ka = a_ref.shape[1]

    def finish_chunk():
        y = (buf_ref[j] - mu_ref[...]) * rstd_ref[...] * lg_ref[...] + lb_ref[...]
        x1_ref[...] = y
        x1p_ref[...] = _pack_bf16_pair(y[:, :tn // 2], y[:, tn // 2:])
        y_hi = y.astype(BF16)
        y_lo = (y - y_hi.astype(F32)).astype(BF16)
        start = jnp.where(j == 0, jnp.broadcast_to(br_ref[...], lo_ref.shape), lo_ref[...])
        lo_ref[...] = start + (jnp.dot(y_hi, wr_ref[0], preferred_element_type=F32)
                               + (jnp.dot(y_lo, wr_ref[0], preferred_element_type=F32)
                                  + jnp.dot(y_hi, wr_ref[1], preferred_element_type=F32)))

    def matmul_chunk():
        mix = (jnp.dot(a_ref[...], w_ref[:ka, :], preferred_element_type=F32)
               + jnp.dot(g_ref[...], w_ref[ka:, :], preferred_element_type=F32))
        buf_ref[j] = DN_ALPHA * x_ref[...] + mix

        @pl.when(j == n_chunks - 1)
        def _():
            mu, rstd = _layer_norm_chunks(buf_ref, n_chunks, n_chunks * tn)
            mu_ref[...] = mu
            rstd_ref[...] = rstd

    @pl.when(i == 0)
    def _():
        matmul_chunk()

    @pl.when((i > 0) & (i < n_tiles))
    def _():
        finish_chunk()
        matmul_chunk()

    @pl.when(i == n_tiles)
    def _():
        finish_chunk()


def _out_proj_ln_router(attn2, gm2, wb, x2, ln_g, ln_b, w_route, b_route, tm=1024, tn=X1_PACK_GROUP):
    m, d = x2.shape
    ka = attn2.shape[1]
    kg = gm2.shape[1]
    n_chunks = d // tn
    n_tiles = m // tm
    tile_in = lambda i: jnp.minimum(i, n_tiles - 1)
    tile_out = lambda i: jnp.maximum(i - 1, 0)
    chunk_in = lambda i, j: jnp.where(i < n_tiles, j, n_chunks - 1)
    chunk_out = lambda i, j: jnp.where(i >= 1, j, 0)
    kernel = functools.partial(_out_proj_kernel, n_tiles=n_tiles, n_chunks=n_chunks, tn=tn)
    return pl.pallas_call(
        kernel,
        grid=(n_tiles + 1, n_chunks),
        in_specs=[
            pl.BlockSpec((tm, ka), lambda i, j: (tile_in(i), 0)),
            pl.BlockSpec((tm, kg), lambda i, j: (tile_in(i), 0)),
            pl.BlockSpec((ka + kg, tn), lambda i, j: (0, chunk_in(i, j))),
            pl.BlockSpec((tm, tn), lambda i, j: (tile_in(i), chunk_in(i, j))),
            pl.BlockSpec((1, tn), lambda i, j: (0, chunk_out(i, j))),
            pl.BlockSpec((1, tn), lambda i, j: (0, chunk_out(i, j))),
            pl.BlockSpec((2, tn, LANES), lambda i, j: (0, chunk_out(i, j), 0)),
            pl.BlockSpec((1, LANES), lambda i, j: (0, 0)),
        ],
        out_specs=[
            pl.BlockSpec((tm, tn), lambda i, j: (tile_out(i), chunk_out(i, j))),
            pl.BlockSpec((tm, tn // 2), lambda i, j: (tile_out(i), chunk_out(i, j))),
            pl.BlockSpec((tm, LANES), lambda i, j: (tile_out(i), 0)),
        ],
        out_shape=[jax.ShapeDtypeStruct((m, d), F32), jax.ShapeDtypeStruct((m, d // 2), jnp.uint32),
                   jax.ShapeDtypeStruct((m, LANES), F32)],
        scratch_shapes=[pltpu.VMEM((n_chunks, tm, tn), F32), pltpu.VMEM((tm, 1), F32), pltpu.VMEM((tm, 1), F32)],
        compiler_params=_params("arbitrary", "arbitrary"),
        name="out_proj_ln_router",
    )(attn2, gm2, wb, x2, ln_g, ln_b, w_route, b_route)


def _route_kernel(lo_ref, eid_ref, wt_ref, cnt_ref):
    lg = lo_ref[...]
    lane = lax.broadcasted_iota(jnp.int32, lg.shape, 1)
    lane_f = lane.astype(F32)
    none = float(LANES)
    first = lambda hit: jnp.min(jnp.where(hit, lane_f, none), axis=-1, keepdims=True)

    in_groups = lane < N_GROUPS
    g_logits = jnp.where(in_groups, lg, -jnp.inf)
    g_max = jnp.max(g_logits, axis=-1, keepdims=True)
    g_idx = first(g_logits == g_max)
    g_w = 1.0 / jnp.sum(jnp.where(in_groups, jnp.exp(lg - g_max), 0.0), axis=-1, keepdims=True)

    lo = N_GROUPS + g_idx * EXPERTS_PER_GROUP
    in_group = (lane_f >= lo) & (lane_f < lo + EXPERTS_PER_GROUP)
    e_logits = jnp.where(in_group, lg, -jnp.inf)
    t1 = jnp.max(e_logits, axis=-1, keepdims=True)
    i1 = first(e_logits == t1)
    e_rest = jnp.where(lane_f == i1, -jnp.inf, e_logits)
    t2 = jnp.max(e_rest, axis=-1, keepdims=True)
    i2 = first(e_rest == t2)
    ex = jnp.exp(t2 - t1)
    w1 = g_w * (1.0 / (1.0 + ex))
    w2 = g_w * (ex / (1.0 + ex))
    wt_ref[...] = jnp.where(lane == 0, w1, jnp.where(lane == 1, w2, 0.0))

    @pl.when(pl.program_id(0) == 0)
    def _():
        cnt_ref[...] = jnp.zeros(cnt_ref.shape, F32)

    tm = lg.shape[0]
    pick1 = lane_f == i1 - N_GROUPS
    pick2 = lane_f == i2 - N_GROUPS
    chosen = pick1.astype(F32) + pick2.astype(F32)
    earlier = (lax.broadcasted_iota(jnp.int32, (tm, tm), 1) < lax.broadcasted_iota(jnp.int32, (tm, tm), 0))
    before = jnp.dot(earlier.astype(BF16), chosen.astype(BF16), preferred_element_type=F32) + cnt_ref[...]
    r1 = jnp.sum(jnp.where(pick1, before, 0.0), axis=-1, keepdims=True)
    r2 = jnp.sum(jnp.where(pick2, before, 0.0), axis=-1, keepdims=True)
    cnt_ref[...] += jnp.sum(chosen, axis=0, keepdims=True)
    picks = (jnp.where(lane == 0, i1 - N_GROUPS, 0.0) + jnp.where(lane == 1, i2 - N_GROUPS, 0.0)
             + jnp.where(lane == 2, r1, 0.0) + jnp.where(lane == 3, r2, 0.0))
    eid_ref[...] = picks.astype(jnp.int32)


def _route(logits, tm=1024):
    m = logits.shape[0]
    spec = pl.BlockSpec((tm, LANES), lambda i: (i, 0))
    return pl.pallas_call(
        _route_kernel,
        grid=(m // tm,),
        in_specs=[spec],
        out_specs=[spec, spec, pl.BlockSpec((1, LANES), lambda i: (0, 0))],
        out_shape=[jax.ShapeDtypeStruct((m, LANES), jnp.int32), jax.ShapeDtypeStruct((m, LANES), F32),
                   jax.ShapeDtypeStruct((1, LANES), F32)],
        compiler_params=_params("arbitrary"),
        name="route",
    )(logits)


def _expert_kernel(be_ref, cnt_ref, nreal_ref, tok_ref, aid_ref, x_hbm, wg_ref, wu_ref, wd_ref, ys_hbm,
                   xbuf, xb, acc, obuf, gsem, ssem, *, tm, nf):
    half = xb.shape[1] // 2
    b = pl.program_id(0)
    f = pl.program_id(1)
    nreal = nreal_ref[0]
    slot = lax.rem(b, 2)

    def rows_of(blk):
        return pl.multiple_of(cnt_ref[blk], SUBLANES)

    def for_each_row(n_rows, copy_of_row):
        def body(grp, carry):
            base = pl.multiple_of(grp * SUBLANES, SUBLANES)
            for k in range(SUBLANES):
                copy_of_row(base + k).start(priority=k % 2)
            return carry
        lax.fori_loop(0, lax.shift_right_logical(n_rows, SUBLANES.bit_length() - 1), body, 0)

    def start_gather(slot_, n_rows):
        for_each_row(n_rows, lambda r: pltpu.make_async_copy(
            x_hbm.at[pl.ds(tok_ref[0, r], 1)], xbuf.at[slot_, pl.ds(r, 1)], gsem.at[slot_]))

    def wait_gather(slot_, n_rows):
        pltpu.make_async_copy(x_hbm.at[pl.ds(0, n_rows)], xbuf.at[slot_, pl.ds(0, n_rows)], gsem.at[slot_]).wait()

    def start_scatter(slot_, n_rows):
        for_each_row(n_rows, lambda r: pltpu.make_async_copy(
            obuf.at[slot_, pl.ds(r, 1)], ys_hbm.at[pl.ds(aid_ref[0, r], 1)], ssem.at[slot_]))

    def wait_scatter(slot_, n_rows):
        pltpu.make_async_copy(obuf.at[slot_, pl.ds(0, n_rows)], ys_hbm.at[pl.ds(0, n_rows)], ssem.at[slot_]).wait()

    @pl.when(b < nreal)
    def _():
        @pl.when(f == 0)
        def _():
            @pl.when(b == 0)
            def _():
                xbuf[...] = jnp.zeros(xbuf.shape, xbuf.dtype)
                start_gather(0, rows_of(0))
            wait_gather(slot, rows_of(b))
            lo, hi = _unpack_bf16_pair(xbuf[slot])
            gw = X1_PACK_GROUP // 2
            for grp in range(half // gw):
                xb[:, 2 * grp * gw:(2 * grp + 1) * gw] = lo[:, grp * gw:(grp + 1) * gw].astype(BF16)
                xb[:, (2 * grp + 1) * gw:(2 * grp + 2) * gw] = hi[:, grp * gw:(grp + 1) * gw].astype(BF16)

        x = xb[...]
        g = jnp.dot(x, wg_ref[...].astype(BF16), preferred_element_type=F32)
        u = jnp.dot(x, wu_ref[...].astype(BF16), preferred_element_type=F32)
        hidden = (g * (1.0 / (1.0 + jnp.exp(-g))) * u).astype(BF16)
        contrib = jnp.dot(hidden, wd_ref[...].astype(BF16), preferred_element_type=F32)

        @pl.when(f == 0)
        def _():
            acc[...] = contrib

        @pl.when((f > 0) & (f < nf - 1))
        def _():
            acc[...] += contrib

        @pl.when(f == nf - 1)
        def _():
            @pl.when(b >= 1)
            def _():
                wait_scatter(1 - slot, rows_of(b - 1))
            total = acc[...] + contrib
            obuf[slot] = _pack_bf16_pair(total[:, :half], total[:, half:])
            start_scatter(slot, rows_of(b))

            @pl.when(b + 1 < nreal)
            def _():
                start_gather(1 - slot, rows_of(b + 1))

            @pl.when(b + 1 == nreal)
            def _():
                wait_scatter(slot, rows_of(b))


def _expert_mlp(x1p, plan, w_gate, w_up, w_down, tm, fc=256):
    tok_blocks, aid_blocks, block_expert, block_rows, n_real = plan
    n, dp = x1p.shape
    d = 2 * dp
    n_blocks = tok_blocks.shape[0]
    d_ff = w_gate.shape[2]
    nf = d_ff // fc
    assert nf >= 2

    def live(b, f, nreal):
        is_real = b < nreal[0]
        return jnp.where(is_real, b, nreal[0] - 1), jnp.where(is_real, f, nf - 1)

    def tok_map(b, f, be, cnt, nreal):
        bb, ff = live(b, f, nreal)
        return jnp.minimum(bb + (ff == nf - 1).astype(jnp.int32), nreal[0] - 1), 0, 0

    def aid_map(b, f, be, cnt, nreal):
        bb, _ = live(b, f, nreal)
        return bb, 0, 0

    def w_in_map(b, f, be, cnt, nreal):
        bb, ff = live(b, f, nreal)
        return be[bb], 0, ff

    def w_out_map(b, f, be, cnt, nreal):
        bb, ff = live(b, f, nreal)
        return be[bb], ff, 0

    grid_spec = pltpu.PrefetchScalarGridSpec(
        num_scalar_prefetch=3,
        grid=(n_blocks, nf),
        in_specs=[
            pl.BlockSpec((None, 1, tm), tok_map, memory_space=pltpu.SMEM),
            pl.BlockSpec((None, 1, tm), aid_map, memory_space=pltpu.SMEM),
            pl.BlockSpec(memory_space=pl.ANY),
            pl.BlockSpec((None, d, fc), w_in_map),
            pl.BlockSpec((None, d, fc), w_in_map),
            pl.BlockSpec((None, fc, d), w_out_map),
        ],
        out_specs=pl.BlockSpec(memory_space=pl.ANY),
        scratch_shapes=[pltpu.VMEM((2, tm, dp), jnp.uint32), pltpu.VMEM((tm, d), BF16), pltpu.VMEM((tm, d), F32),
                        pltpu.VMEM((2, tm, dp), jnp.uint32),
                        pltpu.SemaphoreType.DMA((2,)), pltpu.SemaphoreType.DMA((2,))],
    )
    kernel = functools.partial(_expert_kernel, tm=tm, nf=nf)
    return pl.pallas_call(
        kernel,
        grid_spec=grid_spec,
        out_shape=jax.ShapeDtypeStruct((n * TOP_K + 2 * tm, dp), jnp.uint32),
        compiler_params=_params("arbitrary", "arbitrary"),
        name="expert_mlp",
    )(block_expert, block_rows, n_real, tok_blocks, aid_blocks, x1p, w_gate, w_up, w_down)


def _combine_kernel(x1_ref, y0_ref, y1_ref, wt_ref, lg_ref, lb_ref, o_ref, buf_ref):
    half = x1_ref.shape[1] // 2
    wt = wt_ref[...]
    y0 = _unpack_bf16_pair(y0_ref[...])
    y1 = _unpack_bf16_pair(y1_ref[...])
    for p, cs in enumerate((slice(0, half), slice(half, 2 * half))):
        buf_ref[:, cs] = DN_ALPHA * x1_ref[:, cs] + (wt[:, 0:1] * y0[p] + wt[:, 1:2] * y1[p])
    hsum = buf_ref[...]
    mu = jnp.mean(hsum, axis=-1, keepdims=True)
    hc = hsum - mu
    var = jnp.mean(jnp.square(hc), axis=-1, keepdims=True)
    o_ref[...] = hc * lax.rsqrt(var + LN_EPS) * lg_ref[...] + lb_ref[...]


def _combine_ln(x1, ys, wts, ln_g, ln_b, tm=256):
    m, d = x1.shape
    big = pl.BlockSpec((tm, d), lambda i: (i, 0))
    first = pl.BlockSpec((tm, d // 2), lambda i: (i, 0))
    second = pl.BlockSpec((tm, d // 2), lambda i: (m // tm + i, 0))
    row_vec = pl.BlockSpec((1, d), lambda i: (0, 0))
    return pl.pallas_call(
        _combine_kernel,
        grid=(m // tm,),
        in_specs=[big, first, second, pl.BlockSpec((tm, LANES), lambda i: (i, 0)), row_vec, row_vec],
        out_specs=big,
        out_shape=jax.ShapeDtypeStruct((m, d), F32),
        scratch_shapes=[pltpu.VMEM((tm, d), F32)],
        compiler_params=_params("arbitrary"),
        name="combine_ln",
    )(x1, ys, ys, wts, ln_g, ln_b)


def _plan_blocks(eid, rank, counts, tm):
    n = eid.shape[0]
    a = n * TOP_K
    expert_id = eid.reshape(-1)
    padded = (counts + tm - 1) // tm * tm
    pad_ends = jnp.cumsum(padded)
    pad_starts = pad_ends - padded
    n_blocks = a // tm + N_EXPERTS
    experts = jnp.arange(N_EXPERTS, dtype=jnp.int32)
    start_of = jnp.dot((expert_id[:, None] == experts[None, :]).astype(F32), pad_starts.astype(F32),
                       precision=lax.Precision.HIGHEST).astype(jnp.int32)
    pos = start_of + rank.reshape(-1)
    spare = a + jnp.arange(n_blocks * tm, dtype=jnp.int32) % (2 * tm)
    flat = jnp.arange(a, dtype=jnp.int32)
    aid_pad = spare.at[pos].set((flat % TOP_K) * n + flat // TOP_K, unique_indices=True)
    tok_pad = jnp.where(aid_pad < a, aid_pad % n, 0)
    block_start = jnp.arange(n_blocks, dtype=jnp.int32) * tm
    block_expert = jnp.minimum(jnp.sum(pad_ends[None, :] <= block_start[:, None], axis=1), N_EXPERTS - 1).astype(jnp.int32)
    block_count = jnp.clip(counts[block_expert] - (block_start - pad_starts[block_expert]), 0, tm)
    block_rows = ((block_count + SUBLANES - 1) // SUBLANES * SUBLANES).astype(jnp.int32)
    n_real = (pad_ends[-1] // tm).astype(jnp.int32).reshape(1)
    return tok_pad.reshape(n_blocks, 1, tm), aid_pad.reshape(n_blocks, 1, tm), block_expert, block_rows, n_real


def kernel(x, w_in, w_out, lambda_q1, lambda_k1, lambda_q2, lambda_k2, subln_g, beta_attn, gmlp_ln_g, gmlp_ln_b,
           spatial_w, spatial_b, beta_gmlp, rel_bias, ln1_g, ln1_b, w_group, b_group, w_expert, b_expert,
           w_gate, w_up, w_down, ln2_g, ln2_b):
    b, s, d = x.shape
    n = b * s
    d_attn = ATTN_HEADS * ATTN_V_DIM
    d_gmlp = d - d_attn
    moe_tm = 320
    for l in range(DEPTH):
        lambda_init = 0.8 - 0.6 * math.exp(-0.3 * l)
        x2 = x.reshape(n, d)
        proj = _proj_matmul(x2.astype(BF16), w_in[l])
        attn = _diff_attention(proj.reshape(b, s, -1), rel_bias, lambda_q1[l][None], lambda_k1[l][None],
                               lambda_q2[l][None], lambda_k2[l][None], subln_g[l][None], beta_attn[l][None],
                               lambda_init)
        gm = _spatial_gating(proj, gmlp_ln_g[l][None], gmlp_ln_b[l][None], spatial_w[l], spatial_b[l],
                             beta_gmlp[l][None], d_gmlp, u_col=3 * d_attn // d_gmlp, g_col=3 * d_attn // d_gmlp + 1)
        n_route = N_GROUPS + N_EXPERTS
        w_route = jnp.concatenate([w_group[l], w_expert[l].reshape(d, N_EXPERTS),
                                   jnp.zeros((d, LANES - n_route), F32)], axis=1)
        w_route_hi = w_route.astype(BF16)
        w_route = jnp.stack([w_route_hi, (w_route - w_route_hi.astype(F32)).astype(BF16)])
        b_route = jnp.concatenate([b_group[l], b_expert[l].reshape(-1), jnp.zeros((LANES - n_route,), F32)])[None]
        x1, x1p, logits = _out_proj_ln_router(attn.reshape(n, d_attn), gm, w_out[l].astype(BF16), x2,
                                         ln1_g[l][None], ln1_b[l][None], w_route, b_route)
        picks, wts, counts = _route(logits)
        plan = _plan_blocks(picks[:, :TOP_K], picks[:, TOP_K:2 * TOP_K], counts[0, :N_EXPERTS].astype(jnp.int32),
                            moe_tm)
        ys = _expert_mlp(x1p, plan, w_gate[l], w_up[l], w_down[l], moe_tm)
        x = _combine_ln(x1, ys, wts, ln2_g[l][None], ln2_b[l][None]).reshape(b, s, d)
    return x
```

```python
import functools
import math

import jax
import jax.numpy as jnp
from jax import lax
from jax.experimental import pallas as pl
from jax.experimental.pallas import tpu as pltpu

F32 = jnp.float32
BF16 = jnp.bfloat16

ATTN_HEADS = 8
ATTN_HEAD_DIM = 128
ATTN_V_DIM = 2 * ATTN_HEAD_DIM
GMLP_HEADS = 8
CHUNK = 128
REL_BUCKETS = 32
REL_MAX_DIST = 128
N_GROUPS = 8
EXPERTS_PER_GROUP = 8
N_EXPERTS = N_GROUPS * EXPERTS_PER_GROUP
TOP_K = 2
LN_EPS = 1e-5
DEPTH = 1
DN_ALPHA = (2 * DEPTH) ** 0.25
NEG_INF = -1e30
LOG2E = math.log2(math.e)
LANES = 128
SUBLANES = 8
VMEM_LIMIT = 56 * 1024 * 1024
VMEM_LIMIT_LARGE = 60 * 1024 * 1024
X1_PACK_GROUP = 256


def _params(*semantics, vmem_limit=VMEM_LIMIT):
    return pltpu.CompilerParams(dimension_semantics=semantics, vmem_limit_bytes=vmem_limit)


def _matmul_kernel(x_ref, w_ref, o_ref, wb_ref):
    @pl.when(pl.program_id(1) == 0)
    def _():
        wb_ref[...] = w_ref[...].astype(BF16)

    o_ref[...] = jnp.dot(x_ref[...], wb_ref[...], preferred_element_type=F32).astype(o_ref.dtype)


def _proj_matmul(xb, w, tm=512, tn=1024):
    m, k = xb.shape
    n = w.shape[1]
    return pl.pallas_call(
        _matmul_kernel,
        grid=(n // tn, m // tm),
        in_specs=[pl.BlockSpec((tm, k), lambda j, i: (i, 0)),
                  pl.BlockSpec((k, tn), lambda j, i: (0, j))],
        out_specs=pl.BlockSpec((tm, tn), lambda j, i: (i, j)),
        out_shape=jax.ShapeDtypeStruct((m, n), BF16),
        scratch_shapes=[pltpu.VMEM((k, tn), BF16)],
        compiler_params=_params("arbitrary", "arbitrary"),
        name="proj_matmul",
    )(xb, w)


def _rel_bucket(n):
    max_exact = REL_BUCKETS // 2
    nf = jnp.maximum(n, max_exact).astype(F32)
    large = max_exact + (jnp.log(nf / max_exact) / math.log(REL_MAX_DIST / max_exact)
                         * (REL_BUCKETS - max_exact)).astype(jnp.int32)
    large = jnp.minimum(large, REL_BUCKETS - 1)
    return jnp.where(n < max_exact, n, large)


def _rel_bias_tiles(rel_bias, blk):
    pos = jnp.arange(blk)
    tiles = []
    for d in (0, 1):
        bucket = _rel_bucket(jnp.maximum(pos[:, None] + d * blk - pos[None, :], 0))
        hit = bucket[None] == jnp.arange(REL_BUCKETS)[:, None, None]
        tiles.append(jnp.sum(jnp.where(hit[:, None], rel_bias[:, :, None, None], 0.0), axis=0))
    return jnp.stack(tiles, axis=1).astype(F32)


def _attn_kernel(far_ref, q_ref, k_ref, v_ref, bias_ref, lq1_ref, lk1_ref, lq2_ref, lk2_ref,
                 sg_ref, ba_ref, o_ref, mx_ref, sh_ref, mrow_ref, l_ref, s_ref, acc_ref, *, blk, hp, lambda_init):
    g = pl.program_id(1)
    i = pl.program_id(2)
    dh = ATTN_HEAD_DIM
    dv = ATTN_V_DIM
    c1 = ATTN_HEAD_DIM ** -0.5 * LOG2E
    streams = [(hh, mi) for hh in range(hp) for mi in range(2)]
    q = q_ref[...]
    qs = [q[:, hh * dv + mi * dh:hh * dv + (mi + 1) * dh] for hh, mi in streams]
    contract_last = (((1,), (1,)), ((), ()))
    far2 = [far_ref[g * hp + hh] * LOG2E for hh in range(hp)]
    n_far = jnp.maximum(i - 1, 0)

    def lane_chunks(x):
        return [x[:, c * LANES:(c + 1) * LANES] for c in range(x.shape[1] // LANES)]

    def raw_scores(start, width):
        kj = k_ref[pl.ds(start, width), :]
        return [lax.dot_general(qs[sid], kj[:, hh * dv + mi * dh:hh * dv + (mi + 1) * dh], contract_last,
                                preferred_element_type=F32) for sid, (hh, mi) in enumerate(streams)]

    def over_far_blocks(body):
        def pair(t, carry):
            body(pl.multiple_of(t * (2 * blk), 2 * blk), 2 * blk)
            return carry
        lax.fori_loop(0, lax.shift_right_logical(n_far, 1), pair, 0)

        @pl.when(lax.rem(n_far, 2) == 1)
        def _():
            body(pl.multiple_of((n_far - 1) * blk, blk), blk)

    mx_ref[...] = jnp.full(mx_ref.shape, -jnp.inf, F32)

    def max_body(start, width):
        s = raw_scores(start, width)
        for sid in range(len(streams)):
            best = mx_ref[sid]
            for chunk in lane_chunks(s[sid]):
                best = jnp.maximum(best, chunk)
            mx_ref[sid] = best

    over_far_blocks(max_body)

    row = lax.broadcasted_iota(jnp.int32, (blk, blk), 0)
    col = lax.broadcasted_iota(jnp.int32, (blk, blk), 1)
    near = raw_scores(pl.multiple_of(n_far * blk, blk), blk)
    diag = raw_scores(pl.multiple_of(i * blk, blk), blk)
    for sid, (hh, mi) in enumerate(streams):
        s_near = jnp.where(i >= 1, near[sid] * c1 + bias_ref[hh, 1], NEG_INF)
        s_diag = jnp.where(col <= row, diag[sid] * c1 + bias_ref[hh, 0], NEG_INF)
        s_ref[sid, 0] = s_near
        s_ref[sid, 1] = s_diag
        best = mx_ref[sid] * c1 + far2[hh]
        for chunk in lane_chunks(s_near) + lane_chunks(s_diag):
            best = jnp.maximum(best, chunk)
        m_rows = jnp.broadcast_to(jnp.max(best, axis=-1, keepdims=True), (blk, LANES))
        mrow_ref[sid] = m_rows
        sh_ref[sid] = far2[hh] - m_rows

    l_ref[...] = jnp.zeros(l_ref.shape, F32)
    acc_ref[...] = jnp.zeros(acc_ref.shape, F32)

    def accumulate(p_chunks, vj):
        ps = []
        for sid in range(len(streams)):
            tot = l_ref[sid]
            for chunk in p_chunks[sid]:
                tot = tot + chunk
            l_ref[sid] = tot
            ps.append(jnp.concatenate(p_chunks[sid], axis=1).astype(BF16))
        for hh in range(hp):
            acc_ref[hh] += jnp.dot(jnp.concatenate(ps[2 * hh:2 * hh + 2], axis=0), vj[:, hh * dv:(hh + 1) * dv],
                                   preferred_element_type=F32)

    def pv_body(start, width):
        s = raw_scores(start, width)
        accumulate([[jnp.exp2(chunk * c1 + sh_ref[sid]) for chunk in lane_chunks(s[sid])]
                    for sid in range(len(streams))], v_ref[pl.ds(start, width), :])

    over_far_blocks(pv_body)

    for t, start in ((0, pl.multiple_of(n_far * blk, blk)), (1, pl.multiple_of(i * blk, blk))):
        accumulate([[jnp.exp2(chunk - mrow_ref[sid]) for chunk in lane_chunks(s_ref[sid, t])]
                    for sid in range(len(streams))], v_ref[pl.ds(start, blk), :])

    lam = (jnp.exp(jnp.sum(lq1_ref[...] * lk1_ref[...], axis=-1, keepdims=True))
           - jnp.exp(jnp.sum(lq2_ref[...] * lk2_ref[...], axis=-1, keepdims=True)) + lambda_init)
    for hh in range(hp):
        l1 = jnp.sum(l_ref[2 * hh], axis=-1, keepdims=True)
        l2 = jnp.sum(l_ref[2 * hh + 1], axis=-1, keepdims=True)
        o = acc_ref[hh, :blk] / l1 - lam * (acc_ref[hh, blk:] / l2)
        o = o * lax.rsqrt(jnp.mean(jnp.square(o), axis=-1, keepdims=True) + LN_EPS) * sg_ref[...]
        o = o * (1.0 - lambda_init) * ba_ref[:, hh * dv:(hh + 1) * dv]
        o_ref[:, hh * dv:(hh + 1) * dv] = o.astype(o_ref.dtype)


def _diff_attention(proj3, rel_bias, lq1, lk1, lq2, lk2, subln_g, beta_attn, lambda_init, blk=256, hp=4):
    b, s, _ = proj3.shape
    h = ATTN_HEADS
    dv = ATTN_V_DIM
    gw = hp * dv
    groups = h // hp
    assert REL_BUCKETS // 2 + int(math.log((blk + 1) / (REL_BUCKETS // 2)) / math.log(REL_MAX_DIST / (REL_BUCKETS // 2))
                                  * (REL_BUCKETS // 2)) >= REL_BUCKETS
    tiles = _rel_bias_tiles(rel_bias, blk) * LOG2E
    far = rel_bias[REL_BUCKETS - 1].astype(F32)
    vec = lambda c: pl.BlockSpec((1, c), lambda bi, gi, qi: (0, 0))
    kernel = functools.partial(_attn_kernel, blk=blk, hp=hp, lambda_init=lambda_init)
    n_streams = 2 * hp
    stat = pltpu.VMEM((n_streams, blk, LANES), F32)
    return pl.pallas_call(
        kernel,
        grid=(b, groups, s // blk),
        in_specs=[
            pl.BlockSpec(memory_space=pltpu.SMEM),
            pl.BlockSpec((None, blk, gw), lambda bi, gi, qi: (bi, qi, gi)),
            pl.BlockSpec((None, s, gw), lambda bi, gi, qi: (bi, 0, groups + gi)),
            pl.BlockSpec((None, s, gw), lambda bi, gi, qi: (bi, 0, 2 * groups + gi)),
            pl.BlockSpec((hp, 2, blk, blk), lambda bi, gi, qi: (gi, 0, 0, 0)),
            vec(ATTN_HEAD_DIM), vec(ATTN_HEAD_DIM), vec(ATTN_HEAD_DIM), vec(ATTN_HEAD_DIM),
            vec(dv),
            pl.BlockSpec((1, gw), lambda bi, gi, qi: (0, gi)),
        ],
        out_specs=pl.BlockSpec((None, blk, gw), lambda bi, gi, qi: (bi, qi, gi)),
        out_shape=jax.ShapeDtypeStruct((b, s, h * dv), BF16),
        scratch_shapes=[stat, stat, stat, stat,
                        pltpu.VMEM((n_streams, 2, blk, blk), F32), pltpu.VMEM((hp, 2 * blk, dv), F32)],
        compiler_params=_params("arbitrary", "arbitrary", "arbitrary"),
        name="diff_attention",
    )(far, proj3, proj3, proj3, tiles, lq1, lk1, lq2, lk2, subln_g, beta_attn)


def _gelu(x):
    return 0.5 * x * (1.0 + lax.erf(x * (2.0 ** -0.5)))


def _gating_chunk(u_ref, g_ref, lg_ref, lb_ref, ws_ref, bs_ref, bg_ref, buf_ref):
    hd = u_ref.shape[1] // GMLP_HEADS
    g = _gelu(g_ref[...].astype(F32))
    mu = jnp.mean(g, axis=-1, keepdims=True)
    gc = g - mu
    var = jnp.mean(jnp.square(gc), axis=-1, keepdims=True)
    v = (gc * lax.rsqrt(var + LN_EPS) * lg_ref[...] + lb_ref[...]).astype(BF16)
    row = lax.broadcasted_iota(jnp.int32, (CHUNK, CHUNK), 0)
    col = lax.broadcasted_iota(jnp.int32, (CHUNK, CHUNK), 1)
    tril = col <= row
    ss = jnp.zeros((CHUNK, 1), F32)
    for hh in range(GMLP_HEADS):
        w = jnp.where(tril, ws_ref[hh], 0.0).astype(BF16)
        mixed = jnp.dot(w, v[:, hh * hd:(hh + 1) * hd], preferred_element_type=F32) + bs_ref[hh]
        out = _gelu(u_ref[:, hh * hd:(hh + 1) * hd].astype(F32)) * mixed
        ss = ss + jnp.sum(jnp.square(out), axis=-1, keepdims=True)
        buf_ref[:, hh * hd:(hh + 1) * hd] = out
    rstd = lax.rsqrt(ss / u_ref.shape[1] + LN_EPS)
    return (buf_ref[...] * rstd * bg_ref[...]).astype(BF16)


def _layer_norm_chunks(buf_ref, n_chunks, d):
    tot = jnp.sum(buf_ref[0], axis=-1, keepdims=True)
    for c in range(1, n_chunks):
        tot = tot + jnp.sum(buf_ref[c], axis=-1, keepdims=True)
    mu = tot / d
    sq = jnp.sum(jnp.square(buf_ref[0] - mu), axis=-1, keepdims=True)
    for c in range(1, n_chunks):
        sq = sq + jnp.sum(jnp.square(buf_ref[c] - mu), axis=-1, keepdims=True)
    return mu, lax.rsqrt(sq / d + LN_EPS)


def _pack_bf16_pair(lo, hi):
    lo_bits = lax.bitcast_convert_type(lo.astype(BF16).astype(F32), jnp.uint32) >> 16
    hi_bits = lax.bitcast_convert_type(hi.astype(BF16).astype(F32), jnp.uint32) & jnp.uint32(0xFFFF0000)
    return hi_bits | lo_bits


def _unpack_bf16_pair(words):
    lo = lax.bitcast_convert_type(words << 16, F32)
    hi = lax.bitcast_convert_type(words & jnp.uint32(0xFFFF0000), F32)
    return lo, hi


def _mix_kernel(a_ref, u_ref, g_ref, glg_ref, glb_ref, ws_ref, bs_ref, bg_ref, w_ref, x_ref, lg_ref, lb_ref,
                wr_ref, br_ref, x1_ref, x1p_ref, lo_ref, gm_ref, gbuf_ref, buf_ref, mu_ref, rstd_ref,
                *, n_tiles, n_chunks, tn):
    p = pl.program_id(0)
    j = pl.program_id(1)
    ka = a_ref.shape[1]

    def gate():
        rows = pl.ds(pl.multiple_of(j * CHUNK, CHUNK), CHUNK)
        gm_ref[lax.rem(p, 2), rows, :] = _gating_chunk(u_ref, g_ref, glg_ref, glb_ref, ws_ref, bs_ref, bg_ref,
                                                       gbuf_ref)

    def normalise():
        y = (buf_ref[j] - mu_ref[...]) * rstd_ref[...] * lg_ref[...] + lb_ref[...]
        x1_ref[...] = y
        gw = X1_PACK_GROUP // 2
        for grp in range(tn // X1_PACK_GROUP):
            x1p_ref[:, grp * gw:(grp + 1) * gw] = _pack_bf16_pair(y[:, 2 * grp * gw:(2 * grp + 1) * gw],
                                                                  y[:, (2 * grp + 1) * gw:(2 * grp + 2) * gw])
        y_hi = y.astype(BF16)
        y_lo = (y - y_hi.astype(F32)).astype(BF16)
        start = jnp.where(j == 0, jnp.broadcast_to(br_ref[...], lo_ref.shape), lo_ref[...])
        lo_ref[...] = start + (jnp.dot(y_hi, wr_ref[0], preferred_element_type=F32)
                               + (jnp.dot(y_lo, wr_ref[0], preferred_element_type=F32)
                                  + jnp.dot(y_hi, wr_ref[1], preferred_element_type=F32)))

    def matmul():
        mix = (jnp.dot(a_ref[...], w_ref[:ka, :], preferred_element_type=F32)
               + jnp.dot(gm_ref[lax.rem(p + 1, 2)], w_ref[ka:, :], preferred_element_type=F32))
        buf_ref[j] = DN_ALPHA * x_ref[...] + mix

    def row_stats():
        @pl.when(j == n_chunks - 1)
        def _():
            mu, rstd = _layer_norm_chunks(buf_ref, n_chunks, n_chunks * tn)
            mu_ref[...] = mu
            rstd_ref[...] = rstd

    @pl.when(p == 0)
    def _():
        gate()

    @pl.when(p == 1)
    def _():
        matmul()
        gate()
        row_stats()

    @pl.when((p >= 2) & (p < n_tiles))
    def _():
        normalise()
        matmul()
        gate()
        row_stats()

    @pl.when(p == n_tiles)
    def _():
        normalise()
        matmul()
        row_stats()

    @pl.when(p == n_tiles + 1)
    def _():
        normalise()


def _gate_mix_ln_router(attn2, proj2, gating, wb, x2, ln_g, ln_b, w_route, b_route, d_gmlp, u_col, g_col,
                        tm=512, tn=1024):
    g_ln_g, g_ln_b, w_s, b_s, beta_gmlp = gating
    m, d = x2.shape
    ka = attn2.shape[1]
    n_chunks = d // tn
    n_tiles = m // tm
    assert tm // CHUNK == n_chunks and n_tiles >= 3 and tn % X1_PACK_GROUP == 0
    last = n_chunks - 1
    gate_tile = lambda p: jnp.minimum(p, n_tiles - 1)
    mat_tile = lambda p: jnp.clip(p - 1, 0, n_tiles - 1)
    out_tile = lambda p: jnp.maximum(p - 2, 0)
    gate_chunk = lambda p, j: jnp.where(p < n_tiles, j, last)
    mat_chunk = lambda p, j: jnp.where(p < 1, 0, jnp.where(p <= n_tiles, j, last))
    out_chunk = lambda p, j: jnp.where(p >= 2, j, 0)
    gate_rows = lambda col: (lambda p, j: (gate_tile(p) * n_chunks + gate_chunk(p, j), col))
    const2 = lambda p, j: (0, 0)
    const3 = lambda p, j: (0, 0, 0)
    kernel = functools.partial(_mix_kernel, n_tiles=n_tiles, n_chunks=n_chunks, tn=tn)
    return pl.pallas_call(
        kernel,
        grid=(n_tiles + 2, n_chunks),
        in_specs=[
            pl.BlockSpec((tm, ka), lambda p, j: (mat_tile(p), 0)),
            pl.BlockSpec((CHUNK, d_gmlp), gate_rows(u_col)),
            pl.BlockSpec((CHUNK, d_gmlp), gate_rows(g_col)),
            pl.BlockSpec((1, d_gmlp), const2), pl.BlockSpec((1, d_gmlp), const2),
            pl.BlockSpec((GMLP_HEADS, CHUNK, CHUNK), const3),
            pl.BlockSpec((GMLP_HEADS, CHUNK, 1), const3),
            pl.BlockSpec((1, d_gmlp), const2),
            pl.BlockSpec((ka + d_gmlp, tn), lambda p, j: (0, mat_chunk(p, j))),
            pl.BlockSpec((tm, tn), lambda p, j: (mat_tile(p), mat_chunk(p, j))),
            pl.BlockSpec((1, tn), lambda p, j: (0, out_chunk(p, j))),
            pl.BlockSpec((1, tn), lambda p, j: (0, out_chunk(p, j))),
            pl.BlockSpec((2, tn, LANES), lambda p, j: (0, out_chunk(p, j), 0)),
            pl.BlockSpec((1, LANES), const2),
        ],
        out_specs=[
            pl.BlockSpec((tm, tn), lambda p, j: (out_tile(p), out_chunk(p, j))),
            pl.BlockSpec((tm, tn // 2), lambda p, j: (out_tile(p), out_chunk(p, j))),
            pl.BlockSpec((tm, LANES), lambda p, j: (out_tile(p), 0)),
        ],
        out_shape=[jax.ShapeDtypeStruct((m, d), F32), jax.ShapeDtypeStruct((m, d // 2), jnp.uint32),
                   jax.ShapeDtypeStruct((m, LANES), F32)],
        scratch_shapes=[pltpu.VMEM((2, tm, d_gmlp), BF16), pltpu.VMEM((CHUNK, d_gmlp), F32),
                        pltpu.VMEM((n_chunks, tm, tn), F32), pltpu.VMEM((tm, 1), F32), pltpu.VMEM((tm, 1), F32)],
        compiler_params=_params("arbitrary", "arbitrary", vmem_limit=VMEM_LIMIT_LARGE),
        name="gate_mix_ln_router",
    )(attn2, proj2, proj2, g_ln_g, g_ln_b, w_s, b_s[:, :, None], beta_gmlp, wb, x2, ln_g, ln_b, w_route, b_route)


def _route_kernel(lo_ref, eid_ref, wt_ref, cnt_ref):
    lg = lo_ref[...]
    lane = lax.broadcasted_iota(jnp.int32, lg.shape, 1)
    lane_f = lane.astype(F32)
    none = float(LANES)
    first = lambda hit: jnp.min(jnp.where(hit, lane_f, none), axis=-1, keepdims=True)

    in_groups = lane < N_GROUPS
    g_logits = jnp.where(in_groups, lg, -jnp.inf)
    g_max = jnp.max(g_logits, axis=-1, keepdims=True)
    g_idx = first(g_logits == g_max)
    g_w = 1.0 / jnp.sum(jnp.where(in_groups, jnp.exp(lg - g_max), 0.0), axis=-1, keepdims=True)

    lo = N_GROUPS + g_idx * EXPERTS_PER_GROUP
    in_group = (lane_f >= lo) & (lane_f < lo + EXPERTS_PER_GROUP)
    e_logits = jnp.where(in_group, lg, -jnp.inf)
    t1 = jnp.max(e_logits, axis=-1, keepdims=True)
    i1 = first(e_logits == t1)
    e_rest = jnp.where(lane_f == i1, -jnp.inf, e_logits)
    t2 = jnp.max(e_rest, axis=-1, keepdims=True)
    i2 = first(e_rest == t2)
    ex = jnp.exp(t2 - t1)
    w1 = g_w * (1.0 / (1.0 + ex))
    w2 = g_w * (ex / (1.0 + ex))
    wt_ref[...] = jnp.where(lane == 0, w1, jnp.where(lane == 1, w2, 0.0))

    @pl.when(pl.program_id(0) == 0)
    def _():
        cnt_ref[...] = jnp.zeros(cnt_ref.shape, F32)

    tm = lg.shape[0]
    pick1 = lane_f == i1 - N_GROUPS
    pick2 = lane_f == i2 - N_GROUPS
    chosen = pick1.astype(F32) + pick2.astype(F32)
    earlier = (lax.broadcasted_iota(jnp.int32, (tm, tm), 1) < lax.broadcasted_iota(jnp.int32, (tm, tm), 0))
    before = jnp.dot(earlier.astype(BF16), chosen.astype(BF16), preferred_element_type=F32) + cnt_ref[...]
    r1 = jnp.sum(jnp.where(pick1, before, 0.0), axis=-1, keepdims=True)
    r2 = jnp.sum(jnp.where(pick2, before, 0.0), axis=-1, keepdims=True)
    cnt_ref[...] += jnp.sum(chosen, axis=0, keepdims=True)
    picks = (jnp.where(lane == 0, i1 - N_GROUPS, 0.0) + jnp.where(lane == 1, i2 - N_GROUPS, 0.0)
             + jnp.where(lane == 2, r1, 0.0) + jnp.where(lane == 3, r2, 0.0))
    eid_ref[...] = picks.astype(jnp.int32)


def _route(logits, tm=1024):
    m = logits.shape[0]
    spec = pl.BlockSpec((tm, LANES), lambda i: (i, 0))
    return pl.pallas_call(
        _route_kernel,
        grid=(m // tm,),
        in_specs=[spec],
        out_specs=[spec, spec, pl.BlockSpec((1, LANES), lambda i: (0, 0))],
        out_shape=[jax.ShapeDtypeStruct((m, LANES), jnp.int32), jax.ShapeDtypeStruct((m, LANES), F32),
                   jax.ShapeDtypeStruct((1, LANES), F32)],
        compiler_params=_params("arbitrary"),
        name="route",
    )(logits)


def _expert_kernel(be_ref, cnt_ref, nreal_ref, tok_ref, aid_ref, x_hbm, wg_ref, wu_ref, wd_ref, ys_hbm,
                   xbuf, xb, acc, obuf, gsem, ssem, *, tm, nf):
    half = xb.shape[1] // 2
    b = pl.program_id(0)
    f = pl.program_id(1)
    nreal = nreal_ref[0]
    slot = lax.rem(b, 2)

    def rows_of(blk):
        return pl.multiple_of(cnt_ref[blk], SUBLANES)

    def for_each_row(n_rows, copy_of_row):
        def body(grp, carry):
            base = pl.multiple_of(grp * SUBLANES, SUBLANES)
            for k in range(SUBLANES):
                copy_of_row(base + k).start(priority=k % 2)
            return carry
        lax.fori_loop(0, lax.shift_right_logical(n_rows, SUBLANES.bit_length() - 1), body, 0)

    def start_gather(slot_, n_rows):
        for_each_row(n_rows, lambda r: pltpu.make_async_copy(
            x_hbm.at[pl.ds(tok_ref[0, r], 1)], xbuf.at[slot_, pl.ds(r, 1)], gsem.at[slot_]))

    def wait_gather(slot_, n_rows):
        pltpu.make_async_copy(x_hbm.at[pl.ds(0, n_rows)], xbuf.at[slot_, pl.ds(0, n_rows)], gsem.at[slot_]).wait()

    def start_scatter(slot_, n_rows):
        for_each_row(n_rows, lambda r: pltpu.make_async_copy(
            obuf.at[slot_, pl.ds(r, 1)], ys_hbm.at[pl.ds(aid_ref[0, r], 1)], ssem.at[slot_]))

    def wait_scatter(slot_, n_rows):
        pltpu.make_async_copy(obuf.at[slot_, pl.ds(0, n_rows)], ys_hbm.at[pl.ds(0, n_rows)], ssem.at[slot_]).wait()

    @pl.when(b < nreal)
    def _():
        @pl.when(f == 0)
        def _():
            @pl.when(b == 0)
            def _():
                xbuf[...] = jnp.zeros(xbuf.shape, xbuf.dtype)
                start_gather(0, rows_of(0))
            wait_gather(slot, rows_of(b))
            lo, hi = _unpack_bf16_pair(xbuf[slot])
            gw = X1_PACK_GROUP // 2
            for grp in range(half // gw):
                xb[:, 2 * grp * gw:(2 * grp + 1) * gw] = lo[:, grp * gw:(grp + 1) * gw].astype(BF16)
                xb[:, (2 * grp + 1) * gw:(2 * grp + 2) * gw] = hi[:, grp * gw:(grp + 1) * gw].astype(BF16)

        x = xb[...]
        g = jnp.dot(x, wg_ref[...].astype(BF16), preferred_element_type=F32)
        u = jnp.dot(x, wu_ref[...].astype(BF16), preferred_element_type=F32)
        hidden = (g * (1.0 / (1.0 + jnp.exp(-g))) * u).astype(BF16)
        contrib = jnp.dot(hidden, wd_ref[...].astype(BF16), preferred_element_type=F32)

        @pl.when(f == 0)
        def _():
            acc[...] = contrib

        @pl.when((f > 0) & (f < nf - 1))
        def _():
            acc[...] += contrib

        @pl.when(f == nf - 1)
        def _():
            @pl.when(b >= 1)
            def _():
                wait_scatter(1 - slot, rows_of(b - 1))
            total = acc[...] + contrib
            obuf[slot] = _pack_bf16_pair(total[:, :half], total[:, half:])
            start_scatter(slot, rows_of(b))

            @pl.when(b + 1 < nreal)
            def _():
                start_gather(1 - slot, rows_of(b + 1))

            @pl.when(b + 1 == nreal)
            def _():
                wait_scatter(slot, rows_of(b))


def _expert_mlp(x1p, plan, w_gate, w_up, w_down, tm, fc=256):
    tok_blocks, aid_blocks, block_expert, block_rows, n_real = plan
    n, dp = x1p.shape
    d = 2 * dp
    n_blocks = tok_blocks.shape[0]
    d_ff = w_gate.shape[2]
    nf = d_ff // fc
    assert nf >= 2

    def live(b, f, nreal):
        is_real = b < nreal[0]
        return jnp.where(is_real, b, nreal[0] - 1), jnp.where(is_real, f, nf - 1)

    def tok_map(b, f, be, cnt, nreal):
        bb, ff = live(b, f, nreal)
        return jnp.minimum(bb + (ff == nf - 1).astype(jnp.int32), nreal[0] - 1), 0, 0

    def aid_map(b, f, be, cnt, nreal):
        bb, _ = live(b, f, nreal)
        return bb, 0, 0

    def w_in_map(b, f, be, cnt, nreal):
        bb, ff = live(b, f, nreal)
        return be[bb], 0, ff

    def w_out_map(b, f, be, cnt, nreal):
        bb, ff = live(b, f, nreal)
        return be[bb], ff, 0

    grid_spec = pltpu.PrefetchScalarGridSpec(
        num_scalar_prefetch=3,
        grid=(n_blocks, nf),
        in_specs=[
            pl.BlockSpec((None, 1, tm), tok_map, memory_space=pltpu.SMEM),
            pl.BlockSpec((None, 1, tm), aid_map, memory_space=pltpu.SMEM),
            pl.BlockSpec(memory_space=pl.ANY),
            pl.BlockSpec((None, d, fc), w_in_map),
            pl.BlockSpec((None, d, fc), w_in_map),
            pl.BlockSpec((None, fc, d), w_out_map),
        ],
        out_specs=pl.BlockSpec(memory_space=pl.ANY),
        scratch_shapes=[pltpu.VMEM((2, tm, dp), jnp.uint32), pltpu.VMEM((tm, d), BF16), pltpu.VMEM((tm, d), F32),
                        pltpu.VMEM((2, tm, dp), jnp.uint32),
                        pltpu.SemaphoreType.DMA((2,)), pltpu.SemaphoreType.DMA((2,))],
    )
    kernel = functools.partial(_expert_kernel, tm=tm, nf=nf)
    return pl.pallas_call(
        kernel,
        grid_spec=grid_spec,
        out_shape=jax.ShapeDtypeStruct((n * TOP_K + 2 * tm, dp), jnp.uint32),
        compiler_params=_params("arbitrary", "arbitrary"),
        name="expert_mlp",
    )(block_expert, block_rows, n_real, tok_blocks, aid_blocks, x1p, w_gate, w_up, w_down)


def _combine_kernel(x1_ref, y0_ref, y1_ref, wt_ref, lg_ref, lb_ref, o_ref, buf_ref):
    half = x1_ref.shape[1] // 2
    wt = wt_ref[...]
    y0 = _unpack_bf16_pair(y0_ref[...])
    y1 = _unpack_bf16_pair(y1_ref[...])
    for p, cs in enumerate((slice(0, half), slice(half, 2 * half))):
        buf_ref[:, cs] = DN_ALPHA * x1_ref[:, cs] + (wt[:, 0:1] * y0[p] + wt[:, 1:2] * y1[p])
    hsum = buf_ref[...]
    mu = jnp.mean(hsum, axis=-1, keepdims=True)
    hc = hsum - mu
    var = jnp.mean(jnp.square(hc), axis=-1, keepdims=True)
    o_ref[...] = hc * lax.rsqrt(var + LN_EPS) * lg_ref[...] + lb_ref[...]


def _combine_ln(x1, ys, wts, ln_g, ln_b, tm=256):
    m, d = x1.shape
    big = pl.BlockSpec((tm, d), lambda i: (i, 0))
    first = pl.BlockSpec((tm, d // 2), lambda i: (i, 0))
    second = pl.BlockSpec((tm, d // 2), lambda i: (m // tm + i, 0))
    row_vec = pl.BlockSpec((1, d), lambda i: (0, 0))
    return pl.pallas_call(
        _combine_kernel,
        grid=(m // tm,),
        in_specs=[big, first, second, pl.BlockSpec((tm, LANES), lambda i: (i, 0)), row_vec, row_vec],
        out_specs=big,
        out_shape=jax.ShapeDtypeStruct((m, d), F32),
        scratch_shapes=[pltpu.VMEM((tm, d), F32)],
        compiler_params=_params("arbitrary"),
        name="combine_ln",
    )(x1, ys, ys, wts, ln_g, ln_b)


def _plan_blocks(eid, rank, counts, tm):
    n = eid.shape[0]
    a = n * TOP_K
    expert_id = eid.reshape(-1)
    padded = (counts + tm - 1) // tm * tm
    pad_ends = jnp.cumsum(padded)
    pad_starts = pad_ends - padded
    n_blocks = a // tm + N_EXPERTS
    experts = jnp.arange(N_EXPERTS, dtype=jnp.int32)
    start_of = jnp.dot((expert_id[:, None] == experts[None, :]).astype(F32), pad_starts.astype(F32),
                       precision=lax.Precision.HIGHEST).astype(jnp.int32)
    pos = start_of + rank.reshape(-1)
    spare = a + jnp.arange(n_blocks * tm, dtype=jnp.int32) % (2 * tm)
    flat = jnp.arange(a, dtype=jnp.int32)
    aid_pad = spare.at[pos].set((flat % TOP_K) * n + flat // TOP_K, unique_indices=True)
    tok_pad = jnp.where(aid_pad < a, aid_pad % n, 0)
    block_start = jnp.arange(n_blocks, dtype=jnp.int32) * tm
    block_expert = jnp.minimum(jnp.sum(pad_ends[None, :] <= block_start[:, None], axis=1), N_EXPERTS - 1).astype(jnp.int32)
    block_count = jnp.clip(counts[block_expert] - (block_start - pad_starts[block_expert]), 0, tm)
    block_rows = ((block_count + SUBLANES - 1) // SUBLANES * SUBLANES).astype(jnp.int32)
    n_real = (pad_ends[-1] // tm).astype(jnp.int32).reshape(1)
    return tok_pad.reshape(n_blocks, 1, tm), aid_pad.reshape(n_blocks, 1, tm), block_expert, block_rows, n_real


def kernel(x, w_in, w_out, lambda_q1, lambda_k1, lambda_q2, lambda_k2, subln_g, beta_attn, gmlp_ln_g, gmlp_ln_b,
           spatial_w, spatial_b, beta_gmlp, rel_bias, ln1_g, ln1_b, w_group, b_group, w_expert, b_expert,
           w_gate, w_up, w_down, ln2_g, ln2_b):
    b, s, d = x.shape
    n = b * s
    d_attn = ATTN_HEADS * ATTN_V_DIM
    d_gmlp = d - d_attn
    moe_tm = 320
    for l in range(DEPTH):
        lambda_init = 0.8 - 0.6 * math.exp(-0.3 * l)
        x2 = x.reshape(n, d)
        proj = _proj_matmul(x2.astype(BF16), w_in[l])
        attn = _diff_attention(proj.reshape(b, s, -1), rel_bias, lambda_q1[l][None], lambda_k1[l][None],
                               lambda_q2[l][None], lambda_k2[l][None], subln_g[l][None], beta_attn[l][None],
                               lambda_init)
        gating = (gmlp_ln_g[l][None], gmlp_ln_b[l][None], spatial_w[l], spatial_b[l], beta_gmlp[l][None])
        n_route = N_GROUPS + N_EXPERTS
        w_route = jnp.concatenate([w_group[l], w_expert[l].reshape(d, N_EXPERTS),
                                   jnp.zeros((d, LANES - n_route), F32)], axis=1)
        w_route_hi = w_route.astype(BF16)
        w_route = jnp.stack([w_route_hi, (w_route - w_route_hi.astype(F32)).astype(BF16)])
        b_route = jnp.concatenate([b_group[l], b_expert[l].reshape(-1), jnp.zeros((LANES - n_route,), F32)])[None]
        x1, x1p, logits = _gate_mix_ln_router(attn.reshape(n, d_attn), proj, gating, w_out[l].astype(BF16), x2,
                                              ln1_g[l][None], ln1_b[l][None], w_route, b_route, d_gmlp,
                                              u_col=3 * d_attn // d_gmlp, g_col=3 * d_attn // d_gmlp + 1)
        picks, wts, counts = _route(logits)
        plan = _plan_blocks(picks[:, :TOP_K], picks[:, TOP_K:2 * TOP_K], counts[0, :N_EXPERTS].astype(jnp.int32),
                            moe_tm)
        ys = _expert_mlp(x1p, plan, w_gate[l], w_up[l], w_down[l], moe_tm)
        x = _combine_ln(x1, ys, wts, ln2_g[l][None], ln2_b[l][None]).reshape(b, s, d)
    return x
```

```python
import functools
import math

import jax
import jax.numpy as jnp
from jax import lax
from jax.experimental import pallas as pl
from jax.experimental.pallas import tpu as pltpu

F32 = jnp.float32
BF16 = jnp.bfloat16

ATTN_HEADS = 8
ATTN_HEAD_DIM = 128
ATTN_V_DIM = 2 * ATTN_HEAD_DIM
GMLP_HEADS = 8
CHUNK = 128
REL_BUCKETS = 32
REL_MAX_DIST = 128
N_GROUPS = 8
EXPERTS_PER_GROUP = 8
N_EXPERTS = N_GROUPS * EXPERTS_PER_GROUP
TOP_K = 2
LN_EPS = 1e-5
DEPTH = 1
DN_ALPHA = (2 * DEPTH) ** 0.25
NEG_INF = -1e30
LOG2E = math.log2(math.e)
LANES = 128
SUBLANES = 8
VMEM_LIMIT = 56 * 1024 * 1024
VMEM_LIMIT_LARGE = 60 * 1024 * 1024
X1_PACK_GROUP = 256


def _params(*semantics, vmem_limit=VMEM_LIMIT):
    return pltpu.CompilerParams(dimension_semantics=semantics, vmem_limit_bytes=vmem_limit)


def _matmul_kernel(x_ref, w_ref, o_ref, wb_ref):
    @pl.when(pl.program_id(1) == 0)
    def _():
        wb_ref[...] = w_ref[...].astype(BF16)

    o_ref[...] = jnp.dot(x_ref[...], wb_ref[...], preferred_element_type=F32).astype(o_ref.dtype)


def _proj_matmul(xb, w, tm=512, tn=1024):
    m, k = xb.shape
    n = w.shape[1]
    return pl.pallas_call(
        _matmul_kernel,
        grid=(n // tn, m // tm),
        in_specs=[pl.BlockSpec((tm, k), lambda j, i: (i, 0)),
                  pl.BlockSpec((k, tn), lambda j, i: (0, j))],
        out_specs=pl.BlockSpec((tm, tn), lambda j, i: (i, j)),
        out_shape=jax.ShapeDtypeStruct((m, n), BF16),
        scratch_shapes=[pltpu.VMEM((k, tn), BF16)],
        compiler_params=_params("arbitrary", "arbitrary"),
        name="proj_matmul",
    )(xb, w)


def _rel_bucket(n):
    max_exact = REL_BUCKETS // 2
    nf = jnp.maximum(n, max_exact).astype(F32)
    large = max_exact + (jnp.log(nf / max_exact) / math.log(REL_MAX_DIST / max_exact)
                         * (REL_BUCKETS - max_exact)).astype(jnp.int32)
    large = jnp.minimum(large, REL_BUCKETS - 1)
    return jnp.where(n < max_exact, n, large)


def _rel_bias_tiles(rel_bias, blk):
    pos = jnp.arange(blk)
    tiles = []
    for d in (0, 1):
        bucket = _rel_bucket(jnp.maximum(pos[:, None] + d * blk - pos[None, :], 0))
        hit = bucket[None] == jnp.arange(REL_BUCKETS)[:, None, None]
        tiles.append(jnp.sum(jnp.where(hit[:, None], rel_bias[:, :, None, None], 0.0), axis=0))
    return jnp.stack(tiles, axis=1).astype(F32)


def _attn_kernel(far_ref, q_ref, k_ref, v_ref, bias_ref, lq1_ref, lk1_ref, lq2_ref, lk2_ref,
                 sg_ref, ba_ref, o_ref, mx_ref, sh_ref, mrow_ref, l_ref, s_ref, acc_ref, *, blk, hp, lambda_init):
    g = pl.program_id(1)
    i = pl.program_id(2)
    dh = ATTN_HEAD_DIM
    dv = ATTN_V_DIM
    c1 = ATTN_HEAD_DIM ** -0.5 * LOG2E
    streams = [(hh, mi) for hh in range(hp) for mi in range(2)]
    q = q_ref[...]
    qs = [q[:, hh * dv + mi * dh:hh * dv + (mi + 1) * dh] for hh, mi in streams]
    contract_last = (((1,), (1,)), ((), ()))
    far2 = [far_ref[g * hp + hh] * LOG2E for hh in range(hp)]
    n_far = jnp.maximum(i - 1, 0)

    def lane_chunks(x):
        return [x[:, c * LANES:(c + 1) * LANES] for c in range(x.shape[1] // LANES)]

    def raw_scores(start, width):
        kj = k_ref[pl.ds(start, width), :]
        return [lax.dot_general(qs[sid], kj[:, hh * dv + mi * dh:hh * dv + (mi + 1) * dh], contract_last,
                                preferred_element_type=F32) for sid, (hh, mi) in enumerate(streams)]

    def over_far_blocks(body):
        def pair(t, carry):
            body(pl.multiple_of(t * (2 * blk), 2 * blk), 2 * blk)
            return carry
        lax.fori_loop(0, lax.shift_right_logical(n_far, 1), pair, 0)

        @pl.when(lax.rem(n_far, 2) == 1)
        def _():
            body(pl.multiple_of((n_far - 1) * blk, blk), blk)

    mx_ref[...] = jnp.full(mx_ref.shape, -jnp.inf, F32)

    def max_body(start, width):
        s = raw_scores(start, width)
        for sid in range(len(streams)):
            best = mx_ref[sid]
            for chunk in lane_chunks(s[sid]):
                best = jnp.maximum(best, chunk)
            mx_ref[sid] = best

    over_far_blocks(max_body)

    row = lax.broadcasted_iota(jnp.int32, (blk, blk), 0)
    col = lax.broadcasted_iota(jnp.int32, (blk, blk), 1)
    near = raw_scores(pl.multiple_of(n_far * blk, blk), blk)
    diag = raw_scores(pl.multiple_of(i * blk, blk), blk)
    for sid, (hh, mi) in enumerate(streams):
        s_near = jnp.where(i >= 1, near[sid] * c1 + bias_ref[hh, 1], NEG_INF)
        s_diag = jnp.where(col <= row, diag[sid] * c1 + bias_ref[hh, 0], NEG_INF)
        s_ref[sid, 0] = s_near
        s_ref[sid, 1] = s_diag
        best = mx_ref[sid] * c1 + far2[hh]
        for chunk in lane_chunks(s_near) + lane_chunks(s_diag):
            best = jnp.maximum(best, chunk)
        m_rows = jnp.broadcast_to(jnp.max(best, axis=-1, keepdims=True), (blk, LANES))
        mrow_ref[sid] = m_rows
        sh_ref[sid] = far2[hh] - m_rows

    l_ref[...] = jnp.zeros(l_ref.shape, F32)
    acc_ref[...] = jnp.zeros(acc_ref.shape, F32)

    def accumulate(p_chunks, vj):
        ps = []
        for sid in range(len(streams)):
            tot = l_ref[sid]
            for chunk in p_chunks[sid]:
                tot = tot + chunk
            l_ref[sid] = tot
            ps.append(jnp.concatenate(p_chunks[sid], axis=1).astype(BF16))
        for hh in range(hp):
            acc_ref[hh] += jnp.dot(jnp.concatenate(ps[2 * hh:2 * hh + 2], axis=0), vj[:, hh * dv:(hh + 1) * dv],
                                   preferred_element_type=F32)

    def pv_body(start, width):
        s = raw_scores(start, width)
        accumulate([[jnp.exp2(chunk * c1 + sh_ref[sid]) for chunk in lane_chunks(s[sid])]
                    for sid in range(len(streams))], v_ref[pl.ds(start, width), :])

    over_far_blocks(pv_body)

    for t, start in ((0, pl.multiple_of(n_far * blk, blk)), (1, pl.multiple_of(i * blk, blk))):
        accumulate([[jnp.exp2(chunk - mrow_ref[sid]) for chunk in lane_chunks(s_ref[sid, t])]
                    for sid in range(len(streams))], v_ref[pl.ds(start, blk), :])

    lam = (jnp.exp(jnp.sum(lq1_ref[...] * lk1_ref[...], axis=-1, keepdims=True))
           - jnp.exp(jnp.sum(lq2_ref[...] * lk2_ref[...], axis=-1, keepdims=True)) + lambda_init)
    for hh in range(hp):
        l1 = jnp.sum(l_ref[2 * hh], axis=-1, keepdims=True)
        l2 = jnp.sum(l_ref[2 * hh + 1], axis=-1, keepdims=True)
        o = acc_ref[hh, :blk] / l1 - lam * (acc_ref[hh, blk:] / l2)
        o = o * lax.rsqrt(jnp.mean(jnp.square(o), axis=-1, keepdims=True) + LN_EPS) * sg_ref[...]
        o = o * (1.0 - lambda_init) * ba_ref[:, hh * dv:(hh + 1) * dv]
        o_ref[:, hh * dv:(hh + 1) * dv] = o.astype(o_ref.dtype)


def _diff_attention(proj3, rel_bias, lq1, lk1, lq2, lk2, subln_g, beta_attn, lambda_init, blk=256, hp=4):
    b, s, _ = proj3.shape
    h = ATTN_HEADS
    dv = ATTN_V_DIM
    gw = hp * dv
    groups = h // hp
    assert REL_BUCKETS // 2 + int(math.log((blk + 1) / (REL_BUCKETS // 2)) / math.log(REL_MAX_DIST / (REL_BUCKETS // 2))
                                  * (REL_BUCKETS // 2)) >= REL_BUCKETS
    tiles = _rel_bias_tiles(rel_bias, blk) * LOG2E
    far = rel_bias[REL_BUCKETS - 1].astype(F32)
    vec = lambda c: pl.BlockSpec((1, c), lambda bi, gi, qi: (0, 0))
    kernel = functools.partial(_attn_kernel, blk=blk, hp=hp, lambda_init=lambda_init)
    n_streams = 2 * hp
    stat = pltpu.VMEM((n_streams, blk, LANES), F32)
    return pl.pallas_call(
        kernel,
        grid=(b, groups, s // blk),
        in_specs=[
            pl.BlockSpec(memory_space=pltpu.SMEM),
            pl.BlockSpec((None, blk, gw), lambda bi, gi, qi: (bi, qi, gi)),
            pl.BlockSpec((None, s, gw), lambda bi, gi, qi: (bi, 0, groups + gi)),
            pl.BlockSpec((None, s, gw), lambda bi, gi, qi: (bi, 0, 2 * groups + gi)),
            pl.BlockSpec((hp, 2, blk, blk), lambda bi, gi, qi: (gi, 0, 0, 0)),
            vec(ATTN_HEAD_DIM), vec(ATTN_HEAD_DIM), vec(ATTN_HEAD_DIM), vec(ATTN_HEAD_DIM),
            vec(dv),
            pl.BlockSpec((1, gw), lambda bi, gi, qi: (0, gi)),
        ],
        out_specs=pl.BlockSpec((None, blk, gw), lambda bi, gi, qi: (bi, qi, gi)),
        out_shape=jax.ShapeDtypeStruct((b, s, h * dv), BF16),
        scratch_shapes=[stat, stat, stat, stat,
                        pltpu.VMEM((n_streams, 2, blk, blk), F32), pltpu.VMEM((hp, 2 * blk, dv), F32)],
        compiler_params=_params("arbitrary", "arbitrary", "arbitrary"),
        name="diff_attention",
    )(far, proj3, proj3, proj3, tiles, lq1, lk1, lq2, lk2, subln_g, beta_attn)


def _gelu(x):
    return 0.5 * x * (1.0 + lax.erf(x * (2.0 ** -0.5)))


def _gating_chunk(u_ref, g_ref, lg_ref, lb_ref, ws_ref, bs_ref, bg_ref, buf_ref):
    hd = u_ref.shape[1] // GMLP_HEADS
    g = _gelu(g_ref[...].astype(F32))
    mu = jnp.mean(g, axis=-1, keepdims=True)
    gc = g - mu
    var = jnp.mean(jnp.square(gc), axis=-1, keepdims=True)
    v = (gc * lax.rsqrt(var + LN_EPS) * lg_ref[...] + lb_ref[...]).astype(BF16)
    row = lax.broadcasted_iota(jnp.int32, (CHUNK, CHUNK), 0)
    col = lax.broadcasted_iota(jnp.int32, (CHUNK, CHUNK), 1)
    tril = col <= row
    ss = jnp.zeros((CHUNK, 1), F32)
    for hh in range(GMLP_HEADS):
        w = jnp.where(tril, ws_ref[hh], 0.0).astype(BF16)
        mixed = jnp.dot(w, v[:, hh * hd:(hh + 1) * hd], preferred_element_type=F32) + bs_ref[hh]
        out = _gelu(u_ref[:, hh * hd:(hh + 1) * hd].astype(F32)) * mixed
        ss = ss + jnp.sum(jnp.square(out), axis=-1, keepdims=True)
        buf_ref[:, hh * hd:(hh + 1) * hd] = out
    rstd = lax.rsqrt(ss / u_ref.shape[1] + LN_EPS)
    return (buf_ref[...] * rstd * bg_ref[...]).astype(BF16)


def _layer_norm_chunks(buf_ref, n_chunks, d):
    tot = jnp.sum(buf_ref[0], axis=-1, keepdims=True)
    for c in range(1, n_chunks):
        tot = tot + jnp.sum(buf_ref[c], axis=-1, keepdims=True)
    mu = tot / d
    sq = jnp.sum(jnp.square(buf_ref[0] - mu), axis=-1, keepdims=True)
    for c in range(1, n_chunks):
        sq = sq + jnp.sum(jnp.square(buf_ref[c] - mu), axis=-1, keepdims=True)
    return mu, lax.rsqrt(sq / d + LN_EPS)


def _pack_bf16_pair(lo, hi):
    lo_bits = lax.bitcast_convert_type(lo.astype(BF16).astype(F32), jnp.uint32) >> 16
    hi_bits = lax.bitcast_convert_type(hi.astype(BF16).astype(F32), jnp.uint32) & jnp.uint32(0xFFFF0000)
    return hi_bits | lo_bits


def _unpack_bf16_pair(words):
    lo = lax.bitcast_convert_type(words << 16, F32)
    hi = lax.bitcast_convert_type(words & jnp.uint32(0xFFFF0000), F32)
    return lo, hi


def _mix_kernel(a_ref, u_ref, g_ref, glg_ref, glb_ref, ws_ref, bs_ref, bg_ref, w_ref, x_ref, lg_ref, lb_ref,
                wr_ref, br_ref, x1_ref, x1p_ref, lo_ref, gm_ref, gbuf_ref, buf_ref, mu_ref, rstd_ref,
                *, n_tiles, n_chunks, tn):
    p = pl.program_id(0)
    j = pl.program_id(1)
    ka = a_ref.shape[1]

    def gate():
        rows = pl.ds(pl.multiple_of(j * CHUNK, CHUNK), CHUNK)
        gm_ref[lax.rem(p, 2), rows, :] = _gating_chunk(u_ref, g_ref, glg_ref, glb_ref, ws_ref, bs_ref, bg_ref,
                                                       gbuf_ref)

    def normalise():
        y = (buf_ref[j] - mu_ref[...]) * rstd_ref[...] * lg_ref[...] + lb_ref[...]
        x1_ref[...] = y
        gw = X1_PACK_GROUP // 2
        for grp in range(tn // X1_PACK_GROUP):
            x1p_ref[:, grp * gw:(grp + 1) * gw] = _pack_bf16_pair(y[:, 2 * grp * gw:(2 * grp + 1) * gw],
                                                                  y[:, (2 * grp + 1) * gw:(2 * grp + 2) * gw])
        y_hi = y.astype(BF16)
        y_lo = (y - y_hi.astype(F32)).astype(BF16)
        start = jnp.where(j == 0, jnp.broadcast_to(br_ref[...], lo_ref.shape), lo_ref[...])
        lo_ref[...] = start + (jnp.dot(y_hi, wr_ref[0], preferred_element_type=F32)
                               + (jnp.dot(y_lo, wr_ref[0], preferred_element_type=F32)
                                  + jnp.dot(y_hi, wr_ref[1], preferred_element_type=F32)))

    def matmul():
        mix = (jnp.dot(a_ref[...], w_ref[:ka, :], preferred_element_type=F32)
               + jnp.dot(gm_ref[lax.rem(p + 1, 2)], w_ref[ka:, :], preferred_element_type=F32))
        buf_ref[j] = DN_ALPHA * x_ref[...] + mix

    def row_stats():
        @pl.when(j == n_chunks - 1)
        def _():
            mu, rstd = _layer_norm_chunks(buf_ref, n_chunks, n_chunks * tn)
            mu_ref[...] = mu
            rstd_ref[...] = rstd

    @pl.when(p == 0)
    def _():
        gate()

    @pl.when(p == 1)
    def _():
        matmul()
        gate()
        row_stats()

    @pl.when((p >= 2) & (p < n_tiles))
    def _():
        normalise()
        matmul()
        gate()
        row_stats()

    @pl.when(p == n_tiles)
    def _():
        normalise()
        matmul()
        row_stats()

    @pl.when(p == n_tiles + 1)
    def _():
        normalise()


def _gate_mix_ln_router(attn2, proj2, gating, wb, x2, ln_g, ln_b, w_route, b_route, d_gmlp, u_col, g_col,
                        tm=512, tn=1024):
    g_ln_g, g_ln_b, w_s, b_s, beta_gmlp = gating
    m, d = x2.shape
    ka = attn2.shape[1]
    n_chunks = d // tn
    n_tiles = m // tm
    assert tm // CHUNK == n_chunks and n_tiles >= 3 and tn % X1_PACK_GROUP == 0
    last = n_chunks - 1
    gate_tile = lambda p: jnp.minimum(p, n_tiles - 1)
    mat_tile = lambda p: jnp.clip(p - 1, 0, n_tiles - 1)
    out_tile = lambda p: jnp.maximum(p - 2, 0)
    gate_chunk = lambda p, j: jnp.where(p < n_tiles, j, last)
    mat_chunk = lambda p, j: jnp.where(p < 1, 0, jnp.where(p <= n_tiles, j, last))
    out_chunk = lambda p, j: jnp.where(p >= 2, j, 0)
    gate_rows = lambda col: (lambda p, j: (gate_tile(p) * n_chunks + gate_chunk(p, j), col))
    const2 = lambda p, j: (0, 0)
    const3 = lambda p, j: (0, 0, 0)
    kernel = functools.partial(_mix_kernel, n_tiles=n_tiles, n_chunks=n_chunks, tn=tn)
    return pl.pallas_call(
        kernel,
        grid=(n_tiles + 2, n_chunks),
        in_specs=[
            pl.BlockSpec((tm, ka), lambda p, j: (mat_tile(p), 0)),
            pl.BlockSpec((CHUNK, d_gmlp), gate_rows(u_col)),
            pl.BlockSpec((CHUNK, d_gmlp), gate_rows(g_col)),
            pl.BlockSpec((1, d_gmlp), const2), pl.BlockSpec((1, d_gmlp), const2),
            pl.BlockSpec((GMLP_HEADS, CHUNK, CHUNK), const3),
            pl.BlockSpec((GMLP_HEADS, CHUNK, 1), const3),
            pl.BlockSpec((1, d_gmlp), const2),
            pl.BlockSpec((ka + d_gmlp, tn), lambda p, j: (0, mat_chunk(p, j))),
            pl.BlockSpec((tm, tn), lambda p, j: (mat_tile(p), mat_chunk(p, j))),
            pl.BlockSpec((1, tn), lambda p, j: (0, out_chunk(p, j))),
            pl.BlockSpec((1, tn), lambda p, j: (0, out_chunk(p, j))),
            pl.BlockSpec((2, tn, LANES), lambda p, j: (0, out_chunk(p, j), 0)),
            pl.BlockSpec((1, LANES), const2),
        ],
        out_specs=[
            pl.BlockSpec((tm, tn), lambda p, j: (out_tile(p), out_chunk(p, j))),
            pl.BlockSpec((tm, tn // 2), lambda p, j: (out_tile(p), out_chunk(p, j))),
            pl.BlockSpec((tm, LANES), lambda p, j: (out_tile(p), 0)),
        ],
        out_shape=[jax.ShapeDtypeStruct((m, d), F32), jax.ShapeDtypeStruct((m, d // 2), jnp.uint32),
                   jax.ShapeDtypeStruct((m, LANES), F32)],
        scratch_shapes=[pltpu.VMEM((2, tm, d_gmlp), BF16), pltpu.VMEM((CHUNK, d_gmlp), F32),
                        pltpu.VMEM((n_chunks, tm, tn), F32), pltpu.VMEM((tm, 1), F32), pltpu.VMEM((tm, 1), F32)],
        compiler_params=_params("arbitrary", "arbitrary", vmem_limit=VMEM_LIMIT_LARGE),
        name="gate_mix_ln_router",
    )(attn2, proj2, proj2, g_ln_g, g_ln_b, w_s, b_s[:, :, None], beta_gmlp, wb, x2, ln_g, ln_b, w_route, b_route)


def _route_kernel(lo_ref, eid_ref, wt_ref, cnt_ref):
    lg = lo_ref[...]
    lane = lax.broadcasted_iota(jnp.int32, lg.shape, 1)
    lane_f = lane.astype(F32)
    none = float(LANES)
    first = lambda hit: jnp.min(jnp.where(hit, lane_f, none), axis=-1, keepdims=True)

    in_groups = lane < N_GROUPS
    g_logits = jnp.where(in_groups, lg, -jnp.inf)
    g_max = jnp.max(g_logits, axis=-1, keepdims=True)
    g_idx = first(g_logits == g_max)
    g_w = 1.0 / jnp.sum(jnp.where(in_groups, jnp.exp(lg - g_max), 0.0), axis=-1, keepdims=True)

    lo = N_GROUPS + g_idx * EXPERTS_PER_GROUP
    in_group = (lane_f >= lo) & (lane_f < lo + EXPERTS_PER_GROUP)
    e_logits = jnp.where(in_group, lg, -jnp.inf)
    t1 = jnp.max(e_logits, axis=-1, keepdims=True)
    i1 = first(e_logits == t1)
    e_rest = jnp.where(lane_f == i1, -jnp.inf, e_logits)
    t2 = jnp.max(e_rest, axis=-1, keepdims=True)
    i2 = first(e_rest == t2)
    ex = jnp.exp(t2 - t1)
    w1 = g_w * (1.0 / (1.0 + ex))
    w2 = g_w * (ex / (1.0 + ex))
    wt_ref[...] = jnp.where(lane == 0, w1, jnp.where(lane == 1, w2, 0.0))

    @pl.when(pl.program_id(0) == 0)
    def _():
        cnt_ref[...] = jnp.zeros(cnt_ref.shape, F32)

    tm = lg.shape[0]
    pick1 = lane_f == i1 - N_GROUPS
    pick2 = lane_f == i2 - N_GROUPS
    chosen = pick1.astype(F32) + pick2.astype(F32)
    earlier = (lax.broadcasted_iota(jnp.int32, (tm, tm), 1) < lax.broadcasted_iota(jnp.int32, (tm, tm), 0))
    before = jnp.dot(earlier.astype(BF16), chosen.astype(BF16), preferred_element_type=F32) + cnt_ref[...]
    r1 = jnp.sum(jnp.where(pick1, before, 0.0), axis=-1, keepdims=True)
    r2 = jnp.sum(jnp.where(pick2, before, 0.0), axis=-1, keepdims=True)
    cnt_ref[...] += jnp.sum(chosen, axis=0, keepdims=True)
    picks = (jnp.where(lane == 0, i1 - N_GROUPS, 0.0) + jnp.where(lane == 1, i2 - N_GROUPS, 0.0)
             + jnp.where(lane == 2, r1, 0.0) + jnp.where(lane == 3, r2, 0.0))
    eid_ref[...] = picks.astype(jnp.int32)


def _route(logits, tm=1024):
    m = logits.shape[0]
    spec = pl.BlockSpec((tm, LANES), lambda i: (i, 0))
    return pl.pallas_call(
        _route_kernel,
        grid=(m // tm,),
        in_specs=[spec],
        out_specs=[spec, spec, pl.BlockSpec((1, LANES), lambda i: (0, 0))],
        out_shape=[jax.ShapeDtypeStruct((m, LANES), jnp.int32), jax.ShapeDtypeStruct((m, LANES), F32),
                   jax.ShapeDtypeStruct((1, LANES), F32)],
        compiler_params=_params("arbitrary"),
        name="route",
    )(logits)


def _expert_kernel(be_ref, cnt_ref, nreal_ref, tok_ref, aid_ref, x_hbm, wg_hbm, wu_hbm, wd_hbm, ys_hbm,
                   xbuf, xb, acc, obuf, wgb, wub, wdb, gsem, ssem, wsem, *, tm, nf):
    half = xb.shape[1] // 2
    b = pl.program_id(0)
    f = pl.program_id(1)
    nreal = nreal_ref[0]
    slot = lax.rem(b, 2)

    def rows_of(blk):
        return pl.multiple_of(cnt_ref[blk], SUBLANES)

    def for_each_row(n_rows, copy_of_row):
        def body(grp, carry):
            base = pl.multiple_of(grp * SUBLANES, SUBLANES)
            for k in range(SUBLANES):
                copy_of_row(base + k).start(priority=k % 2)
            return carry
        lax.fori_loop(0, lax.shift_right_logical(n_rows, SUBLANES.bit_length() - 1), body, 0)

    def start_gather(slot_, n_rows):
        for_each_row(n_rows, lambda r: pltpu.make_async_copy(
            x_hbm.at[pl.ds(tok_ref[0, r], 1)], xbuf.at[slot_, pl.ds(r, 1)], gsem.at[slot_]))

    def wait_gather(slot_, n_rows):
        pltpu.make_async_copy(x_hbm.at[pl.ds(0, n_rows)], xbuf.at[slot_, pl.ds(0, n_rows)], gsem.at[slot_]).wait()

    def start_scatter(slot_, n_rows):
        for_each_row(n_rows, lambda r: pltpu.make_async_copy(
            obuf.at[slot_, pl.ds(r, 1)], ys_hbm.at[pl.ds(aid_ref[0, r], 1)], ssem.at[slot_]))

    def wait_scatter(slot_, n_rows):
        pltpu.make_async_copy(obuf.at[slot_, pl.ds(0, n_rows)], ys_hbm.at[pl.ds(0, n_rows)], ssem.at[slot_]).wait()

    fc = wgb.shape[2]

    def weight_copies(blk, chunk):
        e = be_ref[blk]
        cols = pl.ds(chunk * fc, fc)
        halves = [pl.ds(chunk * fc + h * (fc // 2), fc // 2) for h in range(2)]
        return [(pltpu.make_async_copy(wg_hbm.at[e, :, cols], wgb.at[chunk], wsem.at[chunk]), 0),
                (pltpu.make_async_copy(wu_hbm.at[e, :, cols], wub.at[chunk], wsem.at[chunk]), 1)] + [
                (pltpu.make_async_copy(wd_hbm.at[e, halves[h], :], wdb.at[chunk, pl.ds(h * (fc // 2), fc // 2)],
                                       wsem.at[chunk]), h) for h in range(2)]

    def start_weights(blk, chunk):
        for copy, priority in weight_copies(blk, chunk):
            copy.start(priority=priority)

    def wait_weights(blk, chunk):
        for copy, _ in weight_copies(blk, chunk):
            copy.wait()

    @pl.when(b < nreal)
    def _():
        @pl.when(f == 0)
        def _():
            @pl.when(b == 0)
            def _():
                xbuf[...] = jnp.zeros(xbuf.shape, xbuf.dtype)
                start_gather(0, rows_of(0))
            wait_gather(slot, rows_of(b))
            lo, hi = _unpack_bf16_pair(xbuf[slot])
            gw = X1_PACK_GROUP // 2
            for grp in range(half // gw):
                xb[:, 2 * grp * gw:(2 * grp + 1) * gw] = lo[:, grp * gw:(grp + 1) * gw].astype(BF16)
                xb[:, (2 * grp + 1) * gw:(2 * grp + 2) * gw] = hi[:, grp * gw:(grp + 1) * gw].astype(BF16)

        for chunk in range(nf):
            @pl.when(f == chunk)
            def _():
                @pl.when((b == 0) & (chunk == 0))
                def _():
                    start_weights(0, 0)
                wait_weights(b, chunk)
                if chunk + 1 < nf:
                    start_weights(b, chunk + 1)
                else:
                    @pl.when(b + 1 < nreal)
                    def _():
                        start_weights(b + 1, 0)

        x = xb[...]
        g = jnp.dot(x, wgb[f].astype(BF16), preferred_element_type=F32)
        u = jnp.dot(x, wub[f].astype(BF16), preferred_element_type=F32)
        hidden = (g * (1.0 / (1.0 + jnp.exp(-g))) * u).astype(BF16)
        contrib = jnp.dot(hidden, wdb[f].astype(BF16), preferred_element_type=F32)

        @pl.when(f == 0)
        def _():
            acc[...] = contrib

        @pl.when((f > 0) & (f < nf - 1))
        def _():
            acc[...] += contrib

        @pl.when(f == nf - 1)
        def _():
            @pl.when(b >= 1)
            def _():
                wait_scatter(1 - slot, rows_of(b - 1))
            total = acc[...] + contrib
            obuf[slot] = _pack_bf16_pair(total[:, :half], total[:, half:])
            start_scatter(slot, rows_of(b))

            @pl.when(b + 1 < nreal)
            def _():
                start_gather(1 - slot, rows_of(b + 1))

            @pl.when(b + 1 == nreal)
            def _():
                wait_scatter(slot, rows_of(b))


def _expert_mlp(x1p, plan, w_gate, w_up, w_down, tm, fc=256):
    tok_blocks, aid_blocks, block_expert, block_rows, n_real = plan
    n, dp = x1p.shape
    d = 2 * dp
    n_blocks = tok_blocks.shape[0]
    d_ff = w_gate.shape[2]
    nf = d_ff // fc
    assert nf >= 2

    def live(b, f, nreal):
        is_real = b < nreal[0]
        return jnp.where(is_real, b, nreal[0] - 1), jnp.where(is_real, f, nf - 1)

    def tok_map(b, f, be, cnt, nreal):
        bb, ff = live(b, f, nreal)
        return jnp.minimum(bb + (ff == nf - 1).astype(jnp.int32), nreal[0] - 1), 0, 0

    def aid_map(b, f, be, cnt, nreal):
        bb, _ = live(b, f, nreal)
        return bb, 0, 0

    grid_spec = pltpu.PrefetchScalarGridSpec(
        num_scalar_prefetch=3,
        grid=(n_blocks, nf),
        in_specs=[
            pl.BlockSpec((None, 1, tm), tok_map, memory_space=pltpu.SMEM),
            pl.BlockSpec((None, 1, tm), aid_map, memory_space=pltpu.SMEM),
            pl.BlockSpec(memory_space=pl.ANY),
            pl.BlockSpec(memory_space=pl.ANY), pl.BlockSpec(memory_space=pl.ANY), pl.BlockSpec(memory_space=pl.ANY),
        ],
        out_specs=pl.BlockSpec(memory_space=pl.ANY),
        scratch_shapes=[pltpu.VMEM((2, tm, dp), jnp.uint32), pltpu.VMEM((tm, d), BF16), pltpu.VMEM((tm, d), F32),
                        pltpu.VMEM((2, tm, dp), jnp.uint32),
                        pltpu.VMEM((nf, d, fc), F32), pltpu.VMEM((nf, d, fc), F32), pltpu.VMEM((nf, fc, d), F32),
                        pltpu.SemaphoreType.DMA((2,)), pltpu.SemaphoreType.DMA((2,)),
                        pltpu.SemaphoreType.DMA((nf,))],
    )
    kernel = functools.partial(_expert_kernel, tm=tm, nf=nf)
    return pl.pallas_call(
        kernel,
        grid_spec=grid_spec,
        out_shape=jax.ShapeDtypeStruct((n * TOP_K + 2 * tm, dp), jnp.uint32),
        compiler_params=_params("arbitrary", "arbitrary"),
        name="expert_mlp",
    )(block_expert, block_rows, n_real, tok_blocks, aid_blocks, x1p, w_gate, w_up, w_down)


def _combine_kernel(x1_ref, y0_ref, y1_ref, wt_ref, lg_ref, lb_ref, o_ref, buf_ref):
    half = x1_ref.shape[1] // 2
    wt = wt_ref[...]
    y0 = _unpack_bf16_pair(y0_ref[...])
    y1 = _unpack_bf16_pair(y1_ref[...])
    for p, cs in enumerate((slice(0, half), slice(half, 2 * half))):
        buf_ref[:, cs] = DN_ALPHA * x1_ref[:, cs] + (wt[:, 0:1] * y0[p] + wt[:, 1:2] * y1[p])
    hsum = buf_ref[...]
    mu = jnp.mean(hsum, axis=-1, keepdims=True)
    hc = hsum - mu
    var = jnp.mean(jnp.square(hc), axis=-1, keepdims=True)
    o_ref[...] = hc * lax.rsqrt(var + LN_EPS) * lg_ref[...] + lb_ref[...]


def _combine_ln(x1, ys, wts, ln_g, ln_b, tm=256):
    m, d = x1.shape
    big = pl.BlockSpec((tm, d), lambda i: (i, 0))
    first = pl.BlockSpec((tm, d // 2), lambda i: (i, 0))
    second = pl.BlockSpec((tm, d // 2), lambda i: (m // tm + i, 0))
    row_vec = pl.BlockSpec((1, d), lambda i: (0, 0))
    return pl.pallas_call(
        _combine_kernel,
        grid=(m // tm,),
        in_specs=[big, first, second, pl.BlockSpec((tm, LANES), lambda i: (i, 0)), row_vec, row_vec],
        out_specs=big,
        out_shape=jax.ShapeDtypeStruct((m, d), F32),
        scratch_shapes=[pltpu.VMEM((tm, d), F32)],
        compiler_params=_params("arbitrary"),
        name="combine_ln",
    )(x1, ys, ys, wts, ln_g, ln_b)


def _plan_blocks(eid, rank, counts, tm):
    n = eid.shape[0]
    a = n * TOP_K
    expert_id = eid.reshape(-1)
    padded = (counts + tm - 1) // tm * tm
    pad_ends = jnp.cumsum(padded)
    pad_starts = pad_ends - padded
    n_blocks = a // tm + N_EXPERTS
    experts = jnp.arange(N_EXPERTS, dtype=jnp.int32)
    start_of = jnp.dot((expert_id[:, None] == experts[None, :]).astype(F32), pad_starts.astype(F32),
                       precision=lax.Precision.HIGHEST).astype(jnp.int32)
    pos = start_of + rank.reshape(-1)
    spare = a + jnp.arange(n_blocks * tm, dtype=jnp.int32) % (2 * tm)
    flat = jnp.arange(a, dtype=jnp.int32)
    aid_pad = spare.at[pos].set((flat % TOP_K) * n + flat // TOP_K, unique_indices=True)
    tok_pad = jnp.where(aid_pad < a, aid_pad % n, 0)
    block_start = jnp.arange(n_blocks, dtype=jnp.int32) * tm
    block_expert = jnp.minimum(jnp.sum(pad_ends[None, :] <= block_start[:, None], axis=1), N_EXPERTS - 1).astype(jnp.int32)
    block_count = jnp.clip(counts[block_expert] - (block_start - pad_starts[block_expert]), 0, tm)
    block_rows = ((block_count + SUBLANES - 1) // SUBLANES * SUBLANES).astype(jnp.int32)
    n_real = (pad_ends[-1] // tm).astype(jnp.int32).reshape(1)
    return tok_pad.reshape(n_blocks, 1, tm), aid_pad.reshape(n_blocks, 1, tm), block_expert, block_rows, n_real


def kernel(x, w_in, w_out, lambda_q1, lambda_k1, lambda_q2, lambda_k2, subln_g, beta_attn, gmlp_ln_g, gmlp_ln_b,
           spatial_w, spatial_b, beta_gmlp, rel_bias, ln1_g, ln1_b, w_group, b_group, w_expert, b_expert,
           w_gate, w_up, w_down, ln2_g, ln2_b):
    b, s, d = x.shape
    n = b * s
    d_attn = ATTN_HEADS * ATTN_V_DIM
    d_gmlp = d - d_attn
    moe_tm = 320
    for l in range(DEPTH):
        lambda_init = 0.8 - 0.6 * math.exp(-0.3 * l)
        x2 = x.reshape(n, d)
        proj = _proj_matmul(x2.astype(BF16), w_in[l])
        attn = _diff_attention(proj.reshape(b, s, -1), rel_bias, lambda_q1[l][None], lambda_k1[l][None],
                               lambda_q2[l][None], lambda_k2[l][None], subln_g[l][None], beta_attn[l][None],
                               lambda_init)
        gating = (gmlp_ln_g[l][None], gmlp_ln_b[l][None], spatial_w[l], spatial_b[l], beta_gmlp[l][None])
        n_route = N_GROUPS + N_EXPERTS
        w_route = jnp.concatenate([w_group[l], w_expert[l].reshape(d, N_EXPERTS),
                                   jnp.zeros((d, LANES - n_route), F32)], axis=1)
        w_route_hi = w_route.astype(BF16)
        w_route = jnp.stack([w_route_hi, (w_route - w_route_hi.astype(F32)).astype(BF16)])
        b_route = jnp.concatenate([b_group[l], b_expert[l].reshape(-1), jnp.zeros((LANES - n_route,), F32)])[None]
        x1, x1p, logits = _gate_mix_ln_router(attn.reshape(n, d_attn), proj, gating, w_out[l].astype(BF16), x2,
                                              ln1_g[l][None], ln1_b[l][None], w_route, b_route, d_gmlp,
                                              u_col=3 * d_attn // d_gmlp, g_col=3 * d_attn // d_gmlp + 1)
        picks, wts, counts = _route(logits)
        plan = _plan_blocks(picks[:, :TOP_K], picks[:, TOP_K:2 * TOP_K], counts[0, :N_EXPERTS].astype(jnp.int32),
                            moe_tm)
        ys = _expert_mlp(x1p, plan, w_gate[l], w_up[l], w_down[l], moe_tm)
        x = _combine_ln(x1, ys, wts, ln2_g[l][None], ln2_b[l][None]).reshape(b, s, d)
    return x
```

```python
import functools
import math

import jax
import jax.numpy as jnp
from jax import lax
from jax.experimental import pallas as pl
from jax.experimental.pallas import tpu as pltpu

F32 = jnp.float32
BF16 = jnp.bfloat16

ATTN_HEADS = 8
ATTN_HEAD_DIM = 128
ATTN_V_DIM = 2 * ATTN_HEAD_DIM
GMLP_HEADS = 8
CHUNK = 128
REL_BUCKETS = 32
REL_MAX_DIST = 128
N_GROUPS = 8
EXPERTS_PER_GROUP = 8
N_EXPERTS = N_GROUPS * EXPERTS_PER_GROUP
TOP_K = 2
LN_EPS = 1e-5
DEPTH = 1
DN_ALPHA = (2 * DEPTH) ** 0.25
NEG_INF = -1e30
LOG2E = math.log2(math.e)
LANES = 128
SUBLANES = 8
VMEM_LIMIT = 56 * 1024 * 1024
VMEM_LIMIT_LARGE = 60 * 1024 * 1024
X1_PACK_GROUP = 256


def _params(*semantics, vmem_limit=VMEM_LIMIT):
    return pltpu.CompilerParams(dimension_semantics=semantics, vmem_limit_bytes=vmem_limit)


def _matmul_kernel(x_ref, w_ref, o_ref, wb_ref):
    @pl.when(pl.program_id(1) == 0)
    def _():
        wb_ref[...] = w_ref[...].astype(BF16)

    o_ref[...] = jnp.dot(x_ref[...], wb_ref[...], preferred_element_type=F32).astype(o_ref.dtype)


def _proj_matmul(xb, w, tm=512, tn=1024):
    m, k = xb.shape
    n = w.shape[1]
    return pl.pallas_call(
        _matmul_kernel,
        grid=(n // tn, m // tm),
        in_specs=[pl.BlockSpec((tm, k), lambda j, i: (i, 0)),
                  pl.BlockSpec((k, tn), lambda j, i: (0, j))],
        out_specs=pl.BlockSpec((tm, tn), lambda j, i: (i, j)),
        out_shape=jax.ShapeDtypeStruct((m, n), BF16),
        scratch_shapes=[pltpu.VMEM((k, tn), BF16)],
        compiler_params=_params("arbitrary", "arbitrary"),
        name="proj_matmul",
    )(xb, w)


def _rel_bucket(n):
    max_exact = REL_BUCKETS // 2
    nf = jnp.maximum(n, max_exact).astype(F32)
    large = max_exact + (jnp.log(nf / max_exact) / math.log(REL_MAX_DIST / max_exact)
                         * (REL_BUCKETS - max_exact)).astype(jnp.int32)
    large = jnp.minimum(large, REL_BUCKETS - 1)
    return jnp.where(n < max_exact, n, large)


def _rel_bias_tiles(rel_bias, blk):
    pos = jnp.arange(blk)
    tiles = []
    for d in (0, 1):
        bucket = _rel_bucket(jnp.maximum(pos[:, None] + d * blk - pos[None, :], 0))
        hit = bucket[None] == jnp.arange(REL_BUCKETS)[:, None, None]
        tiles.append(jnp.sum(jnp.where(hit[:, None], rel_bias[:, :, None, None], 0.0), axis=0))
    return jnp.stack(tiles, axis=1).astype(F32)


def _attn_kernel(far_ref, q_ref, k_ref, v_ref, bias_ref, lq1_ref, lk1_ref, lq2_ref, lk2_ref,
                 sg_ref, ba_ref, o_ref, mx_ref, sh_ref, mrow_ref, l_ref, s_ref, acc_ref, *, blk, hp, lambda_init):
    g = pl.program_id(1)
    i = pl.program_id(2)
    dh = ATTN_HEAD_DIM
    dv = ATTN_V_DIM
    c1 = ATTN_HEAD_DIM ** -0.5 * LOG2E
    streams = [(hh, mi) for hh in range(hp) for mi in range(2)]
    q = q_ref[...]
    qs = [q[:, hh * dv + mi * dh:hh * dv + (mi + 1) * dh] for hh, mi in streams]
    contract_last = (((1,), (1,)), ((), ()))
    far2 = [far_ref[g * hp + hh] * LOG2E for hh in range(hp)]
    n_far = jnp.maximum(i - 1, 0)

    def lane_chunks(x):
        return [x[:, c * LANES:(c + 1) * LANES] for c in range(x.shape[1] // LANES)]

    def raw_scores(start, width):
        kj = k_ref[pl.ds(start, width), :]
        return [lax.dot_general(qs[sid], kj[:, hh * dv + mi * dh:hh * dv + (mi + 1) * dh], contract_last,
                                preferred_element_type=F32) for sid, (hh, mi) in enumerate(streams)]

    def over_far_blocks(body):
        def pair(t, carry):
            body(pl.multiple_of(t * (2 * blk), 2 * blk), 2 * blk)
            return carry
        lax.fori_loop(0, lax.shift_right_logical(n_far, 1), pair, 0)

        @pl.when(lax.rem(n_far, 2) == 1)
        def _():
            body(pl.multiple_of((n_far - 1) * blk, blk), blk)

    mx_ref[...] = jnp.full(mx_ref.shape, -jnp.inf, F32)

    def max_body(start, width):
        s = raw_scores(start, width)
        for sid in range(len(streams)):
            best = mx_ref[sid]
            for chunk in lane_chunks(s[sid]):
                best = jnp.maximum(best, chunk)
            mx_ref[sid] = best

    over_far_blocks(max_body)

    row = lax.broadcasted_iota(jnp.int32, (blk, blk), 0)
    col = lax.broadcasted_iota(jnp.int32, (blk, blk), 1)
    near = raw_scores(pl.multiple_of(n_far * blk, blk), blk)
    diag = raw_scores(pl.multiple_of(i * blk, blk), blk)
    for sid, (hh, mi) in enumerate(streams):
        s_near = jnp.where(i >= 1, near[sid] * c1 + bias_ref[hh, 1], NEG_INF)
        s_diag = jnp.where(col <= row, diag[sid] * c1 + bias_ref[hh, 0], NEG_INF)
        s_ref[sid, 0] = s_near
        s_ref[sid, 1] = s_diag
        best = mx_ref[sid] * c1 + far2[hh]
        for chunk in lane_chunks(s_near) + lane_chunks(s_diag):
            best = jnp.maximum(best, chunk)
        m_rows = jnp.broadcast_to(jnp.max(best, axis=-1, keepdims=True), (blk, LANES))
        mrow_ref[sid] = m_rows
        sh_ref[sid] = far2[hh] - m_rows

    l_ref[...] = jnp.zeros(l_ref.shape, F32)
    acc_ref[...] = jnp.zeros(acc_ref.shape, F32)

    def accumulate(p_chunks, vj):
        ps = []
        for sid in range(len(streams)):
            tot = l_ref[sid]
            for chunk in p_chunks[sid]:
                tot = tot + chunk
            l_ref[sid] = tot
            ps.append(jnp.concatenate(p_chunks[sid], axis=1).astype(BF16))
        for hh in range(hp):
            acc_ref[hh] += jnp.dot(jnp.concatenate(ps[2 * hh:2 * hh + 2], axis=0), vj[:, hh * dv:(hh + 1) * dv],
                                   preferred_element_type=F32)

    def pv_body(start, width):
        s = raw_scores(start, width)
        accumulate([[jnp.exp2(chunk * c1 + sh_ref[sid]) for chunk in lane_chunks(s[sid])]
                    for sid in range(len(streams))], v_ref[pl.ds(start, width), :])

    over_far_blocks(pv_body)

    for t, start in ((0, pl.multiple_of(n_far * blk, blk)), (1, pl.multiple_of(i * blk, blk))):
        accumulate([[jnp.exp2(chunk - mrow_ref[sid]) for chunk in lane_chunks(s_ref[sid, t])]
                    for sid in range(len(streams))], v_ref[pl.ds(start, blk), :])

    lam = (jnp.exp(jnp.sum(lq1_ref[...] * lk1_ref[...], axis=-1, keepdims=True))
           - jnp.exp(jnp.sum(lq2_ref[...] * lk2_ref[...], axis=-1, keepdims=True)) + lambda_init)
    for hh in range(hp):
        l1 = jnp.sum(l_ref[2 * hh], axis=-1, keepdims=True)
        l2 = jnp.sum(l_ref[2 * hh + 1], axis=-1, keepdims=True)
        o = acc_ref[hh, :blk] / l1 - lam * (acc_ref[hh, blk:] / l2)
        o = o * lax.rsqrt(jnp.mean(jnp.square(o), axis=-1, keepdims=True) + LN_EPS) * sg_ref[...]
        o = o * (1.0 - lambda_init) * ba_ref[:, hh * dv:(hh + 1) * dv]
        o_ref[:, hh * dv:(hh + 1) * dv] = o.astype(o_ref.dtype)


def _diff_attention(proj3, rel_bias, lq1, lk1, lq2, lk2, subln_g, beta_attn, lambda_init, blk=256, hp=4):
    b, s, _ = proj3.shape
    h = ATTN_HEADS
    dv = ATTN_V_DIM
    gw = hp * dv
    groups = h // hp
    assert REL_BUCKETS // 2 + int(math.log((blk + 1) / (REL_BUCKETS // 2)) / math.log(REL_MAX_DIST / (REL_BUCKETS // 2))
                                  * (REL_BUCKETS // 2)) >= REL_BUCKETS
    tiles = _rel_bias_tiles(rel_bias, blk) * LOG2E
    far = rel_bias[REL_BUCKETS - 1].astype(F32)
    vec = lambda c: pl.BlockSpec((1, c), lambda bi, gi, qi: (0, 0))
    kernel = functools.partial(_attn_kernel, blk=blk, hp=hp, lambda_init=lambda_init)
    n_streams = 2 * hp
    stat = pltpu.VMEM((n_streams, blk, LANES), F32)
    return pl.pallas_call(
        kernel,
        grid=(b, groups, s // blk),
        in_specs=[
            pl.BlockSpec(memory_space=pltpu.SMEM),
            pl.BlockSpec((None, blk, gw), lambda bi, gi, qi: (bi, qi, gi)),
            pl.BlockSpec((None, s, gw), lambda bi, gi, qi: (bi, 0, groups + gi)),
            pl.BlockSpec((None, s, gw), lambda bi, gi, qi: (bi, 0, 2 * groups + gi)),
            pl.BlockSpec((hp, 2, blk, blk), lambda bi, gi, qi: (gi, 0, 0, 0)),
            vec(ATTN_HEAD_DIM), vec(ATTN_HEAD_DIM), vec(ATTN_HEAD_DIM), vec(ATTN_HEAD_DIM),
            vec(dv),
            pl.BlockSpec((1, gw), lambda bi, gi, qi: (0, gi)),
        ],
        out_specs=pl.BlockSpec((None, blk, gw), lambda bi, gi, qi: (bi, qi, gi)),
        out_shape=jax.ShapeDtypeStruct((b, s, h * dv), BF16),
        scratch_shapes=[stat, stat, stat, stat,
                        pltpu.VMEM((n_streams, 2, blk, blk), F32), pltpu.VMEM((hp, 2 * blk, dv), F32)],
        compiler_params=_params("arbitrary", "arbitrary", "arbitrary"),
        name="diff_attention",
    )(far, proj3, proj3, proj3, tiles, lq1, lk1, lq2, lk2, subln_g, beta_attn)


def _gelu(x):
    return 0.5 * x * (1.0 + lax.erf(x * (2.0 ** -0.5)))


def _gating_chunk(u_ref, g_ref, lg_ref, lb_ref, ws_ref, bs_ref, bg_ref, buf_ref):
    hd = u_ref.shape[1] // GMLP_HEADS
    g = _gelu(g_ref[...].astype(F32))
    mu = jnp.mean(g, axis=-1, keepdims=True)
    gc = g - mu
    var = jnp.mean(jnp.square(gc), axis=-1, keepdims=True)
    v = (gc * lax.rsqrt(var + LN_EPS) * lg_ref[...] + lb_ref[...]).astype(BF16)
    row = lax.broadcasted_iota(jnp.int32, (CHUNK, CHUNK), 0)
    col = lax.broadcasted_iota(jnp.int32, (CHUNK, CHUNK), 1)
    tril = col <= row
    ss = jnp.zeros((CHUNK, 1), F32)
    for hh in range(GMLP_HEADS):
        w = jnp.where(tril, ws_ref[hh], 0.0).astype(BF16)
        mixed = jnp.dot(w, v[:, hh * hd:(hh + 1) * hd], preferred_element_type=F32) + bs_ref[hh]
        out = _gelu(u_ref[:, hh * hd:(hh + 1) * hd].astype(F32)) * mixed
        ss = ss + jnp.sum(jnp.square(out), axis=-1, keepdims=True)
        buf_ref[:, hh * hd:(hh + 1) * hd] = out
    rstd = lax.rsqrt(ss / u_ref.shape[1] + LN_EPS)
    return (buf_ref[...] * rstd * bg_ref[...]).astype(BF16)


def _layer_norm_chunks(buf_ref, n_chunks, d):
    tot = jnp.sum(buf_ref[0], axis=-1, keepdims=True)
    for c in range(1, n_chunks):
        tot = tot + jnp.sum(buf_ref[c], axis=-1, keepdims=True)
    mu = tot / d
    sq = jnp.sum(jnp.square(buf_ref[0] - mu), axis=-1, keepdims=True)
    for c in range(1, n_chunks):
        sq = sq + jnp.sum(jnp.square(buf_ref[c] - mu), axis=-1, keepdims=True)
    return mu, lax.rsqrt(sq / d + LN_EPS)


def _pack_bf16_pair(lo, hi):
    lo_bits = lax.bitcast_convert_type(lo.astype(BF16).astype(F32), jnp.uint32) >> 16
    hi_bits = lax.bitcast_convert_type(hi.astype(BF16).astype(F32), jnp.uint32) & jnp.uint32(0xFFFF0000)
    return hi_bits | lo_bits


def _unpack_bf16_pair(words):
    lo = lax.bitcast_convert_type(words << 16, F32)
    hi = lax.bitcast_convert_type(words & jnp.uint32(0xFFFF0000), F32)
    return lo, hi


def _mix_kernel(a_ref, u_ref, g_ref, glg_ref, glb_ref, ws_ref, bs_ref, bg_ref, w_ref, x_ref, lg_ref, lb_ref,
                wr_ref, br_ref, x1_ref, x1p_ref, lo_ref, gm_ref, gbuf_ref, buf_ref, mu_ref, rstd_ref,
                *, n_tiles, n_chunks, tn):
    p = pl.program_id(0)
    j = pl.program_id(1)
    ka = a_ref.shape[1]

    def gate():
        rows = pl.ds(pl.multiple_of(j * CHUNK, CHUNK), CHUNK)
        gm_ref[lax.rem(p, 2), rows, :] = _gating_chunk(u_ref, g_ref, glg_ref, glb_ref, ws_ref, bs_ref, bg_ref,
                                                       gbuf_ref)

    def normalise():
        y = (buf_ref[j] - mu_ref[...]) * rstd_ref[...] * lg_ref[...] + lb_ref[...]
        x1_ref[...] = y
        gw = X1_PACK_GROUP // 2
        for grp in range(tn // X1_PACK_GROUP):
            x1p_ref[:, grp * gw:(grp + 1) * gw] = _pack_bf16_pair(y[:, 2 * grp * gw:(2 * grp + 1) * gw],
                                                                  y[:, (2 * grp + 1) * gw:(2 * grp + 2) * gw])
        y_hi = y.astype(BF16)
        y_lo = (y - y_hi.astype(F32)).astype(BF16)
        start = jnp.where(j == 0, jnp.broadcast_to(br_ref[...], lo_ref.shape), lo_ref[...])
        by_hi = jnp.dot(y_hi, wr_ref[...], preferred_element_type=F32)
        by_lo = jnp.dot(y_lo, wr_ref[:, :LANES], preferred_element_type=F32)
        lo_ref[...] = start + (by_hi[:, :LANES] + (by_lo + by_hi[:, LANES:]))

    def matmul():
        mix = (jnp.dot(a_ref[...], w_ref[:ka, :], preferred_element_type=F32)
               + jnp.dot(gm_ref[lax.rem(p + 1, 2)], w_ref[ka:, :], preferred_element_type=F32))
        buf_ref[j] = DN_ALPHA * x_ref[...] + mix

    def row_stats():
        @pl.when(j == n_chunks - 1)
        def _():
            mu, rstd = _layer_norm_chunks(buf_ref, n_chunks, n_chunks * tn)
            mu_ref[...] = mu
            rstd_ref[...] = rstd

    @pl.when(p == 0)
    def _():
        gate()

    @pl.when(p == 1)
    def _():
        matmul()
        gate()
        row_stats()

    @pl.when((p >= 2) & (p < n_tiles))
    def _():
        normalise()
        matmul()
        gate()
        row_stats()

    @pl.when(p == n_tiles)
    def _():
        normalise()
        matmul()
        row_stats()

    @pl.when(p == n_tiles + 1)
    def _():
        normalise()


def _gate_mix_ln_router(attn2, proj2, gating, wb, x2, ln_g, ln_b, w_route, b_route, d_gmlp, u_col, g_col,
                        tm=512, tn=1024):
    g_ln_g, g_ln_b, w_s, b_s, beta_gmlp = gating
    m, d = x2.shape
    ka = attn2.shape[1]
    n_chunks = d // tn
    n_tiles = m // tm
    assert tm // CHUNK == n_chunks and n_tiles >= 3 and tn % X1_PACK_GROUP == 0
    last = n_chunks - 1
    gate_tile = lambda p: jnp.minimum(p, n_tiles - 1)
    mat_tile = lambda p: jnp.clip(p - 1, 0, n_tiles - 1)
    out_tile = lambda p: jnp.maximum(p - 2, 0)
    gate_chunk = lambda p, j: jnp.where(p < n_tiles, j, last)
    mat_chunk = lambda p, j: jnp.where(p < 1, 0, jnp.where(p <= n_tiles, j, last))
    out_chunk = lambda p, j: jnp.where(p >= 2, j, 0)
    gate_rows = lambda col: (lambda p, j: (gate_tile(p) * n_chunks + gate_chunk(p, j), col))
    const2 = lambda p, j: (0, 0)
    const3 = lambda p, j: (0, 0, 0)
    kernel = functools.partial(_mix_kernel, n_tiles=n_tiles, n_chunks=n_chunks, tn=tn)
    return pl.pallas_call(
        kernel,
        grid=(n_tiles + 2, n_chunks),
        in_specs=[
            pl.BlockSpec((tm, ka), lambda p, j: (mat_tile(p), 0)),
            pl.BlockSpec((CHUNK, d_gmlp), gate_rows(u_col)),
            pl.BlockSpec((CHUNK, d_gmlp), gate_rows(g_col)),
            pl.BlockSpec((1, d_gmlp), const2), pl.BlockSpec((1, d_gmlp), const2),
            pl.BlockSpec((GMLP_HEADS, CHUNK, CHUNK), const3),
            pl.BlockSpec((GMLP_HEADS, CHUNK, 1), const3),
            pl.BlockSpec((1, d_gmlp), const2),
            pl.BlockSpec((ka + d_gmlp, tn), lambda p, j: (0, mat_chunk(p, j))),
            pl.BlockSpec((tm, tn), lambda p, j: (mat_tile(p), mat_chunk(p, j))),
            pl.BlockSpec((1, tn), lambda p, j: (0, out_chunk(p, j))),
            pl.BlockSpec((1, tn), lambda p, j: (0, out_chunk(p, j))),
            pl.BlockSpec((tn, 2 * LANES), lambda p, j: (out_chunk(p, j), 0)),
            pl.BlockSpec((1, LANES), const2),
        ],
        out_specs=[
            pl.BlockSpec((tm, tn), lambda p, j: (out_tile(p), out_chunk(p, j))),
            pl.BlockSpec((tm, tn // 2), lambda p, j: (out_tile(p), out_chunk(p, j))),
            pl.BlockSpec((tm, LANES), lambda p, j: (out_tile(p), 0)),
        ],
        out_shape=[jax.ShapeDtypeStruct((m, d), F32), jax.ShapeDtypeStruct((m, d // 2), jnp.uint32),
                   jax.ShapeDtypeStruct((m, LANES), F32)],
        scratch_shapes=[pltpu.VMEM((2, tm, d_gmlp), BF16), pltpu.VMEM((CHUNK, d_gmlp), F32),
                        pltpu.VMEM((n_chunks, tm, tn), F32), pltpu.VMEM((tm, 1), F32), pltpu.VMEM((tm, 1), F32)],
        compiler_params=_params("arbitrary", "arbitrary", vmem_limit=VMEM_LIMIT_LARGE),
        name="gate_mix_ln_router",
    )(attn2, proj2, proj2, g_ln_g, g_ln_b, w_s, b_s[:, :, None], beta_gmlp, wb, x2, ln_g, ln_b, w_route, b_route)


def _route_kernel(lo_ref, eid_ref, wt_ref, cnt_ref):
    lg = lo_ref[...]
    lane = lax.broadcasted_iota(jnp.int32, lg.shape, 1)
    lane_f = lane.astype(F32)
    none = float(LANES)
    first = lambda hit: jnp.min(jnp.where(hit, lane_f, none), axis=-1, keepdims=True)

    in_groups = lane < N_GROUPS
    g_logits = jnp.where(in_groups, lg, -jnp.inf)
    g_max = jnp.max(g_logits, axis=-1, keepdims=True)
    g_idx = first(g_logits == g_max)
    g_w = 1.0 / jnp.sum(jnp.where(in_groups, jnp.exp(lg - g_max), 0.0), axis=-1, keepdims=True)

    lo = N_GROUPS + g_idx * EXPERTS_PER_GROUP
    in_group = (lane_f >= lo) & (lane_f < lo + EXPERTS_PER_GROUP)
    e_logits = jnp.where(in_group, lg, -jnp.inf)
    t1 = jnp.max(e_logits, axis=-1, keepdims=True)
    i1 = first(e_logits == t1)
    e_rest = jnp.where(lane_f == i1, -jnp.inf, e_logits)
    t2 = jnp.max(e_rest, axis=-1, keepdims=True)
    i2 = first(e_rest == t2)
    ex = jnp.exp(t2 - t1)
    w1 = g_w * (1.0 / (1.0 + ex))
    w2 = g_w * (ex / (1.0 + ex))
    wt_ref[...] = jnp.where(lane == 0, w1, jnp.where(lane == 1, w2, 0.0))

    @pl.when(pl.program_id(0) == 0)
    def _():
        cnt_ref[...] = jnp.zeros(cnt_ref.shape, F32)

    tm = lg.shape[0]
    pick1 = lane_f == i1 - N_GROUPS
    pick2 = lane_f == i2 - N_GROUPS
    chosen = pick1.astype(F32) + pick2.astype(F32)
    earlier = (lax.broadcasted_iota(jnp.int32, (tm, tm), 1) < lax.broadcasted_iota(jnp.int32, (tm, tm), 0))
    before = jnp.dot(earlier.astype(BF16), chosen.astype(BF16), preferred_element_type=F32) + cnt_ref[...]
    r1 = jnp.sum(jnp.where(pick1, before, 0.0), axis=-1, keepdims=True)
    r2 = jnp.sum(jnp.where(pick2, before, 0.0), axis=-1, keepdims=True)
    cnt_ref[...] += jnp.sum(chosen, axis=0, keepdims=True)
    picks = (jnp.where(lane == 0, i1 - N_GROUPS, 0.0) + jnp.where(lane == 1, i2 - N_GROUPS, 0.0)
             + jnp.where(lane == 2, r1, 0.0) + jnp.where(lane == 3, r2, 0.0))
    eid_ref[...] = picks.astype(jnp.int32)


def _route(logits, tm=1024):
    m = logits.shape[0]
    spec = pl.BlockSpec((tm, LANES), lambda i: (i, 0))
    return pl.pallas_call(
        _route_kernel,
        grid=(m // tm,),
        in_specs=[spec],
        out_specs=[spec, spec, pl.BlockSpec((1, LANES), lambda i: (0, 0))],
        out_shape=[jax.ShapeDtypeStruct((m, LANES), jnp.int32), jax.ShapeDtypeStruct((m, LANES), F32),
                   jax.ShapeDtypeStruct((1, LANES), F32)],
        compiler_params=_params("arbitrary"),
        name="route",
    )(logits)


def _expert_kernel(be_ref, cnt_ref, nreal_ref, tok_ref, aid_ref, x_hbm, wg_ref, wu_ref, wd_ref, ys_hbm,
                   xbuf, xb, acc, obuf, gsem, ssem, *, tm, nf):
    half = xb.shape[1] // 2
    b = pl.program_id(0)
    f = pl.program_id(1)
    nreal = nreal_ref[0]
    slot = lax.rem(b, 2)

    def rows_of(blk):
        return pl.multiple_of(cnt_ref[blk], SUBLANES)

    def for_each_row(n_rows, copy_of_row):
        def body(grp, carry):
            base = pl.multiple_of(grp * SUBLANES, SUBLANES)
            for k in range(SUBLANES):
                copy_of_row(base + k).start(priority=k % 2)
            return carry
        lax.fori_loop(0, lax.shift_right_logical(n_rows, SUBLANES.bit_length() - 1), body, 0)

    def start_gather(slot_, n_rows):
        for_each_row(n_rows, lambda r: pltpu.make_async_copy(
            x_hbm.at[pl.ds(tok_ref[0, r], 1)], xbuf.at[slot_, pl.ds(r, 1)], gsem.at[slot_]))

    def wait_gather(slot_, n_rows):
        pltpu.make_async_copy(x_hbm.at[pl.ds(0, n_rows)], xbuf.at[slot_, pl.ds(0, n_rows)], gsem.at[slot_]).wait()

    def start_scatter(slot_, n_rows):
        for_each_row(n_rows, lambda r: pltpu.make_async_copy(
            obuf.at[slot_, pl.ds(r, 1)], ys_hbm.at[pl.ds(aid_ref[0, r], 1)], ssem.at[slot_]))

    def wait_scatter(slot_, n_rows):
        pltpu.make_async_copy(obuf.at[slot_, pl.ds(0, n_rows)], ys_hbm.at[pl.ds(0, n_rows)], ssem.at[slot_]).wait()

    @pl.when(b < nreal)
    def _():
        @pl.when(f == 0)
        def _():
            @pl.when(b == 0)
            def _():
                xbuf[...] = jnp.zeros(xbuf.shape, xbuf.dtype)
                start_gather(0, rows_of(0))
            wait_gather(slot, rows_of(b))
            lo, hi = _unpack_bf16_pair(xbuf[slot])
            gw = X1_PACK_GROUP // 2
            for grp in range(half // gw):
                xb[:, 2 * grp * gw:(2 * grp + 1) * gw] = lo[:, grp * gw:(grp + 1) * gw].astype(BF16)
                xb[:, (2 * grp + 1) * gw:(2 * grp + 2) * gw] = hi[:, grp * gw:(grp + 1) * gw].astype(BF16)

        x = xb[...]
        g = jnp.dot(x, wg_ref[...].astype(BF16), preferred_element_type=F32)
        u = jnp.dot(x, wu_ref[...].astype(BF16), preferred_element_type=F32)
        hidden = (g * (1.0 / (1.0 + jnp.exp(-g))) * u).astype(BF16)
        contrib = jnp.dot(hidden, wd_ref[...].astype(BF16), preferred_element_type=F32)

        @pl.when(f == 0)
        def _():
            acc[...] = contrib

        @pl.when((f > 0) & (f < nf - 1))
        def _():
            acc[...] += contrib

        @pl.when(f == nf - 1)
        def _():
            @pl.when(b >= 1)
            def _():
                wait_scatter(1 - slot, rows_of(b - 1))
            total = acc[...] + contrib
            obuf[slot] = _pack_bf16_pair(total[:, :half], total[:, half:])
            start_scatter(slot, rows_of(b))

            @pl.when(b + 1 < nreal)
            def _():
                start_gather(1 - slot, rows_of(b + 1))

            @pl.when(b + 1 == nreal)
            def _():
                wait_scatter(slot, rows_of(b))


def _expert_mlp(x1p, plan, w_gate, w_up, w_down, tm, fc=256):
    tok_blocks, aid_blocks, block_expert, block_rows, n_real = plan
    n, dp = x1p.shape
    d = 2 * dp
    n_blocks = tok_blocks.shape[0]
    d_ff = w_gate.shape[2]
    nf = d_ff // fc
    assert nf >= 2

    def live(b, f, nreal):
        is_real = b < nreal[0]
        return jnp.where(is_real, b, nreal[0] - 1), jnp.where(is_real, f, nf - 1)

    def tok_map(b, f, be, cnt, nreal):
        bb, ff = live(b, f, nreal)
        return jnp.minimum(bb + (ff == nf - 1).astype(jnp.int32), nreal[0] - 1), 0, 0

    def aid_map(b, f, be, cnt, nreal):
        bb, _ = live(b, f, nreal)
        return bb, 0, 0

    def w_in_map(b, f, be, cnt, nreal):
        bb, ff = live(b, f, nreal)
        return be[bb], 0, ff

    def w_out_map(b, f, be, cnt, nreal):
        bb, ff = live(b, f, nreal)
        return be[bb], ff, 0

    grid_spec = pltpu.PrefetchScalarGridSpec(
        num_scalar_prefetch=3,
        grid=(n_blocks, nf),
        in_specs=[
            pl.BlockSpec((None, 1, tm), tok_map, memory_space=pltpu.SMEM),
            pl.BlockSpec((None, 1, tm), aid_map, memory_space=pltpu.SMEM),
            pl.BlockSpec(memory_space=pl.ANY),
            pl.BlockSpec((None, d, fc), w_in_map),
            pl.BlockSpec((None, d, fc), w_in_map),
            pl.BlockSpec((None, fc, d), w_out_map),
        ],
        out_specs=pl.BlockSpec(memory_space=pl.ANY),
        scratch_shapes=[pltpu.VMEM((2, tm, dp), jnp.uint32), pltpu.VMEM((tm, d), BF16), pltpu.VMEM((tm, d), F32),
                        pltpu.VMEM((2, tm, dp), jnp.uint32),
                        pltpu.SemaphoreType.DMA((2,)), pltpu.SemaphoreType.DMA((2,))],
    )
    kernel = functools.partial(_expert_kernel, tm=tm, nf=nf)
    return pl.pallas_call(
        kernel,
        grid_spec=grid_spec,
        out_shape=jax.ShapeDtypeStruct((n * TOP_K + 2 * tm, dp), jnp.uint32),
        compiler_params=_params("arbitrary", "arbitrary"),
        name="expert_mlp",
    )(block_expert, block_rows, n_real, tok_blocks, aid_blocks, x1p, w_gate, w_up, w_down)


def _combine_kernel(x1_ref, y0_ref, y1_ref, wt_ref, lg_ref, lb_ref, o_ref, buf_ref):
    half = x1_ref.shape[1] // 2
    wt = wt_ref[...]
    y0 = _unpack_bf16_pair(y0_ref[...])
    y1 = _unpack_bf16_pair(y1_ref[...])
    for p, cs in enumerate((slice(0, half), slice(half, 2 * half))):
        buf_ref[:, cs] = DN_ALPHA * x1_ref[:, cs] + (wt[:, 0:1] * y0[p] + wt[:, 1:2] * y1[p])
    hsum = buf_ref[...]
    mu = jnp.mean(hsum, axis=-1, keepdims=True)
    hc = hsum - mu
    var = jnp.mean(jnp.square(hc), axis=-1, keepdims=True)
    o_ref[...] = hc * lax.rsqrt(var + LN_EPS) * lg_ref[...] + lb_ref[...]


def _combine_ln(x1, ys, wts, ln_g, ln_b, tm=256):
    m, d = x1.shape
    big = pl.BlockSpec((tm, d), lambda i: (i, 0))
    first = pl.BlockSpec((tm, d // 2), lambda i: (i, 0))
    second = pl.BlockSpec((tm, d // 2), lambda i: (m // tm + i, 0))
    row_vec = pl.BlockSpec((1, d), lambda i: (0, 0))
    return pl.pallas_call(
        _combine_kernel,
        grid=(m // tm,),
        in_specs=[big, first, second, pl.BlockSpec((tm, LANES), lambda i: (i, 0)), row_vec, row_vec],
        out_specs=big,
        out_shape=jax.ShapeDtypeStruct((m, d), F32),
        scratch_shapes=[pltpu.VMEM((tm, d), F32)],
        compiler_params=_params("arbitrary"),
        name="combine_ln",
    )(x1, ys, ys, wts, ln_g, ln_b)


def _plan_blocks(eid, rank, counts, tm):
    n = eid.shape[0]
    a = n * TOP_K
    expert_id = eid.reshape(-1)
    padded = (counts + tm - 1) // tm * tm
    pad_ends = jnp.cumsum(padded)
    pad_starts = pad_ends - padded
    n_blocks = a // tm + N_EXPERTS
    experts = jnp.arange(N_EXPERTS, dtype=jnp.int32)
    start_of = jnp.dot((expert_id[:, None] == experts[None, :]).astype(F32), pad_starts.astype(F32),
                       precision=lax.Precision.HIGHEST).astype(jnp.int32)
    pos = start_of + rank.reshape(-1)
    spare = a + jnp.arange(n_blocks * tm, dtype=jnp.int32) % (2 * tm)
    flat = jnp.arange(a, dtype=jnp.int32)
    aid_pad = spare.at[pos].set((flat % TOP_K) * n + flat // TOP_K, unique_indices=True)
    tok_pad = jnp.where(aid_pad < a, aid_pad % n, 0)
    block_start = jnp.arange(n_blocks, dtype=jnp.int32) * tm
    block_expert = jnp.minimum(jnp.sum(pad_ends[None, :] <= block_start[:, None], axis=1), N_EXPERTS - 1).astype(jnp.int32)
    block_count = jnp.clip(counts[block_expert] - (block_start - pad_starts[block_expert]), 0, tm)
    block_rows = ((block_count + SUBLANES - 1) // SUBLANES * SUBLANES).astype(jnp.int32)
    n_real = (pad_ends[-1] // tm).astype(jnp.int32).reshape(1)
    return tok_pad.reshape(n_blocks, 1, tm), aid_pad.reshape(n_blocks, 1, tm), block_expert, block_rows, n_real


def kernel(x, w_in, w_out, lambda_q1, lambda_k1, lambda_q2, lambda_k2, subln_g, beta_attn, gmlp_ln_g, gmlp_ln_b,
           spatial_w, spatial_b, beta_gmlp, rel_bias, ln1_g, ln1_b, w_group, b_group, w_expert, b_expert,
           w_gate, w_up, w_down, ln2_g, ln2_b):
    b, s, d = x.shape
    n = b * s
    d_attn = ATTN_HEADS * ATTN_V_DIM
    d_gmlp = d - d_attn
    moe_tm = 320
    for l in range(DEPTH):
        lambda_init = 0.8 - 0.6 * math.exp(-0.3 * l)
        x2 = x.reshape(n, d)
        proj = _proj_matmul(x2.astype(BF16), w_in[l])
        attn = _diff_attention(proj.reshape(b, s, -1), rel_bias, lambda_q1[l][None], lambda_k1[l][None],
                               lambda_q2[l][None], lambda_k2[l][None], subln_g[l][None], beta_attn[l][None],
                               lambda_init)
        gating = (gmlp_ln_g[l][None], gmlp_ln_b[l][None], spatial_w[l], spatial_b[l], beta_gmlp[l][None])
        n_route = N_GROUPS + N_EXPERTS
        w_route = jnp.concatenate([w_group[l], w_expert[l].reshape(d, N_EXPERTS),
                                   jnp.zeros((d, LANES - n_route), F32)], axis=1)
        w_route_hi = w_route.astype(BF16)
        w_route = jnp.concatenate([w_route_hi, (w_route - w_route_hi.astype(F32)).astype(BF16)], axis=1)
        b_route = jnp.concatenate([b_group[l], b_expert[l].reshape(-1), jnp.zeros((LANES - n_route,), F32)])[None]
        x1, x1p, logits = _gate_mix_ln_router(attn.reshape(n, d_attn), proj, gating, w_out[l].astype(BF16), x2,
                                              ln1_g[l][None], ln1_b[l][None], w_route, b_route, d_gmlp,
                                              u_col=3 * d_attn // d_gmlp, g_col=3 * d_attn // d_gmlp + 1)
        picks, wts, counts = _route(logits)
        plan = _plan_blocks(picks[:, :TOP_K], picks[:, TOP_K:2 * TOP_K], counts[0, :N_EXPERTS].astype(jnp.int32),
                            moe_tm)
        ys = _expert_mlp(x1p, plan, w_gate[l], w_up[l], w_down[l], moe_tm)
        x = _combine_ln(x1, ys, wts, ln2_g[l][None], ln2_b[l][None]).reshape(b, s, d)
    return x
```

```python
import functools
import math

import jax
import jax.numpy as jnp
from jax import lax
from jax.experimental import pallas as pl
from jax.experimental.pallas import tpu as pltpu

F32 = jnp.float32
BF16 = jnp.bfloat16

ATTN_HEADS = 8
ATTN_HEAD_DIM = 128
ATTN_V_DIM = 2 * ATTN_HEAD_DIM
GMLP_HEADS = 8
CHUNK = 128
REL_BUCKETS = 32
REL_MAX_DIST = 128
N_GROUPS = 8
EXPERTS_PER_GROUP = 8
N_EXPERTS = N_GROUPS * EXPERTS_PER_GROUP
TOP_K = 2
LN_EPS = 1e-5
DEPTH = 1
DN_ALPHA = (2 * DEPTH) ** 0.25
NEG_INF = -1e30
LOG2E = math.log2(math.e)
LANES = 128
SUBLANES = 8
VMEM_LIMIT = 56 * 1024 * 1024
VMEM_LIMIT_LARGE = 60 * 1024 * 1024
X1_PACK_GROUP = 256


def _params(*semantics, vmem_limit=VMEM_LIMIT):
    return pltpu.CompilerParams(dimension_semantics=semantics, vmem_limit_bytes=vmem_limit)


def _matmul_kernel(x_ref, w_ref, o_ref, wb_ref):
    @pl.when(pl.program_id(1) == 0)
    def _():
        wb_ref[...] = w_ref[...].astype(BF16)

    o_ref[...] = jnp.dot(x_ref[...], wb_ref[...], preferred_element_type=F32).astype(o_ref.dtype)


def _proj_matmul(xb, w, tm=512, tn=1024):
    m, k = xb.shape
    n = w.shape[1]
    return pl.pallas_call(
        _matmul_kernel,
        grid=(n // tn, m // tm),
        in_specs=[pl.BlockSpec((tm, k), lambda j, i: (i, 0)),
                  pl.BlockSpec((k, tn), lambda j, i: (0, j))],
        out_specs=pl.BlockSpec((tm, tn), lambda j, i: (i, j)),
        out_shape=jax.ShapeDtypeStruct((m, n), BF16),
        scratch_shapes=[pltpu.VMEM((k, tn), BF16)],
        compiler_params=_params("arbitrary", "arbitrary"),
        name="proj_matmul",
    )(xb, w)


def _rel_bucket(n):
    max_exact = REL_BUCKETS // 2
    nf = jnp.maximum(n, max_exact).astype(F32)
    large = max_exact + (jnp.log(nf / max_exact) / math.log(REL_MAX_DIST / max_exact)
                         * (REL_BUCKETS - max_exact)).astype(jnp.int32)
    large = jnp.minimum(large, REL_BUCKETS - 1)
    return jnp.where(n < max_exact, n, large)


def _rel_bias_tiles(rel_bias, blk):
    pos = jnp.arange(blk)
    tiles = []
    for d in (0, 1):
        bucket = _rel_bucket(jnp.maximum(pos[:, None] + d * blk - pos[None, :], 0))
        hit = bucket[None] == jnp.arange(REL_BUCKETS)[:, None, None]
        tiles.append(jnp.sum(jnp.where(hit[:, None], rel_bias[:, :, None, None], 0.0), axis=0))
    return jnp.stack(tiles, axis=1).astype(F32)


def _attn_kernel(far_ref, q_ref, k_ref, v_ref, bias_ref, lq1_ref, lk1_ref, lq2_ref, lk2_ref,
                 sg_ref, ba_ref, o_ref, mx_ref, sh_ref, mrow_ref, l_ref, s_ref, acc_ref, *, blk, hp, lambda_init):
    g = pl.program_id(1)
    i = pl.program_id(2)
    dh = ATTN_HEAD_DIM
    dv = ATTN_V_DIM
    c1 = ATTN_HEAD_DIM ** -0.5 * LOG2E
    streams = [(hh, mi) for hh in range(hp) for mi in range(2)]
    q = q_ref[...]
    qs = [q[:, hh * dv + mi * dh:hh * dv + (mi + 1) * dh] for hh, mi in streams]
    contract_last = (((1,), (1,)), ((), ()))
    far2 = [far_ref[g * hp + hh] * LOG2E for hh in range(hp)]
    n_far = jnp.maximum(i - 1, 0)

    def lane_chunks(x):
        return [x[:, c * LANES:(c + 1) * LANES] for c in range(x.shape[1] // LANES)]

    def raw_scores(start, width):
        kj = k_ref[pl.ds(start, width), :]
        return [lax.dot_general(qs[sid], kj[:, hh * dv + mi * dh:hh * dv + (mi + 1) * dh], contract_last,
                                preferred_element_type=F32) for sid, (hh, mi) in enumerate(streams)]

    def over_far_blocks(body):
        def pair(t, carry):
            body(pl.multiple_of(t * (2 * blk), 2 * blk), 2 * blk)
            return carry
        lax.fori_loop(0, lax.shift_right_logical(n_far, 1), pair, 0)

        @pl.when(lax.rem(n_far, 2) == 1)
        def _():
            body(pl.multiple_of((n_far - 1) * blk, blk), blk)

    mx_ref[...] = jnp.full(mx_ref.shape, -jnp.inf, F32)

    def max_body(start, width):
        s = raw_scores(start, width)
        for sid in range(len(streams)):
            best = mx_ref[sid]
            for chunk in lane_chunks(s[sid]):
                best = jnp.maximum(best, chunk)
            mx_ref[sid] = best

    over_far_blocks(max_body)

    row = lax.broadcasted_iota(jnp.int32, (blk, blk), 0)
    col = lax.broadcasted_iota(jnp.int32, (blk, blk), 1)
    near = raw_scores(pl.multiple_of(n_far * blk, blk), blk)
    diag = raw_scores(pl.multiple_of(i * blk, blk), blk)
    for sid, (hh, mi) in enumerate(streams):
        s_near = jnp.where(i >= 1, near[sid] * c1 + bias_ref[hh, 1], NEG_INF)
        s_diag = jnp.where(col <= row, diag[sid] * c1 + bias_ref[hh, 0], NEG_INF)
        s_ref[sid, 0] = s_near
        s_ref[sid, 1] = s_diag
        best = mx_ref[sid] * c1 + far2[hh]
        for chunk in lane_chunks(s_near) + lane_chunks(s_diag):
            best = jnp.maximum(best, chunk)
        m_rows = jnp.broadcast_to(jnp.max(best, axis=-1, keepdims=True), (blk, LANES))
        mrow_ref[sid] = m_rows
        sh_ref[sid] = far2[hh] - m_rows

    l_ref[...] = jnp.zeros(l_ref.shape, F32)
    acc_ref[...] = jnp.zeros(acc_ref.shape, F32)

    def accumulate(p_chunks, vj):
        ps = []
        for sid in range(len(streams)):
            tot = l_ref[sid]
            for chunk in p_chunks[sid]:
                tot = tot + chunk
            l_ref[sid] = tot
            ps.append(jnp.concatenate(p_chunks[sid], axis=1).astype(BF16))
        for hh in range(hp):
            acc_ref[hh] += jnp.dot(jnp.concatenate(ps[2 * hh:2 * hh + 2], axis=0), vj[:, hh * dv:(hh + 1) * dv],
                                   preferred_element_type=F32)

    def pv_body(start, width):
        s = raw_scores(start, width)
        accumulate([[jnp.exp2(chunk * c1 + sh_ref[sid]) for chunk in lane_chunks(s[sid])]
                    for sid in range(len(streams))], v_ref[pl.ds(start, width), :])

    over_far_blocks(pv_body)

    for t, start in ((0, pl.multiple_of(n_far * blk, blk)), (1, pl.multiple_of(i * blk, blk))):
        accumulate([[jnp.exp2(chunk - mrow_ref[sid]) for chunk in lane_chunks(s_ref[sid, t])]
                    for sid in range(len(streams))], v_ref[pl.ds(start, blk), :])

    lam = (jnp.exp(jnp.sum(lq1_ref[...] * lk1_ref[...], axis=-1, keepdims=True))
           - jnp.exp(jnp.sum(lq2_ref[...] * lk2_ref[...], axis=-1, keepdims=True)) + lambda_init)
    for hh in range(hp):
        l1 = jnp.sum(l_ref[2 * hh], axis=-1, keepdims=True)
        l2 = jnp.sum(l_ref[2 * hh + 1], axis=-1, keepdims=True)
        o = acc_ref[hh, :blk] / l1 - lam * (acc_ref[hh, blk:] / l2)
        o = o * lax.rsqrt(jnp.mean(jnp.square(o), axis=-1, keepdims=True) + LN_EPS) * sg_ref[...]
        o = o * (1.0 - lambda_init) * ba_ref[:, hh * dv:(hh + 1) * dv]
        o_ref[:, hh * dv:(hh + 1) * dv] = o.astype(o_ref.dtype)


def _diff_attention(proj3, rel_bias, lq1, lk1, lq2, lk2, subln_g, beta_attn, lambda_init, blk=256, hp=4):
    b, s, _ = proj3.shape
    h = ATTN_HEADS
    dv = ATTN_V_DIM
    gw = hp * dv
    groups = h // hp
    assert REL_BUCKETS // 2 + int(math.log((blk + 1) / (REL_BUCKETS // 2)) / math.log(REL_MAX_DIST / (REL_BUCKETS // 2))
                                  * (REL_BUCKETS // 2)) >= REL_BUCKETS
    tiles = _rel_bias_tiles(rel_bias, blk) * LOG2E
    far = rel_bias[REL_BUCKETS - 1].astype(F32)
    vec = lambda c: pl.BlockSpec((1, c), lambda bi, gi, qi: (0, 0))
    kernel = functools.partial(_attn_kernel, blk=blk, hp=hp, lambda_init=lambda_init)
    n_streams = 2 * hp
    stat = pltpu.VMEM((n_streams, blk, LANES), F32)
    return pl.pallas_call(
        kernel,
        grid=(b, groups, s // blk),
        in_specs=[
            pl.BlockSpec(memory_space=pltpu.SMEM),
            pl.BlockSpec((None, blk, gw), lambda bi, gi, qi: (bi, qi, gi)),
            pl.BlockSpec((None, s, gw), lambda bi, gi, qi: (bi, 0, groups + gi)),
            pl.BlockSpec((None, s, gw), lambda bi, gi, qi: (bi, 0, 2 * groups + gi)),
            pl.BlockSpec((hp, 2, blk, blk), lambda bi, gi, qi: (gi, 0, 0, 0)),
            vec(ATTN_HEAD_DIM), vec(ATTN_HEAD_DIM), vec(ATTN_HEAD_DIM), vec(ATTN_HEAD_DIM),
            vec(dv),
            pl.BlockSpec((1, gw), lambda bi, gi, qi: (0, gi)),
        ],
        out_specs=pl.BlockSpec((None, blk, gw), lambda bi, gi, qi: (bi, qi, gi)),
        out_shape=jax.ShapeDtypeStruct((b, s, h * dv), BF16),
        scratch_shapes=[stat, stat, stat, stat,
                        pltpu.VMEM((n_streams, 2, blk, blk), F32), pltpu.VMEM((hp, 2 * blk, dv), F32)],
        compiler_params=_params("arbitrary", "arbitrary", "arbitrary"),
        name="diff_attention",
    )(far, proj3, proj3, proj3, tiles, lq1, lk1, lq2, lk2, subln_g, beta_attn)


def _gelu(x):
    return 0.5 * x * (1.0 + lax.erf(x * (2.0 ** -0.5)))


def _gating_chunk(u_ref, g_ref, lg_ref, lb_ref, ws_ref, bs_ref, bg_ref, buf_ref):
    hd = u_ref.shape[1] // GMLP_HEADS
    g = _gelu(g_ref[...].astype(F32))
    mu = jnp.mean(g, axis=-1, keepdims=True)
    gc = g - mu
    var = jnp.mean(jnp.square(gc), axis=-1, keepdims=True)
    v = (gc * lax.rsqrt(var + LN_EPS) * lg_ref[...] + lb_ref[...]).astype(BF16)
    row = lax.broadcasted_iota(jnp.int32, (CHUNK, CHUNK), 0)
    col = lax.broadcasted_iota(jnp.int32, (CHUNK, CHUNK), 1)
    tril = col <= row
    ss = jnp.zeros((CHUNK, 1), F32)
    for hh in range(GMLP_HEADS):
        w = jnp.where(tril, ws_ref[hh], 0.0).astype(BF16)
        mixed = jnp.dot(w, v[:, hh * hd:(hh + 1) * hd], preferred_element_type=F32) + bs_ref[hh]
        out = _gelu(u_ref[:, hh * hd:(hh + 1) * hd].astype(F32)) * mixed
        ss = ss + jnp.sum(jnp.square(out), axis=-1, keepdims=True)
        buf_ref[:, hh * hd:(hh + 1) * hd] = out
    rstd = lax.rsqrt(ss / u_ref.shape[1] + LN_EPS)
    return (buf_ref[...] * rstd * bg_ref[...]).astype(BF16)


def _pack_bf16_pair(lo, hi):
    lo_bits = lax.bitcast_convert_type(lo.astype(BF16).astype(F32), jnp.uint32) >> 16
    hi_bits = lax.bitcast_convert_type(hi.astype(BF16).astype(F32), jnp.uint32) & jnp.uint32(0xFFFF0000)
    return hi_bits | lo_bits


def _unpack_bf16_pair(words):
    lo = lax.bitcast_convert_type(words << 16, F32)
    hi = lax.bitcast_convert_type(words & jnp.uint32(0xFFFF0000), F32)
    return lo, hi


def _mix_kernel(a_ref, u_ref, g_ref, glg_ref, glb_ref, ws_ref, bs_ref, bg_ref, w_ref, x_ref, lg_ref, lb_ref,
                wr_ref, br_ref, x1_ref, x1p_ref, lo_ref, gm_ref, gbuf_ref, buf_ref, sum_ref, mu_ref, rstd_ref,
                *, n_tiles, n_chunks, tn):
    p = pl.program_id(0)
    j = pl.program_id(1)
    ka = a_ref.shape[1]

    def gate():
        rows = pl.ds(pl.multiple_of(j * CHUNK, CHUNK), CHUNK)
        gm_ref[lax.rem(p, 2), rows, :] = _gating_chunk(u_ref, g_ref, glg_ref, glb_ref, ws_ref, bs_ref, bg_ref,
                                                       gbuf_ref)

    def normalise():
        y = (buf_ref[j] - mu_ref[...]) * rstd_ref[...] * lg_ref[...] + lb_ref[...]
        x1_ref[...] = y
        gw = X1_PACK_GROUP // 2
        for grp in range(tn // X1_PACK_GROUP):
            x1p_ref[:, grp * gw:(grp + 1) * gw] = _pack_bf16_pair(y[:, 2 * grp * gw:(2 * grp + 1) * gw],
                                                                  y[:, (2 * grp + 1) * gw:(2 * grp + 2) * gw])
        y_hi = y.astype(BF16)
        y_lo = (y - y_hi.astype(F32)).astype(BF16)
        start = jnp.where(j == 0, jnp.broadcast_to(br_ref[...], lo_ref.shape), lo_ref[...])
        by_hi = jnp.dot(y_hi, wr_ref[...], preferred_element_type=F32)
        by_lo = jnp.dot(y_lo, wr_ref[:, :LANES], preferred_element_type=F32)
        lo_ref[...] = start + (by_hi[:, :LANES] + (by_lo + by_hi[:, LANES:]))

    def matmul():
        mix = (jnp.dot(a_ref[...], w_ref[:ka, :], preferred_element_type=F32)
               + jnp.dot(gm_ref[lax.rem(p + 1, 2)], w_ref[ka:, :], preferred_element_type=F32))
        h = DN_ALPHA * x_ref[...] + mix
        buf_ref[j] = h
        part = jnp.sum(h, axis=-1, keepdims=True)
        sum_ref[...] = jnp.where(j == 0, part, sum_ref[...] + part)

    def row_stats():
        @pl.when(j == n_chunks - 1)
        def _():
            d = n_chunks * tn
            mu = sum_ref[...] / d
            sq = jnp.sum(jnp.square(buf_ref[0] - mu), axis=-1, keepdims=True)
            for c in range(1, n_chunks):
                sq = sq + jnp.sum(jnp.square(buf_ref[c] - mu), axis=-1, keepdims=True)
            mu_ref[...] = mu
            rstd_ref[...] = lax.rsqrt(sq / d + LN_EPS)

    @pl.when(p == 0)
    def _():
        gate()

    @pl.when(p == 1)
    def _():
        matmul()
        gate()
        row_stats()

    @pl.when((p >= 2) & (p < n_tiles))
    def _():
        normalise()
        matmul()
        gate()
        row_stats()

    @pl.when(p == n_tiles)
    def _():
        normalise()
        matmul()
        row_stats()

    @pl.when(p == n_tiles + 1)
    def _():
        normalise()


def _gate_mix_ln_router(attn2, proj2, gating, wb, x2, ln_g, ln_b, w_route, b_route, d_gmlp, u_col, g_col,
                        tm=512, tn=1024):
    g_ln_g, g_ln_b, w_s, b_s, beta_gmlp = gating
    m, d = x2.shape
    ka = attn2.shape[1]
    n_chunks = d // tn
    n_tiles = m // tm
    assert tm // CHUNK == n_chunks and n_tiles >= 3 and tn % X1_PACK_GROUP == 0
    last = n_chunks - 1
    gate_tile = lambda p: jnp.minimum(p, n_tiles - 1)
    mat_tile = lambda p: jnp.clip(p - 1, 0, n_tiles - 1)
    out_tile = lambda p: jnp.maximum(p - 2, 0)
    gate_chunk = lambda p, j: jnp.where(p < n_tiles, j, last)
    mat_chunk = lambda p, j: jnp.where(p < 1, 0, jnp.where(p <= n_tiles, j, last))
    out_chunk = lambda p, j: jnp.where(p >= 2, j, 0)
    gate_rows = lambda col: (lambda p, j: (gate_tile(p) * n_chunks + gate_chunk(p, j), col))
    const2 = lambda p, j: (0, 0)
    const3 = lambda p, j: (0, 0, 0)
    kernel = functools.partial(_mix_kernel, n_tiles=n_tiles, n_chunks=n_chunks, tn=tn)
    return pl.pallas_call(
        kernel,
        grid=(n_tiles + 2, n_chunks),
        in_specs=[
            pl.BlockSpec((tm, ka), lambda p, j: (mat_tile(p), 0)),
            pl.BlockSpec((CHUNK, d_gmlp), gate_rows(u_col)),
            pl.BlockSpec((CHUNK, d_gmlp), gate_rows(g_col)),
            pl.BlockSpec((1, d_gmlp), const2), pl.BlockSpec((1, d_gmlp), const2),
            pl.BlockSpec((GMLP_HEADS, CHUNK, CHUNK), const3),
            pl.BlockSpec((GMLP_HEADS, CHUNK, 1), const3),
            pl.BlockSpec((1, d_gmlp), const2),
            pl.BlockSpec((ka + d_gmlp, tn), lambda p, j: (0, mat_chunk(p, j))),
            pl.BlockSpec((tm, tn), lambda p, j: (mat_tile(p), mat_chunk(p, j))),
            pl.BlockSpec((1, tn), lambda p, j: (0, out_chunk(p, j))),
            pl.BlockSpec((1, tn), lambda p, j: (0, out_chunk(p, j))),
            pl.BlockSpec((tn, 2 * LANES), lambda p, j: (out_chunk(p, j), 0)),
            pl.BlockSpec((1, LANES), const2),
        ],
        out_specs=[
            pl.BlockSpec((tm, tn), lambda p, j: (out_tile(p), out_chunk(p, j))),
            pl.BlockSpec((tm, tn // 2), lambda p, j: (out_tile(p), out_chunk(p, j))),
            pl.BlockSpec((tm, LANES), lambda p, j: (out_tile(p), 0)),
        ],
        out_shape=[jax.ShapeDtypeStruct((m, d), F32), jax.ShapeDtypeStruct((m, d // 2), jnp.uint32),
                   jax.ShapeDtypeStruct((m, LANES), F32)],
        scratch_shapes=[pltpu.VMEM((2, tm, d_gmlp), BF16), pltpu.VMEM((CHUNK, d_gmlp), F32),
                        pltpu.VMEM((n_chunks, tm, tn), F32), pltpu.VMEM((tm, 1), F32), pltpu.VMEM((tm, 1), F32),
                        pltpu.VMEM((tm, 1), F32)],
        compiler_params=_params("arbitrary", "arbitrary", vmem_limit=VMEM_LIMIT_LARGE),
        name="gate_mix_ln_router",
    )(attn2, proj2, proj2, g_ln_g, g_ln_b, w_s, b_s[:, :, None], beta_gmlp, wb, x2, ln_g, ln_b, w_route, b_route)


def _route_kernel(lo_ref, eid_ref, wt_ref, cnt_ref):
    lg = lo_ref[...]
    lane = lax.broadcasted_iota(jnp.int32, lg.shape, 1)
    lane_f = lane.astype(F32)
    none = float(LANES)
    first = lambda hit: jnp.min(jnp.where(hit, lane_f, none), axis=-1, keepdims=True)

    in_groups = lane < N_GROUPS
    g_logits = jnp.where(in_groups, lg, -jnp.inf)
    g_max = jnp.max(g_logits, axis=-1, keepdims=True)
    g_idx = first(g_logits == g_max)
    g_w = 1.0 / jnp.sum(jnp.where(in_groups, jnp.exp(lg - g_max), 0.0), axis=-1, keepdims=True)

    lo = N_GROUPS + g_idx * EXPERTS_PER_GROUP
    in_group = (lane_f >= lo) & (lane_f < lo + EXPERTS_PER_GROUP)
    e_logits = jnp.where(in_group, lg, -jnp.inf)
    t1 = jnp.max(e_logits, axis=-1, keepdims=True)
    i1 = first(e_logits == t1)
    e_rest = jnp.where(lane_f == i1, -jnp.inf, e_logits)
    t2 = jnp.max(e_rest, axis=-1, keepdims=True)
    i2 = first(e_rest == t2)
    ex = jnp.exp(t2 - t1)
    w1 = g_w * (1.0 / (1.0 + ex))
    w2 = g_w * (ex / (1.0 + ex))
    wt_ref[...] = jnp.where(lane == 0, w1, jnp.where(lane == 1, w2, 0.0))

    @pl.when(pl.program_id(0) == 0)
    def _():
        cnt_ref[...] = jnp.zeros(cnt_ref.shape, F32)

    tm = lg.shape[0]
    pick1 = lane_f == i1 - N_GROUPS
    pick2 = lane_f == i2 - N_GROUPS
    chosen = pick1.astype(F32) + pick2.astype(F32)
    earlier = (lax.broadcasted_iota(jnp.int32, (tm, tm), 1) < lax.broadcasted_iota(jnp.int32, (tm, tm), 0))
    before = jnp.dot(earlier.astype(BF16), chosen.astype(BF16), preferred_element_type=F32) + cnt_ref[...]
    r1 = jnp.sum(jnp.where(pick1, before, 0.0), axis=-1, keepdims=True)
    r2 = jnp.sum(jnp.where(pick2, before, 0.0), axis=-1, keepdims=True)
    cnt_ref[...] += jnp.sum(chosen, axis=0, keepdims=True)
    picks = (jnp.where(lane == 0, i1 - N_GROUPS, 0.0) + jnp.where(lane == 1, i2 - N_GROUPS, 0.0)
             + jnp.where(lane == 2, r1, 0.0) + jnp.where(lane == 3, r2, 0.0))
    eid_ref[...] = picks.astype(jnp.int32)


def _route(logits, tm=1024):
    m = logits.shape[0]
    spec = pl.BlockSpec((tm, LANES), lambda i: (i, 0))
    return pl.pallas_call(
        _route_kernel,
        grid=(m // tm,),
        in_specs=[spec],
        out_specs=[spec, spec, pl.BlockSpec((1, LANES), lambda i: (0, 0))],
        out_shape=[jax.ShapeDtypeStruct((m, LANES), jnp.int32), jax.ShapeDtypeStruct((m, LANES), F32),
                   jax.ShapeDtypeStruct((1, LANES), F32)],
        compiler_params=_params("arbitrary"),
        name="route",
    )(logits)


def _expert_kernel(be_ref, cnt_ref, nreal_ref, tok_ref, aid_ref, x_hbm, wg_ref, wu_ref, wd_ref, ys_hbm,
                   xbuf, xb, acc, obuf, gsem, ssem, *, tm, nf):
    half = xb.shape[1] // 2
    b = pl.program_id(0)
    f = pl.program_id(1)
    nreal = nreal_ref[0]
    slot = lax.rem(b, 2)

    def rows_of(blk):
        return pl.multiple_of(cnt_ref[blk], SUBLANES)

    def for_each_row(n_rows, copy_of_row):
        def body(grp, carry):
            base = pl.multiple_of(grp * SUBLANES, SUBLANES)
            for k in range(SUBLANES):
                copy_of_row(base + k).start(priority=k % 2)
            return carry
        lax.fori_loop(0, lax.shift_right_logical(n_rows, SUBLANES.bit_length() - 1), body, 0)

    def start_gather(slot_, n_rows):
        for_each_row(n_rows, lambda r: pltpu.make_async_copy(
            x_hbm.at[pl.ds(tok_ref[0, r], 1)], xbuf.at[slot_, pl.ds(r, 1)], gsem.at[slot_]))

    def wait_gather(slot_, n_rows):
        pltpu.make_async_copy(x_hbm.at[pl.ds(0, n_rows)], xbuf.at[slot_, pl.ds(0, n_rows)], gsem.at[slot_]).wait()

    def start_scatter(slot_, n_rows):
        for_each_row(n_rows, lambda r: pltpu.make_async_copy(
            obuf.at[slot_, pl.ds(r, 1)], ys_hbm.at[pl.ds(aid_ref[0, r], 1)], ssem.at[slot_]))

    def wait_scatter(slot_, n_rows):
        pltpu.make_async_copy(obuf.at[slot_, pl.ds(0, n_rows)], ys_hbm.at[pl.ds(0, n_rows)], ssem.at[slot_]).wait()

    @pl.when(b < nreal)
    def _():
        @pl.when(f == 0)
        def _():
            @pl.when(b == 0)
            def _():
                xbuf[...] = jnp.zeros(xbuf.shape, xbuf.dtype)
                spare0 = ys_hbm.shape[0] - 2 * tm
                fills = [pltpu.make_async_copy(xbuf.at[h], ys_hbm.at[pl.ds(spare0 + h * tm, tm)], ssem.at[h])
                         for h in range(2)]
                for fill in fills:
                    fill.start()
                for fill in fills:
                    fill.wait()
                start_gather(0, rows_of(0))
            wait_gather(slot, rows_of(b))
            lo, hi = _unpack_bf16_pair(xbuf[slot])
            gw = X1_PACK_GROUP // 2
            for grp in range(half // gw):
                xb[:, 2 * grp * gw:(2 * grp + 1) * gw] = lo[:, grp * gw:(grp + 1) * gw].astype(BF16)
                xb[:, (2 * grp + 1) * gw:(2 * grp + 2) * gw] = hi[:, grp * gw:(grp + 1) * gw].astype(BF16)

        x = xb[...]
        g = jnp.dot(x, wg_ref[...].astype(BF16), preferred_element_type=F32)
        u = jnp.dot(x, wu_ref[...].astype(BF16), preferred_element_type=F32)
        hidden = (g * (1.0 / (1.0 + jnp.exp(-g))) * u).astype(BF16)
        contrib = jnp.dot(hidden, wd_ref[...].astype(BF16), preferred_element_type=F32)

        @pl.when(f == 0)
        def _():
            acc[...] = contrib

        @pl.when((f > 0) & (f < nf - 1))
        def _():
            acc[...] += contrib

        @pl.when(f == nf - 1)
        def _():
            @pl.when(b >= 1)
            def _():
                wait_scatter(1 - slot, rows_of(b - 1))
            total = acc[...] + contrib
            obuf[slot] = _pack_bf16_pair(total[:, :half], total[:, half:])
            start_scatter(slot, rows_of(b))

            @pl.when(b + 1 < nreal)
            def _():
                start_gather(1 - slot, rows_of(b + 1))

            @pl.when(b + 1 == nreal)
            def _():
                wait_scatter(slot, rows_of(b))


def _expert_mlp(x1p, plan, w_gate, w_up, w_down, tm, fc=256):
    tok_blocks, aid_blocks, block_expert, block_rows, n_real = plan
    n, dp = x1p.shape
    d = 2 * dp
    n_blocks = tok_blocks.shape[0]
    d_ff = w_gate.shape[2]
    nf = d_ff // fc
    assert nf >= 2

    def live(b, f, nreal):
        is_real = b < nreal[0]
        return jnp.where(is_real, b, nreal[0] - 1), jnp.where(is_real, f, nf - 1)

    def tok_map(b, f, be, cnt, nreal):
        bb, ff = live(b, f, nreal)
        return jnp.minimum(bb + (ff == nf - 1).astype(jnp.int32), nreal[0] - 1), 0, 0

    def aid_map(b, f, be, cnt, nreal):
        bb, _ = live(b, f, nreal)
        return bb, 0, 0

    def w_in_map(b, f, be, cnt, nreal):
        bb, ff = live(b, f, nreal)
        return be[bb], 0, ff

    def w_out_map(b, f, be, cnt, nreal):
        bb, ff = live(b, f, nreal)
        return be[bb], ff, 0

    grid_spec = pltpu.PrefetchScalarGridSpec(
        num_scalar_prefetch=3,
        grid=(n_blocks, nf),
        in_specs=[
            pl.BlockSpec((None, 1, tm), tok_map, memory_space=pltpu.SMEM),
            pl.BlockSpec((None, 1, tm), aid_map, memory_space=pltpu.SMEM),
            pl.BlockSpec(memory_space=pl.ANY),
            pl.BlockSpec((None, d, fc), w_in_map),
            pl.BlockSpec((None, d, fc), w_in_map),
            pl.BlockSpec((None, fc, d), w_out_map),
        ],
        out_specs=pl.BlockSpec(memory_space=pl.ANY),
        scratch_shapes=[pltpu.VMEM((2, tm, dp), jnp.uint32), pltpu.VMEM((tm, d), BF16), pltpu.VMEM((tm, d), F32),
                        pltpu.VMEM((2, tm, dp), jnp.uint32),
                        pltpu.SemaphoreType.DMA((2,)), pltpu.SemaphoreType.DMA((2,))],
    )
    kernel = functools.partial(_expert_kernel, tm=tm, nf=nf)
    return pl.pallas_call(
        kernel,
        grid_spec=grid_spec,
        out_shape=jax.ShapeDtypeStruct((n * TOP_K + 2 * tm, dp), jnp.uint32),
        compiler_params=_params("arbitrary", "arbitrary"),
        name="expert_mlp",
    )(block_expert, block_rows, n_real, tok_blocks, aid_blocks, x1p, w_gate, w_up, w_down)


def _combine_kernel(x1_ref, y0_ref, y1_ref, wt_ref, lg_ref, lb_ref, o_ref, buf_ref):
    half = x1_ref.shape[1] // 2
    wt = wt_ref[...]
    y0 = _unpack_bf16_pair(y0_ref[...])
    y1 = _unpack_bf16_pair(y1_ref[...])
    for p, cs in enumerate((slice(0, half), slice(half, 2 * half))):
        buf_ref[:, cs] = DN_ALPHA * x1_ref[:, cs] + (wt[:, 0:1] * y0[p] + wt[:, 1:2] * y1[p])
    hsum = buf_ref[...]
    mu = jnp.mean(hsum, axis=-1, keepdims=True)
    hc = hsum - mu
    var = jnp.mean(jnp.square(hc), axis=-1, keepdims=True)
    o_ref[...] = hc * lax.rsqrt(var + LN_EPS) * lg_ref[...] + lb_ref[...]


def _combine_ln(x1, ys, wts, ln_g, ln_b, tm=256):
    m, d = x1.shape
    big = pl.BlockSpec((tm, d), lambda i: (i, 0))
    first = pl.BlockSpec((tm, d // 2), lambda i: (i, 0))
    second = pl.BlockSpec((tm, d // 2), lambda i: (m // tm + i, 0))
    row_vec = pl.BlockSpec((1, d), lambda i: (0, 0))
    return pl.pallas_call(
        _combine_kernel,
        grid=(m // tm,),
        in_specs=[big, first, second, pl.BlockSpec((tm, LANES), lambda i: (i, 0)), row_vec, row_vec],
        out_specs=big,
        out_shape=jax.ShapeDtypeStruct((m, d), F32),
        scratch_shapes=[pltpu.VMEM((tm, d), F32)],
        compiler_params=_params("arbitrary"),
        name="combine_ln",
    )(x1, ys, ys, wts, ln_g, ln_b)


def _plan_blocks(eid, rank, counts, tm):
    n = eid.shape[0]
    a = n * TOP_K
    expert_id = eid.reshape(-1)
    padded = (counts + tm - 1) // tm * tm
    pad_ends = jnp.cumsum(padded)
    pad_starts = pad_ends - padded
    n_blocks = a // tm + N_EXPERTS
    experts = jnp.arange(N_EXPERTS, dtype=jnp.int32)
    start_of = jnp.dot((expert_id[:, None] == experts[None, :]).astype(F32), pad_starts.astype(F32),
                       precision=lax.Precision.HIGHEST).astype(jnp.int32)
    pos = start_of + rank.reshape(-1)
    spare = a + jnp.arange(n_blocks * tm, dtype=jnp.int32) % (2 * tm)
    flat = jnp.arange(a, dtype=jnp.int32)
    aid_pad = spare.at[pos].set((flat % TOP_K) * n + flat // TOP_K, unique_indices=True)
    tok_pad = jnp.where(aid_pad < a, aid_pad % n, 0)
    block_start = jnp.arange(n_blocks, dtype=jnp.int32) * tm
    block_expert = jnp.minimum(jnp.sum(pad_ends[None, :] <= block_start[:, None], axis=1), N_EXPERTS - 1).astype(jnp.int32)
    block_count = jnp.clip(counts[block_expert] - (block_start - pad_starts[block_expert]), 0, tm)
    block_rows = ((block_count + SUBLANES - 1) // SUBLANES * SUBLANES).astype(jnp.int32)
    n_real = (pad_ends[-1] // tm).astype(jnp.int32).reshape(1)
    return tok_pad.reshape(n_blocks, 1, tm), aid_pad.reshape(n_blocks, 1, tm), block_expert, block_rows, n_real


def kernel(x, w_in, w_out, lambda_q1, lambda_k1, lambda_q2, lambda_k2, subln_g, beta_attn, gmlp_ln_g, gmlp_ln_b,
           spatial_w, spatial_b, beta_gmlp, rel_bias, ln1_g, ln1_b, w_group, b_group, w_expert, b_expert,
           w_gate, w_up, w_down, ln2_g, ln2_b):
    b, s, d = x.shape
    n = b * s
    d_attn = ATTN_HEADS * ATTN_V_DIM
    d_gmlp = d - d_attn
    moe_tm = 320
    for l in range(DEPTH):
        lambda_init = 0.8 - 0.6 * math.exp(-0.3 * l)
        x2 = x.reshape(n, d)
        proj = _proj_matmul(x2.astype(BF16), w_in[l])
        attn = _diff_attention(proj.reshape(b, s, -1), rel_bias, lambda_q1[l][None], lambda_k1[l][None],
                               lambda_q2[l][None], lambda_k2[l][None], subln_g[l][None], beta_attn[l][None],
                               lambda_init)
        gating = (gmlp_ln_g[l][None], gmlp_ln_b[l][None], spatial_w[l], spatial_b[l], beta_gmlp[l][None])
        n_route = N_GROUPS + N_EXPERTS
        w_route = jnp.concatenate([w_group[l], w_expert[l].reshape(d, N_EXPERTS),
                                   jnp.zeros((d, LANES - n_route), F32)], axis=1)
        w_route_hi = w_route.astype(BF16)
        w_route = jnp.concatenate([w_route_hi, (w_route - w_route_hi.astype(F32)).astype(BF16)], axis=1)
        b_route = jnp.concatenate([b_group[l], b_expert[l].reshape(-1), jnp.zeros((LANES - n_route,), F32)])[None]
        x1, x1p, logits = _gate_mix_ln_router(attn.reshape(n, d_attn), proj, gating, w_out[l].astype(BF16), x2,
                                              ln1_g[l][None], ln1_b[l][None], w_route, b_route, d_gmlp,
                                              u_col=3 * d_attn // d_gmlp, g_col=3 * d_attn // d_gmlp + 1)
        picks, wts, counts = _route(logits)
        plan = _plan_blocks(picks[:, :TOP_K], picks[:, TOP_K:2 * TOP_K], counts[0, :N_EXPERTS].astype(jnp.int32),
                            moe_tm)
        ys = _expert_mlp(x1p, plan, w_gate[l], w_up[l], w_down[l], moe_tm)
        x = _combine_ln(x1, ys, wts, ln2_g[l][None], ln2_b[l][None]).reshape(b, s, d)
    return x
```

```python
import functools
import math

import jax
import jax.numpy as jnp
from jax import lax
from jax.experimental import pallas as pl
from jax.experimental.pallas import tpu as pltpu

F32 = jnp.float32
BF16 = jnp.bfloat16

ATTN_HEADS = 8
ATTN_HEAD_DIM = 128
ATTN_V_DIM = 2 * ATTN_HEAD_DIM
GMLP_HEADS = 8
CHUNK = 128
REL_BUCKETS = 32
REL_MAX_DIST = 128
N_GROUPS = 8
EXPERTS_PER_GROUP = 8
N_EXPERTS = N_GROUPS * EXPERTS_PER_GROUP
TOP_K = 2
LN_EPS = 1e-5
DEPTH = 1
DN_ALPHA = (2 * DEPTH) ** 0.25
NEG_INF = -1e30
LOG2E = math.log2(math.e)
LANES = 128
SUBLANES = 8
VMEM_LIMIT = 56 * 1024 * 1024
VMEM_LIMIT_LARGE = 60 * 1024 * 1024
X1_PACK_GROUP = 256


def _params(*semantics, vmem_limit=VMEM_LIMIT):
    return pltpu.CompilerParams(dimension_semantics=semantics, vmem_limit_bytes=vmem_limit)


def _matmul_kernel(x_ref, w_ref, o_ref, wb_ref):
    @pl.when(pl.program_id(1) == 0)
    def _():
        wb_ref[...] = w_ref[...].astype(BF16)

    o_ref[...] = jnp.dot(x_ref[...], wb_ref[...], preferred_element_type=F32).astype(o_ref.dtype)


def _proj_matmul(xb, w, tm=512, tn=1024):
    m, k = xb.shape
    n = w.shape[1]
    return pl.pallas_call(
        _matmul_kernel,
        grid=(n // tn, m // tm),
        in_specs=[pl.BlockSpec((tm, k), lambda j, i: (i, 0)),
                  pl.BlockSpec((k, tn), lambda j, i: (0, j))],
        out_specs=pl.BlockSpec((tm, tn), lambda j, i: (i, j)),
        out_shape=jax.ShapeDtypeStruct((m, n), BF16),
        scratch_shapes=[pltpu.VMEM((k, tn), BF16)],
        compiler_params=_params("arbitrary", "arbitrary"),
        name="proj_matmul",
    )(xb, w)


def _rel_bucket(n):
    max_exact = REL_BUCKETS // 2
    nf = jnp.maximum(n, max_exact).astype(F32)
    large = max_exact + (jnp.log(nf / max_exact) / math.log(REL_MAX_DIST / max_exact)
                         * (REL_BUCKETS - max_exact)).astype(jnp.int32)
    large = jnp.minimum(large, REL_BUCKETS - 1)
    return jnp.where(n < max_exact, n, large)


def _rel_bias_tiles(rel_bias, blk):
    pos = jnp.arange(blk)
    tiles = []
    for d in (0, 1):
        bucket = _rel_bucket(jnp.maximum(pos[:, None] + d * blk - pos[None, :], 0))
        hit = bucket[None] == jnp.arange(REL_BUCKETS)[:, None, None]
        tiles.append(jnp.sum(jnp.where(hit[:, None], rel_bias[:, :, None, None], 0.0), axis=0))
    return jnp.stack(tiles, axis=1).astype(F32)


def _attn_kernel(far_ref, q_ref, k_ref, v_ref, bias_ref, lq1_ref, lk1_ref, lq2_ref, lk2_ref,
                 sg_ref, ba_ref, o_ref, mx_ref, sh_ref, mrow_ref, l_ref, s_ref, acc_ref, *, blk, hp, lambda_init):
    g = pl.program_id(1)
    i = pl.program_id(2)
    dh = ATTN_HEAD_DIM
    dv = ATTN_V_DIM
    c1 = ATTN_HEAD_DIM ** -0.5 * LOG2E
    streams = [(hh, mi) for hh in range(hp) for mi in range(2)]
    q = q_ref[...]
    qs = [q[:, hh * dv + mi * dh:hh * dv + (mi + 1) * dh] for hh, mi in streams]
    contract_last = (((1,), (1,)), ((), ()))
    far2 = [far_ref[g * hp + hh] * LOG2E for hh in range(hp)]
    n_far = jnp.maximum(i - 1, 0)

    def lane_chunks(x):
        return [x[:, c * LANES:(c + 1) * LANES] for c in range(x.shape[1] // LANES)]

    def raw_scores(start, width):
        kj = k_ref[pl.ds(start, width), :]
        return [lax.dot_general(qs[sid], kj[:, hh * dv + mi * dh:hh * dv + (mi + 1) * dh], contract_last,
                                preferred_element_type=F32) for sid, (hh, mi) in enumerate(streams)]

    def over_far_blocks(body):
        def pair(t, carry):
            body(pl.multiple_of(t * (2 * blk), 2 * blk), 2 * blk)
            return carry
        lax.fori_loop(0, lax.shift_right_logical(n_far, 1), pair, 0)

        @pl.when(lax.rem(n_far, 2) == 1)
        def _():
            body(pl.multiple_of((n_far - 1) * blk, blk), blk)

    mx_ref[...] = jnp.full(mx_ref.shape, -jnp.inf, F32)

    def max_body(start, width):
        s = raw_scores(start, width)
        for sid in range(len(streams)):
            best = mx_ref[sid]
            for chunk in lane_chunks(s[sid]):
                best = jnp.maximum(best, chunk)
            mx_ref[sid] = best

    over_far_blocks(max_body)

    row = lax.broadcasted_iota(jnp.int32, (blk, blk), 0)
    col = lax.broadcasted_iota(jnp.int32, (blk, blk), 1)
    near = raw_scores(pl.multiple_of(n_far * blk, blk), blk)
    diag = raw_scores(pl.multiple_of(i * blk, blk), blk)
    for sid, (hh, mi) in enumerate(streams):
        s_near = jnp.where(i >= 1, near[sid] * c1 + bias_ref[hh, 1], NEG_INF)
        s_diag = jnp.where(col <= row, diag[sid] * c1 + bias_ref[hh, 0], NEG_INF)
        s_ref[sid, 0] = s_near
        s_ref[sid, 1] = s_diag
        best = mx_ref[sid] * c1 + far2[hh]
        for chunk in lane_chunks(s_near) + lane_chunks(s_diag):
            best = jnp.maximum(best, chunk)
        m_rows = jnp.broadcast_to(jnp.max(best, axis=-1, keepdims=True), (blk, LANES))
        mrow_ref[sid] = m_rows
        sh_ref[sid] = far2[hh] - m_rows

    l_ref[...] = jnp.zeros(l_ref.shape, F32)
    acc_ref[...] = jnp.zeros(acc_ref.shape, F32)

    def accumulate(p_chunks, vj):
        ps = []
        for sid in range(len(streams)):
            tot = l_ref[sid]
            for chunk in p_chunks[sid]:
                tot = tot + chunk
            l_ref[sid] = tot
            ps.append(jnp.concatenate(p_chunks[sid], axis=1).astype(BF16))
        for hh in range(hp):
            acc_ref[hh] += jnp.dot(jnp.concatenate(ps[2 * hh:2 * hh + 2], axis=0), vj[:, hh * dv:(hh + 1) * dv],
                                   preferred_element_type=F32)

    def pv_body(start, width):
        s = raw_scores(start, width)
        accumulate([[jnp.exp2(chunk * c1 + sh_ref[sid]) for chunk in lane_chunks(s[sid])]
                    for sid in range(len(streams))], v_ref[pl.ds(start, width), :])

    over_far_blocks(pv_body)

    for t, start in ((0, pl.multiple_of(n_far * blk, blk)), (1, pl.multiple_of(i * blk, blk))):
        accumulate([[jnp.exp2(chunk - mrow_ref[sid]) for chunk in lane_chunks(s_ref[sid, t])]
                    for sid in range(len(streams))], v_ref[pl.ds(start, blk), :])

    lam = (jnp.exp(jnp.sum(lq1_ref[...] * lk1_ref[...], axis=-1, keepdims=True))
           - jnp.exp(jnp.sum(lq2_ref[...] * lk2_ref[...], axis=-1, keepdims=True)) + lambda_init)
    for hh in range(hp):
        l1 = jnp.sum(l_ref[2 * hh], axis=-1, keepdims=True)
        l2 = jnp.sum(l_ref[2 * hh + 1], axis=-1, keepdims=True)
        o = acc_ref[hh, :blk] / l1 - lam * (acc_ref[hh, blk:] / l2)
        o = o * lax.rsqrt(jnp.mean(jnp.square(o), axis=-1, keepdims=True) + LN_EPS) * sg_ref[...]
        o = o * (1.0 - lambda_init) * ba_ref[:, hh * dv:(hh + 1) * dv]
        o_ref[:, hh * dv:(hh + 1) * dv] = o.astype(o_ref.dtype)


def _diff_attention(proj3, rel_bias, lq1, lk1, lq2, lk2, subln_g, beta_attn, lambda_init, blk=256, hp=4):
    b, s, _ = proj3.shape
    h = ATTN_HEADS
    dv = ATTN_V_DIM
    gw = hp * dv
    groups = h // hp
    assert REL_BUCKETS // 2 + int(math.log((blk + 1) / (REL_BUCKETS // 2)) / math.log(REL_MAX_DIST / (REL_BUCKETS // 2))
                                  * (REL_BUCKETS // 2)) >= REL_BUCKETS
    tiles = _rel_bias_tiles(rel_bias, blk) * LOG2E
    far = rel_bias[REL_BUCKETS - 1].astype(F32)
    vec = lambda c: pl.BlockSpec((1, c), lambda bi, gi, qi: (0, 0))
    kernel = functools.partial(_attn_kernel, blk=blk, hp=hp, lambda_init=lambda_init)
    n_streams = 2 * hp
    stat = pltpu.VMEM((n_streams, blk, LANES), F32)
    return pl.pallas_call(
        kernel,
        grid=(b, groups, s // blk),
        in_specs=[
            pl.BlockSpec(memory_space=pltpu.SMEM),
            pl.BlockSpec((None, blk, gw), lambda bi, gi, qi: (bi, qi, gi)),
            pl.BlockSpec((None, s, gw), lambda bi, gi, qi: (bi, 0, groups + gi)),
            pl.BlockSpec((None, s, gw), lambda bi, gi, qi: (bi, 0, 2 * groups + gi)),
            pl.BlockSpec((hp, 2, blk, blk), lambda bi, gi, qi: (gi, 0, 0, 0)),
            vec(ATTN_HEAD_DIM), vec(ATTN_HEAD_DIM), vec(ATTN_HEAD_DIM), vec(ATTN_HEAD_DIM),
            vec(dv),
            pl.BlockSpec((1, gw), lambda bi, gi, qi: (0, gi)),
        ],
        out_specs=pl.BlockSpec((None, blk, gw), lambda bi, gi, qi: (bi, qi, gi)),
        out_shape=jax.ShapeDtypeStruct((b, s, h * dv), BF16),
        scratch_shapes=[stat, stat, stat, stat,
                        pltpu.VMEM((n_streams, 2, blk, blk), F32), pltpu.VMEM((hp, 2 * blk, dv), F32)],
        compiler_params=_params("arbitrary", "arbitrary", "arbitrary"),
        name="diff_attention",
    )(far, proj3, proj3, proj3, tiles, lq1, lk1, lq2, lk2, subln_g, beta_attn)


def _gelu(x):
    return 0.5 * x * (1.0 + lax.erf(x * (2.0 ** -0.5)))


def _gating_chunk(u_ref, g_ref, lg_ref, lb_ref, ws_ref, bs_ref, bg_ref, buf_ref):
    hd = u_ref.shape[1] // GMLP_HEADS
    g = _gelu(g_ref[...].astype(F32))
    mu = jnp.mean(g, axis=-1, keepdims=True)
    gc = g - mu
    var = jnp.mean(jnp.square(gc), axis=-1, keepdims=True)
    v = (gc * lax.rsqrt(var + LN_EPS) * lg_ref[...] + lb_ref[...]).astype(BF16)
    row = lax.broadcasted_iota(jnp.int32, (CHUNK, CHUNK), 0)
    col = lax.broadcasted_iota(jnp.int32, (CHUNK, CHUNK), 1)
    tril = col <= row
    ss = jnp.zeros((CHUNK, 1), F32)
    for hh in range(GMLP_HEADS):
        w = jnp.where(tril, ws_ref[hh], 0.0).astype(BF16)
        mixed = jnp.dot(w, v[:, hh * hd:(hh + 1) * hd], preferred_element_type=F32) + bs_ref[hh]
        out = _gelu(u_ref[:, hh * hd:(hh + 1) * hd].astype(F32)) * mixed
        ss = ss + jnp.sum(jnp.square(out), axis=-1, keepdims=True)
        buf_ref[:, hh * hd:(hh + 1) * hd] = out
    rstd = lax.rsqrt(ss / u_ref.shape[1] + LN_EPS)
    return (buf_ref[...] * rstd * bg_ref[...]).astype(BF16)


def _pack_bf16_pair(lo, hi):
    lo_bits = lax.bitcast_convert_type(lo.astype(BF16).astype(F32), jnp.uint32) >> 16
    hi_bits = lax.bitcast_convert_type(hi.astype(BF16).astype(F32), jnp.uint32) & jnp.uint32(0xFFFF0000)
    return hi_bits | lo_bits


def _unpack_bf16_pair(words):
    lo = lax.bitcast_convert_type(words << 16, F32)
    hi = lax.bitcast_convert_type(words & jnp.uint32(0xFFFF0000), F32)
    return lo, hi


def _mix_kernel(a_ref, u_ref, g_ref, glg_ref, glb_ref, ws_ref, bs_ref, bg_ref, w_ref, x_ref, lg_ref, lb_ref,
                wr_ref, br_ref, x1_ref, x1p_ref, lo_ref, gm_ref, gbuf_ref, buf_ref, sum_ref, mu_ref, rstd_ref,
                *, n_tiles, n_chunks, tn):
    p = pl.program_id(0)
    j = pl.program_id(1)
    ka = a_ref.shape[1]

    def gate():
        rows = pl.ds(pl.multiple_of(j * CHUNK, CHUNK), CHUNK)
        gm_ref[lax.rem(p, 2), rows, :] = _gating_chunk(u_ref, g_ref, glg_ref, glb_ref, ws_ref, bs_ref, bg_ref,
                                                       gbuf_ref)

    def normalise():
        y = (buf_ref[j] - mu_ref[...]) * rstd_ref[...] * lg_ref[...] + lb_ref[...]
        x1_ref[...] = y
        gw = X1_PACK_GROUP // 2
        for grp in range(tn // X1_PACK_GROUP):
            x1p_ref[:, grp * gw:(grp + 1) * gw] = _pack_bf16_pair(y[:, 2 * grp * gw:(2 * grp + 1) * gw],
                                                                  y[:, (2 * grp + 1) * gw:(2 * grp + 2) * gw])
        y_hi = y.astype(BF16)
        y_lo = (y - y_hi.astype(F32)).astype(BF16)
        start = jnp.where(j == 0, jnp.broadcast_to(br_ref[...], lo_ref.shape), lo_ref[...])
        by_hi = jnp.dot(y_hi, wr_ref[...], preferred_element_type=F32)
        by_lo = jnp.dot(y_lo, wr_ref[:, :LANES], preferred_element_type=F32)
        lo_ref[...] = start + (by_hi[:, :LANES] + (by_lo + by_hi[:, LANES:]))

    def matmul():
        mix = (jnp.dot(a_ref[...], w_ref[:ka, :], preferred_element_type=F32)
               + jnp.dot(gm_ref[lax.rem(p + 1, 2)], w_ref[ka:, :], preferred_element_type=F32))
        h = DN_ALPHA * x_ref[...] + mix
        buf_ref[j] = h
        part = jnp.sum(h, axis=-1, keepdims=True)
        sum_ref[...] = jnp.where(j == 0, part, sum_ref[...] + part)

    def row_stats():
        @pl.when(j == n_chunks - 1)
        def _():
            d = n_chunks * tn
            mu = sum_ref[...] / d
            sq = jnp.sum(jnp.square(buf_ref[0] - mu), axis=-1, keepdims=True)
            for c in range(1, n_chunks):
                sq = sq + jnp.sum(jnp.square(buf_ref[c] - mu), axis=-1, keepdims=True)
            mu_ref[...] = mu
            rstd_ref[...] = lax.rsqrt(sq / d + LN_EPS)

    @pl.when(p == 0)
    def _():
        gate()

    @pl.when(p == 1)
    def _():
        matmul()
        gate()
        row_stats()

    @pl.when((p >= 2) & (p < n_tiles))
    def _():
        normalise()
        matmul()
        gate()
        row_stats()

    @pl.when(p == n_tiles)
    def _():
        normalise()
        matmul()
        row_stats()

    @pl.when(p == n_tiles + 1)
    def _():
        normalise()


def _gate_mix_ln_router(attn2, proj2, gating, wb, x2, ln_g, ln_b, w_route, b_route, d_gmlp, u_col, g_col,
                        tm=512, tn=1024):
    g_ln_g, g_ln_b, w_s, b_s, beta_gmlp = gating
    m, d = x2.shape
    ka = attn2.shape[1]
    n_chunks = d // tn
    n_tiles = m // tm
    assert tm // CHUNK == n_chunks and n_tiles >= 3 and tn % X1_PACK_GROUP == 0
    last = n_chunks - 1
    gate_tile = lambda p: jnp.minimum(p, n_tiles - 1)
    mat_tile = lambda p: jnp.clip(p - 1, 0, n_tiles - 1)
    out_tile = lambda p: jnp.maximum(p - 2, 0)
    gate_chunk = lambda p, j: jnp.where(p < n_tiles, j, last)
    mat_chunk = lambda p, j: jnp.where(p < 1, 0, jnp.where(p <= n_tiles, j, last))
    out_chunk = lambda p, j: jnp.where(p >= 2, j, 0)
    gate_rows = lambda col: (lambda p, j: (gate_tile(p) * n_chunks + gate_chunk(p, j), col))
    const2 = lambda p, j: (0, 0)
    const3 = lambda p, j: (0, 0, 0)
    kernel = functools.partial(_mix_kernel, n_tiles=n_tiles, n_chunks=n_chunks, tn=tn)
    return pl.pallas_call(
        kernel,
        grid=(n_tiles + 2, n_chunks),
        in_specs=[
            pl.BlockSpec((tm, ka), lambda p, j: (mat_tile(p), 0)),
            pl.BlockSpec((CHUNK, d_gmlp), gate_rows(u_col)),
            pl.BlockSpec((CHUNK, d_gmlp), gate_rows(g_col)),
            pl.BlockSpec((1, d_gmlp), const2), pl.BlockSpec((1, d_gmlp), const2),
            pl.BlockSpec((GMLP_HEADS, CHUNK, CHUNK), const3),
            pl.BlockSpec((GMLP_HEADS, CHUNK, 1), const3),
            pl.BlockSpec((1, d_gmlp), const2),
            pl.BlockSpec((ka + d_gmlp, tn), lambda p, j: (0, mat_chunk(p, j))),
            pl.BlockSpec((tm, tn), lambda p, j: (mat_tile(p), mat_chunk(p, j))),
            pl.BlockSpec((1, tn), lambda p, j: (0, out_chunk(p, j))),
            pl.BlockSpec((1, tn), lambda p, j: (0, out_chunk(p, j))),
            pl.BlockSpec((tn, 2 * LANES), lambda p, j: (out_chunk(p, j), 0)),
            pl.BlockSpec((1, LANES), const2),
        ],
        out_specs=[
            pl.BlockSpec((tm, tn), lambda p, j: (out_tile(p), out_chunk(p, j))),
            pl.BlockSpec((tm, tn // 2), lambda p, j: (out_tile(p), out_chunk(p, j))),
            pl.BlockSpec((tm, LANES), lambda p, j: (out_tile(p), 0)),
        ],
        out_shape=[jax.ShapeDtypeStruct((m, d), F32), jax.ShapeDtypeStruct((m, d // 2), jnp.uint32),
                   jax.ShapeDtypeStruct((m, LANES), F32)],
        scratch_shapes=[pltpu.VMEM((2, tm, d_gmlp), BF16), pltpu.VMEM((CHUNK, d_gmlp), F32),
                        pltpu.VMEM((n_chunks, tm, tn), F32), pltpu.VMEM((tm, 1), F32), pltpu.VMEM((tm, 1), F32),
                        pltpu.VMEM((tm, 1), F32)],
        compiler_params=_params("arbitrary", "arbitrary", vmem_limit=VMEM_LIMIT_LARGE),
        name="gate_mix_ln_router",
    )(attn2, proj2, proj2, g_ln_g, g_ln_b, w_s, b_s[:, :, None], beta_gmlp, wb, x2, ln_g, ln_b, w_route, b_route)


def _route_kernel(lo_ref, eid_ref, wt_ref, cnt_ref):
    lg = lo_ref[...]
    lane = lax.broadcasted_iota(jnp.int32, lg.shape, 1)
    lane_f = lane.astype(F32)
    none = float(LANES)
    first = lambda hit: jnp.min(jnp.where(hit, lane_f, none), axis=-1, keepdims=True)

    in_groups = lane < N_GROUPS
    g_logits = jnp.where(in_groups, lg, -jnp.inf)
    g_max = jnp.max(g_logits, axis=-1, keepdims=True)
    g_idx = first(g_logits == g_max)
    g_w = 1.0 / jnp.sum(jnp.where(in_groups, jnp.exp(lg - g_max), 0.0), axis=-1, keepdims=True)

    lo = N_GROUPS + g_idx * EXPERTS_PER_GROUP
    in_group = (lane_f >= lo) & (lane_f < lo + EXPERTS_PER_GROUP)
    e_logits = jnp.where(in_group, lg, -jnp.inf)
    t1 = jnp.max(e_logits, axis=-1, keepdims=True)
    i1 = first(e_logits == t1)
    e_rest = jnp.where(lane_f == i1, -jnp.inf, e_logits)
    t2 = jnp.max(e_rest, axis=-1, keepdims=True)
    i2 = first(e_rest == t2)
    ex = jnp.exp(t2 - t1)
    w1 = g_w * (1.0 / (1.0 + ex))
    w2 = g_w * (ex / (1.0 + ex))
    wt_ref[...] = jnp.where(lane == 0, w1, jnp.where(lane == 1, w2, 0.0))

    @pl.when(pl.program_id(0) == 0)
    def _():
        cnt_ref[...] = jnp.zeros(cnt_ref.shape, F32)

    tm = lg.shape[0]
    pick1 = lane_f == i1 - N_GROUPS
    pick2 = lane_f == i2 - N_GROUPS
    chosen = pick1.astype(F32) + pick2.astype(F32)
    earlier = (lax.broadcasted_iota(jnp.int32, (tm, tm), 1) < lax.broadcasted_iota(jnp.int32, (tm, tm), 0))
    before = jnp.dot(earlier.astype(BF16), chosen.astype(BF16), preferred_element_type=F32) + cnt_ref[...]
    r1 = jnp.sum(jnp.where(pick1, before, 0.0), axis=-1, keepdims=True)
    r2 = jnp.sum(jnp.where(pick2, before, 0.0), axis=-1, keepdims=True)
    cnt_ref[...] += jnp.sum(chosen, axis=0, keepdims=True)
    picks = (jnp.where(lane == 0, i1 - N_GROUPS, 0.0) + jnp.where(lane == 1, i2 - N_GROUPS, 0.0)
             + jnp.where(lane == 2, r1, 0.0) + jnp.where(lane == 3, r2, 0.0))
    eid_ref[...] = picks.astype(jnp.int32)


def _route(logits, tm=1024):
    m = logits.shape[0]
    spec = pl.BlockSpec((tm, LANES), lambda i: (i, 0))
    return pl.pallas_call(
        _route_kernel,
        grid=(m // tm,),
        in_specs=[spec],
        out_specs=[spec, spec, pl.BlockSpec((1, LANES), lambda i: (0, 0))],
        out_shape=[jax.ShapeDtypeStruct((m, LANES), jnp.int32), jax.ShapeDtypeStruct((m, LANES), F32),
                   jax.ShapeDtypeStruct((1, LANES), F32)],
        compiler_params=_params("arbitrary"),
        name="route",
    )(logits)


def _expert_kernel(be_ref, cnt_ref, nreal_ref, tok_ref, aid_ref, x_hbm, wg_ref, wu_ref, wd_ref, ys_hbm,
                   xbuf, xb, acc, obuf, gsem, ssem, *, tm, nf):
    half = xb.shape[1] // 2
    b = pl.program_id(0)
    f = pl.program_id(1)
    nreal = nreal_ref[0]
    slot = lax.rem(b, 2)

    def rows_of(blk):
        return pl.multiple_of(cnt_ref[blk], SUBLANES)

    def for_each_row(n_rows, copy_of_row):
        def body(grp, carry):
            base = pl.multiple_of(grp * SUBLANES, SUBLANES)
            for k in range(SUBLANES):
                copy_of_row(base + k).start(priority=k % 2)
            return carry
        lax.fori_loop(0, lax.shift_right_logical(n_rows, SUBLANES.bit_length() - 1), body, 0)

    def start_gather(slot_, n_rows):
        for_each_row(n_rows, lambda r: pltpu.make_async_copy(
            x_hbm.at[pl.ds(tok_ref[0, r], 1)], xbuf.at[slot_, pl.ds(r, 1)], gsem.at[slot_]))

    def wait_gather(slot_, n_rows):
        pltpu.make_async_copy(x_hbm.at[pl.ds(0, n_rows)], xbuf.at[slot_, pl.ds(0, n_rows)], gsem.at[slot_]).wait()

    def start_scatter(slot_, n_rows):
        for_each_row(n_rows, lambda r: pltpu.make_async_copy(
            obuf.at[slot_, pl.ds(r, 1)], ys_hbm.at[pl.ds(aid_ref[0, r], 1)], ssem.at[slot_]))

    def wait_scatter(slot_, n_rows):
        pltpu.make_async_copy(obuf.at[slot_, pl.ds(0, n_rows)], ys_hbm.at[pl.ds(0, n_rows)], ssem.at[slot_]).wait()

    @pl.when(b < nreal)
    def _():
        @pl.when(f == 0)
        def _():
            @pl.when(b == 0)
            def _():
                xbuf[...] = jnp.zeros(xbuf.shape, xbuf.dtype)
                spare0 = ys_hbm.shape[0] - 2 * tm
                fills = [pltpu.make_async_copy(xbuf.at[h], ys_hbm.at[pl.ds(spare0 + h * tm, tm)], ssem.at[h])
                         for h in range(2)]
                for fill in fills:
                    fill.start()
                for fill in fills:
                    fill.wait()
                start_gather(0, rows_of(0))
            wait_gather(slot, rows_of(b))
            lo, hi = _unpack_bf16_pair(xbuf[slot])
            gw = X1_PACK_GROUP // 2
            for grp in range(half // gw):
                xb[:, 2 * grp * gw:(2 * grp + 1) * gw] = lo[:, grp * gw:(grp + 1) * gw].astype(BF16)
                xb[:, (2 * grp + 1) * gw:(2 * grp + 2) * gw] = hi[:, grp * gw:(grp + 1) * gw].astype(BF16)

        @pl.when((f == nf - 1) & (b + 1 < nreal))
        def _():
            start_gather(1 - slot, rows_of(b + 1))

        x = xb[...]
        g = jnp.dot(x, wg_ref[...].astype(BF16), preferred_element_type=F32)
        u = jnp.dot(x, wu_ref[...].astype(BF16), preferred_element_type=F32)
        hidden = (g * (1.0 / (1.0 + jnp.exp(-g))) * u).astype(BF16)
        contrib = jnp.dot(hidden, wd_ref[...].astype(BF16), preferred_element_type=F32)

        @pl.when(f == 0)
        def _():
            acc[...] = contrib

        @pl.when((f > 0) & (f < nf - 1))
        def _():
            acc[...] += contrib

        @pl.when(f == nf - 1)
        def _():
            @pl.when(b >= 1)
            def _():
                wait_scatter(1 - slot, rows_of(b - 1))
            total = acc[...] + contrib
            obuf[slot] = _pack_bf16_pair(total[:, :half], total[:, half:])
            start_scatter(slot, rows_of(b))

            @pl.when(b + 1 == nreal)
            def _():
                wait_scatter(slot, rows_of(b))


def _expert_mlp(x1p, plan, w_gate, w_up, w_down, tm, fc=256):
    tok_blocks, aid_blocks, block_expert, block_rows, n_real = plan
    n, dp = x1p.shape
    d = 2 * dp
    n_blocks = tok_blocks.shape[0]
    d_ff = w_gate.shape[2]
    nf = d_ff // fc
    assert nf >= 2

    def live(b, f, nreal):
        is_real = b < nreal[0]
        return jnp.where(is_real, b, nreal[0] - 1), jnp.where(is_real, f, nf - 1)

    def tok_map(b, f, be, cnt, nreal):
        bb, ff = live(b, f, nreal)
        return jnp.minimum(bb + (ff == nf - 1).astype(jnp.int32), nreal[0] - 1), 0, 0

    def aid_map(b, f, be, cnt, nreal):
        bb, _ = live(b, f, nreal)
        return bb, 0, 0

    def w_in_map(b, f, be, cnt, nreal):
        bb, ff = live(b, f, nreal)
        return be[bb], 0, ff

    def w_out_map(b, f, be, cnt, nreal):
        bb, ff = live(b, f, nreal)
        return be[bb], ff, 0

    grid_spec = pltpu.PrefetchScalarGridSpec(
        num_scalar_prefetch=3,
        grid=(n_blocks, nf),
        in_specs=[
            pl.BlockSpec((None, 1, tm), tok_map, memory_space=pltpu.SMEM),
            pl.BlockSpec((None, 1, tm), aid_map, memory_space=pltpu.SMEM),
            pl.BlockSpec(memory_space=pl.ANY),
            pl.BlockSpec((None, d, fc), w_in_map),
            pl.BlockSpec((None, d, fc), w_in_map),
            pl.BlockSpec((None, fc, d), w_out_map),
        ],
        out_specs=pl.BlockSpec(memory_space=pl.ANY),
        scratch_shapes=[pltpu.VMEM((2, tm, dp), jnp.uint32), pltpu.VMEM((tm, d), BF16), pltpu.VMEM((tm, d), F32),
                        pltpu.VMEM((2, tm, dp), jnp.uint32),
                        pltpu.SemaphoreType.DMA((2,)), pltpu.SemaphoreType.DMA((2,))],
    )
    kernel = functools.partial(_expert_kernel, tm=tm, nf=nf)
    return pl.pallas_call(
        kernel,
        grid_spec=grid_spec,
        out_shape=jax.ShapeDtypeStruct((n * TOP_K + 2 * tm, dp), jnp.uint32),
        compiler_params=_params("arbitrary", "arbitrary"),
        name="expert_mlp",
    )(block_expert, block_rows, n_real, tok_blocks, aid_blocks, x1p, w_gate, w_up, w_down)


def _combine_kernel(x1_ref, y0_ref, y1_ref, wt_ref, lg_ref, lb_ref, o_ref, buf_ref):
    half = x1_ref.shape[1] // 2
    wt = wt_ref[...]
    y0 = _unpack_bf16_pair(y0_ref[...])
    y1 = _unpack_bf16_pair(y1_ref[...])
    for p, cs in enumerate((slice(0, half), slice(half, 2 * half))):
        buf_ref[:, cs] = DN_ALPHA * x1_ref[:, cs] + (wt[:, 0:1] * y0[p] + wt[:, 1:2] * y1[p])
    hsum = buf_ref[...]
    mu = jnp.mean(hsum, axis=-1, keepdims=True)
    hc = hsum - mu
    var = jnp.mean(jnp.square(hc), axis=-1, keepdims=True)
    o_ref[...] = hc * lax.rsqrt(var + LN_EPS) * lg_ref[...] + lb_ref[...]


def _combine_ln(x1, ys, wts, ln_g, ln_b, tm=256):
    m, d = x1.shape
    big = pl.BlockSpec((tm, d), lambda i: (i, 0))
    first = pl.BlockSpec((tm, d // 2), lambda i: (i, 0))
    second = pl.BlockSpec((tm, d // 2), lambda i: (m // tm + i, 0))
    row_vec = pl.BlockSpec((1, d), lambda i: (0, 0))
    return pl.pallas_call(
        _combine_kernel,
        grid=(m // tm,),
        in_specs=[big, first, second, pl.BlockSpec((tm, LANES), lambda i: (i, 0)), row_vec, row_vec],
        out_specs=big,
        out_shape=jax.ShapeDtypeStruct((m, d), F32),
        scratch_shapes=[pltpu.VMEM((tm, d), F32)],
        compiler_params=_params("arbitrary"),
        name="combine_ln",
    )(x1, ys, ys, wts, ln_g, ln_b)


def _plan_blocks(eid, rank, counts, tm):
    n = eid.shape[0]
    a = n * TOP_K
    expert_id = eid.reshape(-1)
    padded = (counts + tm - 1) // tm * tm
    pad_ends = jnp.cumsum(padded)
    pad_starts = pad_ends - padded
    n_blocks = a // tm + N_EXPERTS
    experts = jnp.arange(N_EXPERTS, dtype=jnp.int32)
    start_of = jnp.dot((expert_id[:, None] == experts[None, :]).astype(F32), pad_starts.astype(F32),
                       precision=lax.Precision.HIGHEST).astype(jnp.int32)
    pos = start_of + rank.reshape(-1)
    spare = a + jnp.arange(n_blocks * tm, dtype=jnp.int32) % (2 * tm)
    flat = jnp.arange(a, dtype=jnp.int32)
    aid_pad = spare.at[pos].set((flat % TOP_K) * n + flat // TOP_K, unique_indices=True)
    tok_pad = jnp.where(aid_pad < a, aid_pad % n, 0)
    block_start = jnp.arange(n_blocks, dtype=jnp.int32) * tm
    block_expert = jnp.minimum(jnp.sum(pad_ends[None, :] <= block_start[:, None], axis=1), N_EXPERTS - 1).astype(jnp.int32)
    block_count = jnp.clip(counts[block_expert] - (block_start - pad_starts[block_expert]), 0, tm)
    block_rows = ((block_count + SUBLANES - 1) // SUBLANES * SUBLANES).astype(jnp.int32)
    n_real = (pad_ends[-1] // tm).astype(jnp.int32).reshape(1)
    return tok_pad.reshape(n_blocks, 1, tm), aid_pad.reshape(n_blocks, 1, tm), block_expert, block_rows, n_real


def kernel(x, w_in, w_out, lambda_q1, lambda_k1, lambda_q2, lambda_k2, subln_g, beta_attn, gmlp_ln_g, gmlp_ln_b,
           spatial_w, spatial_b, beta_gmlp, rel_bias, ln1_g, ln1_b, w_group, b_group, w_expert, b_expert,
           w_gate, w_up, w_down, ln2_g, ln2_b):
    b, s, d = x.shape
    n = b * s
    d_attn = ATTN_HEADS * ATTN_V_DIM
    d_gmlp = d - d_attn
    moe_tm = 320
    for l in range(DEPTH):
        lambda_init = 0.8 - 0.6 * math.exp(-0.3 * l)
        x2 = x.reshape(n, d)
        proj = _proj_matmul(x2.astype(BF16), w_in[l])
        attn = _diff_attention(proj.reshape(b, s, -1), rel_bias, lambda_q1[l][None], lambda_k1[l][None],
                               lambda_q2[l][None], lambda_k2[l][None], subln_g[l][None], beta_attn[l][None],
                               lambda_init)
        gating = (gmlp_ln_g[l][None], gmlp_ln_b[l][None], spatial_w[l], spatial_b[l], beta_gmlp[l][None])
        n_route = N_GROUPS + N_EXPERTS
        w_route = jnp.concatenate([w_group[l], w_expert[l].reshape(d, N_EXPERTS),
                                   jnp.zeros((d, LANES - n_route), F32)], axis=1)
        w_route_hi = w_route.astype(BF16)
        w_route = jnp.concatenate([w_route_hi, (w_route - w_route_hi.astype(F32)).astype(BF16)], axis=1)
        b_route = jnp.concatenate([b_group[l], b_expert[l].reshape(-1), jnp.zeros((LANES - n_route,), F32)])[None]
        x1, x1p, logits = _gate_mix_ln_router(attn.reshape(n, d_attn), proj, gating, w_out[l].astype(BF16), x2,
                                              ln1_g[l][None], ln1_b[l][None], w_route, b_route, d_gmlp,
                                              u_col=3 * d_attn // d_gmlp, g_col=3 * d_attn // d_gmlp + 1)
        picks, wts, counts = _route(logits)
        plan = _plan_blocks(picks[:, :TOP_K], picks[:, TOP_K:2 * TOP_K], counts[0, :N_EXPERTS].astype(jnp.int32),
                            moe_tm)
        ys = _expert_mlp(x1p, plan, w_gate[l], w_up[l], w_down[l], moe_tm)
        x = _combine_ln(x1, ys, wts, ln2_g[l][None], ln2_b[l][None]).reshape(b, s, d)
    return x
```

```python
import functools
import math

import jax
import jax.numpy as jnp
from jax import lax
from jax.experimental import pallas as pl
from jax.experimental.pallas import tpu as pltpu

F32 = jnp.float32
BF16 = jnp.bfloat16

ATTN_HEADS = 8
ATTN_HEAD_DIM = 128
ATTN_V_DIM = 2 * ATTN_HEAD_DIM
GMLP_HEADS = 8
CHUNK = 128
REL_BUCKETS = 32
REL_MAX_DIST = 128
N_GROUPS = 8
EXPERTS_PER_GROUP = 8
N_EXPERTS = N_GROUPS * EXPERTS_PER_GROUP
TOP_K = 2
LN_EPS = 1e-5
DEPTH = 1
DN_ALPHA = (2 * DEPTH) ** 0.25
NEG_INF = -1e30
LOG2E = math.log2(math.e)
LANES = 128
SUBLANES = 8
VMEM_LIMIT = 56 * 1024 * 1024
VMEM_LIMIT_LARGE = 60 * 1024 * 1024
X1_PACK_GROUP = 256


def _params(*semantics, vmem_limit=VMEM_LIMIT):
    return pltpu.CompilerParams(dimension_semantics=semantics, vmem_limit_bytes=vmem_limit)


def _matmul_kernel(x_ref, w_ref, o_ref, wb_ref):
    @pl.when(pl.program_id(1) == 0)
    def _():
        wb_ref[...] = w_ref[...].astype(BF16)

    o_ref[...] = jnp.dot(x_ref[...], wb_ref[...], preferred_element_type=F32).astype(o_ref.dtype)


def _proj_matmul(xb, w, tm=512, tn=1024):
    m, k = xb.shape
    n = w.shape[1]
    return pl.pallas_call(
        _matmul_kernel,
        grid=(n // tn, m // tm),
        in_specs=[pl.BlockSpec((tm, k), lambda j, i: (i, 0)),
                  pl.BlockSpec((k, tn), lambda j, i: (0, j))],
        out_specs=pl.BlockSpec((tm, tn), lambda j, i: (i, j)),
        out_shape=jax.ShapeDtypeStruct((m, n), BF16),
        scratch_shapes=[pltpu.VMEM((k, tn), BF16)],
        compiler_params=_params("arbitrary", "arbitrary"),
        name="proj_matmul",
    )(xb, w)


def _rel_bucket(n):
    max_exact = REL_BUCKETS // 2
    nf = jnp.maximum(n, max_exact).astype(F32)
    large = max_exact + (jnp.log(nf / max_exact) / math.log(REL_MAX_DIST / max_exact)
                         * (REL_BUCKETS - max_exact)).astype(jnp.int32)
    large = jnp.minimum(large, REL_BUCKETS - 1)
    return jnp.where(n < max_exact, n, large)


def _rel_bias_tiles(rel_bias, blk):
    pos = jnp.arange(blk)
    tiles = []
    for d in (0, 1):
        bucket = _rel_bucket(jnp.maximum(pos[:, None] + d * blk - pos[None, :], 0))
        hit = bucket[None] == jnp.arange(REL_BUCKETS)[:, None, None]
        tiles.append(jnp.sum(jnp.where(hit[:, None], rel_bias[:, :, None, None], 0.0), axis=0))
    return jnp.stack(tiles, axis=1).astype(F32)


def _attn_kernel(far_ref, q_ref, k_ref, v_ref, bias_ref, lq1_ref, lk1_ref, lq2_ref, lk2_ref,
                 sg_ref, ba_ref, o_ref, mx_ref, sh_ref, mrow_ref, l_ref, s_ref, acc_ref, *, blk, hp, lambda_init):
    g = pl.program_id(1)
    i = pl.program_id(2)
    dh = ATTN_HEAD_DIM
    dv = ATTN_V_DIM
    c1 = ATTN_HEAD_DIM ** -0.5 * LOG2E
    streams = [(hh, mi) for hh in range(hp) for mi in range(2)]
    q = q_ref[...]
    qs = [q[:, hh * dv + mi * dh:hh * dv + (mi + 1) * dh] for hh, mi in streams]
    contract_last = (((1,), (1,)), ((), ()))
    far2 = [far_ref[g * hp + hh] * LOG2E for hh in range(hp)]
    n_far = jnp.maximum(i - 1, 0)

    def lane_chunks(x):
        return [x[:, c * LANES:(c + 1) * LANES] for c in range(x.shape[1] // LANES)]

    def raw_scores(start, width):
        kj = k_ref[pl.ds(start, width), :]
        return [lax.dot_general(qs[sid], kj[:, hh * dv + mi * dh:hh * dv + (mi + 1) * dh], contract_last,
                                preferred_element_type=F32) for sid, (hh, mi) in enumerate(streams)]

    def over_far_blocks(body):
        def pair(t, carry):
            body(pl.multiple_of(t * (2 * blk), 2 * blk), 2 * blk)
            return carry
        lax.fori_loop(0, lax.shift_right_logical(n_far, 1), pair, 0)

        @pl.when(lax.rem(n_far, 2) == 1)
        def _():
            body(pl.multiple_of((n_far - 1) * blk, blk), blk)

    mx_ref[...] = jnp.full(mx_ref.shape, -jnp.inf, F32)

    def max_body(start, width):
        s = raw_scores(start, width)
        for sid in range(len(streams)):
            best = mx_ref[sid]
            for chunk in lane_chunks(s[sid]):
                best = jnp.maximum(best, chunk)
            mx_ref[sid] = best

    over_far_blocks(max_body)

    row = lax.broadcasted_iota(jnp.int32, (blk, blk), 0)
    col = lax.broadcasted_iota(jnp.int32, (blk, blk), 1)
    near = raw_scores(pl.multiple_of(n_far * blk, blk), blk)
    diag = raw_scores(pl.multiple_of(i * blk, blk), blk)
    for sid, (hh, mi) in enumerate(streams):
        s_near = jnp.where(i >= 1, near[sid] * c1 + bias_ref[hh, 1], NEG_INF)
        s_diag = jnp.where(col <= row, diag[sid] * c1 + bias_ref[hh, 0], NEG_INF)
        s_ref[sid, 0] = s_near
        s_ref[sid, 1] = s_diag
        best = mx_ref[sid] * c1 + far2[hh]
        for chunk in lane_chunks(s_near) + lane_chunks(s_diag):
            best = jnp.maximum(best, chunk)
        m_rows = jnp.broadcast_to(jnp.max(best, axis=-1, keepdims=True), (blk, LANES))
        mrow_ref[sid] = m_rows
        sh_ref[sid] = far2[hh] - m_rows

    l_ref[...] = jnp.zeros(l_ref.shape, F32)
    acc_ref[...] = jnp.zeros(acc_ref.shape, F32)

    def accumulate(p_chunks, vj):
        ps = []
        for sid in range(len(streams)):
            tot = l_ref[sid]
            for chunk in p_chunks[sid]:
                tot = tot + chunk
            l_ref[sid] = tot
            ps.append(jnp.concatenate(p_chunks[sid], axis=1).astype(BF16))
        for hh in range(hp):
            acc_ref[hh] += jnp.dot(jnp.concatenate(ps[2 * hh:2 * hh + 2], axis=0), vj[:, hh * dv:(hh + 1) * dv],
                                   preferred_element_type=F32)

    def pv_body(start, width):
        s = raw_scores(start, width)
        accumulate([[jnp.exp2(chunk * c1 + sh_ref[sid]) for chunk in lane_chunks(s[sid])]
                    for sid in range(len(streams))], v_ref[pl.ds(start, width), :])

    over_far_blocks(pv_body)

    for t, start in ((0, pl.multiple_of(n_far * blk, blk)), (1, pl.multiple_of(i * blk, blk))):
        accumulate([[jnp.exp2(chunk - mrow_ref[sid]) for chunk in lane_chunks(s_ref[sid, t])]
                    for sid in range(len(streams))], v_ref[pl.ds(start, blk), :])

    lam = (jnp.exp(jnp.sum(lq1_ref[...] * lk1_ref[...], axis=-1, keepdims=True))
           - jnp.exp(jnp.sum(lq2_ref[...] * lk2_ref[...], axis=-1, keepdims=True)) + lambda_init)
    for hh in range(hp):
        l1 = jnp.sum(l_ref[2 * hh], axis=-1, keepdims=True)
        l2 = jnp.sum(l_ref[2 * hh + 1], axis=-1, keepdims=True)
        o = acc_ref[hh, :blk] / l1 - lam * (acc_ref[hh, blk:] / l2)
        o = o * lax.rsqrt(jnp.mean(jnp.square(o), axis=-1, keepdims=True) + LN_EPS) * sg_ref[...]
        o = o * (1.0 - lambda_init) * ba_ref[:, hh * dv:(hh + 1) * dv]
        o_ref[:, hh * dv:(hh + 1) * dv] = o.astype(o_ref.dtype)


def _diff_attention(proj3, rel_bias, lq1, lk1, lq2, lk2, subln_g, beta_attn, lambda_init, blk=256, hp=4):
    b, s, _ = proj3.shape
    h = ATTN_HEADS
    dv = ATTN_V_DIM
    gw = hp * dv
    groups = h // hp
    assert REL_BUCKETS // 2 + int(math.log((blk + 1) / (REL_BUCKETS // 2)) / math.log(REL_MAX_DIST / (REL_BUCKETS // 2))
                                  * (REL_BUCKETS // 2)) >= REL_BUCKETS
    tiles = _rel_bias_tiles(rel_bias, blk) * LOG2E
    far = rel_bias[REL_BUCKETS - 1].astype(F32)
    vec = lambda c: pl.BlockSpec((1, c), lambda bi, gi, qi: (0, 0))
    kernel = functools.partial(_attn_kernel, blk=blk, hp=hp, lambda_init=lambda_init)
    n_streams = 2 * hp
    stat = pltpu.VMEM((n_streams, blk, LANES), F32)
    return pl.pallas_call(
        kernel,
        grid=(b, groups, s // blk),
        in_specs=[
            pl.BlockSpec(memory_space=pltpu.SMEM),
            pl.BlockSpec((None, blk, gw), lambda bi, gi, qi: (bi, qi, gi)),
            pl.BlockSpec((None, s, gw), lambda bi, gi, qi: (bi, 0, groups + gi)),
            pl.BlockSpec((None, s, gw), lambda bi, gi, qi: (bi, 0, 2 * groups + gi)),
            pl.BlockSpec((hp, 2, blk, blk), lambda bi, gi, qi: (gi, 0, 0, 0)),
            vec(ATTN_HEAD_DIM), vec(ATTN_HEAD_DIM), vec(ATTN_HEAD_DIM), vec(ATTN_HEAD_DIM),
            vec(dv),
            pl.BlockSpec((1, gw), lambda bi, gi, qi: (0, gi)),
        ],
        out_specs=pl.BlockSpec((None, blk, gw), lambda bi, gi, qi: (bi, qi, gi)),
        out_shape=jax.ShapeDtypeStruct((b, s, h * dv), BF16),
        scratch_shapes=[stat, stat, stat, stat,
                        pltpu.VMEM((n_streams, 2, blk, blk), F32), pltpu.VMEM((hp, 2 * blk, dv), F32)],
        compiler_params=_params("arbitrary", "arbitrary", "arbitrary"),
        name="diff_attention",
    )(far, proj3, proj3, proj3, tiles, lq1, lk1, lq2, lk2, subln_g, beta_attn)


def _gelu(x):
    return 0.5 * x * (1.0 + lax.erf(x * (2.0 ** -0.5)))


def _gating_chunk(u_ref, g_ref, lg_ref, lb_ref, ws_ref, bs_ref, bg_ref, buf_ref):
    hd = u_ref.shape[1] // GMLP_HEADS
    g = _gelu(g_ref[...].astype(F32))
    mu = jnp.mean(g, axis=-1, keepdims=True)
    gc = g - mu
    var = jnp.mean(jnp.square(gc), axis=-1, keepdims=True)
    v = (gc * lax.rsqrt(var + LN_EPS) * lg_ref[...] + lb_ref[...]).astype(BF16)
    row = lax.broadcasted_iota(jnp.int32, (CHUNK, CHUNK), 0)
    col = lax.broadcasted_iota(jnp.int32, (CHUNK, CHUNK), 1)
    tril = col <= row
    ss = jnp.zeros((CHUNK, 1), F32)
    for hh in range(GMLP_HEADS):
        w = jnp.where(tril, ws_ref[hh], 0.0).astype(BF16)
        mixed = jnp.dot(w, v[:, hh * hd:(hh + 1) * hd], preferred_element_type=F32) + bs_ref[hh]
        out = _gelu(u_ref[:, hh * hd:(hh + 1) * hd].astype(F32)) * mixed
        ss = ss + jnp.sum(jnp.square(out), axis=-1, keepdims=True)
        buf_ref[:, hh * hd:(hh + 1) * hd] = out
    rstd = lax.rsqrt(ss / u_ref.shape[1] + LN_EPS)
    return (buf_ref[...] * rstd * bg_ref[...]).astype(BF16)


def _pack_bf16_pair(lo, hi):
    lo_bits = lax.bitcast_convert_type(lo.astype(BF16).astype(F32), jnp.uint32) >> 16
    hi_bits = lax.bitcast_convert_type(hi.astype(BF16).astype(F32), jnp.uint32) & jnp.uint32(0xFFFF0000)
    return hi_bits | lo_bits


def _unpack_bf16_pair(words):
    lo = lax.bitcast_convert_type(words << 16, F32)
    hi = lax.bitcast_convert_type(words & jnp.uint32(0xFFFF0000), F32)
    return lo, hi


def _mix_kernel(a_ref, u_ref, g_ref, glg_ref, glb_ref, ws_ref, bs_ref, bg_ref, w_ref, x_ref, lg_ref, lb_ref,
                wr_ref, br_ref, x1_ref, x1p_ref, lo_ref, gm_ref, gbuf_ref, buf_ref, sum_ref, mu_ref, rstd_ref,
                *, n_tiles, n_chunks, tn):
    p = pl.program_id(0)
    j = pl.program_id(1)
    ka = a_ref.shape[1]

    def gate():
        rows = pl.ds(pl.multiple_of(j * CHUNK, CHUNK), CHUNK)
        gm_ref[lax.rem(p, 2), rows, :] = _gating_chunk(u_ref, g_ref, glg_ref, glb_ref, ws_ref, bs_ref, bg_ref,
                                                       gbuf_ref)

    def normalise():
        y = (buf_ref[j] - mu_ref[...]) * rstd_ref[...] * lg_ref[...] + lb_ref[...]
        x1_ref[...] = y
        gw = X1_PACK_GROUP // 2
        for grp in range(tn // X1_PACK_GROUP):
            x1p_ref[:, grp * gw:(grp + 1) * gw] = _pack_bf16_pair(y[:, 2 * grp * gw:(2 * grp + 1) * gw],
                                                                  y[:, (2 * grp + 1) * gw:(2 * grp + 2) * gw])
        y_hi = y.astype(BF16)
        y_lo = (y - y_hi.astype(F32)).astype(BF16)
        start = jnp.where(j == 0, jnp.broadcast_to(br_ref[...], lo_ref.shape), lo_ref[...])
        by_hi = jnp.dot(y_hi, wr_ref[...], preferred_element_type=F32)
        by_lo = jnp.dot(y_lo, wr_ref[:, :LANES], preferred_element_type=F32)
        lo_ref[...] = start + (by_hi[:, :LANES] + (by_lo + by_hi[:, LANES:]))

    def matmul():
        mix = (jnp.dot(a_ref[...], w_ref[:ka, :], preferred_element_type=F32)
               + jnp.dot(gm_ref[lax.rem(p + 1, 2)], w_ref[ka:, :], preferred_element_type=F32))
        h = DN_ALPHA * x_ref[...] + mix
        buf_ref[j] = h
        part = jnp.sum(h, axis=-1, keepdims=True)
        sum_ref[...] = jnp.where(j == 0, part, sum_ref[...] + part)

    def row_stats():
        @pl.when(j == n_chunks - 1)
        def _():
            d = n_chunks * tn
            mu = sum_ref[...] / d
            sq = jnp.sum(jnp.square(buf_ref[0] - mu), axis=-1, keepdims=True)
            for c in range(1, n_chunks):
                sq = sq + jnp.sum(jnp.square(buf_ref[c] - mu), axis=-1, keepdims=True)
            mu_ref[...] = mu
            rstd_ref[...] = lax.rsqrt(sq / d + LN_EPS)

    @pl.when(p == 0)
    def _():
        gate()

    @pl.when(p == 1)
    def _():
        matmul()
        gate()
        row_stats()

    @pl.when((p >= 2) & (p < n_tiles))
    def _():
        normalise()
        matmul()
        gate()
        row_stats()

    @pl.when(p == n_tiles)
    def _():
        normalise()
        matmul()
        row_stats()

    @pl.when(p == n_tiles + 1)
    def _():
        normalise()


def _gate_mix_ln_router(attn2, proj2, gating, wb, x2, ln_g, ln_b, w_route, b_route, d_gmlp, u_col, g_col,
                        tm=512, tn=1024):
    g_ln_g, g_ln_b, w_s, b_s, beta_gmlp = gating
    m, d = x2.shape
    ka = attn2.shape[1]
    n_chunks = d // tn
    n_tiles = m // tm
    assert tm // CHUNK == n_chunks and n_tiles >= 3 and tn % X1_PACK_GROUP == 0
    last = n_chunks - 1
    gate_tile = lambda p: jnp.minimum(p, n_tiles - 1)
    mat_tile = lambda p: jnp.clip(p - 1, 0, n_tiles - 1)
    out_tile = lambda p: jnp.maximum(p - 2, 0)
    gate_chunk = lambda p, j: jnp.where(p < n_tiles, j, last)
    mat_chunk = lambda p, j: jnp.where(p < 1, 0, jnp.where(p <= n_tiles, j, last))
    out_chunk = lambda p, j: jnp.where(p >= 2, j, 0)
    gate_rows = lambda col: (lambda p, j: (gate_tile(p) * n_chunks + gate_chunk(p, j), col))
    const2 = lambda p, j: (0, 0)
    const3 = lambda p, j: (0, 0, 0)
    kernel = functools.partial(_mix_kernel, n_tiles=n_tiles, n_chunks=n_chunks, tn=tn)
    return pl.pallas_call(
        kernel,
        grid=(n_tiles + 2, n_chunks),
        in_specs=[
            pl.BlockSpec((tm, ka), lambda p, j: (mat_tile(p), 0)),
            pl.BlockSpec((CHUNK, d_gmlp), gate_rows(u_col)),
            pl.BlockSpec((CHUNK, d_gmlp), gate_rows(g_col)),
            pl.BlockSpec((1, d_gmlp), const2), pl.BlockSpec((1, d_gmlp), const2),
            pl.BlockSpec((GMLP_HEADS, CHUNK, CHUNK), const3),
            pl.BlockSpec((GMLP_HEADS, CHUNK, 1), const3),
            pl.BlockSpec((1, d_gmlp), const2),
            pl.BlockSpec((ka + d_gmlp, tn), lambda p, j: (0, mat_chunk(p, j))),
            pl.BlockSpec((tm, tn), lambda p, j: (mat_tile(p), mat_chunk(p, j))),
            pl.BlockSpec((1, tn), lambda p, j: (0, out_chunk(p, j))),
            pl.BlockSpec((1, tn), lambda p, j: (0, out_chunk(p, j))),
            pl.BlockSpec((tn, 2 * LANES), lambda p, j: (out_chunk(p, j), 0)),
            pl.BlockSpec((1, LANES), const2),
        ],
        out_specs=[
            pl.BlockSpec((tm, tn), lambda p, j: (out_tile(p), out_chunk(p, j))),
            pl.BlockSpec((tm, tn // 2), lambda p, j: (out_tile(p), out_chunk(p, j))),
            pl.BlockSpec((tm, LANES), lambda p, j: (out_tile(p), 0)),
        ],
        out_shape=[jax.ShapeDtypeStruct((m, d), F32), jax.ShapeDtypeStruct((m, d // 2), jnp.uint32),
                   jax.ShapeDtypeStruct((m, LANES), F32)],
        scratch_shapes=[pltpu.VMEM((2, tm, d_gmlp), BF16), pltpu.VMEM((CHUNK, d_gmlp), F32),
                        pltpu.VMEM((n_chunks, tm, tn), F32), pltpu.VMEM((tm, 1), F32), pltpu.VMEM((tm, 1), F32),
                        pltpu.VMEM((tm, 1), F32)],
        compiler_params=_params("arbitrary", "arbitrary", vmem_limit=VMEM_LIMIT_LARGE),
        name="gate_mix_ln_router",
    )(attn2, proj2, proj2, g_ln_g, g_ln_b, w_s, b_s[:, :, None], beta_gmlp, wb, x2, ln_g, ln_b, w_route, b_route)


def _route_kernel(lo_ref, eid_ref, wt_ref, cnt_ref):
    lg = lo_ref[...]
    lane = lax.broadcasted_iota(jnp.int32, lg.shape, 1)
    lane_f = lane.astype(F32)
    none = float(LANES)
    first = lambda hit: jnp.min(jnp.where(hit, lane_f, none), axis=-1, keepdims=True)

    in_groups = lane < N_GROUPS
    g_logits = jnp.where(in_groups, lg, -jnp.inf)
    g_max = jnp.max(g_logits, axis=-1, keepdims=True)
    g_idx = first(g_logits == g_max)
    g_w = 1.0 / jnp.sum(jnp.where(in_groups, jnp.exp(lg - g_max), 0.0), axis=-1, keepdims=True)

    lo = N_GROUPS + g_idx * EXPERTS_PER_GROUP
    in_group = (lane_f >= lo) & (lane_f < lo + EXPERTS_PER_GROUP)
    e_logits = jnp.where(in_group, lg, -jnp.inf)
    t1 = jnp.max(e_logits, axis=-1, keepdims=True)
    i1 = first(e_logits == t1)
    e_rest = jnp.where(lane_f == i1, -jnp.inf, e_logits)
    t2 = jnp.max(e_rest, axis=-1, keepdims=True)
    i2 = first(e_rest == t2)
    ex = jnp.exp(t2 - t1)
    w1 = g_w * (1.0 / (1.0 + ex))
    w2 = g_w * (ex / (1.0 + ex))
    wt_ref[...] = jnp.where(lane == 0, w1, jnp.where(lane == 1, w2, 0.0))

    @pl.when(pl.program_id(0) == 0)
    def _():
        cnt_ref[...] = jnp.zeros(cnt_ref.shape, F32)

    tm = lg.shape[0]
    pick1 = lane_f == i1 - N_GROUPS
    pick2 = lane_f == i2 - N_GROUPS
    chosen = pick1.astype(F32) + pick2.astype(F32)
    earlier = (lax.broadcasted_iota(jnp.int32, (tm, tm), 1) < lax.broadcasted_iota(jnp.int32, (tm, tm), 0))
    before = jnp.dot(earlier.astype(BF16), chosen.astype(BF16), preferred_element_type=F32) + cnt_ref[...]
    r1 = jnp.sum(jnp.where(pick1, before, 0.0), axis=-1, keepdims=True)
    r2 = jnp.sum(jnp.where(pick2, before, 0.0), axis=-1, keepdims=True)
    cnt_ref[...] += jnp.sum(chosen, axis=0, keepdims=True)
    picks = (jnp.where(lane == 0, i1 - N_GROUPS, 0.0) + jnp.where(lane == 1, i2 - N_GROUPS, 0.0)
             + jnp.where(lane == 2, r1, 0.0) + jnp.where(lane == 3, r2, 0.0))
    eid_ref[...] = picks.astype(jnp.int32)


def _route(logits, tm=1024):
    m = logits.shape[0]
    spec = pl.BlockSpec((tm, LANES), lambda i: (i, 0))
    return pl.pallas_call(
        _route_kernel,
        grid=(m // tm,),
        in_specs=[spec],
        out_specs=[spec, spec, pl.BlockSpec((1, LANES), lambda i: (0, 0))],
        out_shape=[jax.ShapeDtypeStruct((m, LANES), jnp.int32), jax.ShapeDtypeStruct((m, LANES), F32),
                   jax.ShapeDtypeStruct((1, LANES), F32)],
        compiler_params=_params("arbitrary"),
        name="route",
    )(logits)


def _expert_kernel(be_ref, cnt_ref, nreal_ref, tok_ref, aid_ref, x_hbm, wg_ref, wu_ref, wd_ref, ys_hbm,
                   xbuf, xb, acc, obuf, gsem, ssem, *, tm, nf):
    half = xb.shape[1] // 2
    b = pl.program_id(0)
    f = pl.program_id(1)
    nreal = nreal_ref[0]
    slot = lax.rem(b, 2)

    def rows_of(blk):
        return pl.multiple_of(cnt_ref[blk], SUBLANES)

    def for_each_row(n_rows, copy_of_row):
        def body(grp, carry):
            base = pl.multiple_of(grp * SUBLANES, SUBLANES)
            for k in range(SUBLANES):
                copy_of_row(base + k).start(priority=1)
            return carry
        lax.fori_loop(0, lax.shift_right_logical(n_rows, SUBLANES.bit_length() - 1), body, 0)

    def start_gather(slot_, n_rows):
        for_each_row(n_rows, lambda r: pltpu.make_async_copy(
            x_hbm.at[pl.ds(tok_ref[0, r], 1)], xbuf.at[slot_, pl.ds(r, 1)], gsem.at[slot_]))

    def wait_gather(slot_, n_rows):
        pltpu.make_async_copy(x_hbm.at[pl.ds(0, n_rows)], xbuf.at[slot_, pl.ds(0, n_rows)], gsem.at[slot_]).wait()

    def start_scatter(slot_, n_rows):
        for_each_row(n_rows, lambda r: pltpu.make_async_copy(
            obuf.at[slot_, pl.ds(r, 1)], ys_hbm.at[pl.ds(aid_ref[0, r], 1)], ssem.at[slot_]))

    def wait_scatter(slot_, n_rows):
        pltpu.make_async_copy(obuf.at[slot_, pl.ds(0, n_rows)], ys_hbm.at[pl.ds(0, n_rows)], ssem.at[slot_]).wait()

    @pl.when(b < nreal)
    def _():
        @pl.when(f == 0)
        def _():
            @pl.when(b == 0)
            def _():
                xbuf[...] = jnp.zeros(xbuf.shape, xbuf.dtype)
                spare0 = ys_hbm.shape[0] - 2 * tm
                fills = [pltpu.make_async_copy(xbuf.at[h], ys_hbm.at[pl.ds(spare0 + h * tm, tm)], ssem.at[h])
                         for h in range(2)]
                for fill in fills:
                    fill.start()
                for fill in fills:
                    fill.wait()
                start_gather(0, rows_of(0))
            wait_gather(slot, rows_of(b))
            lo, hi = _unpack_bf16_pair(xbuf[slot])
            gw = X1_PACK_GROUP // 2
            for grp in range(half // gw):
                xb[:, 2 * grp * gw:(2 * grp + 1) * gw] = lo[:, grp * gw:(grp + 1) * gw].astype(BF16)
                xb[:, (2 * grp + 1) * gw:(2 * grp + 2) * gw] = hi[:, grp * gw:(grp + 1) * gw].astype(BF16)

        @pl.when((f == nf - 1) & (b + 1 < nreal))
        def _():
            start_gather(1 - slot, rows_of(b + 1))

        x = xb[...]
        g = jnp.dot(x, wg_ref[...].astype(BF16), preferred_element_type=F32)
        u = jnp.dot(x, wu_ref[...].astype(BF16), preferred_element_type=F32)
        hidden = (g * (1.0 / (1.0 + jnp.exp(-g))) * u).astype(BF16)
        contrib = jnp.dot(hidden, wd_ref[...].astype(BF16), preferred_element_type=F32)

        @pl.when(f == 0)
        def _():
            acc[...] = contrib

        @pl.when((f > 0) & (f < nf - 1))
        def _():
            acc[...] += contrib

        @pl.when(f == nf - 1)
        def _():
            @pl.when(b >= 1)
            def _():
                wait_scatter(1 - slot, rows_of(b - 1))
            total = acc[...] + contrib
            obuf[slot] = _pack_bf16_pair(total[:, :half], total[:, half:])
            start_scatter(slot, rows_of(b))

            @pl.when(b + 1 == nreal)
            def _():
                wait_scatter(slot, rows_of(b))


def _expert_mlp(x1p, plan, w_gate, w_up, w_down, tm, fc=256):
    tok_blocks, aid_blocks, block_expert, block_rows, n_real = plan
    n, dp = x1p.shape
    d = 2 * dp
    n_blocks = tok_blocks.shape[0]
    d_ff = w_gate.shape[2]
    nf = d_ff // fc
    assert nf >= 2

    def live(b, f, nreal):
        is_real = b < nreal[0]
        return jnp.where(is_real, b, nreal[0] - 1), jnp.where(is_real, f, nf - 1)

    def tok_map(b, f, be, cnt, nreal):
        bb, ff = live(b, f, nreal)
        return jnp.minimum(bb + (ff == nf - 1).astype(jnp.int32), nreal[0] - 1), 0, 0

    def aid_map(b, f, be, cnt, nreal):
        bb, _ = live(b, f, nreal)
        return bb, 0, 0

    def w_in_map(b, f, be, cnt, nreal):
        bb, ff = live(b, f, nreal)
        return be[bb], 0, ff

    def w_out_map(b, f, be, cnt, nreal):
        bb, ff = live(b, f, nreal)
        return be[bb], ff, 0

    grid_spec = pltpu.PrefetchScalarGridSpec(
        num_scalar_prefetch=3,
        grid=(n_blocks, nf),
        in_specs=[
            pl.BlockSpec((None, 1, tm), tok_map, memory_space=pltpu.SMEM),
            pl.BlockSpec((None, 1, tm), aid_map, memory_space=pltpu.SMEM),
            pl.BlockSpec(memory_space=pl.ANY),
            pl.BlockSpec((None, d, fc), w_in_map),
            pl.BlockSpec((None, d, fc), w_in_map),
            pl.BlockSpec((None, fc, d), w_out_map),
        ],
        out_specs=pl.BlockSpec(memory_space=pl.ANY),
        scratch_shapes=[pltpu.VMEM((2, tm, dp), jnp.uint32), pltpu.VMEM((tm, d), BF16), pltpu.VMEM((tm, d), F32),
                        pltpu.VMEM((2, tm, dp), jnp.uint32),
                        pltpu.SemaphoreType.DMA((2,)), pltpu.SemaphoreType.DMA((2,))],
    )
    kernel = functools.partial(_expert_kernel, tm=tm, nf=nf)
    return pl.pallas_call(
        kernel,
        grid_spec=grid_spec,
        out_shape=jax.ShapeDtypeStruct((n * TOP_K + 2 * tm, dp), jnp.uint32),
        compiler_params=_params("arbitrary", "arbitrary"),
        name="expert_mlp",
    )(block_expert, block_rows, n_real, tok_blocks, aid_blocks, x1p, w_gate, w_up, w_down)


def _combine_kernel(x1_ref, y0_ref, y1_ref, wt_ref, lg_ref, lb_ref, o_ref, buf_ref):
    half = x1_ref.shape[1] // 2
    wt = wt_ref[...]
    y0 = _unpack_bf16_pair(y0_ref[...])
    y1 = _unpack_bf16_pair(y1_ref[...])
    for p, cs in enumerate((slice(0, half), slice(half, 2 * half))):
        buf_ref[:, cs] = DN_ALPHA * x1_ref[:, cs] + (wt[:, 0:1] * y0[p] + wt[:, 1:2] * y1[p])
    hsum = buf_ref[...]
    mu = jnp.mean(hsum, axis=-1, keepdims=True)
    hc = hsum - mu
    var = jnp.mean(jnp.square(hc), axis=-1, keepdims=True)
    o_ref[...] = hc * lax.rsqrt(var + LN_EPS) * lg_ref[...] + lb_ref[...]


def _combine_ln(x1, ys, wts, ln_g, ln_b, tm=256):
    m, d = x1.shape
    big = pl.BlockSpec((tm, d), lambda i: (i, 0))
    first = pl.BlockSpec((tm, d // 2), lambda i: (i, 0))
    second = pl.BlockSpec((tm, d // 2), lambda i: (m // tm + i, 0))
    row_vec = pl.BlockSpec((1, d), lambda i: (0, 0))
    return pl.pallas_call(
        _combine_kernel,
        grid=(m // tm,),
        in_specs=[big, first, second, pl.BlockSpec((tm, LANES), lambda i: (i, 0)), row_vec, row_vec],
        out_specs=big,
        out_shape=jax.ShapeDtypeStruct((m, d), F32),
        scratch_shapes=[pltpu.VMEM((tm, d), F32)],
        compiler_params=_params("arbitrary"),
        name="combine_ln",
    )(x1, ys, ys, wts, ln_g, ln_b)


def _plan_blocks(eid, rank, counts, tm):
    n = eid.shape[0]
    a = n * TOP_K
    expert_id = eid.reshape(-1)
    padded = (counts + tm - 1) // tm * tm
    pad_ends = jnp.cumsum(padded)
    pad_starts = pad_ends - padded
    n_blocks = a // tm + N_EXPERTS
    experts = jnp.arange(N_EXPERTS, dtype=jnp.int32)
    start_of = jnp.dot((expert_id[:, None] == experts[None, :]).astype(F32), pad_starts.astype(F32),
                       precision=lax.Precision.HIGHEST).astype(jnp.int32)
    pos = start_of + rank.reshape(-1)
    spare = a + jnp.arange(n_blocks * tm, dtype=jnp.int32) % (2 * tm)
    flat = jnp.arange(a, dtype=jnp.int32)
    aid_pad = spare.at[pos].set((flat % TOP_K) * n + flat // TOP_K, unique_indices=True)
    tok_pad = jnp.where(aid_pad < a, aid_pad % n, 0)
    block_start = jnp.arange(n_blocks, dtype=jnp.int32) * tm
    block_expert = jnp.minimum(jnp.sum(pad_ends[None, :] <= block_start[:, None], axis=1), N_EXPERTS - 1).astype(jnp.int32)
    block_count = jnp.clip(counts[block_expert] - (block_start - pad_starts[block_expert]), 0, tm)
    block_rows = ((block_count + SUBLANES - 1) // SUBLANES * SUBLANES).astype(jnp.int32)
    n_real = (pad_ends[-1] // tm).astype(jnp.int32).reshape(1)
    return tok_pad.reshape(n_blocks, 1, tm), aid_pad.reshape(n_blocks, 1, tm), block_expert, block_rows, n_real


def kernel(x, w_in, w_out, lambda_q1, lambda_k1, lambda_q2, lambda_k2, subln_g, beta_attn, gmlp_ln_g, gmlp_ln_b,
           spatial_w, spatial_b, beta_gmlp, rel_bias, ln1_g, ln1_b, w_group, b_group, w_expert, b_expert,
           w_gate, w_up, w_down, ln2_g, ln2_b):
    b, s, d = x.shape
    n = b * s
    d_attn = ATTN_HEADS * ATTN_V_DIM
    d_gmlp = d - d_attn
    moe_tm = 320
    for l in range(DEPTH):
        lambda_init = 0.8 - 0.6 * math.exp(-0.3 * l)
        x2 = x.reshape(n, d)
        proj = _proj_matmul(x2.astype(BF16), w_in[l])
        attn = _diff_attention(proj.reshape(b, s, -1), rel_bias, lambda_q1[l][None], lambda_k1[l][None],
                               lambda_q2[l][None], lambda_k2[l][None], subln_g[l][None], beta_attn[l][None],
                               lambda_init)
        gating = (gmlp_ln_g[l][None], gmlp_ln_b[l][None], spatial_w[l], spatial_b[l], beta_gmlp[l][None])
        n_route = N_GROUPS + N_EXPERTS
        w_route = jnp.concatenate([w_group[l], w_expert[l].reshape(d, N_EXPERTS),
                                   jnp.zeros((d, LANES - n_route), F32)], axis=1)
        w_route_hi = w_route.astype(BF16)
        w_route = jnp.concatenate([w_route_hi, (w_route - w_route_hi.astype(F32)).astype(BF16)], axis=1)
        b_route = jnp.concatenate([b_group[l], b_expert[l].reshape(-1), jnp.zeros((LANES - n_route,), F32)])[None]
        x1, x1p, logits = _gate_mix_ln_router(attn.reshape(n, d_attn), proj, gating, w_out[l].astype(BF16), x2,
                                              ln1_g[l][None], ln1_b[l][None], w_route, b_route, d_gmlp,
                                              u_col=3 * d_attn // d_gmlp, g_col=3 * d_attn // d_gmlp + 1)
        picks, wts, counts = _route(logits)
        plan = _plan_blocks(picks[:, :TOP_K], picks[:, TOP_K:2 * TOP_K], counts[0, :N_EXPERTS].astype(jnp.int32),
                            moe_tm)
        ys = _expert_mlp(x1p, plan, w_gate[l], w_up[l], w_down[l], moe_tm)
        x = _combine_ln(x1, ys, wts, ln2_g[l][None], ln2_b[l][None]).reshape(b, s, d)
    return x
```

```python
import functools
import math

import jax
import jax.numpy as jnp
from jax import lax
from jax.experimental import pallas as pl
from jax.experimental.pallas import tpu as pltpu

F32 = jnp.float32
BF16 = jnp.bfloat16

ATTN_HEADS = 8
ATTN_HEAD_DIM = 128
ATTN_V_DIM = 2 * ATTN_HEAD_DIM
GMLP_HEADS = 8
CHUNK = 128
REL_BUCKETS = 32
REL_MAX_DIST = 128
N_GROUPS = 8
EXPERTS_PER_GROUP = 8
N_EXPERTS = N_GROUPS * EXPERTS_PER_GROUP
TOP_K = 2
LN_EPS = 1e-5
DEPTH = 1
DN_ALPHA = (2 * DEPTH) ** 0.25
NEG_INF = -1e30
LOG2E = math.log2(math.e)
LANES = 128
SUBLANES = 8
V7X_VMEM_BYTES = 64 * 1024 * 1024
VMEM_LIMIT = V7X_VMEM_BYTES * 7 // 8
VMEM_LIMIT_LARGE = V7X_VMEM_BYTES * 15 // 16
X1_PACK_GROUP = 256


def _params(*semantics, vmem_limit=VMEM_LIMIT):
    return pltpu.CompilerParams(dimension_semantics=semantics, vmem_limit_bytes=vmem_limit)


def _matmul_kernel(x_ref, w_ref, o_ref, wb_ref):
    @pl.when(pl.program_id(1) == 0)
    def _():
        wb_ref[...] = w_ref[...].astype(BF16)

    o_ref[...] = jnp.dot(x_ref[...], wb_ref[...], preferred_element_type=F32).astype(o_ref.dtype)


def _proj_matmul(xb, w, tm=512, tn=1024):
    m, k = xb.shape
    n = w.shape[1]
    return pl.pallas_call(
        _matmul_kernel,
        grid=(n // tn, m // tm),
        in_specs=[pl.BlockSpec((tm, k), lambda j, i: (i, 0)),
                  pl.BlockSpec((k, tn), lambda j, i: (0, j))],
        out_specs=pl.BlockSpec((tm, tn), lambda j, i: (i, j)),
        out_shape=jax.ShapeDtypeStruct((m, n), BF16),
        scratch_shapes=[pltpu.VMEM((k, tn), BF16)],
        compiler_params=_params("arbitrary", "arbitrary"),
        name="proj_matmul",
    )(xb, w)


def _rel_bucket(n):
    max_exact = REL_BUCKETS // 2
    nf = jnp.maximum(n, max_exact).astype(F32)
    large = max_exact + (jnp.log(nf / max_exact) / math.log(REL_MAX_DIST / max_exact)
                         * (REL_BUCKETS - max_exact)).astype(jnp.int32)
    large = jnp.minimum(large, REL_BUCKETS - 1)
    return jnp.where(n < max_exact, n, large)


def _rel_bias_tiles(rel_bias, blk):
    pos = jnp.arange(blk)
    tiles = []
    for d in (0, 1):
        bucket = _rel_bucket(jnp.maximum(pos[:, None] + d * blk - pos[None, :], 0))
        hit = bucket[None] == jnp.arange(REL_BUCKETS)[:, None, None]
        tiles.append(jnp.sum(jnp.where(hit[:, None], rel_bias[:, :, None, None], 0.0), axis=0))
    return jnp.stack(tiles, axis=1).astype(F32)


def _attn_kernel(far_ref, q_ref, k_ref, v_ref, bias_ref, lq1_ref, lk1_ref, lq2_ref, lk2_ref,
                 sg_ref, ba_ref, o_ref, mx_ref, sh_ref, mrow_ref, l_ref, s_ref, acc_ref, *, blk, hp, lambda_init):
    g = pl.program_id(1)
    i = pl.program_id(2)
    dh = ATTN_HEAD_DIM
    dv = ATTN_V_DIM
    c1 = ATTN_HEAD_DIM ** -0.5 * LOG2E
    streams = [(hh, mi) for hh in range(hp) for mi in range(2)]
    q = q_ref[...]
    qs = [q[:, hh * dv + mi * dh:hh * dv + (mi + 1) * dh] for hh, mi in streams]
    contract_last = (((1,), (1,)), ((), ()))
    far2 = [far_ref[g * hp + hh] * LOG2E for hh in range(hp)]
    n_far = jnp.maximum(i - 1, 0)

    def lane_chunks(x):
        return [x[:, c * LANES:(c + 1) * LANES] for c in range(x.shape[1] // LANES)]

    def raw_scores(start, width):
        kj = k_ref[pl.ds(start, width), :]
        return [lax.dot_general(qs[sid], kj[:, hh * dv + mi * dh:hh * dv + (mi + 1) * dh], contract_last,
                                preferred_element_type=F32) for sid, (hh, mi) in enumerate(streams)]

    def over_far_blocks(body):
        def pair(t, carry):
            body(pl.multiple_of(t * (2 * blk), 2 * blk), 2 * blk)
            return carry
        lax.fori_loop(0, lax.shift_right_logical(n_far, 1), pair, 0)

        @pl.when(lax.rem(n_far, 2) == 1)
        def _():
            body(pl.multiple_of((n_far - 1) * blk, blk), blk)

    mx_ref[...] = jnp.full(mx_ref.shape, -jnp.inf, F32)

    def max_body(start, width):
        s = raw_scores(start, width)
        for sid in range(len(streams)):
            best = mx_ref[sid]
            for chunk in lane_chunks(s[sid]):
                best = jnp.maximum(best, chunk)
            mx_ref[sid] = best

    over_far_blocks(max_body)

    row = lax.broadcasted_iota(jnp.int32, (blk, blk), 0)
    col = lax.broadcasted_iota(jnp.int32, (blk, blk), 1)
    near = raw_scores(pl.multiple_of(n_far * blk, blk), blk)
    diag = raw_scores(pl.multiple_of(i * blk, blk), blk)
    for sid, (hh, mi) in enumerate(streams):
        s_near = jnp.where(i >= 1, near[sid] * c1 + bias_ref[hh, 1], NEG_INF)
        s_diag = jnp.where(col <= row, diag[sid] * c1 + bias_ref[hh, 0], NEG_INF)
        s_ref[sid, 0] = s_near
        s_ref[sid, 1] = s_diag
        best = mx_ref[sid] * c1 + far2[hh]
        for chunk in lane_chunks(s_near) + lane_chunks(s_diag):
            best = jnp.maximum(best, chunk)
        m_rows = jnp.broadcast_to(jnp.max(best, axis=-1, keepdims=True), (blk, LANES))
        mrow_ref[sid] = m_rows
        sh_ref[sid] = far2[hh] - m_rows

    l_ref[...] = jnp.zeros(l_ref.shape, F32)
    acc_ref[...] = jnp.zeros(acc_ref.shape, F32)

    def accumulate(p_chunks, vj):
        ps = []
        for sid in range(len(streams)):
            tot = l_ref[sid]
            for chunk in p_chunks[sid]:
                tot = tot + chunk
            l_ref[sid] = tot
            ps.append(jnp.concatenate(p_chunks[sid], axis=1).astype(BF16))
        for hh in range(hp):
            acc_ref[hh] += jnp.dot(jnp.concatenate(ps[2 * hh:2 * hh + 2], axis=0), vj[:, hh * dv:(hh + 1) * dv],
                                   preferred_element_type=F32)

    def pv_body(start, width):
        s = raw_scores(start, width)
        accumulate([[jnp.exp2(chunk * c1 + sh_ref[sid]) for chunk in lane_chunks(s[sid])]
                    for sid in range(len(streams))], v_ref[pl.ds(start, width), :])

    over_far_blocks(pv_body)

    for t, start in ((0, pl.multiple_of(n_far * blk, blk)), (1, pl.multiple_of(i * blk, blk))):
        accumulate([[jnp.exp2(chunk - mrow_ref[sid]) for chunk in lane_chunks(s_ref[sid, t])]
                    for sid in range(len(streams))], v_ref[pl.ds(start, blk), :])

    lam = (jnp.exp(jnp.sum(lq1_ref[...] * lk1_ref[...], axis=-1, keepdims=True))
           - jnp.exp(jnp.sum(lq2_ref[...] * lk2_ref[...], axis=-1, keepdims=True)) + lambda_init)
    for hh in range(hp):
        l1 = jnp.sum(l_ref[2 * hh], axis=-1, keepdims=True)
        l2 = jnp.sum(l_ref[2 * hh + 1], axis=-1, keepdims=True)
        o = acc_ref[hh, :blk] / l1 - lam * (acc_ref[hh, blk:] / l2)
        o = o * lax.rsqrt(jnp.mean(jnp.square(o), axis=-1, keepdims=True) + LN_EPS) * sg_ref[...]
        o = o * (1.0 - lambda_init) * ba_ref[:, hh * dv:(hh + 1) * dv]
        o_ref[:, hh * dv:(hh + 1) * dv] = o.astype(o_ref.dtype)


def _diff_attention(proj3, rel_bias, lq1, lk1, lq2, lk2, subln_g, beta_attn, lambda_init, blk=256, hp=4):
    b, s, _ = proj3.shape
    h = ATTN_HEADS
    dv = ATTN_V_DIM
    gw = hp * dv
    groups = h // hp
    assert REL_BUCKETS // 2 + int(math.log((blk + 1) / (REL_BUCKETS // 2)) / math.log(REL_MAX_DIST / (REL_BUCKETS // 2))
                                  * (REL_BUCKETS // 2)) >= REL_BUCKETS
    tiles = _rel_bias_tiles(rel_bias, blk) * LOG2E
    far = rel_bias[REL_BUCKETS - 1].astype(F32)
    vec = lambda c: pl.BlockSpec((1, c), lambda bi, gi, qi: (0, 0))
    kernel = functools.partial(_attn_kernel, blk=blk, hp=hp, lambda_init=lambda_init)
    n_streams = 2 * hp
    stat = pltpu.VMEM((n_streams, blk, LANES), F32)
    return pl.pallas_call(
        kernel,
        grid=(b, groups, s // blk),
        in_specs=[
            pl.BlockSpec(memory_space=pltpu.SMEM),
            pl.BlockSpec((None, blk, gw), lambda bi, gi, qi: (bi, qi, gi)),
            pl.BlockSpec((None, s, gw), lambda bi, gi, qi: (bi, 0, groups + gi)),
            pl.BlockSpec((None, s, gw), lambda bi, gi, qi: (bi, 0, 2 * groups + gi)),
            pl.BlockSpec((hp, 2, blk, blk), lambda bi, gi, qi: (gi, 0, 0, 0)),
            vec(ATTN_HEAD_DIM), vec(ATTN_HEAD_DIM), vec(ATTN_HEAD_DIM), vec(ATTN_HEAD_DIM),
            vec(dv),
            pl.BlockSpec((1, gw), lambda bi, gi, qi: (0, gi)),
        ],
        out_specs=pl.BlockSpec((None, blk, gw), lambda bi, gi, qi: (bi, qi, gi)),
        out_shape=jax.ShapeDtypeStruct((b, s, h * dv), BF16),
        scratch_shapes=[stat, stat, stat, stat,
                        pltpu.VMEM((n_streams, 2, blk, blk), F32), pltpu.VMEM((hp, 2 * blk, dv), F32)],
        compiler_params=_params("arbitrary", "arbitrary", "arbitrary"),
        name="diff_attention",
    )(far, proj3, proj3, proj3, tiles, lq1, lk1, lq2, lk2, subln_g, beta_attn)


def _gelu(x):
    return 0.5 * x * (1.0 + lax.erf(x * (2.0 ** -0.5)))


def _gating_chunk(u_ref, g_ref, lg_ref, lb_ref, ws_ref, bs_ref, bg_ref, buf_ref):
    hd = u_ref.shape[1] // GMLP_HEADS
    g = _gelu(g_ref[...].astype(F32))
    mu = jnp.mean(g, axis=-1, keepdims=True)
    gc = g - mu
    var = jnp.mean(jnp.square(gc), axis=-1, keepdims=True)
    v = (gc * lax.rsqrt(var + LN_EPS) * lg_ref[...] + lb_ref[...]).astype(BF16)
    row = lax.broadcasted_iota(jnp.int32, (CHUNK, CHUNK), 0)
    col = lax.broadcasted_iota(jnp.int32, (CHUNK, CHUNK), 1)
    tril = col <= row
    ss = jnp.zeros((CHUNK, 1), F32)
    for hh in range(GMLP_HEADS):
        w = jnp.where(tril, ws_ref[hh], 0.0).astype(BF16)
        mixed = jnp.dot(w, v[:, hh * hd:(hh + 1) * hd], preferred_element_type=F32) + bs_ref[hh]
        out = _gelu(u_ref[:, hh * hd:(hh + 1) * hd].astype(F32)) * mixed
        ss = ss + jnp.sum(jnp.square(out), axis=-1, keepdims=True)
        buf_ref[:, hh * hd:(hh + 1) * hd] = out
    rstd = lax.rsqrt(ss / u_ref.shape[1] + LN_EPS)
    return (buf_ref[...] * rstd * bg_ref[...]).astype(BF16)


def _pack_bf16_pair(lo, hi):
    lo_bits = lax.bitcast_convert_type(lo.astype(BF16).astype(F32), jnp.uint32) >> 16
    hi_bits = lax.bitcast_convert_type(hi.astype(BF16).astype(F32), jnp.uint32) & jnp.uint32(0xFFFF0000)
    return hi_bits | lo_bits


def _unpack_bf16_pair(words):
    lo = lax.bitcast_convert_type(words << 16, F32)
    hi = lax.bitcast_convert_type(words & jnp.uint32(0xFFFF0000), F32)
    return lo, hi


def _mix_kernel(a_ref, u_ref, g_ref, glg_ref, glb_ref, ws_ref, bs_ref, bg_ref, w_ref, x_ref, lg_ref, lb_ref,
                wr_ref, br_ref, x1_ref, x1p_ref, lo_ref, gm_ref, gbuf_ref, buf_ref, sum_ref, mu_ref, rstd_ref,
                *, n_tiles, n_chunks, tn):
    p = pl.program_id(0)
    j = pl.program_id(1)
    ka = a_ref.shape[1]

    def gate():
        rows = pl.ds(pl.multiple_of(j * CHUNK, CHUNK), CHUNK)
        gm_ref[lax.rem(p, 2), rows, :] = _gating_chunk(u_ref, g_ref, glg_ref, glb_ref, ws_ref, bs_ref, bg_ref,
                                                       gbuf_ref)

    def normalise():
        y = (buf_ref[j] - mu_ref[...]) * rstd_ref[...] * lg_ref[...] + lb_ref[...]
        x1_ref[...] = y
        gw = X1_PACK_GROUP // 2
        for grp in range(tn // X1_PACK_GROUP):
            x1p_ref[:, grp * gw:(grp + 1) * gw] = _pack_bf16_pair(y[:, 2 * grp * gw:(2 * grp + 1) * gw],
                                                                  y[:, (2 * grp + 1) * gw:(2 * grp + 2) * gw])
        y_hi = y.astype(BF16)
        y_lo = (y - y_hi.astype(F32)).astype(BF16)
        start = jnp.where(j == 0, jnp.broadcast_to(br_ref[...], lo_ref.shape), lo_ref[...])
        by_hi = jnp.dot(y_hi, wr_ref[...], preferred_element_type=F32)
        by_lo = jnp.dot(y_lo, wr_ref[:, :LANES], preferred_element_type=F32)
        lo_ref[...] = start + (by_hi[:, :LANES] + (by_lo + by_hi[:, LANES:]))

    def matmul():
        mix = (jnp.dot(a_ref[...], w_ref[:ka, :], preferred_element_type=F32)
               + jnp.dot(gm_ref[lax.rem(p + 1, 2)], w_ref[ka:, :], preferred_element_type=F32))
        h = DN_ALPHA * x_ref[...] + mix
        buf_ref[j] = h
        part = jnp.sum(h, axis=-1, keepdims=True)
        sum_ref[...] = jnp.where(j == 0, part, sum_ref[...] + part)

    def row_stats():
        @pl.when(j == n_chunks - 1)
        def _():
            d = n_chunks * tn
            mu = sum_ref[...] / d
            sq = jnp.sum(jnp.square(buf_ref[0] - mu), axis=-1, keepdims=True)
            for c in range(1, n_chunks):
                sq = sq + jnp.sum(jnp.square(buf_ref[c] - mu), axis=-1, keepdims=True)
            mu_ref[...] = mu
            rstd_ref[...] = lax.rsqrt(sq / d + LN_EPS)

    @pl.when(p == 0)
    def _():
        gate()

    @pl.when(p == 1)
    def _():
        matmul()
        gate()
        row_stats()

    @pl.when((p >= 2) & (p < n_tiles))
    def _():
        normalise()
        matmul()
        gate()
        row_stats()

    @pl.when(p == n_tiles)
    def _():
        normalise()
        matmul()
        row_stats()

    @pl.when(p == n_tiles + 1)
    def _():
        normalise()


def _gate_mix_ln_router(attn2, proj2, gating, wb, x2, ln_g, ln_b, w_route, b_route, d_gmlp, u_col, g_col,
                        tm=512, tn=1024):
    g_ln_g, g_ln_b, w_s, b_s, beta_gmlp = gating
    m, d = x2.shape
    ka = attn2.shape[1]
    n_chunks = d // tn
    n_tiles = m // tm
    assert tm // CHUNK == n_chunks and n_tiles >= 3 and tn % X1_PACK_GROUP == 0
    last = n_chunks - 1
    gate_tile = lambda p: jnp.minimum(p, n_tiles - 1)
    mat_tile = lambda p: jnp.clip(p - 1, 0, n_tiles - 1)
    out_tile = lambda p: jnp.maximum(p - 2, 0)
    gate_chunk = lambda p, j: jnp.where(p < n_tiles, j, last)
    mat_chunk = lambda p, j: jnp.where(p < 1, 0, jnp.where(p <= n_tiles, j, last))
    out_chunk = lambda p, j: jnp.where(p >= 2, j, 0)
    gate_rows = lambda col: (lambda p, j: (gate_tile(p) * n_chunks + gate_chunk(p, j), col))
    const2 = lambda p, j: (0, 0)
    const3 = lambda p, j: (0, 0, 0)
    kernel = functools.partial(_mix_kernel, n_tiles=n_tiles, n_chunks=n_chunks, tn=tn)
    return pl.pallas_call(
        kernel,
        grid=(n_tiles + 2, n_chunks),
        in_specs=[
            pl.BlockSpec((tm, ka), lambda p, j: (mat_tile(p), 0)),
            pl.BlockSpec((CHUNK, d_gmlp), gate_rows(u_col)),
            pl.BlockSpec((CHUNK, d_gmlp), gate_rows(g_col)),
            pl.BlockSpec((1, d_gmlp), const2), pl.BlockSpec((1, d_gmlp), const2),
            pl.BlockSpec((GMLP_HEADS, CHUNK, CHUNK), const3),
            pl.BlockSpec((GMLP_HEADS, CHUNK, 1), const3),
            pl.BlockSpec((1, d_gmlp), const2),
            pl.BlockSpec((ka + d_gmlp, tn), lambda p, j: (0, mat_chunk(p, j))),
            pl.BlockSpec((tm, tn), lambda p, j: (mat_tile(p), mat_chunk(p, j))),
            pl.BlockSpec((1, tn), lambda p, j: (0, out_chunk(p, j))),
            pl.BlockSpec((1, tn), lambda p, j: (0, out_chunk(p, j))),
            pl.BlockSpec((tn, 2 * LANES), lambda p, j: (out_chunk(p, j), 0)),
            pl.BlockSpec((1, LANES), const2),
        ],
        out_specs=[
            pl.BlockSpec((tm, tn), lambda p, j: (out_tile(p), out_chunk(p, j))),
            pl.BlockSpec((tm, tn // 2), lambda p, j: (out_tile(p), out_chunk(p, j))),
            pl.BlockSpec((tm, LANES), lambda p, j: (out_tile(p), 0)),
        ],
        out_shape=[jax.ShapeDtypeStruct((m, d), F32), jax.ShapeDtypeStruct((m, d // 2), jnp.uint32),
                   jax.ShapeDtypeStruct((m, LANES), F32)],
        scratch_shapes=[pltpu.VMEM((2, tm, d_gmlp), BF16), pltpu.VMEM((CHUNK, d_gmlp), F32),
                        pltpu.VMEM((n_chunks, tm, tn), F32), pltpu.VMEM((tm, 1), F32), pltpu.VMEM((tm, 1), F32),
                        pltpu.VMEM((tm, 1), F32)],
        compiler_params=_params("arbitrary", "arbitrary", vmem_limit=VMEM_LIMIT_LARGE),
        name="gate_mix_ln_router",
    )(attn2, proj2, proj2, g_ln_g, g_ln_b, w_s, b_s[:, :, None], beta_gmlp, wb, x2, ln_g, ln_b, w_route, b_route)


def _route_kernel(lo_ref, eid_ref, wt_ref, cnt_ref):
    lg = lo_ref[...]
    lane = lax.broadcasted_iota(jnp.int32, lg.shape, 1)
    lane_f = lane.astype(F32)
    none = float(LANES)
    first = lambda hit: jnp.min(jnp.where(hit, lane_f, none), axis=-1, keepdims=True)

    in_groups = lane < N_GROUPS
    g_logits = jnp.where(in_groups, lg, -jnp.inf)
    g_max = jnp.max(g_logits, axis=-1, keepdims=True)
    g_idx = first(g_logits == g_max)
    g_w = 1.0 / jnp.sum(jnp.where(in_groups, jnp.exp(lg - g_max), 0.0), axis=-1, keepdims=True)

    lo = N_GROUPS + g_idx * EXPERTS_PER_GROUP
    in_group = (lane_f >= lo) & (lane_f < lo + EXPERTS_PER_GROUP)
    e_logits = jnp.where(in_group, lg, -jnp.inf)
    t1 = jnp.max(e_logits, axis=-1, keepdims=True)
    i1 = first(e_logits == t1)
    e_rest = jnp.where(lane_f == i1, -jnp.inf, e_logits)
    t2 = jnp.max(e_rest, axis=-1, keepdims=True)
    i2 = first(e_rest == t2)
    ex = jnp.exp(t2 - t1)
    w1 = g_w * (1.0 / (1.0 + ex))
    w2 = g_w * (ex / (1.0 + ex))
    wt_ref[...] = jnp.where(lane == 0, w1, jnp.where(lane == 1, w2, 0.0))

    @pl.when(pl.program_id(0) == 0)
    def _():
        cnt_ref[...] = jnp.zeros(cnt_ref.shape, F32)

    tm = lg.shape[0]
    pick1 = lane_f == i1 - N_GROUPS
    pick2 = lane_f == i2 - N_GROUPS
    chosen = pick1.astype(F32) + pick2.astype(F32)
    earlier = (lax.broadcasted_iota(jnp.int32, (tm, tm), 1) < lax.broadcasted_iota(jnp.int32, (tm, tm), 0))
    before = jnp.dot(earlier.astype(BF16), chosen.astype(BF16), preferred_element_type=F32) + cnt_ref[...]
    r1 = jnp.sum(jnp.where(pick1, before, 0.0), axis=-1, keepdims=True)
    r2 = jnp.sum(jnp.where(pick2, before, 0.0), axis=-1, keepdims=True)
    cnt_ref[...] += jnp.sum(chosen, axis=0, keepdims=True)
    picks = (jnp.where(lane == 0, i1 - N_GROUPS, 0.0) + jnp.where(lane == 1, i2 - N_GROUPS, 0.0)
             + jnp.where(lane == 2, r1, 0.0) + jnp.where(lane == 3, r2, 0.0))
    eid_ref[...] = picks.astype(jnp.int32)


def _route(logits, tm=1024):
    m = logits.shape[0]
    spec = pl.BlockSpec((tm, LANES), lambda i: (i, 0))
    return pl.pallas_call(
        _route_kernel,
        grid=(m // tm,),
        in_specs=[spec],
        out_specs=[spec, spec, pl.BlockSpec((1, LANES), lambda i: (0, 0))],
        out_shape=[jax.ShapeDtypeStruct((m, LANES), jnp.int32), jax.ShapeDtypeStruct((m, LANES), F32),
                   jax.ShapeDtypeStruct((1, LANES), F32)],
        compiler_params=_params("arbitrary"),
        name="route",
    )(logits)


def _expert_kernel(be_ref, cnt_ref, nreal_ref, tok_ref, aid_ref, x_hbm, wg_ref, wu_ref, wd_ref, ys_hbm,
                   xbuf, xb, acc, obuf, gsem, ssem, *, tm, nf):
    half = xb.shape[1] // 2
    b = pl.program_id(0)
    f = pl.program_id(1)
    nreal = nreal_ref[0]
    slot = lax.rem(b, 2)

    def rows_of(blk):
        return pl.multiple_of(cnt_ref[blk], SUBLANES)

    def for_each_row(n_rows, copy_of_row):
        def body(grp, carry):
            base = pl.multiple_of(grp * SUBLANES, SUBLANES)
            for k in range(SUBLANES):
                copy_of_row(base + k).start(priority=k % 2)
            return carry
        lax.fori_loop(0, lax.shift_right_logical(n_rows, SUBLANES.bit_length() - 1), body, 0)

    def start_gather(slot_, n_rows):
        for_each_row(n_rows, lambda r: pltpu.make_async_copy(
            x_hbm.at[pl.ds(tok_ref[0, r], 1)], xbuf.at[slot_, pl.ds(r, 1)], gsem.at[slot_]))

    def wait_gather(slot_, n_rows):
        pltpu.make_async_copy(x_hbm.at[pl.ds(0, n_rows)], xbuf.at[slot_, pl.ds(0, n_rows)], gsem.at[slot_]).wait()

    def start_scatter(slot_, n_rows):
        for_each_row(n_rows, lambda r: pltpu.make_async_copy(
            obuf.at[slot_, pl.ds(r, 1)], ys_hbm.at[pl.ds(aid_ref[0, r], 1)], ssem.at[slot_]))

    def wait_scatter(slot_, n_rows):
        pltpu.make_async_copy(obuf.at[slot_, pl.ds(0, n_rows)], ys_hbm.at[pl.ds(0, n_rows)], ssem.at[slot_]).wait()

    @pl.when(b < nreal)
    def _():
        @pl.when(f == 0)
        def _():
            @pl.when(b == 0)
            def _():
                xbuf[...] = jnp.zeros(xbuf.shape, xbuf.dtype)
                spare0 = ys_hbm.shape[0] - 2 * tm
                fills = [pltpu.make_async_copy(xbuf.at[h], ys_hbm.at[pl.ds(spare0 + h * tm, tm)], ssem.at[h])
                         for h in range(2)]
                for fill in fills:
                    fill.start()
                for fill in fills:
                    fill.wait()
                start_gather(0, rows_of(0))
            wait_gather(slot, rows_of(b))
            lo, hi = _unpack_bf16_pair(xbuf[slot])
            gw = X1_PACK_GROUP // 2
            for grp in range(half // gw):
                xb[:, 2 * grp * gw:(2 * grp + 1) * gw] = lo[:, grp * gw:(grp + 1) * gw].astype(BF16)
                xb[:, (2 * grp + 1) * gw:(2 * grp + 2) * gw] = hi[:, grp * gw:(grp + 1) * gw].astype(BF16)

        x = xb[...]
        g = jnp.dot(x, wg_ref[...].astype(BF16), preferred_element_type=F32)
        u = jnp.dot(x, wu_ref[...].astype(BF16), preferred_element_type=F32)
        hidden = (g * (1.0 / (1.0 + jnp.exp(-g))) * u).astype(BF16)
        contrib = jnp.dot(hidden, wd_ref[...].astype(BF16), preferred_element_type=F32)

        @pl.when(f == 0)
        def _():
            acc[...] = contrib

        @pl.when((f > 0) & (f < nf - 1))
        def _():
            acc[...] += contrib

        @pl.when(f == nf - 1)
        def _():
            @pl.when(b >= 1)
            def _():
                wait_scatter(1 - slot, rows_of(b - 1))
            total = acc[...] + contrib
            obuf[slot] = _pack_bf16_pair(total[:, :half], total[:, half:])
            start_scatter(slot, rows_of(b))

            @pl.when(b + 1 < nreal)
            def _():
                start_gather(1 - slot, rows_of(b + 1))

            @pl.when(b + 1 == nreal)
            def _():
                wait_scatter(slot, rows_of(b))


def _expert_mlp(x1p, plan, w_gate, w_up, w_down, tm, fc=256):
    tok_blocks, aid_blocks, block_expert, block_rows, n_real = plan
    n, dp = x1p.shape
    d = 2 * dp
    n_blocks = tok_blocks.shape[0]
    d_ff = w_gate.shape[2]
    nf = d_ff // fc
    assert nf >= 2

    def live(b, f, nreal):
        is_real = b < nreal[0]
        return jnp.where(is_real, b, nreal[0] - 1), jnp.where(is_real, f, nf - 1)

    def tok_map(b, f, be, cnt, nreal):
        bb, ff = live(b, f, nreal)
        return jnp.minimum(bb + (ff == nf - 1).astype(jnp.int32), nreal[0] - 1), 0, 0

    def aid_map(b, f, be, cnt, nreal):
        bb, _ = live(b, f, nreal)
        return bb, 0, 0

    def w_in_map(b, f, be, cnt, nreal):
        bb, ff = live(b, f, nreal)
        return be[bb], 0, ff

    def w_out_map(b, f, be, cnt, nreal):
        bb, ff = live(b, f, nreal)
        return be[bb], ff, 0

    grid_spec = pltpu.PrefetchScalarGridSpec(
        num_scalar_prefetch=3,
        grid=(n_blocks, nf),
        in_specs=[
            pl.BlockSpec((None, 1, tm), tok_map, memory_space=pltpu.SMEM),
            pl.BlockSpec((None, 1, tm), aid_map, memory_space=pltpu.SMEM),
            pl.BlockSpec(memory_space=pl.ANY),
            pl.BlockSpec((None, d, fc), w_in_map),
            pl.BlockSpec((None, d, fc), w_in_map),
            pl.BlockSpec((None, fc, d), w_out_map),
        ],
        out_specs=pl.BlockSpec(memory_space=pl.ANY),
        scratch_shapes=[pltpu.VMEM((2, tm, dp), jnp.uint32), pltpu.VMEM((tm, d), BF16), pltpu.VMEM((tm, d), F32),
                        pltpu.VMEM((2, tm, dp), jnp.uint32),
                        pltpu.SemaphoreType.DMA((2,)), pltpu.SemaphoreType.DMA((2,))],
    )
    kernel = functools.partial(_expert_kernel, tm=tm, nf=nf)
    return pl.pallas_call(
        kernel,
        grid_spec=grid_spec,
        out_shape=jax.ShapeDtypeStruct((n * TOP_K + 2 * tm, dp), jnp.uint32),
        compiler_params=_params("arbitrary", "arbitrary"),
        name="expert_mlp",
    )(block_expert, block_rows, n_real, tok_blocks, aid_blocks, x1p, w_gate, w_up, w_down)


def _combine_kernel(x1_ref, y0_ref, y1_ref, wt_ref, lg_ref, lb_ref, o_ref, buf_ref):
    half = x1_ref.shape[1] // 2
    wt = wt_ref[...]
    y0 = _unpack_bf16_pair(y0_ref[...])
    y1 = _unpack_bf16_pair(y1_ref[...])
    for p, cs in enumerate((slice(0, half), slice(half, 2 * half))):
        buf_ref[:, cs] = DN_ALPHA * x1_ref[:, cs] + (wt[:, 0:1] * y0[p] + wt[:, 1:2] * y1[p])
    hsum = buf_ref[...]
    mu = jnp.mean(hsum, axis=-1, keepdims=True)
    hc = hsum - mu
    var = jnp.mean(jnp.square(hc), axis=-1, keepdims=True)
    o_ref[...] = hc * lax.rsqrt(var + LN_EPS) * lg_ref[...] + lb_ref[...]


def _combine_ln(x1, ys, wts, ln_g, ln_b, tm=256):
    m, d = x1.shape
    big = pl.BlockSpec((tm, d), lambda i: (i, 0))
    first = pl.BlockSpec((tm, d // 2), lambda i: (i, 0))
    second = pl.BlockSpec((tm, d // 2), lambda i: (m // tm + i, 0))
    row_vec = pl.BlockSpec((1, d), lambda i: (0, 0))
    return pl.pallas_call(
        _combine_kernel,
        grid=(m // tm,),
        in_specs=[big, first, second, pl.BlockSpec((tm, LANES), lambda i: (i, 0)), row_vec, row_vec],
        out_specs=big,
        out_shape=jax.ShapeDtypeStruct((m, d), F32),
        scratch_shapes=[pltpu.VMEM((tm, d), F32)],
        compiler_params=_params("arbitrary"),
        name="combine_ln",
    )(x1, ys, ys, wts, ln_g, ln_b)


def _plan_blocks(eid, rank, counts, tm):
    n = eid.shape[0]
    a = n * TOP_K
    expert_id = eid.reshape(-1)
    padded = (counts + tm - 1) // tm * tm
    pad_ends = jnp.cumsum(padded)
    pad_starts = pad_ends - padded
    n_blocks = a // tm + N_EXPERTS
    experts = jnp.arange(N_EXPERTS, dtype=jnp.int32)
    start_of = jnp.dot((expert_id[:, None] == experts[None, :]).astype(F32), pad_starts.astype(F32),
                       precision=lax.Precision.HIGHEST).astype(jnp.int32)
    pos = start_of + rank.reshape(-1)
    spare = a + jnp.arange(n_blocks * tm, dtype=jnp.int32) % (2 * tm)
    flat = jnp.arange(a, dtype=jnp.int32)
    aid_pad = spare.at[pos].set((flat % TOP_K) * n + flat // TOP_K, unique_indices=True)
    tok_pad = jnp.where(aid_pad < a, aid_pad % n, 0)
    block_start = jnp.arange(n_blocks, dtype=jnp.int32) * tm
    block_expert = jnp.minimum(jnp.sum(pad_ends[None, :] <= block_start[:, None], axis=1), N_EXPERTS - 1).astype(jnp.int32)
    block_count = jnp.clip(counts[block_expert] - (block_start - pad_starts[block_expert]), 0, tm)
    block_rows = ((block_count + SUBLANES - 1) // SUBLANES * SUBLANES).astype(jnp.int32)
    n_real = (pad_ends[-1] // tm).astype(jnp.int32).reshape(1)
    return tok_pad.reshape(n_blocks, 1, tm), aid_pad.reshape(n_blocks, 1, tm), block_expert, block_rows, n_real


def kernel(x, w_in, w_out, lambda_q1, lambda_k1, lambda_q2, lambda_k2, subln_g, beta_attn, gmlp_ln_g, gmlp_ln_b,
           spatial_w, spatial_b, beta_gmlp, rel_bias, ln1_g, ln1_b, w_group, b_group, w_expert, b_expert,
           w_gate, w_up, w_down, ln2_g, ln2_b):
    b, s, d = x.shape
    n = b * s
    d_attn = ATTN_HEADS * ATTN_V_DIM
    d_gmlp = d - d_attn
    moe_tm = 320
    for l in range(DEPTH):
        lambda_init = 0.8 - 0.6 * math.exp(-0.3 * l)
        x2 = x.reshape(n, d)
        proj = _proj_matmul(x2.astype(BF16), w_in[l])
        attn = _diff_attention(proj.reshape(b, s, -1), rel_bias, lambda_q1[l][None], lambda_k1[l][None],
                               lambda_q2[l][None], lambda_k2[l][None], subln_g[l][None], beta_attn[l][None],
                               lambda_init)
        gating = (gmlp_ln_g[l][None], gmlp_ln_b[l][None], spatial_w[l], spatial_b[l], beta_gmlp[l][None])
        n_route = N_GROUPS + N_EXPERTS
        w_route = jnp.concatenate([w_group[l], w_expert[l].reshape(d, N_EXPERTS),
                                   jnp.zeros((d, LANES - n_route), F32)], axis=1)
        w_route_hi = w_route.astype(BF16)
        w_route = jnp.concatenate([w_route_hi, (w_route - w_route_hi.astype(F32)).astype(BF16)], axis=1)
        b_route = jnp.concatenate([b_group[l], b_expert[l].reshape(-1), jnp.zeros((LANES - n_route,), F32)])[None]
        x1, x1p, logits = _gate_mix_ln_router(attn.reshape(n, d_attn), proj, gating, w_out[l].astype(BF16), x2,
                                              ln1_g[l][None], ln1_b[l][None], w_route, b_route, d_gmlp,
                                              u_col=3 * d_attn // d_gmlp, g_col=3 * d_attn // d_gmlp + 1)
        picks, wts, counts = _route(logits)
        plan = _plan_blocks(picks[:, :TOP_K], picks[:, TOP_K:2 * TOP_K], counts[0, :N_EXPERTS].astype(jnp.int32),
                            moe_tm)
        ys = _expert_mlp(x1p, plan, w_gate[l], w_up[l], w_down[l], moe_tm)
        x = _combine_ln(x1, ys, wts, ln2_g[l][None], ln2_b[l][None]).reshape(b, s, d)
    return x
```

```python
import functools
import math

import jax
import jax.numpy as jnp
from jax import lax
from jax.experimental import pallas as pl
from jax.experimental.pallas import tpu as pltpu

F32 = jnp.float32
BF16 = jnp.bfloat16

ATTN_HEADS = 8
ATTN_HEAD_DIM = 128
ATTN_V_DIM = 2 * ATTN_HEAD_DIM
GMLP_HEADS = 8
CHUNK = 128
REL_BUCKETS = 32
REL_MAX_DIST = 128
N_GROUPS = 8
EXPERTS_PER_GROUP = 8
N_EXPERTS = N_GROUPS * EXPERTS_PER_GROUP
TOP_K = 2
LN_EPS = 1e-5
DEPTH = 1
DN_ALPHA = (2 * DEPTH) ** 0.25
NEG_INF = -1e30
LOG2E = math.log2(math.e)
LANES = 128
SUBLANES = 8
V7X_VMEM_BYTES = 64 * 1024 * 1024
VMEM_LIMIT = V7X_VMEM_BYTES * 7 // 8
VMEM_LIMIT_LARGE = V7X_VMEM_BYTES * 15 // 16
X1_PACK_GROUP = 256


def _params(*semantics, vmem_limit=VMEM_LIMIT):
    return pltpu.CompilerParams(dimension_semantics=semantics, vmem_limit_bytes=vmem_limit)


def _matmul_kernel(x_ref, w_ref, side_ref, o_ref, side_out_ref, wb_ref):
    @pl.when(pl.program_id(1) == 0)
    def _():
        wb_ref[...] = w_ref[...].astype(BF16)

    o_ref[...] = jnp.dot(x_ref[...], wb_ref[...], preferred_element_type=F32).astype(o_ref.dtype)
    side_out_ref[...] = side_ref[...].astype(BF16)


def _proj_matmul(xb, w, side, tm=512, tn=1024):
    m, k = xb.shape
    n = w.shape[1]
    steps_i = m // tm
    steps = (n // tn) * steps_i
    side_rows = side.shape[0]
    pack = 2 * SUBLANES
    slab = pack
    while slab * steps < side_rows or side_rows % slab:
        slab += pack
    last_slab = side_rows // slab - 1
    side_map = lambda j, i: (jnp.minimum(j * steps_i + i, last_slab), 0)
    return pl.pallas_call(
        _matmul_kernel,
        grid=(n // tn, steps_i),
        in_specs=[pl.BlockSpec((tm, k), lambda j, i: (i, 0)),
                  pl.BlockSpec((k, tn), lambda j, i: (0, j)),
                  pl.BlockSpec((slab, side.shape[1]), side_map)],
        out_specs=[pl.BlockSpec((tm, tn), lambda j, i: (i, j)),
                   pl.BlockSpec((slab, side.shape[1]), side_map)],
        out_shape=[jax.ShapeDtypeStruct((m, n), BF16), jax.ShapeDtypeStruct(side.shape, BF16)],
        scratch_shapes=[pltpu.VMEM((k, tn), BF16)],
        compiler_params=_params("arbitrary", "arbitrary"),
        name="proj_matmul",
    )(xb, w, side)


def _rel_bucket(n):
    max_exact = REL_BUCKETS // 2
    nf = jnp.maximum(n, max_exact).astype(F32)
    large = max_exact + (jnp.log(nf / max_exact) / math.log(REL_MAX_DIST / max_exact)
                         * (REL_BUCKETS - max_exact)).astype(jnp.int32)
    large = jnp.minimum(large, REL_BUCKETS - 1)
    return jnp.where(n < max_exact, n, large)


def _rel_bias_tiles(rel_bias, blk):
    pos = jnp.arange(blk)
    tiles = []
    for d in (0, 1):
        bucket = _rel_bucket(jnp.maximum(pos[:, None] + d * blk - pos[None, :], 0))
        hit = bucket[None] == jnp.arange(REL_BUCKETS)[:, None, None]
        tiles.append(jnp.sum(jnp.where(hit[:, None], rel_bias[:, :, None, None], 0.0), axis=0))
    return jnp.stack(tiles, axis=1).astype(F32)


def _attn_kernel(far_ref, q_ref, k_ref, v_ref, bias_ref, lq1_ref, lk1_ref, lq2_ref, lk2_ref,
                 sg_ref, ba_ref, o_ref, mx_ref, sh_ref, mrow_ref, l_ref, s_ref, acc_ref, *, blk, hp, lambda_init):
    g = pl.program_id(1)
    i = pl.program_id(2)
    dh = ATTN_HEAD_DIM
    dv = ATTN_V_DIM
    c1 = ATTN_HEAD_DIM ** -0.5 * LOG2E
    streams = [(hh, mi) for hh in range(hp) for mi in range(2)]
    q = q_ref[...]
    qs = [q[:, hh * dv + mi * dh:hh * dv + (mi + 1) * dh] for hh, mi in streams]
    contract_last = (((1,), (1,)), ((), ()))
    far2 = [far_ref[g * hp + hh] * LOG2E for hh in range(hp)]
    n_far = jnp.maximum(i - 1, 0)

    def lane_chunks(x):
        return [x[:, c * LANES:(c + 1) * LANES] for c in range(x.shape[1] // LANES)]

    def raw_scores(start, width):
        kj = k_ref[pl.ds(start, width), :]
        return [lax.dot_general(qs[sid], kj[:, hh * dv + mi * dh:hh * dv + (mi + 1) * dh], contract_last,
                                preferred_element_type=F32) for sid, (hh, mi) in enumerate(streams)]

    def over_far_blocks(body):
        def pair(t, carry):
            body(pl.multiple_of(t * (2 * blk), 2 * blk), 2 * blk)
            return carry
        lax.fori_loop(0, lax.shift_right_logical(n_far, 1), pair, 0)

        @pl.when(lax.rem(n_far, 2) == 1)
        def _():
            body(pl.multiple_of((n_far - 1) * blk, blk), blk)

    mx_ref[...] = jnp.full(mx_ref.shape, -jnp.inf, F32)

    def max_body(start, width):
        s = raw_scores(start, width)
        for sid in range(len(streams)):
            best = mx_ref[sid]
            for chunk in lane_chunks(s[sid]):
                best = jnp.maximum(best, chunk)
            mx_ref[sid] = best

    over_far_blocks(max_body)

    row = lax.broadcasted_iota(jnp.int32, (blk, blk), 0)
    col = lax.broadcasted_iota(jnp.int32, (blk, blk), 1)
    near = raw_scores(pl.multiple_of(n_far * blk, blk), blk)
    diag = raw_scores(pl.multiple_of(i * blk, blk), blk)
    for sid, (hh, mi) in enumerate(streams):
        s_near = jnp.where(i >= 1, near[sid] * c1 + bias_ref[hh, 1], NEG_INF)
        s_diag = jnp.where(col <= row, diag[sid] * c1 + bias_ref[hh, 0], NEG_INF)
        s_ref[sid, 0] = s_near
        s_ref[sid, 1] = s_diag
        best = mx_ref[sid] * c1 + far2[hh]
        for chunk in lane_chunks(s_near) + lane_chunks(s_diag):
            best = jnp.maximum(best, chunk)
        m_rows = jnp.broadcast_to(jnp.max(best, axis=-1, keepdims=True), (blk, LANES))
        mrow_ref[sid] = m_rows
        sh_ref[sid] = far2[hh] - m_rows

    l_ref[...] = jnp.zeros(l_ref.shape, F32)
    acc_ref[...] = jnp.zeros(acc_ref.shape, F32)

    def accumulate(p_chunks, vj):
        ps = []
        for sid in range(len(streams)):
            tot = l_ref[sid]
            for chunk in p_chunks[sid]:
                tot = tot + chunk
            l_ref[sid] = tot
            ps.append(jnp.concatenate(p_chunks[sid], axis=1).astype(BF16))
        for hh in range(hp):
            acc_ref[hh] += jnp.dot(jnp.concatenate(ps[2 * hh:2 * hh + 2], axis=0), vj[:, hh * dv:(hh + 1) * dv],
                                   preferred_element_type=F32)

    def pv_body(start, width):
        s = raw_scores(start, width)
        accumulate([[jnp.exp2(chunk * c1 + sh_ref[sid]) for chunk in lane_chunks(s[sid])]
                    for sid in range(len(streams))], v_ref[pl.ds(start, width), :])

    over_far_blocks(pv_body)

    for t, start in ((0, pl.multiple_of(n_far * blk, blk)), (1, pl.multiple_of(i * blk, blk))):
        accumulate([[jnp.exp2(chunk - mrow_ref[sid]) for chunk in lane_chunks(s_ref[sid, t])]
                    for sid in range(len(streams))], v_ref[pl.ds(start, blk), :])

    lam = (jnp.exp(jnp.sum(lq1_ref[...] * lk1_ref[...], axis=-1, keepdims=True))
           - jnp.exp(jnp.sum(lq2_ref[...] * lk2_ref[...], axis=-1, keepdims=True)) + lambda_init)
    for hh in range(hp):
        l1 = jnp.sum(l_ref[2 * hh], axis=-1, keepdims=True)
        l2 = jnp.sum(l_ref[2 * hh + 1], axis=-1, keepdims=True)
        o = acc_ref[hh, :blk] / l1 - lam * (acc_ref[hh, blk:] / l2)
        o = o * lax.rsqrt(jnp.mean(jnp.square(o), axis=-1, keepdims=True) + LN_EPS) * sg_ref[...]
        o = o * (1.0 - lambda_init) * ba_ref[:, hh * dv:(hh + 1) * dv]
        o_ref[:, hh * dv:(hh + 1) * dv] = o.astype(o_ref.dtype)


def _diff_attention(proj3, rel_bias, lq1, lk1, lq2, lk2, subln_g, beta_attn, lambda_init, blk=256, hp=4):
    b, s, _ = proj3.shape
    h = ATTN_HEADS
    dv = ATTN_V_DIM
    gw = hp * dv
    groups = h // hp
    assert REL_BUCKETS // 2 + int(math.log((blk + 1) / (REL_BUCKETS // 2)) / math.log(REL_MAX_DIST / (REL_BUCKETS // 2))
                                  * (REL_BUCKETS // 2)) >= REL_BUCKETS
    tiles = _rel_bias_tiles(rel_bias, blk) * LOG2E
    far = rel_bias[REL_BUCKETS - 1].astype(F32)
    vec = lambda c: pl.BlockSpec((1, c), lambda bi, gi, qi: (0, 0))
    kernel = functools.partial(_attn_kernel, blk=blk, hp=hp, lambda_init=lambda_init)
    n_streams = 2 * hp
    stat = pltpu.VMEM((n_streams, blk, LANES), F32)
    return pl.pallas_call(
        kernel,
        grid=(b, groups, s // blk),
        in_specs=[
            pl.BlockSpec(memory_space=pltpu.SMEM),
            pl.BlockSpec((None, blk, gw), lambda bi, gi, qi: (bi, qi, gi)),
            pl.BlockSpec((None, s, gw), lambda bi, gi, qi: (bi, 0, groups + gi)),
            pl.BlockSpec((None, s, gw), lambda bi, gi, qi: (bi, 0, 2 * groups + gi)),
            pl.BlockSpec((hp, 2, blk, blk), lambda bi, gi, qi: (gi, 0, 0, 0)),
            vec(ATTN_HEAD_DIM), vec(ATTN_HEAD_DIM), vec(ATTN_HEAD_DIM), vec(ATTN_HEAD_DIM),
            vec(dv),
            pl.BlockSpec((1, gw), lambda bi, gi, qi: (0, gi)),
        ],
        out_specs=pl.BlockSpec((None, blk, gw), lambda bi, gi, qi: (bi, qi, gi)),
        out_shape=jax.ShapeDtypeStruct((b, s, h * dv), BF16),
        scratch_shapes=[stat, stat, stat, stat,
                        pltpu.VMEM((n_streams, 2, blk, blk), F32), pltpu.VMEM((hp, 2 * blk, dv), F32)],
        compiler_params=_params("arbitrary", "arbitrary", "arbitrary"),
        name="diff_attention",
    )(far, proj3, proj3, proj3, tiles, lq1, lk1, lq2, lk2, subln_g, beta_attn)


def _gelu(x):
    return 0.5 * x * (1.0 + lax.erf(x * (2.0 ** -0.5)))


def _gating_chunk(u_ref, g_ref, lg_ref, lb_ref, ws_ref, bs_ref, bg_ref, buf_ref):
    hd = u_ref.shape[1] // GMLP_HEADS
    g = _gelu(g_ref[...].astype(F32))
    mu = jnp.mean(g, axis=-1, keepdims=True)
    gc = g - mu
    var = jnp.mean(jnp.square(gc), axis=-1, keepdims=True)
    v = (gc * lax.rsqrt(var + LN_EPS) * lg_ref[...] + lb_ref[...]).astype(BF16)
    row = lax.broadcasted_iota(jnp.int32, (CHUNK, CHUNK), 0)
    col = lax.broadcasted_iota(jnp.int32, (CHUNK, CHUNK), 1)
    tril = col <= row
    ss = jnp.zeros((CHUNK, 1), F32)
    for hh in range(GMLP_HEADS):
        w = jnp.where(tril, ws_ref[hh], 0.0).astype(BF16)
        mixed = jnp.dot(w, v[:, hh * hd:(hh + 1) * hd], preferred_element_type=F32) + bs_ref[hh]
        out = _gelu(u_ref[:, hh * hd:(hh + 1) * hd].astype(F32)) * mixed
        ss = ss + jnp.sum(jnp.square(out), axis=-1, keepdims=True)
        buf_ref[:, hh * hd:(hh + 1) * hd] = out
    rstd = lax.rsqrt(ss / u_ref.shape[1] + LN_EPS)
    return (buf_ref[...] * rstd * bg_ref[...]).astype(BF16)


def _pack_bf16_pair(lo, hi):
    lo_bits = lax.bitcast_convert_type(lo.astype(BF16).astype(F32), jnp.uint32) >> 16
    hi_bits = lax.bitcast_convert_type(hi.astype(BF16).astype(F32), jnp.uint32) & jnp.uint32(0xFFFF0000)
    return hi_bits | lo_bits


def _unpack_bf16_pair(words):
    lo = lax.bitcast_convert_type(words << 16, F32)
    hi = lax.bitcast_convert_type(words & jnp.uint32(0xFFFF0000), F32)
    return lo, hi


def _mix_kernel(a_ref, u_ref, g_ref, glg_ref, glb_ref, ws_ref, bs_ref, bg_ref, w_ref, x_ref, lg_ref, lb_ref,
                wr_ref, br_ref, x1_ref, x1p_ref, lo_ref, gm_ref, gbuf_ref, buf_ref, sum_ref, mu_ref, rstd_ref,
                *, n_tiles, n_chunks, tn):
    p = pl.program_id(0)
    j = pl.program_id(1)
    ka = a_ref.shape[1]

    def gate():
        rows = pl.ds(pl.multiple_of(j * CHUNK, CHUNK), CHUNK)
        gm_ref[lax.rem(p, 2), rows, :] = _gating_chunk(u_ref, g_ref, glg_ref, glb_ref, ws_ref, bs_ref, bg_ref,
                                                       gbuf_ref)

    def normalise():
        y = (buf_ref[j] - mu_ref[...]) * rstd_ref[...] * lg_ref[...] + lb_ref[...]
        x1_ref[...] = y
        gw = X1_PACK_GROUP // 2
        for grp in range(tn // X1_PACK_GROUP):
            x1p_ref[:, grp * gw:(grp + 1) * gw] = _pack_bf16_pair(y[:, 2 * grp * gw:(2 * grp + 1) * gw],
                                                                  y[:, (2 * grp + 1) * gw:(2 * grp + 2) * gw])
        y_hi = y.astype(BF16)
        y_lo = (y - y_hi.astype(F32)).astype(BF16)
        start = jnp.where(j == 0, jnp.broadcast_to(br_ref[...], lo_ref.shape), lo_ref[...])
        by_hi = jnp.dot(y_hi, wr_ref[...], preferred_element_type=F32)
        by_lo = jnp.dot(y_lo, wr_ref[:, :LANES], preferred_element_type=F32)
        lo_ref[...] = start + (by_hi[:, :LANES] + (by_lo + by_hi[:, LANES:]))

    def matmul():
        mix = (jnp.dot(a_ref[...], w_ref[:ka, :], preferred_element_type=F32)
               + jnp.dot(gm_ref[lax.rem(p + 1, 2)], w_ref[ka:, :], preferred_element_type=F32))
        h = DN_ALPHA * x_ref[...] + mix
        buf_ref[j] = h
        part = jnp.sum(h, axis=-1, keepdims=True)
        sum_ref[...] = jnp.where(j == 0, part, sum_ref[...] + part)

    def row_stats():
        @pl.when(j == n_chunks - 1)
        def _():
            d = n_chunks * tn
            mu = sum_ref[...] / d
            sq = jnp.sum(jnp.square(buf_ref[0] - mu), axis=-1, keepdims=True)
            for c in range(1, n_chunks):
                sq = sq + jnp.sum(jnp.square(buf_ref[c] - mu), axis=-1, keepdims=True)
            mu_ref[...] = mu
            rstd_ref[...] = lax.rsqrt(sq / d + LN_EPS)

    @pl.when(p == 0)
    def _():
        gate()

    @pl.when(p == 1)
    def _():
        matmul()
        gate()
        row_stats()

    @pl.when((p >= 2) & (p < n_tiles))
    def _():
        normalise()
        matmul()
        gate()
        row_stats()

    @pl.when(p == n_tiles)
    def _():
        normalise()
        matmul()
        row_stats()

    @pl.when(p == n_tiles + 1)
    def _():
        normalise()


def _gate_mix_ln_router(attn2, proj2, gating, wb, x2, ln_g, ln_b, w_route, b_route, d_gmlp, u_col, g_col,
                        tm=512, tn=1024):
    g_ln_g, g_ln_b, w_s, b_s, beta_gmlp = gating
    m, d = x2.shape
    ka = attn2.shape[1]
    n_chunks = d // tn
    n_tiles = m // tm
    assert tm // CHUNK == n_chunks and n_tiles >= 3 and tn % X1_PACK_GROUP == 0
    last = n_chunks - 1
    gate_tile = lambda p: jnp.minimum(p, n_tiles - 1)
    mat_tile = lambda p: jnp.clip(p - 1, 0, n_tiles - 1)
    out_tile = lambda p: jnp.maximum(p - 2, 0)
    gate_chunk = lambda p, j: jnp.where(p < n_tiles, j, last)
    mat_chunk = lambda p, j: jnp.where(p < 1, 0, jnp.where(p <= n_tiles, j, last))
    out_chunk = lambda p, j: jnp.where(p >= 2, j, 0)
    gate_rows = lambda col: (lambda p, j: (gate_tile(p) * n_chunks + gate_chunk(p, j), col))
    const2 = lambda p, j: (0, 0)
    const3 = lambda p, j: (0, 0, 0)
    kernel = functools.partial(_mix_kernel, n_tiles=n_tiles, n_chunks=n_chunks, tn=tn)
    return pl.pallas_call(
        kernel,
        grid=(n_tiles + 2, n_chunks),
        in_specs=[
            pl.BlockSpec((tm, ka), lambda p, j: (mat_tile(p), 0)),
            pl.BlockSpec((CHUNK, d_gmlp), gate_rows(u_col)),
            pl.BlockSpec((CHUNK, d_gmlp), gate_rows(g_col)),
            pl.BlockSpec((1, d_gmlp), const2), pl.BlockSpec((1, d_gmlp), const2),
            pl.BlockSpec((GMLP_HEADS, CHUNK, CHUNK), const3),
            pl.BlockSpec((GMLP_HEADS, CHUNK, 1), const3),
            pl.BlockSpec((1, d_gmlp), const2),
            pl.BlockSpec((ka + d_gmlp, tn), lambda p, j: (0, mat_chunk(p, j))),
            pl.BlockSpec((tm, tn), lambda p, j: (mat_tile(p), mat_chunk(p, j))),
            pl.BlockSpec((1, tn), lambda p, j: (0, out_chunk(p, j))),
            pl.BlockSpec((1, tn), lambda p, j: (0, out_chunk(p, j))),
            pl.BlockSpec((tn, 2 * LANES), lambda p, j: (out_chunk(p, j), 0)),
            pl.BlockSpec((1, LANES), const2),
        ],
        out_specs=[
            pl.BlockSpec((tm, tn), lambda p, j: (out_tile(p), out_chunk(p, j))),
            pl.BlockSpec((tm, tn // 2), lambda p, j: (out_tile(p), out_chunk(p, j))),
            pl.BlockSpec((tm, LANES), lambda p, j: (out_tile(p), 0)),
        ],
        out_shape=[jax.ShapeDtypeStruct((m, d), F32), jax.ShapeDtypeStruct((m, d // 2), jnp.uint32),
                   jax.ShapeDtypeStruct((m, LANES), F32)],
        scratch_shapes=[pltpu.VMEM((2, tm, d_gmlp), BF16), pltpu.VMEM((CHUNK, d_gmlp), F32),
                        pltpu.VMEM((n_chunks, tm, tn), F32), pltpu.VMEM((tm, 1), F32), pltpu.VMEM((tm, 1), F32),
                        pltpu.VMEM((tm, 1), F32)],
        compiler_params=_params("arbitrary", "arbitrary", vmem_limit=VMEM_LIMIT_LARGE),
        name="gate_mix_ln_router",
    )(attn2, proj2, proj2, g_ln_g, g_ln_b, w_s, b_s[:, :, None], beta_gmlp, wb, x2, ln_g, ln_b, w_route, b_route)


def _route_kernel(lo_ref, eid_ref, wt_ref, cnt_ref):
    lg = lo_ref[...]
    lane = lax.broadcasted_iota(jnp.int32, lg.shape, 1)
    lane_f = lane.astype(F32)
    none = float(LANES)
    first = lambda hit: jnp.min(jnp.where(hit, lane_f, none), axis=-1, keepdims=True)

    in_groups = lane < N_GROUPS
    g_logits = jnp.where(in_groups, lg, -jnp.inf)
    g_max = jnp.max(g_logits, axis=-1, keepdims=True)
    g_idx = first(g_logits == g_max)
    g_w = 1.0 / jnp.sum(jnp.where(in_groups, jnp.exp(lg - g_max), 0.0), axis=-1, keepdims=True)

    lo = N_GROUPS + g_idx * EXPERTS_PER_GROUP
    in_group = (lane_f >= lo) & (lane_f < lo + EXPERTS_PER_GROUP)
    e_logits = jnp.where(in_group, lg, -jnp.inf)
    t1 = jnp.max(e_logits, axis=-1, keepdims=True)
    i1 = first(e_logits == t1)
    e_rest = jnp.where(lane_f == i1, -jnp.inf, e_logits)
    t2 = jnp.max(e_rest, axis=-1, keepdims=True)
    i2 = first(e_rest == t2)
    ex = jnp.exp(t2 - t1)
    w1 = g_w * (1.0 / (1.0 + ex))
    w2 = g_w * (ex / (1.0 + ex))
    wt_ref[...] = jnp.where(lane == 0, w1, jnp.where(lane == 1, w2, 0.0))

    @pl.when(pl.program_id(0) == 0)
    def _():
        cnt_ref[...] = jnp.zeros(cnt_ref.shape, F32)

    tm = lg.shape[0]
    pick1 = lane_f == i1 - N_GROUPS
    pick2 = lane_f == i2 - N_GROUPS
    chosen = pick1.astype(F32) + pick2.astype(F32)
    earlier = (lax.broadcasted_iota(jnp.int32, (tm, tm), 1) < lax.broadcasted_iota(jnp.int32, (tm, tm), 0))
    before = jnp.dot(earlier.astype(BF16), chosen.astype(BF16), preferred_element_type=F32) + cnt_ref[...]
    r1 = jnp.sum(jnp.where(pick1, before, 0.0), axis=-1, keepdims=True)
    r2 = jnp.sum(jnp.where(pick2, before, 0.0), axis=-1, keepdims=True)
    cnt_ref[...] += jnp.sum(chosen, axis=0, keepdims=True)
    picks = (jnp.where(lane == 0, i1 - N_GROUPS, 0.0) + jnp.where(lane == 1, i2 - N_GROUPS, 0.0)
             + jnp.where(lane == 2, r1, 0.0) + jnp.where(lane == 3, r2, 0.0))
    eid_ref[...] = picks.astype(jnp.int32)


def _route(logits, tm=1024):
    m = logits.shape[0]
    spec = pl.BlockSpec((tm, LANES), lambda i: (i, 0))
    return pl.pallas_call(
        _route_kernel,
        grid=(m // tm,),
        in_specs=[spec],
        out_specs=[spec, spec, pl.BlockSpec((1, LANES), lambda i: (0, 0))],
        out_shape=[jax.ShapeDtypeStruct((m, LANES), jnp.int32), jax.ShapeDtypeStruct((m, LANES), F32),
                   jax.ShapeDtypeStruct((1, LANES), F32)],
        compiler_params=_params("arbitrary"),
        name="route",
    )(logits)


def _expert_kernel(be_ref, cnt_ref, nreal_ref, tok_ref, aid_ref, x_hbm, wg_ref, wu_ref, wd_ref, ys_hbm,
                   xbuf, xb, acc, obuf, gsem, ssem, *, tm, nf):
    half = xb.shape[1] // 2
    b = pl.program_id(0)
    f = pl.program_id(1)
    nreal = nreal_ref[0]
    slot = lax.rem(b, 2)

    def rows_of(blk):
        return pl.multiple_of(cnt_ref[blk], SUBLANES)

    def for_each_row(n_rows, copy_of_row):
        def body(grp, carry):
            base = pl.multiple_of(grp * SUBLANES, SUBLANES)
            for k in range(SUBLANES):
                copy_of_row(base + k).start(priority=k % 2)
            return carry
        lax.fori_loop(0, lax.shift_right_logical(n_rows, SUBLANES.bit_length() - 1), body, 0)

    def start_gather(slot_, n_rows):
        for_each_row(n_rows, lambda r: pltpu.make_async_copy(
            x_hbm.at[pl.ds(tok_ref[0, r], 1)], xbuf.at[slot_, pl.ds(r, 1)], gsem.at[slot_]))

    def wait_gather(slot_, n_rows):
        pltpu.make_async_copy(x_hbm.at[pl.ds(0, n_rows)], xbuf.at[slot_, pl.ds(0, n_rows)], gsem.at[slot_]).wait()

    def start_scatter(slot_, n_rows):
        for_each_row(n_rows, lambda r: pltpu.make_async_copy(
            obuf.at[slot_, pl.ds(r, 1)], ys_hbm.at[pl.ds(aid_ref[0, r], 1)], ssem.at[slot_]))

    def wait_scatter(slot_, n_rows):
        pltpu.make_async_copy(obuf.at[slot_, pl.ds(0, n_rows)], ys_hbm.at[pl.ds(0, n_rows)], ssem.at[slot_]).wait()

    @pl.when(b < nreal)
    def _():
        @pl.when(f == 0)
        def _():
            @pl.when(b == 0)
            def _():
                xbuf[...] = jnp.zeros(xbuf.shape, xbuf.dtype)
                spare0 = ys_hbm.shape[0] - 2 * tm
                fills = [pltpu.make_async_copy(xbuf.at[h], ys_hbm.at[pl.ds(spare0 + h * tm, tm)], ssem.at[h])
                         for h in range(2)]
                for fill in fills:
                    fill.start()
                for fill in fills:
                    fill.wait()
                start_gather(0, rows_of(0))
            wait_gather(slot, rows_of(b))
            lo, hi = _unpack_bf16_pair(xbuf[slot])
            gw = X1_PACK_GROUP // 2
            for grp in range(half // gw):
                xb[:, 2 * grp * gw:(2 * grp + 1) * gw] = lo[:, grp * gw:(grp + 1) * gw].astype(BF16)
                xb[:, (2 * grp + 1) * gw:(2 * grp + 2) * gw] = hi[:, grp * gw:(grp + 1) * gw].astype(BF16)

        x = xb[...]
        g = jnp.dot(x, wg_ref[...].astype(BF16), preferred_element_type=F32)
        u = jnp.dot(x, wu_ref[...].astype(BF16), preferred_element_type=F32)
        hidden = (g * (1.0 / (1.0 + jnp.exp(-g))) * u).astype(BF16)
        contrib = jnp.dot(hidden, wd_ref[...].astype(BF16), preferred_element_type=F32)

        @pl.when(f == 0)
        def _():
            acc[...] = contrib

        @pl.when((f > 0) & (f < nf - 1))
        def _():
            acc[...] += contrib

        @pl.when(f == nf - 1)
        def _():
            @pl.when(b >= 1)
            def _():
                wait_scatter(1 - slot, rows_of(b - 1))
            total = acc[...] + contrib
            obuf[slot] = _pack_bf16_pair(total[:, :half], total[:, half:])
            start_scatter(slot, rows_of(b))

            @pl.when(b + 1 < nreal)
            def _():
                start_gather(1 - slot, rows_of(b + 1))

            @pl.when(b + 1 == nreal)
            def _():
                wait_scatter(slot, rows_of(b))


def _expert_mlp(x1p, plan, w_gate, w_up, w_down, tm, fc=256):
    tok_blocks, aid_blocks, block_expert, block_rows, n_real = plan
    n, dp = x1p.shape
    d = 2 * dp
    n_blocks = tok_blocks.shape[0]
    d_ff = w_gate.shape[2]
    nf = d_ff // fc
    assert nf >= 2

    def live(b, f, nreal):
        is_real = b < nreal[0]
        return jnp.where(is_real, b, nreal[0] - 1), jnp.where(is_real, f, nf - 1)

    def tok_map(b, f, be, cnt, nreal):
        bb, ff = live(b, f, nreal)
        return jnp.minimum(bb + (ff == nf - 1).astype(jnp.int32), nreal[0] - 1), 0, 0

    def aid_map(b, f, be, cnt, nreal):
        bb, _ = live(b, f, nreal)
        return bb, 0, 0

    def w_in_map(b, f, be, cnt, nreal):
        bb, ff = live(b, f, nreal)
        return be[bb], 0, ff

    def w_out_map(b, f, be, cnt, nreal):
        bb, ff = live(b, f, nreal)
        return be[bb], ff, 0

    grid_spec = pltpu.PrefetchScalarGridSpec(
        num_scalar_prefetch=3,
        grid=(n_blocks, nf),
        in_specs=[
            pl.BlockSpec((None, 1, tm), tok_map, memory_space=pltpu.SMEM),
            pl.BlockSpec((None, 1, tm), aid_map, memory_space=pltpu.SMEM),
            pl.BlockSpec(memory_space=pl.ANY),
            pl.BlockSpec((None, d, fc), w_in_map),
            pl.BlockSpec((None, d, fc), w_in_map),
            pl.BlockSpec((None, fc, d), w_out_map),
        ],
        out_specs=pl.BlockSpec(memory_space=pl.ANY),
        scratch_shapes=[pltpu.VMEM((2, tm, dp), jnp.uint32), pltpu.VMEM((tm, d), BF16), pltpu.VMEM((tm, d), F32),
                        pltpu.VMEM((2, tm, dp), jnp.uint32),
                        pltpu.SemaphoreType.DMA((2,)), pltpu.SemaphoreType.DMA((2,))],
    )
    kernel = functools.partial(_expert_kernel, tm=tm, nf=nf)
    return pl.pallas_call(
        kernel,
        grid_spec=grid_spec,
        out_shape=jax.ShapeDtypeStruct((n * TOP_K + 2 * tm, dp), jnp.uint32),
        compiler_params=_params("arbitrary", "arbitrary"),
        name="expert_mlp",
    )(block_expert, block_rows, n_real, tok_blocks, aid_blocks, x1p, w_gate, w_up, w_down)


def _combine_kernel(x1_ref, y0_ref, y1_ref, wt_ref, lg_ref, lb_ref, o_ref, buf_ref):
    half = x1_ref.shape[1] // 2
    wt = wt_ref[...]
    y0 = _unpack_bf16_pair(y0_ref[...])
    y1 = _unpack_bf16_pair(y1_ref[...])
    for p, cs in enumerate((slice(0, half), slice(half, 2 * half))):
        buf_ref[:, cs] = DN_ALPHA * x1_ref[:, cs] + (wt[:, 0:1] * y0[p] + wt[:, 1:2] * y1[p])
    hsum = buf_ref[...]
    mu = jnp.mean(hsum, axis=-1, keepdims=True)
    hc = hsum - mu
    var = jnp.mean(jnp.square(hc), axis=-1, keepdims=True)
    o_ref[...] = hc * lax.rsqrt(var + LN_EPS) * lg_ref[...] + lb_ref[...]


def _combine_ln(x1, ys, wts, ln_g, ln_b, tm=256):
    m, d = x1.shape
    big = pl.BlockSpec((tm, d), lambda i: (i, 0))
    first = pl.BlockSpec((tm, d // 2), lambda i: (i, 0))
    second = pl.BlockSpec((tm, d // 2), lambda i: (m // tm + i, 0))
    row_vec = pl.BlockSpec((1, d), lambda i: (0, 0))
    return pl.pallas_call(
        _combine_kernel,
        grid=(m // tm,),
        in_specs=[big, first, second, pl.BlockSpec((tm, LANES), lambda i: (i, 0)), row_vec, row_vec],
        out_specs=big,
        out_shape=jax.ShapeDtypeStruct((m, d), F32),
        scratch_shapes=[pltpu.VMEM((tm, d), F32)],
        compiler_params=_params("arbitrary"),
        name="combine_ln",
    )(x1, ys, ys, wts, ln_g, ln_b)


def _plan_blocks(eid, rank, counts, tm):
    n = eid.shape[0]
    a = n * TOP_K
    expert_id = eid.reshape(-1)
    padded = (counts + tm - 1) // tm * tm
    pad_ends = jnp.cumsum(padded)
    pad_starts = pad_ends - padded
    n_blocks = a // tm + N_EXPERTS
    experts = jnp.arange(N_EXPERTS, dtype=jnp.int32)
    start_of = jnp.dot((expert_id[:, None] == experts[None, :]).astype(F32), pad_starts.astype(F32),
                       precision=lax.Precision.HIGHEST).astype(jnp.int32)
    pos = start_of + rank.reshape(-1)
    spare = a + jnp.arange(n_blocks * tm, dtype=jnp.int32) % (2 * tm)
    flat = jnp.arange(a, dtype=jnp.int32)
    aid_pad = spare.at[pos].set((flat % TOP_K) * n + flat // TOP_K, unique_indices=True)
    tok_pad = jnp.where(aid_pad < a, aid_pad % n, 0)
    block_start = jnp.arange(n_blocks, dtype=jnp.int32) * tm
    block_expert = jnp.minimum(jnp.sum(pad_ends[None, :] <= block_start[:, None], axis=1), N_EXPERTS - 1).astype(jnp.int32)
    block_count = jnp.clip(counts[block_expert] - (block_start - pad_starts[block_expert]), 0, tm)
    block_rows = ((block_count + SUBLANES - 1) // SUBLANES * SUBLANES).astype(jnp.int32)
    n_real = (pad_ends[-1] // tm).astype(jnp.int32).reshape(1)
    return tok_pad.reshape(n_blocks, 1, tm), aid_pad.reshape(n_blocks, 1, tm), block_expert, block_rows, n_real


def kernel(x, w_in, w_out, lambda_q1, lambda_k1, lambda_q2, lambda_k2, subln_g, beta_attn, gmlp_ln_g, gmlp_ln_b,
           spatial_w, spatial_b, beta_gmlp, rel_bias, ln1_g, ln1_b, w_group, b_group, w_expert, b_expert,
           w_gate, w_up, w_down, ln2_g, ln2_b):
    b, s, d = x.shape
    n = b * s
    d_attn = ATTN_HEADS * ATTN_V_DIM
    d_gmlp = d - d_attn
    moe_tm = 320
    for l in range(DEPTH):
        lambda_init = 0.8 - 0.6 * math.exp(-0.3 * l)
        x2 = x.reshape(n, d)
        proj, w_out_bf16 = _proj_matmul(x2.astype(BF16), w_in[l], w_out[l])
        attn = _diff_attention(proj.reshape(b, s, -1), rel_bias, lambda_q1[l][None], lambda_k1[l][None],
                               lambda_q2[l][None], lambda_k2[l][None], subln_g[l][None], beta_attn[l][None],
                               lambda_init)
        gating = (gmlp_ln_g[l][None], gmlp_ln_b[l][None], spatial_w[l], spatial_b[l], beta_gmlp[l][None])
        n_route = N_GROUPS + N_EXPERTS
        w_route = jnp.concatenate([w_group[l], w_expert[l].reshape(d, N_EXPERTS),
                                   jnp.zeros((d, LANES - n_route), F32)], axis=1)
        w_route_hi = w_route.astype(BF16)
        w_route = jnp.concatenate([w_route_hi, (w_route - w_route_hi.astype(F32)).astype(BF16)], axis=1)
        b_route = jnp.concatenate([b_group[l], b_expert[l].reshape(-1), jnp.zeros((LANES - n_route,), F32)])[None]
        x1, x1p, logits = _gate_mix_ln_router(attn.reshape(n, d_attn), proj, gating, w_out_bf16, x2,
                                              ln1_g[l][None], ln1_b[l][None], w_route, b_route, d_gmlp,
                                              u_col=3 * d_attn // d_gmlp, g_col=3 * d_attn // d_gmlp + 1)
        picks, wts, counts = _route(logits)
        plan = _plan_blocks(picks[:, :TOP_K], picks[:, TOP_K:2 * TOP_K], counts[0, :N_EXPERTS].astype(jnp.int32),
                            moe_tm)
        ys = _expert_mlp(x1p, plan, w_gate[l], w_up[l], w_down[l], moe_tm)
        x = _combine_ln(x1, ys, wts, ln2_g[l][None], ln2_b[l][None]).reshape(b, s, d)
    return x
```
